```python
import math
import jax, jax.numpy as jnp
from jax import lax
import numpy as np

D_MODEL = 2048
BATCH = 2
SEQ = 4096
DEPTH = 1

GRID_W = 64
NA_HEADS = 8
NA_DH = 128
NA_KR_MAX = 8
NA_KW = 16
DF_HEADS = 8
DF_DQ = 64
DF_DV = 2 * DF_DQ
Q_BLOCK = 128
FFN_HID = int(math.ceil(8 * D_MODEL / 3 / 256) * 256)
PLE_DIM = 256
ROPE_THETA = 10000.0
EPS = 1e-6

NA_W = NA_HEADS * NA_DH
DF_QW = DF_HEADS * 2 * DF_DQ
DF_VW = DF_HEADS * DF_DV
IN_COLS = 3 * NA_W + 2 * DF_QW + DF_VW + 2 * D_MODEL

kernel_name = "hybrid_natten_diffattn_gated_encoder"


def rmsnorm(x, g):
    xf = x.astype(jnp.float32)
    y = xf * lax.rsqrt(jnp.mean(xf * xf, axis=-1, keepdims=True) + EPS)
    return (y * g.astype(jnp.float32)).astype(x.dtype)


def rope_tables(S, d):
    inv = 1.0 / (ROPE_THETA ** (jnp.arange(0, d, 2, dtype=jnp.float32) / d))
    ang = jnp.arange(S, dtype=jnp.float32)[:, None] * inv[None, :]
    return jnp.cos(ang), jnp.sin(ang)


def apply_rope(x, cos, sin):
    c = cos.astype(x.dtype)
    s = sin.astype(x.dtype)
    x1, x2 = jnp.split(x, 2, axis=-1)
    return jnp.concatenate([x1 * c - x2 * s, x1 * s + x2 * c], axis=-1)


def neighbourhood_attention(q, k, v, rpb):
    B, S, H, dh = q.shape
    rows = S // GRID_W
    kr = min(NA_KR_MAX, rows)
    scale = dh ** -0.5
    qg = q.transpose(0, 2, 1, 3).reshape(B, H, rows, GRID_W, dh)
    kg = k.transpose(0, 2, 1, 3).reshape(B, H, rows, GRID_W, dh)
    vg = v.transpose(0, 2, 1, 3).reshape(B, H, rows, GRID_W, dh)
    c = np.arange(GRID_W)
    cs = np.clip(c - NA_KW // 2, 0, GRID_W - NA_KW)
    col_idx = cs[:, None] + np.arange(NA_KW)[None, :]
    dc = col_idx - c[:, None] + (NA_KW - 1)

    def one_row(r):
        rs = jnp.clip(r - kr // 2, 0, rows - kr)
        k_band = lax.dynamic_slice_in_dim(kg, rs, kr, axis=2)
        v_band = lax.dynamic_slice_in_dim(vg, rs, kr, axis=2)
        q_row = lax.dynamic_index_in_dim(qg, r, axis=2, keepdims=False)
        k_sel = k_band[:, :, :, col_idx, :]
        v_sel = v_band[:, :, :, col_idx, :]
        dr = rs + jnp.arange(kr) - r + (NA_KR_MAX - 1)
        bias = rpb[:, dr[None, :, None], dc[:, None, :]]
        s = jnp.einsum('bhcd,bhrckd->bhcrk', q_row, k_sel).astype(jnp.float32) * scale
        s = s + bias.astype(jnp.float32)[None]
        pm = jax.nn.softmax(s.reshape(B, H, GRID_W, kr * NA_KW), axis=-1)
        pm = pm.reshape(B, H, GRID_W, kr, NA_KW).astype(v.dtype)
        return jnp.einsum('bhcrk,bhrckd->bhcd', pm, v_sel)

    o = lax.map(one_row, jnp.arange(rows))
    return o.transpose(1, 0, 3, 2, 4).reshape(B, S, H * dh)


def differential_attention(q, k, v, lam):
    B, S, H, _, dq = q.shape
    nb = S // Q_BLOCK
    scale = dq ** -0.5
    q_t = q.transpose(0, 2, 3, 1, 4)
    k_t = k.transpose(0, 2, 3, 1, 4)
    v_t = v.transpose(0, 2, 1, 3)
    q_blocks = q_t.reshape(B, H, 2, nb, Q_BLOCK, dq).transpose(3, 0, 1, 2, 4, 5)

    def one_block(qb):
        s = jnp.einsum('bhmqd,bhmkd->bhmqk', qb, k_t).astype(jnp.float32) * scale
        pm = jax.nn.softmax(s, axis=-1)
        a = pm[:, :, 0] - lam * pm[:, :, 1]
        return jnp.einsum('bhqk,bhkd->bhqd', a.astype(v_t.dtype), v_t)

    o = lax.map(one_block, q_blocks)
    return o.transpose(1, 0, 3, 2, 4).reshape(B, S, H, v.shape[-1])


def setup_inputs(seed: int = 0) -> dict:
    key = jax.random.key(seed)
    ks = jax.random.split(key, 24)
    L = DEPTH
    n = lambda k, shape, s: jax.random.normal(k, shape, jnp.float32) * s
    gain = lambda k, d: 1.0 + 0.02 * jax.random.normal(k, (L, d), jnp.float32)
    return {
        "x": n(ks[0], (BATCH, SEQ, D_MODEL), 1.0),
        "p": n(ks[1], (DEPTH, BATCH, SEQ, PLE_DIM), 1.0),
        "g_mix": gain(ks[2], D_MODEL),
        "w_in": n(ks[3], (L, D_MODEL, IN_COLS), D_MODEL ** -0.5),
        "g_na_q": gain(ks[4], NA_DH),
        "g_na_k": gain(ks[5], NA_DH),
        "na_rpb": n(ks[6], (L, NA_HEADS, 2 * NA_KR_MAX - 1, 2 * NA_KW - 1), 0.1),
        "g_df_q": gain(ks[7], DF_DQ),
        "g_df_k": gain(ks[8], DF_DQ),
        "lam_q1": n(ks[9], (L, DF_DQ), 0.1),
        "lam_k1": n(ks[10], (L, DF_DQ), 0.1),
        "lam_q2": n(ks[11], (L, DF_DQ), 0.1),
        "lam_k2": n(ks[12], (L, DF_DQ), 0.1),
        "g_df_sub": gain(ks[13], DF_DV),
        "w_na_out": n(ks[14], (L, NA_W, D_MODEL), NA_W ** -0.5),
        "w_df_out": n(ks[15], (L, DF_VW, D_MODEL), DF_VW ** -0.5),
        "w_o": n(ks[16], (L, D_MODEL, D_MODEL), D_MODEL ** -0.5),
        "g_ffn": gain(ks[17], D_MODEL),
        "w_gate": n(ks[18], (L, D_MODEL, FFN_HID), D_MODEL ** -0.5),
        "w_up": n(ks[19], (L, D_MODEL, FFN_HID), D_MODEL ** -0.5),
        "w_down": n(ks[20], (L, FFN_HID, D_MODEL), FFN_HID ** -0.5),
        "g_ple": gain(ks[21], D_MODEL),
        "w_ple_gate": n(ks[22], (L, D_MODEL, D_MODEL), D_MODEL ** -0.5),
        "w_ple_proj": n(ks[23], (L, PLE_DIM, D_MODEL), PLE_DIM ** -0.5),
    }


def reference(x, p, g_mix, w_in, g_na_q, g_na_k, na_rpb, g_df_q, g_df_k,
              lam_q1, lam_k1, lam_q2, lam_k2, g_df_sub, w_na_out, w_df_out, w_o,
              g_ffn, w_gate, w_up, w_down, g_ple, w_ple_gate, w_ple_proj):
    B, S, D = x.shape
    cos, sin = rope_tables(S, DF_DQ)
    cos_b = cos[None, :, None, None, :]
    sin_b = sin[None, :, None, None, :]
    splits = np.cumsum([NA_W, NA_W, NA_W, DF_QW, DF_QW, DF_VW, D_MODEL])
    for i in range(DEPTH):
        lam_init = 0.8 - 0.6 * math.exp(-0.3 * i)
        h = rmsnorm(x, g_mix[i])
        proj = h @ w_in[i]
        na_q, na_k, na_v, df_q, df_k, df_v, gate_a, gate_b = jnp.split(proj, splits, axis=-1)

        na_q = rmsnorm(na_q.reshape(B, S, NA_HEADS, NA_DH), g_na_q[i])
        na_k = rmsnorm(na_k.reshape(B, S, NA_HEADS, NA_DH), g_na_k[i])
        na_v = na_v.reshape(B, S, NA_HEADS, NA_DH)
        y_a = neighbourhood_attention(na_q, na_k, na_v, na_rpb[i]) @ w_na_out[i]

        df_q = apply_rope(rmsnorm(df_q.reshape(B, S, DF_HEADS, 2, DF_DQ), g_df_q[i]), cos_b, sin_b)
        df_k = apply_rope(rmsnorm(df_k.reshape(B, S, DF_HEADS, 2, DF_DQ), g_df_k[i]), cos_b, sin_b)
        df_v = df_v.reshape(B, S, DF_HEADS, DF_DV)
        lam = (jnp.exp(jnp.sum(lam_q1[i].astype(jnp.float32) * lam_k1[i].astype(jnp.float32)))
               - jnp.exp(jnp.sum(lam_q2[i].astype(jnp.float32) * lam_k2[i].astype(jnp.float32)))
               + lam_init)
        o_b = differential_attention(df_q, df_k, df_v, lam)
        o_b = rmsnorm(o_b, g_df_sub[i]) * (1.0 - lam_init)
        y_b = o_b.reshape(B, S, DF_VW) @ w_df_out[i]

        merged = jax.nn.sigmoid(gate_a) * y_a + jax.nn.sigmoid(gate_b) * y_b
        x = x + merged @ w_o[i]

        hf = rmsnorm(x, g_ffn[i])
        x = x + (jax.nn.silu(hf @ w_gate[i]) * (hf @ w_up[i])) @ w_down[i]

        hp = rmsnorm(x, g_ple[i])
        x = x + jax.nn.sigmoid(hp @ w_ple_gate[i]) * (p[i] @ w_ple_proj[i])
    return x
```

```python
import functools
import math

import numpy as np
import jax
import jax.numpy as jnp
from jax import lax
from jax.experimental import pallas as pl
from jax.experimental.pallas import tpu as pltpu

D_MODEL = 2048
GRID_W = 64
NA_HEADS = 8
NA_DH = 128
NA_KR = 8
NA_KW = 16
DF_HEADS = 8
DF_DQ = 64
DF_DV = 128
FFN_HID = 5632
PLE_DIM = 256
ROPE_THETA = 10000.0
EPS = 1e-6
LAM_INIT = 0.8 - 0.6 * math.exp(-0.3 * 0)

NA_W = NA_HEADS * NA_DH
DF_QW = DF_HEADS * 2 * DF_DQ
DF_VW = DF_HEADS * DF_DV
IN_COLS = 3 * NA_W + 2 * DF_QW + DF_VW + 2 * D_MODEL

LANES = 128
NEG = -1e30
VMEM_LIMIT = 56 * 1024 * 1024

F32 = jnp.float32
BF16 = jnp.bfloat16

CB_NA_Q, CB_NA_K, CB_NA_V = 0, 8, 16
CB_DF_Q, CB_DF_K, CB_DF_V = 24, 32, 40
COL_GATE_A, COL_GATE_B = 6144, 8192


def _params(sem):
    return pltpu.CompilerParams(dimension_semantics=sem, vmem_limit_bytes=VMEM_LIMIT)


def _rms_rows(x, g):
    ms = jnp.mean(x * x, axis=-1, keepdims=True)
    return x * lax.rsqrt(ms + EPS) * g


NORM_ROWS = 256


def _norm_block(x_ref, g_ref, h_ref):
    def body(c, carry):
        r = pl.ds(pl.multiple_of(c * NORM_ROWS, NORM_ROWS), NORM_ROWS)
        h_ref[r, :] = _rms_rows(x_ref[r, :], g_ref[...]).astype(BF16)
        return carry

    lax.fori_loop(0, x_ref.shape[0] // NORM_ROWS, body, 0)


IN_TM, IN_TN = 1024, 512


def _in_proj_kernel(x_ref, gmix_ref, w_ref, gq_ref, gk_ref, caq_ref, sbq_ref, cak_ref, sbk_ref,
                    o_ref, h_ref):
    j = pl.program_id(1)

    @pl.when(j == 0)
    def _():
        _norm_block(x_ref, gmix_ref, h_ref)

    y = jnp.dot(h_ref[...], w_ref[...], preferred_element_type=F32)
    seg = j // (1024 // IN_TN)
    heads = IN_TN // LANES

    def na_norm(g_ref):
        for k in range(heads):
            yk = y[:, k * LANES:(k + 1) * LANES]
            o_ref[:, k * LANES:(k + 1) * LANES] = _rms_rows(yk, g_ref[...]).astype(BF16)

    def df_norm_rope(ca_ref, sb_ref):
        lane = lax.broadcasted_iota(jnp.int32, (1, LANES), 1)
        comp0 = (lane % 64) < 32
        ca = ca_ref[...]
        sb = sb_ref[...]
        for k in range(heads):
            yk = y[:, k * LANES:(k + 1) * LANES]
            pk = pltpu.roll(yk, 64, 1)
            z = yk * yk + pk * pk
            s0 = jnp.sum(jnp.where(comp0, z, 0.0), axis=-1, keepdims=True)
            s1 = jnp.sum(jnp.where(comp0, 0.0, z), axis=-1, keepdims=True)
            r = jnp.where(comp0, lax.rsqrt(s0 * (0.5 / DF_DQ) + EPS), lax.rsqrt(s1 * (0.5 / DF_DQ) + EPS))
            o_ref[:, k * LANES:(k + 1) * LANES] = ((yk * ca + pk * sb) * r).astype(BF16)

    @pl.when(seg == 0)
    def _():
        na_norm(gq_ref)

    @pl.when(seg == 1)
    def _():
        na_norm(gk_ref)

    @pl.when(seg == 3)
    def _():
        df_norm_rope(caq_ref, sbq_ref)

    @pl.when(seg == 4)
    def _():
        df_norm_rope(cak_ref, sbk_ref)

    @pl.when((seg == 2) | (seg == 5))
    def _():
        o_ref[...] = y.astype(BF16)

    @pl.when(seg >= 6)
    def _():
        o_ref[...] = jax.nn.sigmoid(y).astype(BF16)


def _in_proj(x2, g_mix, w_in, g_na_q, g_na_k, caq, sbq, cak, sbk, seq):
    n = x2.shape[0]
    pos_blocks = seq // IN_TM
    row = lambda i, j: (i, 0)
    const = lambda i, j: (0, 0)
    tab = lambda i, j: (i % pos_blocks, 0)
    return pl.pallas_call(
        _in_proj_kernel,
        out_shape=jax.ShapeDtypeStruct((n, IN_COLS), BF16),
        grid=(n // IN_TM, IN_COLS // IN_TN),
        in_specs=[
            pl.BlockSpec((IN_TM, D_MODEL), row),
            pl.BlockSpec((1, D_MODEL), const),
            pl.BlockSpec((D_MODEL, IN_TN), lambda i, j: (0, j)),
            pl.BlockSpec((1, LANES), const),
            pl.BlockSpec((1, LANES), const),
            pl.BlockSpec((IN_TM, LANES), tab),
            pl.BlockSpec((IN_TM, LANES), tab),
            pl.BlockSpec((IN_TM, LANES), tab),
            pl.BlockSpec((IN_TM, LANES), tab),
        ],
        out_specs=pl.BlockSpec((IN_TM, IN_TN), lambda i, j: (i, j)),
        scratch_shapes=[pltpu.VMEM((IN_TM, D_MODEL), BF16)],
        compiler_params=_params(("parallel", "arbitrary")),
        name="in_proj",
    )(x2, g_mix, w_in, g_na_q, g_na_k, caq, sbq, cak, sbk)


NA_QROWS = 8
NA_WROWS = 16
NA_TQ = NA_QROWS * GRID_W
NA_TK = NA_WROWS * GRID_W


def _na_kernel(q_ref, k_ref, v_ref, b_ref, o_ref, *, rows):
    t = pl.program_id(2)
    w0 = jnp.clip(t * NA_QROWS - NA_KR // 2, 0, rows - NA_WROWS)
    off = pl.multiple_of(w0 * GRID_W, 256)
    kw = k_ref[pl.ds(off, NA_TK), :]
    vw = v_ref[pl.ds(off, NA_TK), :]
    s = lax.dot_general(q_ref[...], kw, (((1,), (1,)), ((), ())), preferred_element_type=F32)
    s = s * (NA_DH ** -0.5) + b_ref[0, 0]
    m = jnp.max(s, axis=-1, keepdims=True)
    p = jnp.exp(s - m)
    l = jnp.sum(p, axis=-1, keepdims=True)
    o = jnp.dot(p.astype(BF16), vw, preferred_element_type=F32)
    o_ref[...] = (o / l).astype(BF16)


def _na_attn(proj, bias, batch, seq):
    rows = seq // GRID_W
    nt = seq // NA_TQ
    cls = lambda t: jnp.where(t == 0, 0, jnp.where(t == nt - 1, 2, 1))
    return pl.pallas_call(
        functools.partial(_na_kernel, rows=rows),
        out_shape=jax.ShapeDtypeStruct((batch * seq, NA_W), BF16),
        grid=(batch, NA_HEADS, nt),
        in_specs=[
            pl.BlockSpec((NA_TQ, LANES), lambda b, h, t: (b * nt + t, CB_NA_Q + h)),
            pl.BlockSpec((seq, LANES), lambda b, h, t: (b, CB_NA_K + h)),
            pl.BlockSpec((seq, LANES), lambda b, h, t: (b, CB_NA_V + h)),
            pl.BlockSpec((1, 1, NA_TQ, NA_TK), lambda b, h, t: (cls(t), h, 0, 0)),
        ],
        out_specs=pl.BlockSpec((NA_TQ, LANES), lambda b, h, t: (b * nt + t, h)),
        compiler_params=_params(("parallel", "parallel", "arbitrary")),
        name="na_attn",
    )(proj, proj, proj, bias)


def _na_bias(rpb, rows):
    c = np.arange(GRID_W)
    cs = np.clip(c - NA_KW // 2, 0, GRID_W - NA_KW)
    valid_c = (c[None, :] >= cs[:, None]) & (c[None, :] < cs[:, None] + NA_KW)
    pad = GRID_W
    rp = jnp.pad(rpb, ((0, 0), (0, 0), (pad, pad)), constant_values=NEG)
    toep = jnp.stack([lax.slice_in_dim(rp, pad + NA_KW - 1 - ci, pad + NA_KW - 1 - ci + GRID_W, axis=2)
                      for ci in range(GRID_W)], axis=2)
    toep = jnp.where(jnp.asarray(valid_c)[None, None], toep, NEG)
    fill = jnp.full((NA_HEADS, 1, GRID_W, GRID_W), NEG, F32)
    nt = rows // NA_QROWS
    out = []
    for t in (0, 1, nt - 1):
        w0 = int(np.clip(t * NA_QROWS - NA_KR // 2, 0, rows - NA_WROWS))
        per_a = []
        for a in range(NA_QROWS):
            r = t * NA_QROWS + a
            rs = int(np.clip(r - NA_KR // 2, 0, rows - NA_KR))
            lo = rs - w0
            dr0 = rs - r + NA_KR - 1
            band = lax.slice_in_dim(toep, dr0, dr0 + NA_KR, axis=1)
            parts = [fill] * lo + [band] + [fill] * (NA_WROWS - NA_KR - lo)
            per_a.append(jnp.concatenate(parts, axis=1))
        blk = jnp.stack(per_a, axis=1)
        out.append(blk.transpose(0, 1, 3, 2, 4).reshape(NA_HEADS, NA_TQ, NA_TK))
    return jnp.stack(out, axis=0)


DF_TQ, DF_TK = 512, 512


def _df_kernel(q_ref, k_ref, v_ref, lam_ref, gsub_ref, o_ref,
               m1_ref, l1_ref, a1_ref, m2_ref, l2_ref, a2_ref, *, seq):
    lane = lax.broadcasted_iota(jnp.int32, (1, LANES), 1)
    comp0 = (lane % 64) < 32
    q = q_ref[...]
    zero = jnp.zeros_like(q)
    q1 = jnp.where(comp0, q, zero)
    q2 = jnp.where(comp0, zero, q)
    for m_ref, l_ref, a_ref in ((m1_ref, l1_ref, a1_ref), (m2_ref, l2_ref, a2_ref)):
        m_ref[...] = jnp.full(m_ref.shape, -jnp.inf, F32)
        l_ref[...] = jnp.zeros(l_ref.shape, F32)
        a_ref[...] = jnp.zeros(a_ref.shape, F32)

    def body(c, carry):
        off = pl.multiple_of(c * DF_TK, DF_TK)
        kc = k_ref[pl.ds(off, DF_TK), :]
        vc = v_ref[pl.ds(off, DF_TK), :]
        for qm, m_ref, l_ref, a_ref in ((q1, m1_ref, l1_ref, a1_ref), (q2, m2_ref, l2_ref, a2_ref)):
            s = lax.dot_general(qm, kc, (((1,), (1,)), ((), ())), preferred_element_type=F32)
            m_prev = m_ref[...]
            m_new = jnp.maximum(m_prev, jnp.max(s, axis=-1, keepdims=True))
            alpha = jnp.exp(m_prev - m_new)
            p = jnp.exp(s - m_new)
            l_ref[...] = alpha * l_ref[...] + jnp.sum(p, axis=-1, keepdims=True)
            a_ref[...] = alpha * a_ref[...] + jnp.dot(p.astype(BF16), vc, preferred_element_type=F32)
            m_ref[...] = m_new
        return carry

    lax.fori_loop(0, seq // DF_TK, body, 0)

    lp = lam_ref[...]
    lam = (jnp.exp(jnp.sum(lp[0:1] * lp[1:2], axis=-1, keepdims=True))
           - jnp.exp(jnp.sum(lp[2:3] * lp[3:4], axis=-1, keepdims=True)) + LAM_INIT)
    o = a1_ref[...] / l1_ref[...] - lam * (a2_ref[...] / l2_ref[...])
    o_ref[...] = (_rms_rows(o, gsub_ref[...]) * (1.0 - LAM_INIT)).astype(BF16)


def _df_attn(proj, lam_p, g_sub, batch, seq):
    nq = seq // DF_TQ
    const = lambda b, h, i: (0, 0)
    return pl.pallas_call(
        functools.partial(_df_kernel, seq=seq),
        out_shape=jax.ShapeDtypeStruct((batch * seq, DF_VW), BF16),
        grid=(batch, DF_HEADS, nq),
        in_specs=[
            pl.BlockSpec((DF_TQ, LANES), lambda b, h, i: (b * nq + i, CB_DF_Q + h)),
            pl.BlockSpec((seq, LANES), lambda b, h, i: (b, CB_DF_K + h)),
            pl.BlockSpec((seq, LANES), lambda b, h, i: (b, CB_DF_V + h)),
            pl.BlockSpec((4, DF_DQ), const),
            pl.BlockSpec((1, DF_DV), const),
        ],
        out_specs=pl.BlockSpec((DF_TQ, LANES), lambda b, h, i: (b * nq + i, h)),
        scratch_shapes=[pltpu.VMEM((DF_TQ, 1), F32), pltpu.VMEM((DF_TQ, 1), F32), pltpu.VMEM((DF_TQ, DF_DV), F32),
                        pltpu.VMEM((DF_TQ, 1), F32), pltpu.VMEM((DF_TQ, 1), F32), pltpu.VMEM((DF_TQ, DF_DV), F32)],
        compiler_params=_params(("parallel", "parallel", "arbitrary")),
        name="df_attn",
    )(proj, proj, proj, lam_p, g_sub)


MG_TM, MG_TN = 1024, 512


def _merge_kernel(oa_ref, ob_ref, wa_ref, wb_ref, sa_ref, sb_ref, o_ref):
    ya = jnp.dot(oa_ref[...], wa_ref[...], preferred_element_type=F32)
    yb = jnp.dot(ob_ref[...], wb_ref[...], preferred_element_type=F32)
    o_ref[...] = (sa_ref[...].astype(F32) * ya + sb_ref[...].astype(F32) * yb).astype(BF16)


def _merge(o_a, o_b, w_na_out, w_df_out, proj):
    n = o_a.shape[0]
    ga, gb = COL_GATE_A // MG_TN, COL_GATE_B // MG_TN
    return pl.pallas_call(
        _merge_kernel,
        out_shape=jax.ShapeDtypeStruct((n, D_MODEL), BF16),
        grid=(n // MG_TM, D_MODEL // MG_TN),
        in_specs=[
            pl.BlockSpec((MG_TM, NA_W), lambda i, j: (i, 0)),
            pl.BlockSpec((MG_TM, DF_VW), lambda i, j: (i, 0)),
            pl.BlockSpec((NA_W, MG_TN), lambda i, j: (0, j)),
            pl.BlockSpec((DF_VW, MG_TN), lambda i, j: (0, j)),
            pl.BlockSpec((MG_TM, MG_TN), lambda i, j: (i, ga + j)),
            pl.BlockSpec((MG_TM, MG_TN), lambda i, j: (i, gb + j)),
        ],
        out_specs=pl.BlockSpec((MG_TM, MG_TN), lambda i, j: (i, j)),
        compiler_params=_params(("parallel", "arbitrary")),
        name="merge",
    )(o_a, o_b, w_na_out, w_df_out, proj, proj)


RS_TM, RS_TN = 1024, 512


def _residual_matmul_kernel(a_ref, w_ref, x_ref, o_ref):
    o_ref[...] = x_ref[...] + jnp.dot(a_ref[...], w_ref[...], preferred_element_type=F32)


def _residual_matmul(a, w, x, name, tm=RS_TM, tn=RS_TN):
    n, k = a.shape
    d = w.shape[1]
    return pl.pallas_call(
        _residual_matmul_kernel,
        out_shape=jax.ShapeDtypeStruct((n, d), F32),
        grid=(n // tm, d // tn),
        in_specs=[
            pl.BlockSpec((tm, k), lambda i, j: (i, 0)),
            pl.BlockSpec((k, tn), lambda i, j: (0, j)),
            pl.BlockSpec((tm, tn), lambda i, j: (i, j)),
        ],
        out_specs=pl.BlockSpec((tm, tn), lambda i, j: (i, j)),
        compiler_params=_params(("parallel", "arbitrary")),
        name=name,
    )(a, w, x)


UP_TM, UP_TN = 1024, 512


def _ffn_up_kernel(x_ref, g_ref, wg_ref, wu_ref, o_ref, h_ref):
    @pl.when(pl.program_id(1) == 0)
    def _():
        _norm_block(x_ref, g_ref, h_ref)

    h = h_ref[...]
    gate = jnp.dot(h, wg_ref[...], preferred_element_type=F32)
    up = jnp.dot(h, wu_ref[...], preferred_element_type=F32)
    o_ref[...] = (gate * jax.nn.sigmoid(gate) * up).astype(BF16)


def _ffn_up(x, g, w_gate, w_up):
    n = x.shape[0]
    return pl.pallas_call(
        _ffn_up_kernel,
        out_shape=jax.ShapeDtypeStruct((n, FFN_HID), BF16),
        grid=(n // UP_TM, FFN_HID // UP_TN),
        in_specs=[
            pl.BlockSpec((UP_TM, D_MODEL), lambda i, j: (i, 0)),
            pl.BlockSpec((1, D_MODEL), lambda i, j: (0, 0)),
            pl.BlockSpec((D_MODEL, UP_TN), lambda i, j: (0, j)),
            pl.BlockSpec((D_MODEL, UP_TN), lambda i, j: (0, j)),
        ],
        out_specs=pl.BlockSpec((UP_TM, UP_TN), lambda i, j: (i, j)),
        scratch_shapes=[pltpu.VMEM((UP_TM, D_MODEL), BF16)],
        compiler_params=_params(("parallel", "arbitrary")),
        name="ffn_up",
    )(x, g, w_gate, w_up)


PLE_TM, PLE_TN = 1024, 512


def _ple_kernel(x_ref, g_ref, wg_ref, p_ref, wp_ref, xt_ref, o_ref, h_ref):
    @pl.when(pl.program_id(1) == 0)
    def _():
        _norm_block(x_ref, g_ref, h_ref)

    gate = jnp.dot(h_ref[...], wg_ref[...], preferred_element_type=F32)
    emb = jnp.dot(p_ref[...].astype(BF16), wp_ref[...], preferred_element_type=F32)
    o_ref[...] = xt_ref[...] + jax.nn.sigmoid(gate) * emb


def _ple(x, g, w_gate, p, w_proj):
    n = x.shape[0]
    return pl.pallas_call(
        _ple_kernel,
        out_shape=jax.ShapeDtypeStruct((n, D_MODEL), F32),
        grid=(n // PLE_TM, D_MODEL // PLE_TN),
        in_specs=[
            pl.BlockSpec((PLE_TM, D_MODEL), lambda i, j: (i, 0)),
            pl.BlockSpec((1, D_MODEL), lambda i, j: (0, 0)),
            pl.BlockSpec((D_MODEL, PLE_TN), lambda i, j: (0, j)),
            pl.BlockSpec((PLE_TM, PLE_DIM), lambda i, j: (i, 0)),
            pl.BlockSpec((PLE_DIM, PLE_TN), lambda i, j: (0, j)),
            pl.BlockSpec((PLE_TM, PLE_TN), lambda i, j: (i, j)),
        ],
        out_specs=pl.BlockSpec((PLE_TM, PLE_TN), lambda i, j: (i, j)),
        scratch_shapes=[pltpu.VMEM((PLE_TM, D_MODEL), BF16)],
        compiler_params=_params(("parallel", "arbitrary")),
        name="ple",
    )(x, g, w_gate, p, w_proj, x)


def _df_cols(w):
    k = w.shape[0]
    return w.reshape(k, DF_HEADS, 2, 2, DF_DQ // 2).transpose(0, 1, 3, 2, 4).reshape(k, DF_QW)


def _rope_tables(seq, g, scale):
    half = DF_DQ // 2
    inv = 1.0 / (ROPE_THETA ** (jnp.arange(0, DF_DQ, 2, dtype=F32) / DF_DQ))
    ang = jnp.arange(seq, dtype=F32)[:, None] * inv[None, :]
    cos, sin = jnp.cos(ang), jnp.sin(ang)
    g1, g2 = g[:half], g[half:]
    ca = jnp.concatenate([cos * g1, cos * g1, cos * g2, cos * g2], axis=1) * scale
    sb = jnp.concatenate([-sin * g2, -sin * g2, sin * g1, sin * g1], axis=1) * scale
    return ca, sb


def kernel(x, p, g_mix, w_in, g_na_q, g_na_k, na_rpb, g_df_q, g_df_k, lam_q1, lam_k1, lam_q2, lam_k2,
           g_df_sub, w_na_out, w_df_out, w_o, g_ffn, w_gate, w_up, w_down, g_ple, w_ple_gate, w_ple_proj):
    batch, seq, d = x.shape
    n = batch * seq
    rows = seq // GRID_W
    depth = w_in.shape[0]
    xf = x.reshape(n, d)
    for i in range(depth):
        wi = w_in[i]
        s0, s1 = 3 * NA_W, 3 * NA_W + DF_QW
        wi = jnp.concatenate([wi[:, :s0], _df_cols(wi[:, s0:s1]), _df_cols(wi[:, s1:s1 + DF_QW]),
                              wi[:, s1 + DF_QW:]], axis=1).astype(BF16)
        caq, sbq = _rope_tables(seq, g_df_q[i], DF_DQ ** -0.5)
        cak, sbk = _rope_tables(seq, g_df_k[i], 1.0)
        proj = _in_proj(xf, g_mix[i][None], wi, g_na_q[i][None], g_na_k[i][None], caq, sbq, cak, sbk, seq)

        o_a = _na_attn(proj, _na_bias(na_rpb[i], rows), batch, seq)
        lam_p = jnp.stack([lam_q1[i], lam_k1[i], lam_q2[i], lam_k2[i]], axis=0)
        o_b = _df_attn(proj, lam_p, g_df_sub[i][None], batch, seq)

        merged = _merge(o_a, o_b, w_na_out[i].astype(BF16), w_df_out[i].astype(BF16), proj)
        xf = _residual_matmul(merged, w_o[i].astype(BF16), xf, "out_proj")

        act = _ffn_up(xf, g_ffn[i][None], w_gate[i].astype(BF16), w_up[i].astype(BF16))
        xf = _residual_matmul(act, w_down[i].astype(BF16), xf, "ffn_down", tm=512)

        xf = _ple(xf, g_ple[i][None], w_ple_gate[i].astype(BF16), p[i].reshape(n, PLE_DIM),
                  w_ple_proj[i].astype(BF16))
    return xf.reshape(batch, seq, d)
```

```python
import functools
import math

import numpy as np
import jax
import jax.numpy as jnp
from jax import lax
from jax.experimental import pallas as pl
from jax.experimental.pallas import tpu as pltpu

D_MODEL = 2048
GRID_W = 64
NA_HEADS = 8
NA_DH = 128
NA_KR = 8
NA_KW = 16
DF_HEADS = 8
DF_DQ = 64
DF_DV = 128
FFN_HID = 5632
PLE_DIM = 256
ROPE_THETA = 10000.0
EPS = 1e-6
LAM_INIT = 0.8 - 0.6 * math.exp(-0.3 * 0)

NA_W = NA_HEADS * NA_DH
DF_QW = DF_HEADS * 2 * DF_DQ
DF_VW = DF_HEADS * DF_DV
IN_COLS = 3 * NA_W + 2 * DF_QW + DF_VW + 2 * D_MODEL

LANES = 128
NEG = -1e30
VMEM_LIMIT = 56 * 1024 * 1024

F32 = jnp.float32
BF16 = jnp.bfloat16

CB_NA_Q, CB_NA_K, CB_NA_V = 0, 8, 16
CB_DF_Q, CB_DF_K, CB_DF_V = 24, 32, 40
COL_GATE_A, COL_GATE_B = 6144, 8192


def _params(sem):
    return pltpu.CompilerParams(dimension_semantics=sem, vmem_limit_bytes=VMEM_LIMIT)


def _rms_rows(x, g):
    ms = jnp.mean(x * x, axis=-1, keepdims=True)
    return x * lax.rsqrt(ms + EPS) * g


NORM_ROWS = 256


def _norm_block(x_ref, g_ref, h_ref):
    def body(c, carry):
        r = pl.ds(pl.multiple_of(c * NORM_ROWS, NORM_ROWS), NORM_ROWS)
        h_ref[r, :] = _rms_rows(x_ref[r, :], g_ref[...]).astype(BF16)
        return carry

    lax.fori_loop(0, x_ref.shape[0] // NORM_ROWS, body, 0)


IN_TM, IN_TN = 1024, 512


def _in_proj_kernel(x_ref, gmix_ref, w_ref, gq_ref, gk_ref, caq_ref, sbq_ref, cak_ref, sbk_ref,
                    o_ref, h_ref):
    j = pl.program_id(1)

    @pl.when(j == 0)
    def _():
        _norm_block(x_ref, gmix_ref, h_ref)

    y = jnp.dot(h_ref[...], w_ref[...], preferred_element_type=F32)
    seg = j // (1024 // IN_TN)
    heads = IN_TN // LANES

    def na_norm(g_ref):
        for k in range(heads):
            yk = y[:, k * LANES:(k + 1) * LANES]
            o_ref[:, k * LANES:(k + 1) * LANES] = _rms_rows(yk, g_ref[...]).astype(BF16)

    def df_norm_rope(ca_ref, sb_ref):
        lane = lax.broadcasted_iota(jnp.int32, (1, LANES), 1)
        comp0 = (lane % 64) < 32
        ca = ca_ref[...]
        sb = sb_ref[...]
        for k in range(heads):
            yk = y[:, k * LANES:(k + 1) * LANES]
            pk = pltpu.roll(yk, 64, 1)
            z = yk * yk + pk * pk
            s0 = jnp.sum(jnp.where(comp0, z, 0.0), axis=-1, keepdims=True)
            s1 = jnp.sum(jnp.where(comp0, 0.0, z), axis=-1, keepdims=True)
            r = jnp.where(comp0, lax.rsqrt(s0 * (0.5 / DF_DQ) + EPS), lax.rsqrt(s1 * (0.5 / DF_DQ) + EPS))
            o_ref[:, k * LANES:(k + 1) * LANES] = ((yk * ca + pk * sb) * r).astype(BF16)

    @pl.when(seg == 0)
    def _():
        na_norm(gq_ref)

    @pl.when(seg == 1)
    def _():
        na_norm(gk_ref)

    @pl.when(seg == 3)
    def _():
        df_norm_rope(caq_ref, sbq_ref)

    @pl.when(seg == 4)
    def _():
        df_norm_rope(cak_ref, sbk_ref)

    @pl.when((seg == 2) | (seg == 5))
    def _():
        o_ref[...] = y.astype(BF16)

    @pl.when(seg >= 6)
    def _():
        o_ref[...] = jax.nn.sigmoid(y).astype(BF16)


def _in_proj(x2, g_mix, w_in, g_na_q, g_na_k, caq, sbq, cak, sbk, seq):
    n = x2.shape[0]
    pos_blocks = seq // IN_TM
    row = lambda i, j: (i, 0)
    const = lambda i, j: (0, 0)
    tab = lambda i, j: (i % pos_blocks, 0)
    return pl.pallas_call(
        _in_proj_kernel,
        out_shape=jax.ShapeDtypeStruct((n, IN_COLS), BF16),
        grid=(n // IN_TM, IN_COLS // IN_TN),
        in_specs=[
            pl.BlockSpec((IN_TM, D_MODEL), row),
            pl.BlockSpec((1, D_MODEL), const),
            pl.BlockSpec((D_MODEL, IN_TN), lambda i, j: (0, j)),
            pl.BlockSpec((1, LANES), const),
            pl.BlockSpec((1, LANES), const),
            pl.BlockSpec((IN_TM, LANES), tab),
            pl.BlockSpec((IN_TM, LANES), tab),
            pl.BlockSpec((IN_TM, LANES), tab),
            pl.BlockSpec((IN_TM, LANES), tab),
        ],
        out_specs=pl.BlockSpec((IN_TM, IN_TN), lambda i, j: (i, j)),
        scratch_shapes=[pltpu.VMEM((IN_TM, D_MODEL), BF16)],
        compiler_params=_params(("parallel", "arbitrary")),
        name="in_proj",
    )(x2, g_mix, w_in, g_na_q, g_na_k, caq, sbq, cak, sbk)


NA_QROWS = 8
NA_WROWS = 16
NA_TQ = NA_QROWS * GRID_W
NA_TK = NA_WROWS * GRID_W


def _na_kernel(q_ref, k_ref, v_ref, b_ref, o_ref, *, rows):
    t = pl.program_id(2)
    w0 = jnp.clip(t * NA_QROWS - NA_KR // 2, 0, rows - NA_WROWS)
    off = pl.multiple_of(w0 * GRID_W, 256)
    kw = k_ref[pl.ds(off, NA_TK), :]
    vw = v_ref[pl.ds(off, NA_TK), :]
    s = lax.dot_general(q_ref[...], kw, (((1,), (1,)), ((), ())), preferred_element_type=F32)
    s = s * (NA_DH ** -0.5) + b_ref[0, 0]
    m = jnp.max(s, axis=-1, keepdims=True)
    p = jnp.exp(s - m)
    l = jnp.sum(p, axis=-1, keepdims=True)
    o = jnp.dot(p.astype(BF16), vw, preferred_element_type=F32)
    o_ref[...] = (o / l).astype(BF16)


def _na_attn(proj, bias, batch, seq):
    rows = seq // GRID_W
    nt = seq // NA_TQ
    cls = lambda t: jnp.where(t == 0, 0, jnp.where(t == nt - 1, 2, 1))
    return pl.pallas_call(
        functools.partial(_na_kernel, rows=rows),
        out_shape=jax.ShapeDtypeStruct((batch * seq, NA_W), BF16),
        grid=(batch, NA_HEADS, nt),
        in_specs=[
            pl.BlockSpec((NA_TQ, LANES), lambda b, h, t: (b * nt + t, CB_NA_Q + h)),
            pl.BlockSpec((seq, LANES), lambda b, h, t: (b, CB_NA_K + h)),
            pl.BlockSpec((seq, LANES), lambda b, h, t: (b, CB_NA_V + h)),
            pl.BlockSpec((1, 1, NA_TQ, NA_TK), lambda b, h, t: (cls(t), h, 0, 0)),
        ],
        out_specs=pl.BlockSpec((NA_TQ, LANES), lambda b, h, t: (b * nt + t, h)),
        compiler_params=_params(("parallel", "parallel", "arbitrary")),
        name="na_attn",
    )(proj, proj, proj, bias)


def _na_bias(rpb, rows):
    c = np.arange(GRID_W)
    cs = np.clip(c - NA_KW // 2, 0, GRID_W - NA_KW)
    valid_c = (c[None, :] >= cs[:, None]) & (c[None, :] < cs[:, None] + NA_KW)
    pad = GRID_W
    rp = jnp.pad(rpb, ((0, 0), (0, 0), (pad, pad)), constant_values=NEG)
    toep = jnp.stack([lax.slice_in_dim(rp, pad + NA_KW - 1 - ci, pad + NA_KW - 1 - ci + GRID_W, axis=2)
                      for ci in range(GRID_W)], axis=2)
    toep = jnp.where(jnp.asarray(valid_c)[None, None], toep, NEG)
    fill = jnp.full((NA_HEADS, 1, GRID_W, GRID_W), NEG, F32)
    nt = rows // NA_QROWS
    out = []
    for t in (0, 1, nt - 1):
        w0 = int(np.clip(t * NA_QROWS - NA_KR // 2, 0, rows - NA_WROWS))
        per_a = []
        for a in range(NA_QROWS):
            r = t * NA_QROWS + a
            rs = int(np.clip(r - NA_KR // 2, 0, rows - NA_KR))
            lo = rs - w0
            dr0 = rs - r + NA_KR - 1
            band = lax.slice_in_dim(toep, dr0, dr0 + NA_KR, axis=1)
            parts = [fill] * lo + [band] + [fill] * (NA_WROWS - NA_KR - lo)
            per_a.append(jnp.concatenate(parts, axis=1))
        blk = jnp.stack(per_a, axis=1)
        out.append(blk.transpose(0, 1, 3, 2, 4).reshape(NA_HEADS, NA_TQ, NA_TK))
    return jnp.stack(out, axis=0)


DF_TQ, DF_TK = 512, 512
DF_TG = 256


def _df_kernel(q_ref, k_ref, vt_ref, lam_ref, gsub_ref, o_ref, q12_ref, s_ref, acc_ref, *, seq):
    lane = lax.broadcasted_iota(jnp.int32, (1, LANES), 1)
    comp0 = (lane % 64) < 32
    q = q_ref[...]
    zero = jnp.zeros_like(q)
    q12_ref[0:DF_TQ, :] = jnp.where(comp0, q, zero)
    q12_ref[DF_TQ:2 * DF_TQ, :] = jnp.where(comp0, zero, q)
    acc_ref[...] = jnp.zeros(acc_ref.shape, F32)

    groups = 2 * DF_TQ // DF_TG
    n_chunks = seq // DF_TK

    def scores(kc, g):
        cols = slice(g * DF_TG, (g + 1) * DF_TG)
        return lax.dot_general(kc, q12_ref[cols, :], (((1,), (1,)), ((), ())), preferred_element_type=F32)

    def keys(c):
        return k_ref[pl.ds(pl.multiple_of(c * DF_TK, DF_TK), DF_TK), :]

    def step(c, cur, carry):
        vc = vt_ref[:, pl.ds(pl.multiple_of(c * DF_TK, DF_TK), DF_TK)]
        kn = keys(jnp.minimum(c + 1, n_chunks - 1))
        out = []
        for g in range(groups):
            m_prev, l_prev = carry[g]
            cols = slice(g * DF_TG, (g + 1) * DF_TG)
            s_ref[1 - cur, :, cols] = scores(kn, g)
            s = s_ref[cur, :, cols]
            m_new = jnp.maximum(m_prev, jnp.max(s, axis=0, keepdims=True))
            alpha = jnp.exp(m_prev - m_new)
            p = jnp.exp(s - m_new)
            l_new = alpha * l_prev + jnp.sum(p, axis=0, keepdims=True)
            pv = jnp.dot(vc, p.astype(BF16), preferred_element_type=F32)
            acc_ref[:, cols] = alpha * acc_ref[:, cols] + pv
            out.append((m_new, l_new))
        return tuple(out)

    k0 = keys(0)
    for g in range(groups):
        s_ref[0, :, g * DF_TG:(g + 1) * DF_TG] = scores(k0, g)

    def body(cc, carry):
        carry = step(2 * cc, 0, carry)
        return step(2 * cc + 1, 1, carry)

    init = tuple((jnp.full((1, DF_TG), -jnp.inf, F32), jnp.zeros((1, DF_TG), F32)) for _ in range(groups))
    fin = lax.fori_loop(0, n_chunks // 2, body, init)
    l = jnp.concatenate([f[1] for f in fin], axis=1)

    lp = lam_ref[...]
    lam = (jnp.exp(jnp.sum(lp[0:1] * lp[1:2], axis=-1, keepdims=True))
           - jnp.exp(jnp.sum(lp[2:3] * lp[3:4], axis=-1, keepdims=True)) + LAM_INIT)
    o12 = acc_ref[...] / l
    o_t = o12[:, 0:DF_TQ] - lam * o12[:, DF_TQ:2 * DF_TQ]
    o_ref[...] = (_rms_rows(o_t.T, gsub_ref[...]) * (1.0 - LAM_INIT)).astype(BF16)


def _df_attn(proj, v_t, lam_p, g_sub, batch, seq):
    nq = seq // DF_TQ
    const = lambda b, h, i: (0, 0)
    return pl.pallas_call(
        functools.partial(_df_kernel, seq=seq),
        out_shape=jax.ShapeDtypeStruct((batch * seq, DF_VW), BF16),
        grid=(batch, DF_HEADS, nq),
        in_specs=[
            pl.BlockSpec((DF_TQ, LANES), lambda b, h, i: (b * nq + i, CB_DF_Q + h)),
            pl.BlockSpec((seq, LANES), lambda b, h, i: (b, CB_DF_K + h)),
            pl.BlockSpec((DF_DV, seq), lambda b, h, i: (h, b)),
            pl.BlockSpec((4, DF_DQ), const),
            pl.BlockSpec((1, DF_DV), const),
        ],
        out_specs=pl.BlockSpec((DF_TQ, LANES), lambda b, h, i: (b * nq + i, h)),
        scratch_shapes=[pltpu.VMEM((2 * DF_TQ, LANES), BF16), pltpu.VMEM((2, DF_TK, 2 * DF_TQ), F32),
                        pltpu.VMEM((DF_DV, 2 * DF_TQ), F32)],
        compiler_params=_params(("parallel", "parallel", "arbitrary")),
        name="df_attn",
    )(proj, proj, v_t, lam_p, g_sub)


MG_TM, MG_TN = 1024, 512


def _merge_kernel(oa_ref, ob_ref, wa_ref, wb_ref, sa_ref, sb_ref, o_ref):
    ya = jnp.dot(oa_ref[...], wa_ref[...], preferred_element_type=F32)
    yb = jnp.dot(ob_ref[...], wb_ref[...], preferred_element_type=F32)
    o_ref[...] = (sa_ref[...].astype(F32) * ya + sb_ref[...].astype(F32) * yb).astype(BF16)


def _merge(o_a, o_b, w_na_out, w_df_out, proj):
    n = o_a.shape[0]
    ga, gb = COL_GATE_A // MG_TN, COL_GATE_B // MG_TN
    return pl.pallas_call(
        _merge_kernel,
        out_shape=jax.ShapeDtypeStruct((n, D_MODEL), BF16),
        grid=(n // MG_TM, D_MODEL // MG_TN),
        in_specs=[
            pl.BlockSpec((MG_TM, NA_W), lambda i, j: (i, 0)),
            pl.BlockSpec((MG_TM, DF_VW), lambda i, j: (i, 0)),
            pl.BlockSpec((NA_W, MG_TN), lambda i, j: (0, j)),
            pl.BlockSpec((DF_VW, MG_TN), lambda i, j: (0, j)),
            pl.BlockSpec((MG_TM, MG_TN), lambda i, j: (i, ga + j)),
            pl.BlockSpec((MG_TM, MG_TN), lambda i, j: (i, gb + j)),
        ],
        out_specs=pl.BlockSpec((MG_TM, MG_TN), lambda i, j: (i, j)),
        compiler_params=_params(("parallel", "arbitrary")),
        name="merge",
    )(o_a, o_b, w_na_out, w_df_out, proj, proj)


RS_TM, RS_TN = 1024, 512


def _residual_matmul_kernel(a_ref, w_ref, x_ref, o_ref):
    o_ref[...] = x_ref[...] + jnp.dot(a_ref[...], w_ref[...], preferred_element_type=F32)


def _residual_matmul(a, w, x, name, tm=RS_TM, tn=RS_TN):
    n, k = a.shape
    d = w.shape[1]
    return pl.pallas_call(
        _residual_matmul_kernel,
        out_shape=jax.ShapeDtypeStruct((n, d), F32),
        grid=(n // tm, d // tn),
        in_specs=[
            pl.BlockSpec((tm, k), lambda i, j: (i, 0)),
            pl.BlockSpec((k, tn), lambda i, j: (0, j)),
            pl.BlockSpec((tm, tn), lambda i, j: (i, j)),
        ],
        out_specs=pl.BlockSpec((tm, tn), lambda i, j: (i, j)),
        compiler_params=_params(("parallel", "arbitrary")),
        name=name,
    )(a, w, x)


UP_TM, UP_TN = 1024, 512


def _ffn_up_kernel(x_ref, g_ref, wg_ref, wu_ref, o_ref, h_ref):
    @pl.when(pl.program_id(1) == 0)
    def _():
        _norm_block(x_ref, g_ref, h_ref)

    h = h_ref[...]
    gate = jnp.dot(h, wg_ref[...], preferred_element_type=F32)
    up = jnp.dot(h, wu_ref[...], preferred_element_type=F32)
    o_ref[...] = (gate * jax.nn.sigmoid(gate) * up).astype(BF16)


def _ffn_up(x, g, w_gate, w_up):
    n = x.shape[0]
    return pl.pallas_call(
        _ffn_up_kernel,
        out_shape=jax.ShapeDtypeStruct((n, FFN_HID), BF16),
        grid=(n // UP_TM, FFN_HID // UP_TN),
        in_specs=[
            pl.BlockSpec((UP_TM, D_MODEL), lambda i, j: (i, 0)),
            pl.BlockSpec((1, D_MODEL), lambda i, j: (0, 0)),
            pl.BlockSpec((D_MODEL, UP_TN), lambda i, j: (0, j)),
            pl.BlockSpec((D_MODEL, UP_TN), lambda i, j: (0, j)),
        ],
        out_specs=pl.BlockSpec((UP_TM, UP_TN), lambda i, j: (i, j)),
        scratch_shapes=[pltpu.VMEM((UP_TM, D_MODEL), BF16)],
        compiler_params=_params(("parallel", "arbitrary")),
        name="ffn_up",
    )(x, g, w_gate, w_up)


PLE_TM, PLE_TN = 1024, 512


def _ple_kernel(x_ref, g_ref, wg_ref, p_ref, wp_ref, xt_ref, o_ref, h_ref):
    @pl.when(pl.program_id(1) == 0)
    def _():
        _norm_block(x_ref, g_ref, h_ref)

    gate = jnp.dot(h_ref[...], wg_ref[...], preferred_element_type=F32)
    emb = jnp.dot(p_ref[...].astype(BF16), wp_ref[...], preferred_element_type=F32)
    o_ref[...] = xt_ref[...] + jax.nn.sigmoid(gate) * emb


def _ple(x, g, w_gate, p, w_proj):
    n = x.shape[0]
    return pl.pallas_call(
        _ple_kernel,
        out_shape=jax.ShapeDtypeStruct((n, D_MODEL), F32),
        grid=(n // PLE_TM, D_MODEL // PLE_TN),
        in_specs=[
            pl.BlockSpec((PLE_TM, D_MODEL), lambda i, j: (i, 0)),
            pl.BlockSpec((1, D_MODEL), lambda i, j: (0, 0)),
            pl.BlockSpec((D_MODEL, PLE_TN), lambda i, j: (0, j)),
            pl.BlockSpec((PLE_TM, PLE_DIM), lambda i, j: (i, 0)),
            pl.BlockSpec((PLE_DIM, PLE_TN), lambda i, j: (0, j)),
            pl.BlockSpec((PLE_TM, PLE_TN), lambda i, j: (i, j)),
        ],
        out_specs=pl.BlockSpec((PLE_TM, PLE_TN), lambda i, j: (i, j)),
        scratch_shapes=[pltpu.VMEM((PLE_TM, D_MODEL), BF16)],
        compiler_params=_params(("parallel", "arbitrary")),
        name="ple",
    )(x, g, w_gate, p, w_proj, x)


def _df_cols(w):
    k = w.shape[0]
    return w.reshape(k, DF_HEADS, 2, 2, DF_DQ // 2).transpose(0, 1, 3, 2, 4).reshape(k, DF_QW)


def _rope_tables(seq, g, scale):
    half = DF_DQ // 2
    inv = 1.0 / (ROPE_THETA ** (jnp.arange(0, DF_DQ, 2, dtype=F32) / DF_DQ))
    ang = jnp.arange(seq, dtype=F32)[:, None] * inv[None, :]
    cos, sin = jnp.cos(ang), jnp.sin(ang)
    g1, g2 = g[:half], g[half:]
    ca = jnp.concatenate([cos * g1, cos * g1, cos * g2, cos * g2], axis=1) * scale
    sb = jnp.concatenate([-sin * g2, -sin * g2, sin * g1, sin * g1], axis=1) * scale
    return ca, sb


def kernel(x, p, g_mix, w_in, g_na_q, g_na_k, na_rpb, g_df_q, g_df_k, lam_q1, lam_k1, lam_q2, lam_k2,
           g_df_sub, w_na_out, w_df_out, w_o, g_ffn, w_gate, w_up, w_down, g_ple, w_ple_gate, w_ple_proj):
    batch, seq, d = x.shape
    n = batch * seq
    rows = seq // GRID_W
    depth = w_in.shape[0]
    xf = x.reshape(n, d)
    for i in range(depth):
        wi = w_in[i]
        s0, s1 = 3 * NA_W, 3 * NA_W + DF_QW
        wi = jnp.concatenate([wi[:, :s0], _df_cols(wi[:, s0:s1]), _df_cols(wi[:, s1:s1 + DF_QW]),
                              wi[:, s1 + DF_QW:]], axis=1).astype(BF16)
        caq, sbq = _rope_tables(seq, g_df_q[i], DF_DQ ** -0.5)
        cak, sbk = _rope_tables(seq, g_df_k[i], 1.0)
        proj = _in_proj(xf, g_mix[i][None], wi, g_na_q[i][None], g_na_k[i][None], caq, sbq, cak, sbk, seq)

        o_a = _na_attn(proj, _na_bias(na_rpb[i], rows), batch, seq)
        lam_p = jnp.stack([lam_q1[i], lam_k1[i], lam_q2[i], lam_k2[i]], axis=0)
        v_t = proj[:, CB_DF_V * LANES:(CB_DF_V + DF_HEADS) * LANES].T
        o_b = _df_attn(proj, v_t, lam_p, g_df_sub[i][None], batch, seq)

        merged = _merge(o_a, o_b, w_na_out[i].astype(BF16), w_df_out[i].astype(BF16), proj)
        xf = _residual_matmul(merged, w_o[i].astype(BF16), xf, "out_proj")

        act = _ffn_up(xf, g_ffn[i][None], w_gate[i].astype(BF16), w_up[i].astype(BF16))
        xf = _residual_matmul(act, w_down[i].astype(BF16), xf, "ffn_down", tm=512)

        xf = _ple(xf, g_ple[i][None], w_ple_gate[i].astype(BF16), p[i].reshape(n, PLE_DIM),
                  w_ple_proj[i].astype(BF16))
    return xf.reshape(batch, seq, d)
```

```python
import functools
import math

import numpy as np
import jax
import jax.numpy as jnp
from jax import lax
from jax.experimental import pallas as pl
from jax.experimental.pallas import tpu as pltpu

D_MODEL = 2048
GRID_W = 64
NA_HEADS = 8
NA_DH = 128
NA_KR = 8
NA_KW = 16
DF_HEADS = 8
DF_DQ = 64
DF_DV = 128
FFN_HID = 5632
PLE_DIM = 256
ROPE_THETA = 10000.0
EPS = 1e-6
LAM_INIT = 0.8 - 0.6 * math.exp(-0.3 * 0)

NA_W = NA_HEADS * NA_DH
DF_QW = DF_HEADS * 2 * DF_DQ
DF_VW = DF_HEADS * DF_DV
IN_COLS = 3 * NA_W + 2 * DF_QW + DF_VW + 2 * D_MODEL

LANES = 128
NEG = -1e30
VMEM_LIMIT = 56 * 1024 * 1024

F32 = jnp.float32
BF16 = jnp.bfloat16

CB_NA_Q, CB_NA_K, CB_NA_V = 0, 8, 16
CB_DF_Q, CB_DF_K, CB_DF_V = 24, 32, 40
COL_GATE_A, COL_GATE_B = 6144, 8192


def _params(sem):
    return pltpu.CompilerParams(dimension_semantics=sem, vmem_limit_bytes=VMEM_LIMIT)


def _rms_rows(x, g):
    ms = jnp.mean(x * x, axis=-1, keepdims=True)
    return x * lax.rsqrt(ms + EPS) * g


NORM_ROWS = 256


def _norm_block(x_ref, g_ref, h_ref):
    def body(c, carry):
        r = pl.ds(pl.multiple_of(c * NORM_ROWS, NORM_ROWS), NORM_ROWS)
        h_ref[r, :] = _rms_rows(x_ref[r, :], g_ref[...]).astype(BF16)
        return carry

    lax.fori_loop(0, x_ref.shape[0] // NORM_ROWS, body, 0)


IN_TM, IN_TN = 1024, 512


def _in_proj_kernel(x_ref, gmix_ref, w_ref, gq_ref, gk_ref, caq_ref, sbq_ref, cak_ref, sbk_ref,
                    o_ref, h_ref):
    j = pl.program_id(1)

    @pl.when(j == 0)
    def _():
        _norm_block(x_ref, gmix_ref, h_ref)

    y = jnp.dot(h_ref[...], w_ref[...], preferred_element_type=F32)
    seg = j // (1024 // IN_TN)
    heads = IN_TN // LANES

    def na_norm(g_ref):
        for k in range(heads):
            yk = y[:, k * LANES:(k + 1) * LANES]
            o_ref[:, k * LANES:(k + 1) * LANES] = _rms_rows(yk, g_ref[...]).astype(BF16)

    def df_norm_rope(ca_ref, sb_ref):
        lane = lax.broadcasted_iota(jnp.int32, (1, LANES), 1)
        comp0 = (lane % 64) < 32
        ca = ca_ref[...]
        sb = sb_ref[...]
        for k in range(heads):
            yk = y[:, k * LANES:(k + 1) * LANES]
            pk = pltpu.roll(yk, 64, 1)
            z = yk * yk + pk * pk
            s0 = jnp.sum(jnp.where(comp0, z, 0.0), axis=-1, keepdims=True)
            s1 = jnp.sum(jnp.where(comp0, 0.0, z), axis=-1, keepdims=True)
            r = jnp.where(comp0, lax.rsqrt(s0 * (0.5 / DF_DQ) + EPS), lax.rsqrt(s1 * (0.5 / DF_DQ) + EPS))
            o_ref[:, k * LANES:(k + 1) * LANES] = ((yk * ca + pk * sb) * r).astype(BF16)

    @pl.when(seg == 0)
    def _():
        na_norm(gq_ref)

    @pl.when(seg == 1)
    def _():
        na_norm(gk_ref)

    @pl.when(seg == 3)
    def _():
        df_norm_rope(caq_ref, sbq_ref)

    @pl.when(seg == 4)
    def _():
        df_norm_rope(cak_ref, sbk_ref)

    @pl.when((seg == 2) | (seg == 5))
    def _():
        o_ref[...] = y.astype(BF16)

    @pl.when(seg >= 6)
    def _():
        o_ref[...] = jax.nn.sigmoid(y).astype(BF16)


def _in_proj(x2, g_mix, w_in, g_na_q, g_na_k, caq, sbq, cak, sbk, seq):
    n = x2.shape[0]
    pos_blocks = seq // IN_TM
    row = lambda i, j: (i, 0)
    const = lambda i, j: (0, 0)
    tab = lambda i, j: (i % pos_blocks, 0)
    return pl.pallas_call(
        _in_proj_kernel,
        out_shape=jax.ShapeDtypeStruct((n, IN_COLS), BF16),
        grid=(n // IN_TM, IN_COLS // IN_TN),
        in_specs=[
            pl.BlockSpec((IN_TM, D_MODEL), row),
            pl.BlockSpec((1, D_MODEL), const),
            pl.BlockSpec((D_MODEL, IN_TN), lambda i, j: (0, j)),
            pl.BlockSpec((1, LANES), const),
            pl.BlockSpec((1, LANES), const),
            pl.BlockSpec((IN_TM, LANES), tab),
            pl.BlockSpec((IN_TM, LANES), tab),
            pl.BlockSpec((IN_TM, LANES), tab),
            pl.BlockSpec((IN_TM, LANES), tab),
        ],
        out_specs=pl.BlockSpec((IN_TM, IN_TN), lambda i, j: (i, j)),
        scratch_shapes=[pltpu.VMEM((IN_TM, D_MODEL), BF16)],
        compiler_params=_params(("parallel", "arbitrary")),
        name="in_proj",
    )(x2, g_mix, w_in, g_na_q, g_na_k, caq, sbq, cak, sbk)


NA_QROWS = 8
NA_WROWS = 16
NA_TQ = NA_QROWS * GRID_W
NA_TK = NA_WROWS * GRID_W


def _na_kernel(q_ref, k_ref, v_ref, b_ref, o_ref, *, rows):
    t = pl.program_id(2)
    w0 = jnp.clip(t * NA_QROWS - NA_KR // 2, 0, rows - NA_WROWS)
    off = pl.multiple_of(w0 * GRID_W, 256)
    kw = k_ref[pl.ds(off, NA_TK), :]
    vw = v_ref[pl.ds(off, NA_TK), :]
    s = lax.dot_general(q_ref[...], kw, (((1,), (1,)), ((), ())), preferred_element_type=F32)
    s = s * (NA_DH ** -0.5) + b_ref[0, 0]
    m = jnp.max(s, axis=-1, keepdims=True)
    p = jnp.exp(s - m)
    l = jnp.sum(p, axis=-1, keepdims=True)
    o = jnp.dot(p.astype(BF16), vw, preferred_element_type=F32)
    o_ref[...] = (o / l).astype(BF16)


def _na_attn(proj, bias, batch, seq):
    rows = seq // GRID_W
    nt = seq // NA_TQ
    cls = lambda t: jnp.where(t == 0, 0, jnp.where(t == nt - 1, 2, 1))
    return pl.pallas_call(
        functools.partial(_na_kernel, rows=rows),
        out_shape=jax.ShapeDtypeStruct((batch * seq, NA_W), BF16),
        grid=(batch, NA_HEADS, nt),
        in_specs=[
            pl.BlockSpec((NA_TQ, LANES), lambda b, h, t: (b * nt + t, CB_NA_Q + h)),
            pl.BlockSpec((seq, LANES), lambda b, h, t: (b, CB_NA_K + h)),
            pl.BlockSpec((seq, LANES), lambda b, h, t: (b, CB_NA_V + h)),
            pl.BlockSpec((1, 1, NA_TQ, NA_TK), lambda b, h, t: (cls(t), h, 0, 0)),
        ],
        out_specs=pl.BlockSpec((NA_TQ, LANES), lambda b, h, t: (b * nt + t, h)),
        compiler_params=_params(("parallel", "parallel", "arbitrary")),
        name="na_attn",
    )(proj, proj, proj, bias)


def _na_bias(rpb, rows):
    c = np.arange(GRID_W)
    cs = np.clip(c - NA_KW // 2, 0, GRID_W - NA_KW)
    valid_c = (c[None, :] >= cs[:, None]) & (c[None, :] < cs[:, None] + NA_KW)
    pad = GRID_W
    rp = jnp.pad(rpb, ((0, 0), (0, 0), (pad, pad)), constant_values=NEG)
    toep = jnp.stack([lax.slice_in_dim(rp, pad + NA_KW - 1 - ci, pad + NA_KW - 1 - ci + GRID_W, axis=2)
                      for ci in range(GRID_W)], axis=2)
    toep = jnp.where(jnp.asarray(valid_c)[None, None], toep, NEG)
    fill = jnp.full((NA_HEADS, 1, GRID_W, GRID_W), NEG, F32)
    nt = rows // NA_QROWS
    out = []
    for t in (0, 1, nt - 1):
        w0 = int(np.clip(t * NA_QROWS - NA_KR // 2, 0, rows - NA_WROWS))
        per_a = []
        for a in range(NA_QROWS):
            r = t * NA_QROWS + a
            rs = int(np.clip(r - NA_KR // 2, 0, rows - NA_KR))
            lo = rs - w0
            dr0 = rs - r + NA_KR - 1
            band = lax.slice_in_dim(toep, dr0, dr0 + NA_KR, axis=1)
            parts = [fill] * lo + [band] + [fill] * (NA_WROWS - NA_KR - lo)
            per_a.append(jnp.concatenate(parts, axis=1))
        blk = jnp.stack(per_a, axis=1)
        out.append(blk.transpose(0, 1, 3, 2, 4).reshape(NA_HEADS, NA_TQ, NA_TK))
    return jnp.stack(out, axis=0)


DF_TQ, DF_TK = 1024, 512
DF_TG = 256
DF_VROWS = DF_DV + 16


def _df_kernel(q_ref, k_ref, vt_ref, lam_ref, gsub_ref, o_ref, q12_ref, s_ref, acc_ref, *, seq):
    lane = lax.broadcasted_iota(jnp.int32, (1, LANES), 1)
    comp0 = (lane % 64) < 32
    q = q_ref[...]
    zero = jnp.zeros_like(q)
    q12_ref[0:DF_TQ, :] = jnp.where(comp0, q, zero)
    q12_ref[DF_TQ:2 * DF_TQ, :] = jnp.where(comp0, zero, q)
    acc_ref[...] = jnp.zeros(acc_ref.shape, F32)

    groups = 2 * DF_TQ // DF_TG
    n_chunks = seq // DF_TK

    def scores(kc, g):
        cols = slice(g * DF_TG, (g + 1) * DF_TG)
        return lax.dot_general(kc, q12_ref[cols, :], (((1,), (1,)), ((), ())), preferred_element_type=F32)

    def keys(c):
        return k_ref[c * DF_TK:(c + 1) * DF_TK, :]

    def step(c, carry):
        cur = c % 2
        vc = vt_ref[:, c * DF_TK:(c + 1) * DF_TK]
        kn = keys(c + 1) if c + 1 < n_chunks else None
        out = []
        for g in range(groups):
            m_prev = carry[g]
            cols = slice(g * DF_TG, (g + 1) * DF_TG)
            if kn is not None:
                s_ref[1 - cur, :, cols] = scores(kn, g)
            s = s_ref[cur, :, cols]
            m_new = jnp.maximum(m_prev, jnp.max(s, axis=0, keepdims=True))
            alpha = jnp.exp2(m_prev - m_new)
            p = jnp.exp2(s - m_new).astype(BF16)
            pv = jnp.dot(vc, p, preferred_element_type=F32)
            acc_ref[:, cols] = alpha * acc_ref[:, cols] + pv
            out.append(m_new)
        return tuple(out)

    k0 = keys(0)
    for g in range(groups):
        s_ref[0, :, g * DF_TG:(g + 1) * DF_TG] = scores(k0, g)

    carry = tuple(jnp.full((1, DF_TG), -jnp.inf, F32) for _ in range(groups))
    for c in range(n_chunks):
        carry = step(c, carry)

    lp = lam_ref[...]
    lam = (jnp.exp(jnp.sum(lp[0:1] * lp[1:2], axis=-1, keepdims=True))
           - jnp.exp(jnp.sum(lp[2:3] * lp[3:4], axis=-1, keepdims=True)) + LAM_INIT)
    o12 = acc_ref[0:DF_DV, :] / acc_ref[DF_DV:DF_DV + 1, :]
    o_t = o12[:, 0:DF_TQ] - lam * o12[:, DF_TQ:2 * DF_TQ]
    o_ref[...] = (_rms_rows(o_t.T, gsub_ref[...]) * (1.0 - LAM_INIT)).astype(BF16)


def _df_attn(proj, v_t, lam_p, g_sub, batch, seq):
    nq = seq // DF_TQ
    const = lambda b, h, i: (0, 0)
    return pl.pallas_call(
        functools.partial(_df_kernel, seq=seq),
        out_shape=jax.ShapeDtypeStruct((batch * seq, DF_VW), BF16),
        grid=(batch, DF_HEADS, nq),
        in_specs=[
            pl.BlockSpec((DF_TQ, LANES), lambda b, h, i: (b * nq + i, CB_DF_Q + h)),
            pl.BlockSpec((seq, LANES), lambda b, h, i: (b, CB_DF_K + h)),
            pl.BlockSpec((DF_VROWS, seq), lambda b, h, i: (h, b)),
            pl.BlockSpec((4, DF_DQ), const),
            pl.BlockSpec((1, DF_DV), const),
        ],
        out_specs=pl.BlockSpec((DF_TQ, LANES), lambda b, h, i: (b * nq + i, h)),
        scratch_shapes=[pltpu.VMEM((2 * DF_TQ, LANES), BF16), pltpu.VMEM((2, DF_TK, 2 * DF_TQ), F32),
                        pltpu.VMEM((DF_VROWS, 2 * DF_TQ), F32)],
        compiler_params=_params(("parallel", "parallel", "arbitrary")),
        name="df_attn",
    )(proj, proj, v_t, lam_p, g_sub)


MG_TM, MG_TN = 1024, 512


def _merge_kernel(oa_ref, ob_ref, wa_ref, wb_ref, sa_ref, sb_ref, o_ref):
    ya = jnp.dot(oa_ref[...], wa_ref[...], preferred_element_type=F32)
    yb = jnp.dot(ob_ref[...], wb_ref[...], preferred_element_type=F32)
    o_ref[...] = (sa_ref[...].astype(F32) * ya + sb_ref[...].astype(F32) * yb).astype(BF16)


def _merge(o_a, o_b, w_na_out, w_df_out, proj):
    n = o_a.shape[0]
    ga, gb = COL_GATE_A // MG_TN, COL_GATE_B // MG_TN
    return pl.pallas_call(
        _merge_kernel,
        out_shape=jax.ShapeDtypeStruct((n, D_MODEL), BF16),
        grid=(n // MG_TM, D_MODEL // MG_TN),
        in_specs=[
            pl.BlockSpec((MG_TM, NA_W), lambda i, j: (i, 0)),
            pl.BlockSpec((MG_TM, DF_VW), lambda i, j: (i, 0)),
            pl.BlockSpec((NA_W, MG_TN), lambda i, j: (0, j)),
            pl.BlockSpec((DF_VW, MG_TN), lambda i, j: (0, j)),
            pl.BlockSpec((MG_TM, MG_TN), lambda i, j: (i, ga + j)),
            pl.BlockSpec((MG_TM, MG_TN), lambda i, j: (i, gb + j)),
        ],
        out_specs=pl.BlockSpec((MG_TM, MG_TN), lambda i, j: (i, j)),
        compiler_params=_params(("parallel", "arbitrary")),
        name="merge",
    )(o_a, o_b, w_na_out, w_df_out, proj, proj)


RS_TM, RS_TN = 1024, 512


def _residual_matmul_kernel(a_ref, w_ref, x_ref, o_ref):
    o_ref[...] = x_ref[...] + jnp.dot(a_ref[...], w_ref[...], preferred_element_type=F32)


def _residual_matmul(a, w, x, name, tm=RS_TM, tn=RS_TN):
    n, k = a.shape
    d = w.shape[1]
    return pl.pallas_call(
        _residual_matmul_kernel,
        out_shape=jax.ShapeDtypeStruct((n, d), F32),
        grid=(n // tm, d // tn),
        in_specs=[
            pl.BlockSpec((tm, k), lambda i, j: (i, 0)),
            pl.BlockSpec((k, tn), lambda i, j: (0, j)),
            pl.BlockSpec((tm, tn), lambda i, j: (i, j)),
        ],
        out_specs=pl.BlockSpec((tm, tn), lambda i, j: (i, j)),
        compiler_params=_params(("parallel", "arbitrary")),
        name=name,
    )(a, w, x)


UP_TM, UP_TN = 1024, 512


def _ffn_up_kernel(x_ref, g_ref, wg_ref, wu_ref, o_ref, h_ref):
    @pl.when(pl.program_id(1) == 0)
    def _():
        _norm_block(x_ref, g_ref, h_ref)

    h = h_ref[...]
    gate = jnp.dot(h, wg_ref[...], preferred_element_type=F32)
    up = jnp.dot(h, wu_ref[...], preferred_element_type=F32)
    o_ref[...] = (gate * jax.nn.sigmoid(gate) * up).astype(BF16)


def _ffn_up(x, g, w_gate, w_up):
    n = x.shape[0]
    return pl.pallas_call(
        _ffn_up_kernel,
        out_shape=jax.ShapeDtypeStruct((n, FFN_HID), BF16),
        grid=(n // UP_TM, FFN_HID // UP_TN),
        in_specs=[
            pl.BlockSpec((UP_TM, D_MODEL), lambda i, j: (i, 0)),
            pl.BlockSpec((1, D_MODEL), lambda i, j: (0, 0)),
            pl.BlockSpec((D_MODEL, UP_TN), lambda i, j: (0, j)),
            pl.BlockSpec((D_MODEL, UP_TN), lambda i, j: (0, j)),
        ],
        out_specs=pl.BlockSpec((UP_TM, UP_TN), lambda i, j: (i, j)),
        scratch_shapes=[pltpu.VMEM((UP_TM, D_MODEL), BF16)],
        compiler_params=_params(("parallel", "arbitrary")),
        name="ffn_up",
    )(x, g, w_gate, w_up)


PLE_TM, PLE_TN = 1024, 512


def _ple_kernel(x_ref, g_ref, wg_ref, p_ref, wp_ref, xt_ref, o_ref, h_ref):
    @pl.when(pl.program_id(1) == 0)
    def _():
        _norm_block(x_ref, g_ref, h_ref)

    gate = jnp.dot(h_ref[...], wg_ref[...], preferred_element_type=F32)
    emb = jnp.dot(p_ref[...].astype(BF16), wp_ref[...], preferred_element_type=F32)
    o_ref[...] = xt_ref[...] + jax.nn.sigmoid(gate) * emb


def _ple(x, g, w_gate, p, w_proj):
    n = x.shape[0]
    return pl.pallas_call(
        _ple_kernel,
        out_shape=jax.ShapeDtypeStruct((n, D_MODEL), F32),
        grid=(n // PLE_TM, D_MODEL // PLE_TN),
        in_specs=[
            pl.BlockSpec((PLE_TM, D_MODEL), lambda i, j: (i, 0)),
            pl.BlockSpec((1, D_MODEL), lambda i, j: (0, 0)),
            pl.BlockSpec((D_MODEL, PLE_TN), lambda i, j: (0, j)),
            pl.BlockSpec((PLE_TM, PLE_DIM), lambda i, j: (i, 0)),
            pl.BlockSpec((PLE_DIM, PLE_TN), lambda i, j: (0, j)),
            pl.BlockSpec((PLE_TM, PLE_TN), lambda i, j: (i, j)),
        ],
        out_specs=pl.BlockSpec((PLE_TM, PLE_TN), lambda i, j: (i, j)),
        scratch_shapes=[pltpu.VMEM((PLE_TM, D_MODEL), BF16)],
        compiler_params=_params(("parallel", "arbitrary")),
        name="ple",
    )(x, g, w_gate, p, w_proj, x)


def _df_cols(w):
    k = w.shape[0]
    return w.reshape(k, DF_HEADS, 2, 2, DF_DQ // 2).transpose(0, 1, 3, 2, 4).reshape(k, DF_QW)


def _rope_tables(seq, g, scale):
    half = DF_DQ // 2
    inv = 1.0 / (ROPE_THETA ** (jnp.arange(0, DF_DQ, 2, dtype=F32) / DF_DQ))
    ang = jnp.arange(seq, dtype=F32)[:, None] * inv[None, :]
    cos, sin = jnp.cos(ang), jnp.sin(ang)
    g1, g2 = g[:half], g[half:]
    ca = jnp.concatenate([cos * g1, cos * g1, cos * g2, cos * g2], axis=1) * scale
    sb = jnp.concatenate([-sin * g2, -sin * g2, sin * g1, sin * g1], axis=1) * scale
    return ca, sb


def kernel(x, p, g_mix, w_in, g_na_q, g_na_k, na_rpb, g_df_q, g_df_k, lam_q1, lam_k1, lam_q2, lam_k2,
           g_df_sub, w_na_out, w_df_out, w_o, g_ffn, w_gate, w_up, w_down, g_ple, w_ple_gate, w_ple_proj):
    batch, seq, d = x.shape
    n = batch * seq
    rows = seq // GRID_W
    depth = w_in.shape[0]
    xf = x.reshape(n, d)
    for i in range(depth):
        wi = w_in[i]
        s0, s1 = 3 * NA_W, 3 * NA_W + DF_QW
        wi = jnp.concatenate([wi[:, :s0], _df_cols(wi[:, s0:s1]), _df_cols(wi[:, s1:s1 + DF_QW]),
                              wi[:, s1 + DF_QW:]], axis=1).astype(BF16)
        caq, sbq = _rope_tables(seq, g_df_q[i], DF_DQ ** -0.5 * math.log2(math.e))
        cak, sbk = _rope_tables(seq, g_df_k[i], 1.0)
        proj = _in_proj(xf, g_mix[i][None], wi, g_na_q[i][None], g_na_k[i][None], caq, sbq, cak, sbk, seq)

        o_a = _na_attn(proj, _na_bias(na_rpb[i], rows), batch, seq)
        lam_p = jnp.stack([lam_q1[i], lam_k1[i], lam_q2[i], lam_k2[i]], axis=0)
        v_t = proj[:, CB_DF_V * LANES:(CB_DF_V + DF_HEADS) * LANES].T.reshape(DF_HEADS, DF_DV, n)
        ones = jnp.zeros((DF_HEADS, DF_VROWS - DF_DV, n), BF16).at[:, 0].set(1.0)
        v_t = jnp.concatenate([v_t, ones], axis=1).reshape(DF_HEADS * DF_VROWS, n)
        o_b = _df_attn(proj, v_t, lam_p, g_df_sub[i][None], batch, seq)

        merged = _merge(o_a, o_b, w_na_out[i].astype(BF16), w_df_out[i].astype(BF16), proj)
        xf = _residual_matmul(merged, w_o[i].astype(BF16), xf, "out_proj")

        act = _ffn_up(xf, g_ffn[i][None], w_gate[i].astype(BF16), w_up[i].astype(BF16))
        xf = _residual_matmul(act, w_down[i].astype(BF16), xf, "ffn_down", tm=512)

        xf = _ple(xf, g_ple[i][None], w_ple_gate[i].astype(BF16), p[i].reshape(n, PLE_DIM),
                  w_ple_proj[i].astype(BF16))
    return xf.reshape(batch, seq, d)
```

```python
import functools
import math

import numpy as np
import jax
import jax.numpy as jnp
from jax import lax
from jax.experimental import pallas as pl
from jax.experimental.pallas import tpu as pltpu

D_MODEL = 2048
GRID_W = 64
NA_HEADS = 8
NA_DH = 128
NA_KR = 8
NA_KW = 16
DF_HEADS = 8
DF_DQ = 64
DF_DV = 128
FFN_HID = 5632
PLE_DIM = 256
ROPE_THETA = 10000.0
EPS = 1e-6
LAM_INIT = 0.8 - 0.6 * math.exp(-0.3 * 0)

NA_W = NA_HEADS * NA_DH
DF_QW = DF_HEADS * 2 * DF_DQ
DF_VW = DF_HEADS * DF_DV
IN_COLS = 3 * NA_W + 2 * DF_QW + DF_VW + 2 * D_MODEL

LANES = 128
NEG = -1e30
VMEM_LIMIT = 56 * 1024 * 1024

F32 = jnp.float32
BF16 = jnp.bfloat16

CB_NA_Q, CB_NA_K, CB_NA_V = 0, 8, 16
CB_DF_Q, CB_DF_K, CB_DF_V = 24, 32, 40
COL_GATE_A, COL_GATE_B = 6144, 8192


def _params(sem):
    return pltpu.CompilerParams(dimension_semantics=sem, vmem_limit_bytes=VMEM_LIMIT)


def _rms_rows(x, g):
    ms = jnp.mean(x * x, axis=-1, keepdims=True)
    return x * lax.rsqrt(ms + EPS) * g


NORM_ROWS = 256


def _norm_block(x_ref, g_ref, h_ref):
    def body(c, carry):
        r = pl.ds(pl.multiple_of(c * NORM_ROWS, NORM_ROWS), NORM_ROWS)
        h_ref[r, :] = _rms_rows(x_ref[r, :], g_ref[...]).astype(BF16)
        return carry

    lax.fori_loop(0, x_ref.shape[0] // NORM_ROWS, body, 0)


IN_TM, IN_TN = 1024, 512
IN_RB = 256
SEG_TILES = 1024 // IN_TN
DF_TILE0 = 3 * SEG_TILES
DF_TILES = 2 * SEG_TILES


def _in_proj_kernel(x_ref, gmix_ref, w_ref, wdf_ref, gq_ref, gk_ref, caq_ref, sbq_ref, cak_ref, sbk_ref,
                    o_ref, h_ref):
    j = pl.program_id(1)

    @pl.when(j == 0)
    def _():
        _norm_block(x_ref, gmix_ref, h_ref)

    seg = j // SEG_TILES
    heads = IN_TN // LANES

    def run(wsrc_ref, epilogue):
        wb = wsrc_ref[...].astype(BF16)
        for r in range(IN_TM // IN_RB):
            rows = slice(r * IN_RB, (r + 1) * IN_RB)
            epilogue(jnp.dot(h_ref[rows, :], wb, preferred_element_type=F32), rows)

    def na_norm(g_ref):
        def epilogue(y, rows):
            for k in range(heads):
                cols = slice(k * LANES, (k + 1) * LANES)
                o_ref[rows, cols] = _rms_rows(y[:, cols], g_ref[...]).astype(BF16)
        return epilogue

    def df_norm_rope(ca_ref, sb_ref):
        def epilogue(y, rows):
            lane = lax.broadcasted_iota(jnp.int32, (1, LANES), 1)
            comp0 = (lane % 64) < 32
            ca = ca_ref[rows, :]
            sb = sb_ref[rows, :]
            for k in range(heads):
                cols = slice(k * LANES, (k + 1) * LANES)
                yk = y[:, cols]
                pk = pltpu.roll(yk, 64, 1)
                z = yk * yk + pk * pk
                s0 = jnp.sum(jnp.where(comp0, z, 0.0), axis=-1, keepdims=True)
                s1 = jnp.sum(jnp.where(comp0, 0.0, z), axis=-1, keepdims=True)
                r = jnp.where(comp0, lax.rsqrt(s0 * (0.5 / DF_DQ) + EPS), lax.rsqrt(s1 * (0.5 / DF_DQ) + EPS))
                o_ref[rows, cols] = ((yk * ca + pk * sb) * r).astype(BF16)
        return epilogue

    def plain(y, rows):
        o_ref[rows, :] = y.astype(BF16)

    def gate(y, rows):
        o_ref[rows, :] = jax.nn.sigmoid(y).astype(BF16)

    @pl.when(seg == 0)
    def _():
        run(w_ref, na_norm(gq_ref))

    @pl.when(seg == 1)
    def _():
        run(w_ref, na_norm(gk_ref))

    @pl.when(seg == 3)
    def _():
        run(wdf_ref, df_norm_rope(caq_ref, sbq_ref))

    @pl.when(seg == 4)
    def _():
        run(wdf_ref, df_norm_rope(cak_ref, sbk_ref))

    @pl.when((seg == 2) | (seg == 5))
    def _():
        run(w_ref, plain)

    @pl.when(seg >= 6)
    def _():
        run(w_ref, gate)


def _in_proj(x2, g_mix, w_in, w_dfqk, g_na_q, g_na_k, caq, sbq, cak, sbk, seq):
    n = x2.shape[0]
    pos_blocks = seq // IN_TM
    row = lambda i, j: (i, 0)
    const = lambda i, j: (0, 0)
    tab = lambda i, j: (i % pos_blocks, 0)
    is_df = lambda j: (j >= DF_TILE0) & (j < DF_TILE0 + DF_TILES)
    return pl.pallas_call(
        _in_proj_kernel,
        out_shape=jax.ShapeDtypeStruct((n, IN_COLS), BF16),
        grid=(n // IN_TM, IN_COLS // IN_TN),
        in_specs=[
            pl.BlockSpec((IN_TM, D_MODEL), row),
            pl.BlockSpec((1, D_MODEL), const),
            pl.BlockSpec((D_MODEL, IN_TN), lambda i, j: (0, jnp.where(is_df(j), DF_TILE0 - 1, j))),
            pl.BlockSpec((D_MODEL, IN_TN), lambda i, j: (0, jnp.clip(j - DF_TILE0, 0, DF_TILES - 1))),
            pl.BlockSpec((1, LANES), const),
            pl.BlockSpec((1, LANES), const),
            pl.BlockSpec((IN_TM, LANES), tab),
            pl.BlockSpec((IN_TM, LANES), tab),
            pl.BlockSpec((IN_TM, LANES), tab),
            pl.BlockSpec((IN_TM, LANES), tab),
        ],
        out_specs=pl.BlockSpec((IN_TM, IN_TN), lambda i, j: (i, j)),
        scratch_shapes=[pltpu.VMEM((IN_TM, D_MODEL), BF16)],
        compiler_params=_params(("parallel", "arbitrary")),
        name="in_proj",
    )(x2, g_mix, w_in, w_dfqk, g_na_q, g_na_k, caq, sbq, cak, sbk)


NA_QROWS = 8
NA_WROWS = 16
NA_TQ = NA_QROWS * GRID_W
NA_TK = NA_WROWS * GRID_W


def _na_kernel(q_ref, k_ref, v_ref, b_ref, o_ref, *, rows):
    t = pl.program_id(2)
    w0 = jnp.clip(t * NA_QROWS - NA_KR // 2, 0, rows - NA_WROWS)
    off = pl.multiple_of(w0 * GRID_W, 256)
    kw = k_ref[pl.ds(off, NA_TK), :]
    vw = v_ref[pl.ds(off, NA_TK), :]
    s = lax.dot_general(q_ref[...], kw, (((1,), (1,)), ((), ())), preferred_element_type=F32)
    s = s * (NA_DH ** -0.5) + b_ref[0, 0]
    m = jnp.max(s, axis=-1, keepdims=True)
    p = jnp.exp(s - m)
    l = jnp.sum(p, axis=-1, keepdims=True)
    o = jnp.dot(p.astype(BF16), vw, preferred_element_type=F32)
    o_ref[...] = (o / l).astype(BF16)


def _na_attn(proj, bias, batch, seq):
    rows = seq // GRID_W
    nt = seq // NA_TQ
    cls = lambda t: jnp.where(t == 0, 0, jnp.where(t == nt - 1, 2, 1))
    return pl.pallas_call(
        functools.partial(_na_kernel, rows=rows),
        out_shape=jax.ShapeDtypeStruct((batch * seq, NA_W), BF16),
        grid=(batch, NA_HEADS, nt),
        in_specs=[
            pl.BlockSpec((NA_TQ, LANES), lambda b, h, t: (b * nt + t, CB_NA_Q + h)),
            pl.BlockSpec((seq, LANES), lambda b, h, t: (b, CB_NA_K + h)),
            pl.BlockSpec((seq, LANES), lambda b, h, t: (b, CB_NA_V + h)),
            pl.BlockSpec((1, 1, NA_TQ, NA_TK), lambda b, h, t: (cls(t), h, 0, 0)),
        ],
        out_specs=pl.BlockSpec((NA_TQ, LANES), lambda b, h, t: (b * nt + t, h)),
        compiler_params=_params(("parallel", "parallel", "arbitrary")),
        name="na_attn",
    )(proj, proj, proj, bias)


def _na_bias(rpb, rows):
    c = np.arange(GRID_W)
    cs = np.clip(c - NA_KW // 2, 0, GRID_W - NA_KW)
    valid_c = (c[None, :] >= cs[:, None]) & (c[None, :] < cs[:, None] + NA_KW)
    pad = GRID_W
    rp = jnp.pad(rpb, ((0, 0), (0, 0), (pad, pad)), constant_values=NEG)
    toep = jnp.stack([lax.slice_in_dim(rp, pad + NA_KW - 1 - ci, pad + NA_KW - 1 - ci + GRID_W, axis=2)
                      for ci in range(GRID_W)], axis=2)
    toep = jnp.where(jnp.asarray(valid_c)[None, None], toep, NEG)
    fill = jnp.full((NA_HEADS, 1, GRID_W, GRID_W), NEG, F32)
    nt = rows // NA_QROWS
    out = []
    for t in (0, 1, nt - 1):
        w0 = int(np.clip(t * NA_QROWS - NA_KR // 2, 0, rows - NA_WROWS))
        per_a = []
        for a in range(NA_QROWS):
            r = t * NA_QROWS + a
            rs = int(np.clip(r - NA_KR // 2, 0, rows - NA_KR))
            lo = rs - w0
            dr0 = rs - r + NA_KR - 1
            band = lax.slice_in_dim(toep, dr0, dr0 + NA_KR, axis=1)
            parts = [fill] * lo + [band] + [fill] * (NA_WROWS - NA_KR - lo)
            per_a.append(jnp.concatenate(parts, axis=1))
        blk = jnp.stack(per_a, axis=1)
        out.append(blk.transpose(0, 1, 3, 2, 4).reshape(NA_HEADS, NA_TQ, NA_TK))
    return jnp.stack(out, axis=0)


DF_TQ, DF_TK = 1024, 512
DF_TG = 256
DF_VROWS = DF_DV + 16


def _df_kernel(q_ref, k_ref, vt_ref, lam_ref, gsub_ref, o_ref, q12_ref, s_ref, acc_ref, *, seq):
    lane = lax.broadcasted_iota(jnp.int32, (1, LANES), 1)
    comp0 = (lane % 64) < 32
    q = q_ref[...]
    zero = jnp.zeros_like(q)
    q12_ref[0:DF_TQ, :] = jnp.where(comp0, q, zero)
    q12_ref[DF_TQ:2 * DF_TQ, :] = jnp.where(comp0, zero, q)
    acc_ref[...] = jnp.zeros(acc_ref.shape, F32)

    groups = 2 * DF_TQ // DF_TG
    n_chunks = seq // DF_TK

    def scores(kc, g):
        cols = slice(g * DF_TG, (g + 1) * DF_TG)
        return lax.dot_general(kc, q12_ref[cols, :], (((1,), (1,)), ((), ())), preferred_element_type=F32)

    def keys(c):
        return k_ref[c * DF_TK:(c + 1) * DF_TK, :]

    def step(c, carry):
        cur = c % 2
        vc = vt_ref[:, c * DF_TK:(c + 1) * DF_TK]
        kn = keys(c + 1) if c + 1 < n_chunks else None
        out = []
        for g in range(groups):
            m_prev = carry[g]
            cols = slice(g * DF_TG, (g + 1) * DF_TG)
            if kn is not None:
                s_ref[1 - cur, :, cols] = scores(kn, g)
            s = s_ref[cur, :, cols]
            m_new = jnp.maximum(m_prev, jnp.max(s, axis=0, keepdims=True))
            alpha = jnp.exp2(m_prev - m_new)
            p = jnp.exp2(s - m_new).astype(BF16)
            pv = jnp.dot(vc, p, preferred_element_type=F32)
            acc_ref[:, cols] = alpha * acc_ref[:, cols] + pv
            out.append(m_new)
        return tuple(out)

    k0 = keys(0)
    for g in range(groups):
        s_ref[0, :, g * DF_TG:(g + 1) * DF_TG] = scores(k0, g)

    carry = tuple(jnp.full((1, DF_TG), -jnp.inf, F32) for _ in range(groups))
    for c in range(n_chunks):
        carry = step(c, carry)

    lp = lam_ref[...]
    lam = (jnp.exp(jnp.sum(lp[0:1] * lp[1:2], axis=-1, keepdims=True))
           - jnp.exp(jnp.sum(lp[2:3] * lp[3:4], axis=-1, keepdims=True)) + LAM_INIT)
    o12 = acc_ref[0:DF_DV, :] / acc_ref[DF_DV:DF_DV + 1, :]
    o_t = o12[:, 0:DF_TQ] - lam * o12[:, DF_TQ:2 * DF_TQ]
    o_ref[...] = (_rms_rows(o_t.T, gsub_ref[...]) * (1.0 - LAM_INIT)).astype(BF16)


def _df_attn(proj, v_t, lam_p, g_sub, batch, seq):
    nq = seq // DF_TQ
    const = lambda b, h, i: (0, 0)
    return pl.pallas_call(
        functools.partial(_df_kernel, seq=seq),
        out_shape=jax.ShapeDtypeStruct((batch * seq, DF_VW), BF16),
        grid=(batch, DF_HEADS, nq),
        in_specs=[
            pl.BlockSpec((DF_TQ, LANES), lambda b, h, i: (b * nq + i, CB_DF_Q + h)),
            pl.BlockSpec((seq, LANES), lambda b, h, i: (b, CB_DF_K + h)),
            pl.BlockSpec((DF_VROWS, seq), lambda b, h, i: (h, b)),
            pl.BlockSpec((4, DF_DQ), const),
            pl.BlockSpec((1, DF_DV), const),
        ],
        out_specs=pl.BlockSpec((DF_TQ, LANES), lambda b, h, i: (b * nq + i, h)),
        scratch_shapes=[pltpu.VMEM((2 * DF_TQ, LANES), BF16), pltpu.VMEM((2, DF_TK, 2 * DF_TQ), F32),
                        pltpu.VMEM((DF_VROWS, 2 * DF_TQ), F32)],
        compiler_params=_params(("parallel", "parallel", "arbitrary")),
        name="df_attn",
    )(proj, proj, v_t, lam_p, g_sub)


MG_TM, MG_TN = 1024, 512


def _merge_kernel(oa_ref, ob_ref, wa_ref, wb_ref, sa_ref, sb_ref, o_ref):
    ya = jnp.dot(oa_ref[...], wa_ref[...].astype(BF16), preferred_element_type=F32)
    yb = jnp.dot(ob_ref[...], wb_ref[...].astype(BF16), preferred_element_type=F32)
    o_ref[...] = (sa_ref[...].astype(F32) * ya + sb_ref[...].astype(F32) * yb).astype(BF16)


def _merge(o_a, o_b, w_na_out, w_df_out, proj):
    n = o_a.shape[0]
    ga, gb = COL_GATE_A // MG_TN, COL_GATE_B // MG_TN
    return pl.pallas_call(
        _merge_kernel,
        out_shape=jax.ShapeDtypeStruct((n, D_MODEL), BF16),
        grid=(n // MG_TM, D_MODEL // MG_TN),
        in_specs=[
            pl.BlockSpec((MG_TM, NA_W), lambda i, j: (i, 0)),
            pl.BlockSpec((MG_TM, DF_VW), lambda i, j: (i, 0)),
            pl.BlockSpec((NA_W, MG_TN), lambda i, j: (0, j)),
            pl.BlockSpec((DF_VW, MG_TN), lambda i, j: (0, j)),
            pl.BlockSpec((MG_TM, MG_TN), lambda i, j: (i, ga + j)),
            pl.BlockSpec((MG_TM, MG_TN), lambda i, j: (i, gb + j)),
        ],
        out_specs=pl.BlockSpec((MG_TM, MG_TN), lambda i, j: (i, j)),
        compiler_params=_params(("parallel", "arbitrary")),
        name="merge",
    )(o_a, o_b, w_na_out, w_df_out, proj, proj)


RS_TM, RS_TN = 1024, 512


def _residual_matmul_kernel(a_ref, w_ref, x_ref, o_ref):
    o_ref[...] = x_ref[...] + jnp.dot(a_ref[...], w_ref[...].astype(BF16), preferred_element_type=F32)


def _residual_matmul(a, w, x, name, tm=RS_TM, tn=RS_TN):
    n, k = a.shape
    d = w.shape[1]
    return pl.pallas_call(
        _residual_matmul_kernel,
        out_shape=jax.ShapeDtypeStruct((n, d), F32),
        grid=(n // tm, d // tn),
        in_specs=[
            pl.BlockSpec((tm, k), lambda i, j: (i, 0)),
            pl.BlockSpec((k, tn), lambda i, j: (0, j)),
            pl.BlockSpec((tm, tn), lambda i, j: (i, j)),
        ],
        out_specs=pl.BlockSpec((tm, tn), lambda i, j: (i, j)),
        compiler_params=_params(("parallel", "arbitrary")),
        name=name,
    )(a, w, x)


UP_TM, UP_TN = 1024, 512


def _ffn_up_kernel(x_ref, g_ref, wg_ref, wu_ref, o_ref, h_ref):
    @pl.when(pl.program_id(1) == 0)
    def _():
        _norm_block(x_ref, g_ref, h_ref)

    h = h_ref[...]
    gate = jnp.dot(h, wg_ref[...].astype(BF16), preferred_element_type=F32)
    up = jnp.dot(h, wu_ref[...].astype(BF16), preferred_element_type=F32)
    o_ref[...] = (gate * jax.nn.sigmoid(gate) * up).astype(BF16)


def _ffn_up(x, g, w_gate, w_up):
    n = x.shape[0]
    return pl.pallas_call(
        _ffn_up_kernel,
        out_shape=jax.ShapeDtypeStruct((n, FFN_HID), BF16),
        grid=(n // UP_TM, FFN_HID // UP_TN),
        in_specs=[
            pl.BlockSpec((UP_TM, D_MODEL), lambda i, j: (i, 0)),
            pl.BlockSpec((1, D_MODEL), lambda i, j: (0, 0)),
            pl.BlockSpec((D_MODEL, UP_TN), lambda i, j: (0, j)),
            pl.BlockSpec((D_MODEL, UP_TN), lambda i, j: (0, j)),
        ],
        out_specs=pl.BlockSpec((UP_TM, UP_TN), lambda i, j: (i, j)),
        scratch_shapes=[pltpu.VMEM((UP_TM, D_MODEL), BF16)],
        compiler_params=_params(("parallel", "arbitrary")),
        name="ffn_up",
    )(x, g, w_gate, w_up)


PLE_TM, PLE_TN = 1024, 512


def _ple_kernel(x_ref, g_ref, wg_ref, p_ref, wp_ref, xt_ref, o_ref, h_ref):
    @pl.when(pl.program_id(1) == 0)
    def _():
        _norm_block(x_ref, g_ref, h_ref)

    gate = jnp.dot(h_ref[...], wg_ref[...].astype(BF16), preferred_element_type=F32)
    emb = jnp.dot(p_ref[...].astype(BF16), wp_ref[...].astype(BF16), preferred_element_type=F32)
    o_ref[...] = xt_ref[...] + jax.nn.sigmoid(gate) * emb


def _ple(x, g, w_gate, p, w_proj):
    n = x.shape[0]
    return pl.pallas_call(
        _ple_kernel,
        out_shape=jax.ShapeDtypeStruct((n, D_MODEL), F32),
        grid=(n // PLE_TM, D_MODEL // PLE_TN),
        in_specs=[
            pl.BlockSpec((PLE_TM, D_MODEL), lambda i, j: (i, 0)),
            pl.BlockSpec((1, D_MODEL), lambda i, j: (0, 0)),
            pl.BlockSpec((D_MODEL, PLE_TN), lambda i, j: (0, j)),
            pl.BlockSpec((PLE_TM, PLE_DIM), lambda i, j: (i, 0)),
            pl.BlockSpec((PLE_DIM, PLE_TN), lambda i, j: (0, j)),
            pl.BlockSpec((PLE_TM, PLE_TN), lambda i, j: (i, j)),
        ],
        out_specs=pl.BlockSpec((PLE_TM, PLE_TN), lambda i, j: (i, j)),
        scratch_shapes=[pltpu.VMEM((PLE_TM, D_MODEL), BF16)],
        compiler_params=_params(("parallel", "arbitrary")),
        name="ple",
    )(x, g, w_gate, p, w_proj, x)


def _df_cols(w):
    k = w.shape[0]
    return w.reshape(k, DF_HEADS, 2, 2, DF_DQ // 2).transpose(0, 1, 3, 2, 4).reshape(k, DF_QW)


def _rope_tables(seq, g, scale):
    half = DF_DQ // 2
    inv = 1.0 / (ROPE_THETA ** (jnp.arange(0, DF_DQ, 2, dtype=F32) / DF_DQ))
    ang = jnp.arange(seq, dtype=F32)[:, None] * inv[None, :]
    cos, sin = jnp.cos(ang), jnp.sin(ang)
    g1, g2 = g[:half], g[half:]
    ca = jnp.concatenate([cos * g1, cos * g1, cos * g2, cos * g2], axis=1) * scale
    sb = jnp.concatenate([-sin * g2, -sin * g2, sin * g1, sin * g1], axis=1) * scale
    return ca, sb


def kernel(x, p, g_mix, w_in, g_na_q, g_na_k, na_rpb, g_df_q, g_df_k, lam_q1, lam_k1, lam_q2, lam_k2,
           g_df_sub, w_na_out, w_df_out, w_o, g_ffn, w_gate, w_up, w_down, g_ple, w_ple_gate, w_ple_proj):
    batch, seq, d = x.shape
    n = batch * seq
    rows = seq // GRID_W
    depth = w_in.shape[0]
    xf = x.reshape(n, d)
    for i in range(depth):
        s0 = 3 * NA_W
        w_dfqk = jnp.concatenate([_df_cols(w_in[i][:, s0:s0 + DF_QW]), _df_cols(w_in[i][:, s0 + DF_QW:s0 + 2 * DF_QW])],
                                 axis=1)
        caq, sbq = _rope_tables(seq, g_df_q[i], DF_DQ ** -0.5 * math.log2(math.e))
        cak, sbk = _rope_tables(seq, g_df_k[i], 1.0)
        proj = _in_proj(xf, g_mix[i][None], w_in[i], w_dfqk, g_na_q[i][None], g_na_k[i][None], caq, sbq, cak, sbk,
                        seq)

        o_a = _na_attn(proj, _na_bias(na_rpb[i], rows), batch, seq)
        lam_p = jnp.stack([lam_q1[i], lam_k1[i], lam_q2[i], lam_k2[i]], axis=0)
        v_t = proj[:, CB_DF_V * LANES:(CB_DF_V + DF_HEADS) * LANES].T.reshape(DF_HEADS, DF_DV, n)
        ones = jnp.zeros((DF_HEADS, DF_VROWS - DF_DV, n), BF16).at[:, 0].set(1.0)
        v_t = jnp.concatenate([v_t, ones], axis=1).reshape(DF_HEADS * DF_VROWS, n)
        o_b = _df_attn(proj, v_t, lam_p, g_df_sub[i][None], batch, seq)

        merged = _merge(o_a, o_b, w_na_out[i], w_df_out[i], proj)
        xf = _residual_matmul(merged, w_o[i], xf, "out_proj")

        act = _ffn_up(xf, g_ffn[i][None], w_gate[i], w_up[i])
        xf = _residual_matmul(act, w_down[i], xf, "ffn_down", tn=256)

        xf = _ple(xf, g_ple[i][None], w_ple_gate[i], p[i].reshape(n, PLE_DIM), w_ple_proj[i])
    return xf.reshape(batch, seq, d)
```

```python
import functools
import math

import numpy as np
import jax
import jax.numpy as jnp
from jax import lax
from jax.experimental import pallas as pl
from jax.experimental.pallas import tpu as pltpu

D_MODEL = 2048
GRID_W = 64
NA_HEADS = 8
NA_DH = 128
NA_KR = 8
NA_KW = 16
DF_HEADS = 8
DF_DQ = 64
DF_DV = 128
FFN_HID = 5632
PLE_DIM = 256
ROPE_THETA = 10000.0
EPS = 1e-6
LAM_INIT = 0.8 - 0.6 * math.exp(-0.3 * 0)

NA_W = NA_HEADS * NA_DH
DF_QW = DF_HEADS * 2 * DF_DQ
DF_VW = DF_HEADS * DF_DV
IN_COLS = 3 * NA_W + 2 * DF_QW + DF_VW + 2 * D_MODEL

LANES = 128
NEG = -1e30
VMEM_LIMIT = 56 * 1024 * 1024

F32 = jnp.float32
BF16 = jnp.bfloat16

CB_NA_Q, CB_NA_K, CB_NA_V = 0, 8, 16
CB_DF_Q, CB_DF_K, CB_DF_V = 24, 32, 40
COL_GATE_A, COL_GATE_B = 6144, 8192


def _params(sem):
    return pltpu.CompilerParams(dimension_semantics=sem, vmem_limit_bytes=VMEM_LIMIT)


def _rms_rows(x, g):
    ms = jnp.mean(x * x, axis=-1, keepdims=True)
    return x * lax.rsqrt(ms + EPS) * g


NORM_ROWS = 256


def _norm_block(x_ref, g_ref, h_ref):
    def body(c, carry):
        r = pl.ds(pl.multiple_of(c * NORM_ROWS, NORM_ROWS), NORM_ROWS)
        h_ref[r, :] = _rms_rows(x_ref[r, :], g_ref[...]).astype(BF16)
        return carry

    lax.fori_loop(0, x_ref.shape[0] // NORM_ROWS, body, 0)


IN_TM, IN_TN = 1024, 512
IN_RB = 256
SEG_TILES = 1024 // IN_TN
DF_TILE0 = 3 * SEG_TILES
DF_TILES = 2 * SEG_TILES


def _in_proj_kernel(x_ref, gmix_ref, w_ref, wdf_ref, gq_ref, gk_ref, caq_ref, sbq_ref, cak_ref, sbk_ref,
                    o_ref, h_ref):
    j = pl.program_id(1)

    @pl.when(j == 0)
    def _():
        _norm_block(x_ref, gmix_ref, h_ref)

    seg = j // SEG_TILES
    heads = IN_TN // LANES

    def run(wsrc_ref, epilogue):
        wb = wsrc_ref[...].astype(BF16)
        for r in range(IN_TM // IN_RB):
            rows = slice(r * IN_RB, (r + 1) * IN_RB)
            epilogue(jnp.dot(h_ref[rows, :], wb, preferred_element_type=F32), rows)

    def na_norm(g_ref):
        def epilogue(y, rows):
            for k in range(heads):
                cols = slice(k * LANES, (k + 1) * LANES)
                o_ref[rows, cols] = _rms_rows(y[:, cols], g_ref[...]).astype(BF16)
        return epilogue

    def df_norm_rope(ca_ref, sb_ref):
        def epilogue(y, rows):
            lane = lax.broadcasted_iota(jnp.int32, (1, LANES), 1)
            comp0 = (lane % 64) < 32
            ca = ca_ref[rows, :]
            sb = sb_ref[rows, :]
            for k in range(heads):
                cols = slice(k * LANES, (k + 1) * LANES)
                yk = y[:, cols]
                pk = pltpu.roll(yk, 64, 1)
                z = yk * yk + pk * pk
                s0 = jnp.sum(jnp.where(comp0, z, 0.0), axis=-1, keepdims=True)
                s1 = jnp.sum(jnp.where(comp0, 0.0, z), axis=-1, keepdims=True)
                r = jnp.where(comp0, lax.rsqrt(s0 * (0.5 / DF_DQ) + EPS), lax.rsqrt(s1 * (0.5 / DF_DQ) + EPS))
                o_ref[rows, cols] = ((yk * ca + pk * sb) * r).astype(BF16)
        return epilogue

    def plain(y, rows):
        o_ref[rows, :] = y.astype(BF16)

    def gate(y, rows):
        o_ref[rows, :] = jax.nn.sigmoid(y).astype(BF16)

    @pl.when(seg == 0)
    def _():
        run(w_ref, na_norm(gq_ref))

    @pl.when(seg == 1)
    def _():
        run(w_ref, na_norm(gk_ref))

    @pl.when(seg == 3)
    def _():
        run(wdf_ref, df_norm_rope(caq_ref, sbq_ref))

    @pl.when(seg == 4)
    def _():
        run(wdf_ref, df_norm_rope(cak_ref, sbk_ref))

    @pl.when((seg == 2) | (seg == 5))
    def _():
        run(w_ref, plain)

    @pl.when(seg >= 6)
    def _():
        run(w_ref, gate)


def _in_proj(x2, g_mix, w_in, w_dfqk, g_na_q, g_na_k, caq, sbq, cak, sbk, seq):
    n = x2.shape[0]
    pos_blocks = seq // IN_TM
    row = lambda i, j: (i, 0)
    const = lambda i, j: (0, 0)
    tab = lambda i, j: (i % pos_blocks, 0)
    is_df = lambda j: (j >= DF_TILE0) & (j < DF_TILE0 + DF_TILES)
    return pl.pallas_call(
        _in_proj_kernel,
        out_shape=jax.ShapeDtypeStruct((n, IN_COLS), BF16),
        grid=(n // IN_TM, IN_COLS // IN_TN),
        in_specs=[
            pl.BlockSpec((IN_TM, D_MODEL), row),
            pl.BlockSpec((1, D_MODEL), const),
            pl.BlockSpec((D_MODEL, IN_TN), lambda i, j: (0, jnp.where(is_df(j), DF_TILE0 - 1, j))),
            pl.BlockSpec((D_MODEL, IN_TN), lambda i, j: (0, jnp.clip(j - DF_TILE0, 0, DF_TILES - 1))),
            pl.BlockSpec((1, LANES), const),
            pl.BlockSpec((1, LANES), const),
            pl.BlockSpec((IN_TM, LANES), tab),
            pl.BlockSpec((IN_TM, LANES), tab),
            pl.BlockSpec((IN_TM, LANES), tab),
            pl.BlockSpec((IN_TM, LANES), tab),
        ],
        out_specs=pl.BlockSpec((IN_TM, IN_TN), lambda i, j: (i, j)),
        scratch_shapes=[pltpu.VMEM((IN_TM, D_MODEL), BF16)],
        compiler_params=_params(("parallel", "arbitrary")),
        name="in_proj",
    )(x2, g_mix, w_in, w_dfqk, g_na_q, g_na_k, caq, sbq, cak, sbk)


NA_QROWS = 8
NA_WROWS = 16
NA_TQ = NA_QROWS * GRID_W
NA_TK = NA_WROWS * GRID_W
NA_PAIR = LANES // GRID_W
NA_NPAIR = NA_QROWS // NA_PAIR
NA_PROWS = NA_KR + NA_PAIR - 1
NA_PK = NA_PROWS * GRID_W
NA_NTAB = 1 + 2 * NA_NPAIR
NA_NEG_SLAB = 2 * NA_KR - 1
NA_VROWS = NA_DH + 16


def _na_geometry(rows):
    nt = rows // NA_QROWS
    assert nt >= 3 and rows >= NA_WROWS
    ws = np.zeros((3, NA_NPAIR), np.int64)
    idx = np.full((NA_NTAB, NA_PROWS, NA_PAIR), NA_NEG_SLAB, np.int64)
    for ci, t in enumerate((0, 1, nt - 1)):
        w0 = int(np.clip(t * NA_QROWS - NA_KR // 2, 0, rows - NA_WROWS))
        for pi in range(NA_NPAIR):
            r = [t * NA_QROWS + NA_PAIR * pi + hb for hb in range(NA_PAIR)]
            rs = [int(np.clip(ri - NA_KR // 2, 0, rows - NA_KR)) for ri in r]
            start = min(min(rs) - w0, NA_WROWS - NA_PROWS)
            assert 0 <= start and max(rs) + NA_KR <= w0 + start + NA_PROWS and min(rs) >= w0 + start
            ws[ci, pi] = start
            tab = 0 if ci == 1 else 1 + (0 if ci == 0 else NA_NPAIR) + pi
            for wp in range(NA_PROWS):
                key_row = w0 + start + wp
                for hb in range(NA_PAIR):
                    if rs[hb] <= key_row < rs[hb] + NA_KR:
                        idx[tab, wp, hb] = key_row - r[hb] + NA_KR - 1
    return ws, idx


def _na_kernel(q_ref, k_ref, vt_ref, tab_ref, o_ref, p_ref, *, rows, ws):
    nt = rows // NA_QROWS

    def window(t):
        return int(np.clip(t * NA_QROWS - NA_KR // 2, 0, rows - NA_WROWS)) * GRID_W

    def probs(t):
        cls = 0 if t == 0 else (2 if t == nt - 1 else 1)
        buf = t
        p_ref[buf] = jnp.zeros(p_ref.shape[1:], BF16)
        for pi in range(NA_NPAIR):
            lanes = slice(pi * LANES, (pi + 1) * LANES)
            tab = 0 if cls == 1 else 1 + (0 if cls == 0 else NA_NPAIR) + pi
            r0 = ws[cls][pi] * GRID_W
            kp = k_ref[window(t) + r0:window(t) + r0 + NA_PK, :]
            qp = q_ref[t * NA_TQ + pi * LANES:t * NA_TQ + (pi + 1) * LANES, :]
            sp = lax.dot_general(kp, qp, (((1,), (1,)), ((), ())), preferred_element_type=F32)
            sp = sp * (NA_DH ** -0.5) + tab_ref[0, tab]
            m = jnp.max(sp, axis=0, keepdims=True)
            p_ref[buf, r0:r0 + NA_PK, lanes] = jnp.exp(sp - m).astype(BF16)

    def outputs(t):
        vw = vt_ref[:, window(t):window(t) + NA_TK]
        ov = jnp.dot(vw, p_ref[t], preferred_element_type=F32)
        o_ref[t * NA_TQ:(t + 1) * NA_TQ, :] = (ov[0:NA_DH, :] / ov[NA_DH:NA_DH + 1, :]).T.astype(BF16)

    for t in range(nt + 1):
        if t < nt:
            probs(t)
        if t >= 1:
            outputs(t - 1)


def _na_attn(proj, v_t, tabs, batch, seq):
    rows = seq // GRID_W
    ws, _ = _na_geometry(rows)
    return pl.pallas_call(
        functools.partial(_na_kernel, rows=rows, ws=tuple(tuple(int(v) for v in row) for row in ws)),
        out_shape=jax.ShapeDtypeStruct((batch * seq, NA_W), BF16),
        grid=(batch, NA_HEADS),
        in_specs=[
            pl.BlockSpec((seq, LANES), lambda b, h: (b, CB_NA_Q + h)),
            pl.BlockSpec((seq, LANES), lambda b, h: (b, CB_NA_K + h)),
            pl.BlockSpec((NA_VROWS, seq), lambda b, h: (h, b)),
            pl.BlockSpec((1, NA_NTAB, NA_PK, LANES), lambda b, h: (h, 0, 0, 0)),
        ],
        out_specs=pl.BlockSpec((seq, LANES), lambda b, h: (b, h)),
        scratch_shapes=[pltpu.VMEM((rows // NA_QROWS, NA_TK, NA_TQ), BF16)],
        compiler_params=_params(("parallel", "arbitrary")),
        name="na_attn",
    )(proj, proj, v_t, tabs)


def _na_tables(rpb, rows):
    c = np.arange(GRID_W)
    cs = np.clip(c - NA_KW // 2, 0, GRID_W - NA_KW)
    valid_c = (c[None, :] >= cs[:, None]) & (c[None, :] < cs[:, None] + NA_KW)
    width = 4 * GRID_W
    base = GRID_W - NA_KW
    rp = jnp.pad(rpb, ((0, 0), (0, 0), (base, width - base - (2 * NA_KW - 1))), constant_values=NEG)
    skew = jnp.tile(rp, (1, 1, GRID_W))[:, :, :GRID_W * (width - 1)].reshape(NA_HEADS, NA_NEG_SLAB, GRID_W, width - 1)
    j0 = NA_KW - 1 + base
    toep = skew[:, :, :, j0:j0 + GRID_W]
    toep = jnp.where(jnp.asarray(valid_c)[None, None], toep, NEG)
    slabs = jnp.concatenate([jnp.swapaxes(toep, 2, 3), jnp.full((NA_HEADS, 1, GRID_W, GRID_W), NEG, F32)], axis=1)
    _, idx = _na_geometry(rows)
    tabs = jnp.take(slabs, jnp.asarray(idx.reshape(-1), jnp.int32), axis=1)
    tabs = tabs.reshape(NA_HEADS, NA_NTAB, NA_PROWS, NA_PAIR, GRID_W, GRID_W).transpose(0, 1, 2, 4, 3, 5)
    return tabs.reshape(NA_HEADS, NA_NTAB, NA_PK, LANES)


def _vt_with_ones(v, heads, dh, vrows):
    n = v.shape[0]
    ones = jnp.zeros((heads, vrows - dh, n), v.dtype).at[:, 0].set(1.0)
    return jnp.concatenate([v.T.reshape(heads, dh, n), ones], axis=1).reshape(heads * vrows, n)


DF_TQ, DF_TK = 1024, 512
DF_TG = 256
DF_VROWS = DF_DV + 16


def _df_kernel(q_ref, k_ref, vt_ref, lam_ref, gsub_ref, o_ref, q12_ref, s_ref, acc_ref, *, seq):
    lane = lax.broadcasted_iota(jnp.int32, (1, LANES), 1)
    comp0 = (lane % 64) < 32
    q = q_ref[...]
    zero = jnp.zeros_like(q)
    q12_ref[0:DF_TQ, :] = jnp.where(comp0, q, zero)
    q12_ref[DF_TQ:2 * DF_TQ, :] = jnp.where(comp0, zero, q)
    acc_ref[...] = jnp.zeros(acc_ref.shape, F32)

    groups = 2 * DF_TQ // DF_TG
    n_chunks = seq // DF_TK

    def scores(kc, g):
        cols = slice(g * DF_TG, (g + 1) * DF_TG)
        return lax.dot_general(kc, q12_ref[cols, :], (((1,), (1,)), ((), ())), preferred_element_type=F32)

    def keys(c):
        return k_ref[c * DF_TK:(c + 1) * DF_TK, :]

    def step(c, carry):
        cur = c % 2
        vc = vt_ref[:, c * DF_TK:(c + 1) * DF_TK]
        kn = keys(c + 1) if c + 1 < n_chunks else None
        out = []
        for g in range(groups):
            m_prev = carry[g]
            cols = slice(g * DF_TG, (g + 1) * DF_TG)
            if kn is not None:
                s_ref[1 - cur, :, cols] = scores(kn, g)
            s = s_ref[cur, :, cols]
            m_new = jnp.maximum(m_prev, jnp.max(s, axis=0, keepdims=True))
            alpha = jnp.exp2(m_prev - m_new)
            p = jnp.exp2(s - m_new).astype(BF16)
            pv = jnp.dot(vc, p, preferred_element_type=F32)
            acc_ref[:, cols] = alpha * acc_ref[:, cols] + pv
            out.append(m_new)
        return tuple(out)

    k0 = keys(0)
    for g in range(groups):
        s_ref[0, :, g * DF_TG:(g + 1) * DF_TG] = scores(k0, g)

    carry = tuple(jnp.full((1, DF_TG), -jnp.inf, F32) for _ in range(groups))
    for c in range(n_chunks):
        carry = step(c, carry)

    lp = lam_ref[...]
    lam = (jnp.exp(jnp.sum(lp[0:1] * lp[1:2], axis=-1, keepdims=True))
           - jnp.exp(jnp.sum(lp[2:3] * lp[3:4], axis=-1, keepdims=True)) + LAM_INIT)
    o12 = acc_ref[0:DF_DV, :] / acc_ref[DF_DV:DF_DV + 1, :]
    o_t = o12[:, 0:DF_TQ] - lam * o12[:, DF_TQ:2 * DF_TQ]
    o_ref[...] = (_rms_rows(o_t.T, gsub_ref[...]) * (1.0 - LAM_INIT)).astype(BF16)


def _df_attn(proj, v_t, lam_p, g_sub, batch, seq):
    nq = seq // DF_TQ
    const = lambda b, h, i: (0, 0)
    return pl.pallas_call(
        functools.partial(_df_kernel, seq=seq),
        out_shape=jax.ShapeDtypeStruct((batch * seq, DF_VW), BF16),
        grid=(batch, DF_HEADS, nq),
        in_specs=[
            pl.BlockSpec((DF_TQ, LANES), lambda b, h, i: (b * nq + i, CB_DF_Q + h)),
            pl.BlockSpec((seq, LANES), lambda b, h, i: (b, CB_DF_K + h)),
            pl.BlockSpec((DF_VROWS, seq), lambda b, h, i: (h, b)),
            pl.BlockSpec((4, DF_DQ), const),
            pl.BlockSpec((1, DF_DV), const),
        ],
        out_specs=pl.BlockSpec((DF_TQ, LANES), lambda b, h, i: (b * nq + i, h)),
        scratch_shapes=[pltpu.VMEM((2 * DF_TQ, LANES), BF16), pltpu.VMEM((2, DF_TK, 2 * DF_TQ), F32),
                        pltpu.VMEM((DF_VROWS, 2 * DF_TQ), F32)],
        compiler_params=_params(("parallel", "parallel", "arbitrary")),
        name="df_attn",
    )(proj, proj, v_t, lam_p, g_sub)


MG_TM, MG_TN = 1024, 512


def _merge_kernel(oa_ref, ob_ref, wa_ref, wb_ref, sa_ref, sb_ref, o_ref):
    ya = jnp.dot(oa_ref[...], wa_ref[...].astype(BF16), preferred_element_type=F32)
    yb = jnp.dot(ob_ref[...], wb_ref[...].astype(BF16), preferred_element_type=F32)
    o_ref[...] = (sa_ref[...].astype(F32) * ya + sb_ref[...].astype(F32) * yb).astype(BF16)


def _merge(o_a, o_b, w_na_out, w_df_out, proj):
    n = o_a.shape[0]
    ga, gb = COL_GATE_A // MG_TN, COL_GATE_B // MG_TN
    return pl.pallas_call(
        _merge_kernel,
        out_shape=jax.ShapeDtypeStruct((n, D_MODEL), BF16),
        grid=(n // MG_TM, D_MODEL // MG_TN),
        in_specs=[
            pl.BlockSpec((MG_TM, NA_W), lambda i, j: (i, 0)),
            pl.BlockSpec((MG_TM, DF_VW), lambda i, j: (i, 0)),
            pl.BlockSpec((NA_W, MG_TN), lambda i, j: (0, j)),
            pl.BlockSpec((DF_VW, MG_TN), lambda i, j: (0, j)),
            pl.BlockSpec((MG_TM, MG_TN), lambda i, j: (i, ga + j)),
            pl.BlockSpec((MG_TM, MG_TN), lambda i, j: (i, gb + j)),
        ],
        out_specs=pl.BlockSpec((MG_TM, MG_TN), lambda i, j: (i, j)),
        compiler_params=_params(("parallel", "arbitrary")),
        name="merge",
    )(o_a, o_b, w_na_out, w_df_out, proj, proj)


RS_TM, RS_TN = 1024, 512


def _residual_matmul_kernel(a_ref, w_ref, x_ref, o_ref):
    o_ref[...] = x_ref[...] + jnp.dot(a_ref[...], w_ref[...].astype(BF16), preferred_element_type=F32)


def _residual_matmul(a, w, x, name, tm=RS_TM, tn=RS_TN):
    n, k = a.shape
    d = w.shape[1]
    return pl.pallas_call(
        _residual_matmul_kernel,
        out_shape=jax.ShapeDtypeStruct((n, d), F32),
        grid=(n // tm, d // tn),
        in_specs=[
            pl.BlockSpec((tm, k), lambda i, j: (i, 0)),
            pl.BlockSpec((k, tn), lambda i, j: (0, j)),
            pl.BlockSpec((tm, tn), lambda i, j: (i, j)),
        ],
        out_specs=pl.BlockSpec((tm, tn), lambda i, j: (i, j)),
        compiler_params=_params(("parallel", "arbitrary")),
        name=name,
    )(a, w, x)


UP_TM, UP_TN = 1024, 512


def _ffn_up_kernel(x_ref, g_ref, wg_ref, wu_ref, o_ref, h_ref):
    @pl.when(pl.program_id(1) == 0)
    def _():
        _norm_block(x_ref, g_ref, h_ref)

    h = h_ref[...]
    gate = jnp.dot(h, wg_ref[...].astype(BF16), preferred_element_type=F32)
    up = jnp.dot(h, wu_ref[...].astype(BF16), preferred_element_type=F32)
    o_ref[...] = (gate * jax.nn.sigmoid(gate) * up).astype(BF16)


def _ffn_up(x, g, w_gate, w_up):
    n = x.shape[0]
    return pl.pallas_call(
        _ffn_up_kernel,
        out_shape=jax.ShapeDtypeStruct((n, FFN_HID), BF16),
        grid=(n // UP_TM, FFN_HID // UP_TN),
        in_specs=[
            pl.BlockSpec((UP_TM, D_MODEL), lambda i, j: (i, 0)),
            pl.BlockSpec((1, D_MODEL), lambda i, j: (0, 0)),
            pl.BlockSpec((D_MODEL, UP_TN), lambda i, j: (0, j)),
            pl.BlockSpec((D_MODEL, UP_TN), lambda i, j: (0, j)),
        ],
        out_specs=pl.BlockSpec((UP_TM, UP_TN), lambda i, j: (i, j)),
        scratch_shapes=[pltpu.VMEM((UP_TM, D_MODEL), BF16)],
        compiler_params=_params(("parallel", "arbitrary")),
        name="ffn_up",
    )(x, g, w_gate, w_up)


PLE_TM, PLE_TN = 1024, 512


def _ple_kernel(x_ref, g_ref, wg_ref, p_ref, wp_ref, xt_ref, o_ref, h_ref):
    @pl.when(pl.program_id(1) == 0)
    def _():
        _norm_block(x_ref, g_ref, h_ref)

    gate = jnp.dot(h_ref[...], wg_ref[...].astype(BF16), preferred_element_type=F32)
    emb = jnp.dot(p_ref[...].astype(BF16), wp_ref[...].astype(BF16), preferred_element_type=F32)
    o_ref[...] = xt_ref[...] + jax.nn.sigmoid(gate) * emb


def _ple(x, g, w_gate, p, w_proj):
    n = x.shape[0]
    return pl.pallas_call(
        _ple_kernel,
        out_shape=jax.ShapeDtypeStruct((n, D_MODEL), F32),
        grid=(n // PLE_TM, D_MODEL // PLE_TN),
        in_specs=[
            pl.BlockSpec((PLE_TM, D_MODEL), lambda i, j: (i, 0)),
            pl.BlockSpec((1, D_MODEL), lambda i, j: (0, 0)),
            pl.BlockSpec((D_MODEL, PLE_TN), lambda i, j: (0, j)),
            pl.BlockSpec((PLE_TM, PLE_DIM), lambda i, j: (i, 0)),
            pl.BlockSpec((PLE_DIM, PLE_TN), lambda i, j: (0, j)),
            pl.BlockSpec((PLE_TM, PLE_TN), lambda i, j: (i, j)),
        ],
        out_specs=pl.BlockSpec((PLE_TM, PLE_TN), lambda i, j: (i, j)),
        scratch_shapes=[pltpu.VMEM((PLE_TM, D_MODEL), BF16)],
        compiler_params=_params(("parallel", "arbitrary")),
        name="ple",
    )(x, g, w_gate, p, w_proj, x)


def _df_cols(w):
    k = w.shape[0]
    return w.reshape(k, DF_HEADS, 2, 2, DF_DQ // 2).transpose(0, 1, 3, 2, 4).reshape(k, DF_QW)


def _rope_tables(seq, g, scale):
    half = DF_DQ // 2
    inv = 1.0 / (ROPE_THETA ** (jnp.arange(0, DF_DQ, 2, dtype=F32) / DF_DQ))
    ang = jnp.arange(seq, dtype=F32)[:, None] * inv[None, :]
    cos, sin = jnp.cos(ang), jnp.sin(ang)
    g1, g2 = g[:half], g[half:]
    ca = jnp.concatenate([cos * g1, cos * g1, cos * g2, cos * g2], axis=1) * scale
    sb = jnp.concatenate([-sin * g2, -sin * g2, sin * g1, sin * g1], axis=1) * scale
    return ca, sb


def kernel(x, p, g_mix, w_in, g_na_q, g_na_k, na_rpb, g_df_q, g_df_k, lam_q1, lam_k1, lam_q2, lam_k2,
           g_df_sub, w_na_out, w_df_out, w_o, g_ffn, w_gate, w_up, w_down, g_ple, w_ple_gate, w_ple_proj):
    batch, seq, d = x.shape
    n = batch * seq
    rows = seq // GRID_W
    depth = w_in.shape[0]
    xf = x.reshape(n, d)
    for i in range(depth):
        s0 = 3 * NA_W
        w_dfqk = jnp.concatenate([_df_cols(w_in[i][:, s0:s0 + DF_QW]), _df_cols(w_in[i][:, s0 + DF_QW:s0 + 2 * DF_QW])],
                                 axis=1)
        caq, sbq = _rope_tables(seq, g_df_q[i], DF_DQ ** -0.5 * math.log2(math.e))
        cak, sbk = _rope_tables(seq, g_df_k[i], 1.0)
        proj = _in_proj(xf, g_mix[i][None], w_in[i], w_dfqk, g_na_q[i][None], g_na_k[i][None], caq, sbq, cak, sbk,
                        seq)

        na_vt = _vt_with_ones(proj[:, CB_NA_V * LANES:(CB_NA_V + NA_HEADS) * LANES], NA_HEADS, NA_DH, NA_VROWS)
        o_a = _na_attn(proj, na_vt, _na_tables(na_rpb[i], rows), batch, seq)
        lam_p = jnp.stack([lam_q1[i], lam_k1[i], lam_q2[i], lam_k2[i]], axis=0)
        df_vt = _vt_with_ones(proj[:, CB_DF_V * LANES:(CB_DF_V + DF_HEADS) * LANES], DF_HEADS, DF_DV, DF_VROWS)
        o_b = _df_attn(proj, df_vt, lam_p, g_df_sub[i][None], batch, seq)

        merged = _merge(o_a, o_b, w_na_out[i], w_df_out[i], proj)
        xf = _residual_matmul(merged, w_o[i], xf, "out_proj")

        act = _ffn_up(xf, g_ffn[i][None], w_gate[i], w_up[i])
        xf = _residual_matmul(act, w_down[i], xf, "ffn_down", tn=256)

        xf = _ple(xf, g_ple[i][None], w_ple_gate[i], p[i].reshape(n, PLE_DIM), w_ple_proj[i])
    return xf.reshape(batch, seq, d)
```

```python
import functools
import math

import numpy as np
import jax
import jax.numpy as jnp
from jax import lax
from jax.experimental import pallas as pl
from jax.experimental.pallas import tpu as pltpu

D_MODEL = 2048
GRID_W = 64
NA_HEADS = 8
NA_DH = 128
NA_KR = 8
NA_KW = 16
DF_HEADS = 8
DF_DQ = 64
DF_DV = 128
FFN_HID = 5632
PLE_DIM = 256
ROPE_THETA = 10000.0
EPS = 1e-6
LAM_INIT = 0.8 - 0.6 * math.exp(-0.3 * 0)

NA_W = NA_HEADS * NA_DH
DF_QW = DF_HEADS * 2 * DF_DQ
DF_VW = DF_HEADS * DF_DV
IN_COLS = 3 * NA_W + 2 * DF_QW + DF_VW + 2 * D_MODEL

LANES = 128
NEG = -1e30
VMEM_LIMIT = 56 * 1024 * 1024

F32 = jnp.float32
BF16 = jnp.bfloat16

CB_NA_Q, CB_NA_K, CB_NA_V = 0, 8, 16
CB_DF_Q, CB_DF_K, CB_DF_V = 24, 32, 40
COL_GATE_A, COL_GATE_B = 6144, 8192


def _params(sem):
    return pltpu.CompilerParams(dimension_semantics=sem, vmem_limit_bytes=VMEM_LIMIT)


def _rms_rows(x, g):
    ms = jnp.mean(x * x, axis=-1, keepdims=True)
    return x * lax.rsqrt(ms + EPS) * g


ONES_ROWS = 16


def _with_ones_rows(v_t):
    row = lax.broadcasted_iota(jnp.int32, (ONES_ROWS, v_t.shape[1]), 0)
    return jnp.concatenate([v_t, jnp.where(row == 0, 1.0, 0.0).astype(v_t.dtype)], axis=0)


NORM_ROWS = 256


def _norm_block(x_ref, g_ref, h_ref):
    def body(c, carry):
        r = pl.ds(pl.multiple_of(c * NORM_ROWS, NORM_ROWS), NORM_ROWS)
        h_ref[r, :] = _rms_rows(x_ref[r, :], g_ref[...]).astype(BF16)
        return carry

    lax.fori_loop(0, x_ref.shape[0] // NORM_ROWS, body, 0)


IN_TM, IN_TN = 1024, 512
IN_RB = 256
SEG_TILES = 1024 // IN_TN
DF_TILE0 = 3 * SEG_TILES
DF_TILES = 2 * SEG_TILES


def _in_proj_kernel(x_ref, gmix_ref, w_ref, wdf_ref, gq_ref, gk_ref, caq_ref, sbq_ref, cak_ref, sbk_ref,
                    o_ref, h_ref):
    j = pl.program_id(1)

    @pl.when(j == 0)
    def _():
        _norm_block(x_ref, gmix_ref, h_ref)

    seg = j // SEG_TILES
    heads = IN_TN // LANES

    def run(wsrc_ref, epilogue):
        wb = wsrc_ref[...].astype(BF16)
        for r in range(IN_TM // IN_RB):
            rows = slice(r * IN_RB, (r + 1) * IN_RB)
            epilogue(jnp.dot(h_ref[rows, :], wb, preferred_element_type=F32), rows)

    def na_norm(g_ref):
        def epilogue(y, rows):
            for k in range(heads):
                cols = slice(k * LANES, (k + 1) * LANES)
                o_ref[rows, cols] = _rms_rows(y[:, cols], g_ref[...]).astype(BF16)
        return epilogue

    def df_norm_rope(ca_ref, sb_ref):
        def epilogue(y, rows):
            lane = lax.broadcasted_iota(jnp.int32, (1, LANES), 1)
            comp0 = (lane % 64) < 32
            ca = ca_ref[rows, :]
            sb = sb_ref[rows, :]
            for k in range(heads):
                cols = slice(k * LANES, (k + 1) * LANES)
                yk = y[:, cols]
                pk = pltpu.roll(yk, 64, 1)
                z = yk * yk + pk * pk
                s0 = jnp.sum(jnp.where(comp0, z, 0.0), axis=-1, keepdims=True)
                s1 = jnp.sum(jnp.where(comp0, 0.0, z), axis=-1, keepdims=True)
                r = jnp.where(comp0, lax.rsqrt(s0 * (0.5 / DF_DQ) + EPS), lax.rsqrt(s1 * (0.5 / DF_DQ) + EPS))
                o_ref[rows, cols] = ((yk * ca + pk * sb) * r).astype(BF16)
        return epilogue

    def plain(y, rows):
        o_ref[rows, :] = y.astype(BF16)

    def gate(y, rows):
        o_ref[rows, :] = jax.nn.sigmoid(y).astype(BF16)

    @pl.when(seg == 0)
    def _():
        run(w_ref, na_norm(gq_ref))

    @pl.when(seg == 1)
    def _():
        run(w_ref, na_norm(gk_ref))

    @pl.when(seg == 3)
    def _():
        run(wdf_ref, df_norm_rope(caq_ref, sbq_ref))

    @pl.when(seg == 4)
    def _():
        run(wdf_ref, df_norm_rope(cak_ref, sbk_ref))

    @pl.when((seg == 2) | (seg == 5))
    def _():
        run(w_ref, plain)

    @pl.when(seg >= 6)
    def _():
        run(w_ref, gate)


def _in_proj(x2, g_mix, w_in, w_dfqk, g_na_q, g_na_k, caq, sbq, cak, sbk, seq):
    n = x2.shape[0]
    pos_blocks = seq // IN_TM
    row = lambda i, j: (i, 0)
    const = lambda i, j: (0, 0)
    tab = lambda i, j: (i % pos_blocks, 0)
    is_df = lambda j: (j >= DF_TILE0) & (j < DF_TILE0 + DF_TILES)
    return pl.pallas_call(
        _in_proj_kernel,
        out_shape=jax.ShapeDtypeStruct((n, IN_COLS), BF16),
        grid=(n // IN_TM, IN_COLS // IN_TN),
        in_specs=[
            pl.BlockSpec((IN_TM, D_MODEL), row),
            pl.BlockSpec((1, D_MODEL), const),
            pl.BlockSpec((D_MODEL, IN_TN), lambda i, j: (0, jnp.where(is_df(j), DF_TILE0 - 1, j))),
            pl.BlockSpec((D_MODEL, IN_TN), lambda i, j: (0, jnp.clip(j - DF_TILE0, 0, DF_TILES - 1))),
            pl.BlockSpec((1, LANES), const),
            pl.BlockSpec((1, LANES), const),
            pl.BlockSpec((IN_TM, LANES), tab),
            pl.BlockSpec((IN_TM, LANES), tab),
            pl.BlockSpec((IN_TM, LANES), tab),
            pl.BlockSpec((IN_TM, LANES), tab),
        ],
        out_specs=pl.BlockSpec((IN_TM, IN_TN), lambda i, j: (i, j)),
        scratch_shapes=[pltpu.VMEM((IN_TM, D_MODEL), BF16)],
        compiler_params=_params(("parallel", "arbitrary")),
        name="in_proj",
    )(x2, g_mix, w_in, w_dfqk, g_na_q, g_na_k, caq, sbq, cak, sbk)


NA_QROWS = 8
NA_WROWS = 16
NA_TQ = NA_QROWS * GRID_W
NA_TK = NA_WROWS * GRID_W
NA_PAIR = LANES // GRID_W
NA_NPAIR = NA_QROWS // NA_PAIR
NA_PROWS = NA_KR + NA_PAIR - 1
NA_PK = NA_PROWS * GRID_W
NA_NTAB = 1 + 2 * NA_NPAIR
NA_NEG_SLAB = 2 * NA_KR - 1


def _na_geometry(rows):
    nt = rows // NA_QROWS
    assert nt >= 3 and rows >= NA_WROWS
    ws = np.zeros((3, NA_NPAIR), np.int64)
    idx = np.full((NA_NTAB, NA_PROWS, NA_PAIR), NA_NEG_SLAB, np.int64)
    for ci, t in enumerate((0, 1, nt - 1)):
        w0 = int(np.clip(t * NA_QROWS - NA_KR // 2, 0, rows - NA_WROWS))
        for pi in range(NA_NPAIR):
            r = [t * NA_QROWS + NA_PAIR * pi + hb for hb in range(NA_PAIR)]
            rs = [int(np.clip(ri - NA_KR // 2, 0, rows - NA_KR)) for ri in r]
            start = min(min(rs) - w0, NA_WROWS - NA_PROWS)
            assert 0 <= start and max(rs) + NA_KR <= w0 + start + NA_PROWS and min(rs) >= w0 + start
            ws[ci, pi] = start
            tab = 0 if ci == 1 else 1 + (0 if ci == 0 else NA_NPAIR) + pi
            for wp in range(NA_PROWS):
                key_row = w0 + start + wp
                for hb in range(NA_PAIR):
                    if rs[hb] <= key_row < rs[hb] + NA_KR:
                        idx[tab, wp, hb] = key_row - r[hb] + NA_KR - 1
    return ws, idx


def _na_kernel(q_ref, k_ref, vt_ref, tab_ref, o_ref, p_ref, *, rows, ws):
    nt = rows // NA_QROWS

    def window(t):
        return int(np.clip(t * NA_QROWS - NA_KR // 2, 0, rows - NA_WROWS)) * GRID_W

    def probs(t):
        cls = 0 if t == 0 else (2 if t == nt - 1 else 1)
        buf = t
        p_ref[buf] = jnp.zeros(p_ref.shape[1:], BF16)
        for pi in range(NA_NPAIR):
            lanes = slice(pi * LANES, (pi + 1) * LANES)
            tab = 0 if cls == 1 else 1 + (0 if cls == 0 else NA_NPAIR) + pi
            r0 = ws[cls][pi] * GRID_W
            kp = k_ref[window(t) + r0:window(t) + r0 + NA_PK, :]
            qp = q_ref[t * NA_TQ + pi * LANES:t * NA_TQ + (pi + 1) * LANES, :]
            sp = lax.dot_general(kp, qp, (((1,), (1,)), ((), ())), preferred_element_type=F32)
            sp = sp * (NA_DH ** -0.5) + tab_ref[0, tab]
            m = jnp.max(sp, axis=0, keepdims=True)
            p_ref[buf, r0:r0 + NA_PK, lanes] = jnp.exp(sp - m).astype(BF16)

    def outputs(t):
        vw = _with_ones_rows(vt_ref[:, window(t):window(t) + NA_TK])
        ov = jnp.dot(vw, p_ref[t], preferred_element_type=F32)
        o_ref[t * NA_TQ:(t + 1) * NA_TQ, :] = (ov[0:NA_DH, :] / ov[NA_DH:NA_DH + 1, :]).T.astype(BF16)

    for t in range(nt + 1):
        if t < nt:
            probs(t)
        if t >= 1:
            outputs(t - 1)


def _na_attn(proj, v_t, tabs, batch, seq):
    rows = seq // GRID_W
    ws, _ = _na_geometry(rows)
    return pl.pallas_call(
        functools.partial(_na_kernel, rows=rows, ws=tuple(tuple(int(v) for v in row) for row in ws)),
        out_shape=jax.ShapeDtypeStruct((batch * seq, NA_W), BF16),
        grid=(batch, NA_HEADS),
        in_specs=[
            pl.BlockSpec((seq, LANES), lambda b, h: (b, CB_NA_Q + h)),
            pl.BlockSpec((seq, LANES), lambda b, h: (b, CB_NA_K + h)),
            pl.BlockSpec((NA_DH, seq), lambda b, h: (h, b)),
            pl.BlockSpec((1, NA_NTAB, NA_PK, LANES), lambda b, h: (h, 0, 0, 0)),
        ],
        out_specs=pl.BlockSpec((seq, LANES), lambda b, h: (b, h)),
        scratch_shapes=[pltpu.VMEM((rows // NA_QROWS, NA_TK, NA_TQ), BF16)],
        compiler_params=_params(("parallel", "arbitrary")),
        name="na_attn",
    )(proj, proj, v_t, tabs)


def _na_tables(rpb, rows):
    c = np.arange(GRID_W)
    cs = np.clip(c - NA_KW // 2, 0, GRID_W - NA_KW)
    valid_c = (c[None, :] >= cs[:, None]) & (c[None, :] < cs[:, None] + NA_KW)
    dc = c[:, None] - c[None, :] + NA_KW - 1
    onehot = (dc[None] == np.arange(2 * NA_KW - 1)[:, None, None]) & valid_c.T[None]
    mask = np.where(valid_c.T, 0.0, NEG).astype(np.float32)
    toep = jnp.sum(rpb[:, :, :, None, None] * jnp.asarray(onehot, F32)[None, None], axis=2) + jnp.asarray(mask)
    slabs = jnp.concatenate([toep, jnp.full((NA_HEADS, 1, GRID_W, GRID_W), NEG, F32)], axis=1)
    _, idx = _na_geometry(rows)
    tabs = jnp.take(slabs, jnp.asarray(idx.reshape(-1), jnp.int32), axis=1)
    tabs = tabs.reshape(NA_HEADS, NA_NTAB, NA_PROWS, NA_PAIR, GRID_W, GRID_W).transpose(0, 1, 2, 4, 3, 5)
    return tabs.reshape(NA_HEADS, NA_NTAB, NA_PK, LANES)


DF_TQ, DF_TK = 1024, 512
DF_TG = 256
DF_VROWS = DF_DV + ONES_ROWS


def _df_kernel(q_ref, k_ref, vt_ref, lam_ref, gsub_ref, o_ref, q12_ref, s_ref, acc_ref, *, seq):
    lane = lax.broadcasted_iota(jnp.int32, (1, LANES), 1)
    comp0 = (lane % 64) < 32
    q = q_ref[...]
    zero = jnp.zeros_like(q)
    q12_ref[0:DF_TQ, :] = jnp.where(comp0, q, zero)
    q12_ref[DF_TQ:2 * DF_TQ, :] = jnp.where(comp0, zero, q)
    acc_ref[...] = jnp.zeros(acc_ref.shape, F32)

    groups = 2 * DF_TQ // DF_TG
    n_chunks = seq // DF_TK

    def scores(kc, g):
        cols = slice(g * DF_TG, (g + 1) * DF_TG)
        return lax.dot_general(kc, q12_ref[cols, :], (((1,), (1,)), ((), ())), preferred_element_type=F32)

    def keys(c):
        return k_ref[c * DF_TK:(c + 1) * DF_TK, :]

    def step(c, carry):
        cur = c % 2
        vc = _with_ones_rows(vt_ref[:, c * DF_TK:(c + 1) * DF_TK])
        kn = keys(c + 1) if c + 1 < n_chunks else None
        out = []
        for g in range(groups):
            m_prev = carry[g]
            cols = slice(g * DF_TG, (g + 1) * DF_TG)
            if kn is not None:
                s_ref[1 - cur, :, cols] = scores(kn, g)
            s = s_ref[cur, :, cols]
            m_new = jnp.maximum(m_prev, jnp.max(s, axis=0, keepdims=True))
            alpha = jnp.exp2(m_prev - m_new)
            p = jnp.exp2(s - m_new).astype(BF16)
            pv = jnp.dot(vc, p, preferred_element_type=F32)
            acc_ref[:, cols] = alpha * acc_ref[:, cols] + pv
            out.append(m_new)
        return tuple(out)

    k0 = keys(0)
    for g in range(groups):
        s_ref[0, :, g * DF_TG:(g + 1) * DF_TG] = scores(k0, g)

    carry = tuple(jnp.full((1, DF_TG), -jnp.inf, F32) for _ in range(groups))
    for c in range(n_chunks):
        carry = step(c, carry)

    lp = lam_ref[...]
    lam = (jnp.exp(jnp.sum(lp[0:1] * lp[1:2], axis=-1, keepdims=True))
           - jnp.exp(jnp.sum(lp[2:3] * lp[3:4], axis=-1, keepdims=True)) + LAM_INIT)
    o12 = acc_ref[0:DF_DV, :] / acc_ref[DF_DV:DF_DV + 1, :]
    o_t = o12[:, 0:DF_TQ] - lam * o12[:, DF_TQ:2 * DF_TQ]
    o_ref[...] = (_rms_rows(o_t.T, gsub_ref[...]) * (1.0 - LAM_INIT)).astype(BF16)


def _df_attn(proj, v_t, lam_p, g_sub, batch, seq):
    nq = seq // DF_TQ
    const = lambda b, h, i: (0, 0)
    return pl.pallas_call(
        functools.partial(_df_kernel, seq=seq),
        out_shape=jax.ShapeDtypeStruct((batch * seq, DF_VW), BF16),
        grid=(batch, DF_HEADS, nq),
        in_specs=[
            pl.BlockSpec((DF_TQ, LANES), lambda b, h, i: (b * nq + i, CB_DF_Q + h)),
            pl.BlockSpec((seq, LANES), lambda b, h, i: (b, CB_DF_K + h)),
            pl.BlockSpec((DF_DV, seq), lambda b, h, i: (h, b)),
            pl.BlockSpec((4, DF_DQ), const),
            pl.BlockSpec((1, DF_DV), const),
        ],
        out_specs=pl.BlockSpec((DF_TQ, LANES), lambda b, h, i: (b * nq + i, h)),
        scratch_shapes=[pltpu.VMEM((2 * DF_TQ, LANES), BF16), pltpu.VMEM((2, DF_TK, 2 * DF_TQ), F32),
                        pltpu.VMEM((DF_VROWS, 2 * DF_TQ), F32)],
        compiler_params=_params(("parallel", "parallel", "arbitrary")),
        name="df_attn",
    )(proj, proj, v_t, lam_p, g_sub)


MG_TM, MG_TN = 1024, 512


def _merge_kernel(oa_ref, ob_ref, wa_ref, wb_ref, sa_ref, sb_ref, o_ref):
    ya = jnp.dot(oa_ref[...], wa_ref[...].astype(BF16), preferred_element_type=F32)
    yb = jnp.dot(ob_ref[...], wb_ref[...].astype(BF16), preferred_element_type=F32)
    o_ref[...] = (sa_ref[...].astype(F32) * ya + sb_ref[...].astype(F32) * yb).astype(BF16)


def _merge(o_a, o_b, w_na_out, w_df_out, proj):
    n = o_a.shape[0]
    ga, gb = COL_GATE_A // MG_TN, COL_GATE_B // MG_TN
    return pl.pallas_call(
        _merge_kernel,
        out_shape=jax.ShapeDtypeStruct((n, D_MODEL), BF16),
        grid=(n // MG_TM, D_MODEL // MG_TN),
        in_specs=[
            pl.BlockSpec((MG_TM, NA_W), lambda i, j: (i, 0)),
            pl.BlockSpec((MG_TM, DF_VW), lambda i, j: (i, 0)),
            pl.BlockSpec((NA_W, MG_TN), lambda i, j: (0, j)),
            pl.BlockSpec((DF_VW, MG_TN), lambda i, j: (0, j)),
            pl.BlockSpec((MG_TM, MG_TN), lambda i, j: (i, ga + j)),
            pl.BlockSpec((MG_TM, MG_TN), lambda i, j: (i, gb + j)),
        ],
        out_specs=pl.BlockSpec((MG_TM, MG_TN), lambda i, j: (i, j)),
        compiler_params=_params(("parallel", "arbitrary")),
        name="merge",
    )(o_a, o_b, w_na_out, w_df_out, proj, proj)


RS_TM, RS_TN = 1024, 512


def _residual_matmul_kernel(a_ref, w_ref, x_ref, o_ref):
    o_ref[...] = x_ref[...] + jnp.dot(a_ref[...], w_ref[...].astype(BF16), preferred_element_type=F32)


def _residual_matmul(a, w, x, name, tm=RS_TM, tn=RS_TN):
    n, k = a.shape
    d = w.shape[1]
    return pl.pallas_call(
        _residual_matmul_kernel,
        out_shape=jax.ShapeDtypeStruct((n, d), F32),
        grid=(n // tm, d // tn),
        in_specs=[
            pl.BlockSpec((tm, k), lambda i, j: (i, 0)),
            pl.BlockSpec((k, tn), lambda i, j: (0, j)),
            pl.BlockSpec((tm, tn), lambda i, j: (i, j)),
        ],
        out_specs=pl.BlockSpec((tm, tn), lambda i, j: (i, j)),
        compiler_params=_params(("parallel", "arbitrary")),
        name=name,
    )(a, w, x)


UP_TM, UP_TN = 1024, 512


def _ffn_up_kernel(x_ref, g_ref, wg_ref, wu_ref, o_ref, h_ref):
    @pl.when(pl.program_id(1) == 0)
    def _():
        _norm_block(x_ref, g_ref, h_ref)

    h = h_ref[...]
    gate = jnp.dot(h, wg_ref[...].astype(BF16), preferred_element_type=F32)
    up = jnp.dot(h, wu_ref[...].astype(BF16), preferred_element_type=F32)
    o_ref[...] = (gate * jax.nn.sigmoid(gate) * up).astype(BF16)


def _ffn_up(x, g, w_gate, w_up):
    n = x.shape[0]
    return pl.pallas_call(
        _ffn_up_kernel,
        out_shape=jax.ShapeDtypeStruct((n, FFN_HID), BF16),
        grid=(n // UP_TM, FFN_HID // UP_TN),
        in_specs=[
            pl.BlockSpec((UP_TM, D_MODEL), lambda i, j: (i, 0)),
            pl.BlockSpec((1, D_MODEL), lambda i, j: (0, 0)),
            pl.BlockSpec((D_MODEL, UP_TN), lambda i, j: (0, j)),
            pl.BlockSpec((D_MODEL, UP_TN), lambda i, j: (0, j)),
        ],
        out_specs=pl.BlockSpec((UP_TM, UP_TN), lambda i, j: (i, j)),
        scratch_shapes=[pltpu.VMEM((UP_TM, D_MODEL), BF16)],
        compiler_params=_params(("parallel", "arbitrary")),
        name="ffn_up",
    )(x, g, w_gate, w_up)


PLE_TM, PLE_TN = 1024, 512


def _ple_kernel(x_ref, g_ref, wg_ref, p_ref, wp_ref, xt_ref, o_ref, h_ref):
    @pl.when(pl.program_id(1) == 0)
    def _():
        _norm_block(x_ref, g_ref, h_ref)

    gate = jnp.dot(h_ref[...], wg_ref[...].astype(BF16), preferred_element_type=F32)
    emb = jnp.dot(p_ref[...].astype(BF16), wp_ref[...].astype(BF16), preferred_element_type=F32)
    o_ref[...] = xt_ref[...] + jax.nn.sigmoid(gate) * emb


def _ple(x, g, w_gate, p, w_proj):
    n = x.shape[0]
    return pl.pallas_call(
        _ple_kernel,
        out_shape=jax.ShapeDtypeStruct((n, D_MODEL), F32),
        grid=(n // PLE_TM, D_MODEL // PLE_TN),
        in_specs=[
            pl.BlockSpec((PLE_TM, D_MODEL), lambda i, j: (i, 0)),
            pl.BlockSpec((1, D_MODEL), lambda i, j: (0, 0)),
            pl.BlockSpec((D_MODEL, PLE_TN), lambda i, j: (0, j)),
            pl.BlockSpec((PLE_TM, PLE_DIM), lambda i, j: (i, 0)),
            pl.BlockSpec((PLE_DIM, PLE_TN), lambda i, j: (0, j)),
            pl.BlockSpec((PLE_TM, PLE_TN), lambda i, j: (i, j)),
        ],
        out_specs=pl.BlockSpec((PLE_TM, PLE_TN), lambda i, j: (i, j)),
        scratch_shapes=[pltpu.VMEM((PLE_TM, D_MODEL), BF16)],
        compiler_params=_params(("parallel", "arbitrary")),
        name="ple",
    )(x, g, w_gate, p, w_proj, x)


def _df_cols(w):
    k = w.shape[0]
    return w.reshape(k, DF_HEADS, 2, 2, DF_DQ // 2).transpose(0, 1, 3, 2, 4).reshape(k, DF_QW)


def _rope_tables(seq, g, scale):
    half = DF_DQ // 2
    inv = 1.0 / (ROPE_THETA ** (jnp.arange(0, DF_DQ, 2, dtype=F32) / DF_DQ))
    ang = jnp.arange(seq, dtype=F32)[:, None] * inv[None, :]
    cos, sin = jnp.cos(ang), jnp.sin(ang)
    g1, g2 = g[:half], g[half:]
    ca = jnp.concatenate([cos * g1, cos * g1, cos * g2, cos * g2], axis=1) * scale
    sb = jnp.concatenate([-sin * g2, -sin * g2, sin * g1, sin * g1], axis=1) * scale
    return ca, sb


def kernel(x, p, g_mix, w_in, g_na_q, g_na_k, na_rpb, g_df_q, g_df_k, lam_q1, lam_k1, lam_q2, lam_k2,
           g_df_sub, w_na_out, w_df_out, w_o, g_ffn, w_gate, w_up, w_down, g_ple, w_ple_gate, w_ple_proj):
    batch, seq, d = x.shape
    n = batch * seq
    rows = seq // GRID_W
    depth = w_in.shape[0]
    xf = x.reshape(n, d)
    for i in range(depth):
        s0 = 3 * NA_W
        w_dfqk = jnp.concatenate([_df_cols(w_in[i][:, s0:s0 + DF_QW]), _df_cols(w_in[i][:, s0 + DF_QW:s0 + 2 * DF_QW])],
                                 axis=1)
        caq, sbq = _rope_tables(seq, g_df_q[i], DF_DQ ** -0.5 * math.log2(math.e))
        cak, sbk = _rope_tables(seq, g_df_k[i], 1.0)
        proj = _in_proj(xf, g_mix[i][None], w_in[i], w_dfqk, g_na_q[i][None], g_na_k[i][None], caq, sbq, cak, sbk,
                        seq)

        na_vt = proj[:, CB_NA_V * LANES:(CB_NA_V + NA_HEADS) * LANES].T
        o_a = _na_attn(proj, na_vt, _na_tables(na_rpb[i], rows), batch, seq)
        lam_p = jnp.stack([lam_q1[i], lam_k1[i], lam_q2[i], lam_k2[i]], axis=0)
        df_vt = proj[:, CB_DF_V * LANES:(CB_DF_V + DF_HEADS) * LANES].T
        o_b = _df_attn(proj, df_vt, lam_p, g_df_sub[i][None], batch, seq)

        merged = _merge(o_a, o_b, w_na_out[i], w_df_out[i], proj)
        xf = _residual_matmul(merged, w_o[i], xf, "out_proj")

        act = _ffn_up(xf, g_ffn[i][None], w_gate[i], w_up[i])
        xf = _residual_matmul(act, w_down[i], xf, "ffn_down", tn=256)

        xf = _ple(xf, g_ple[i][None], w_ple_gate[i], p[i].reshape(n, PLE_DIM), w_ple_proj[i])
    return xf.reshape(batch, seq, d)
```

```python
import functools
import math

import numpy as np
import jax
import jax.numpy as jnp
from jax import lax
from jax.experimental import pallas as pl
from jax.experimental.pallas import tpu as pltpu

D_MODEL = 2048
GRID_W = 64
NA_HEADS = 8
NA_DH = 128
NA_KR = 8
NA_KW = 16
DF_HEADS = 8
DF_DQ = 64
DF_DV = 128
FFN_HID = 5632
PLE_DIM = 256
ROPE_THETA = 10000.0
EPS = 1e-6
LAM_INIT = 0.8 - 0.6 * math.exp(-0.3 * 0)

NA_W = NA_HEADS * NA_DH
DF_QW = DF_HEADS * 2 * DF_DQ
DF_VW = DF_HEADS * DF_DV
IN_COLS = 3 * NA_W + 2 * DF_QW + DF_VW + 2 * D_MODEL

LANES = 128
NEG = -1e30
VMEM_LIMIT = 56 * 1024 * 1024

F32 = jnp.float32
BF16 = jnp.bfloat16

CB_NA_Q, CB_NA_K, CB_NA_V = 0, 8, 16
CB_DF_Q, CB_DF_K, CB_DF_V = 24, 32, 40
COL_GATE_A, COL_GATE_B = 6144, 8192


def _params(sem):
    return pltpu.CompilerParams(dimension_semantics=sem, vmem_limit_bytes=VMEM_LIMIT)


def _rms_rows(x, g):
    ms = jnp.mean(x * x, axis=-1, keepdims=True)
    return x * lax.rsqrt(ms + EPS) * g


ONES_ROWS = 16


def _with_ones_rows(v_t):
    row = lax.broadcasted_iota(jnp.int32, (ONES_ROWS, v_t.shape[1]), 0)
    return jnp.concatenate([v_t, jnp.where(row == 0, 1.0, 0.0).astype(v_t.dtype)], axis=0)


NORM_ROWS = 256


def _norm_block(x_ref, g_ref, h_ref):
    def body(c, carry):
        r = pl.ds(pl.multiple_of(c * NORM_ROWS, NORM_ROWS), NORM_ROWS)
        h_ref[r, :] = _rms_rows(x_ref[r, :], g_ref[...]).astype(BF16)
        return carry

    lax.fori_loop(0, x_ref.shape[0] // NORM_ROWS, body, 0)


IN_TM, IN_TN = 1024, 512
IN_RB = 256
SEG_TILES = 1024 // IN_TN
DF_TILE0 = 3 * SEG_TILES
DF_TILES = 2 * SEG_TILES


def _in_proj_kernel(x_ref, gmix_ref, w_ref, wdf_ref, gq_ref, gk_ref, caq_ref, sbq_ref, cak_ref, sbk_ref,
                    o_ref, h_ref):
    j = pl.program_id(1)

    @pl.when(j == 0)
    def _():
        _norm_block(x_ref, gmix_ref, h_ref)

    seg = j // SEG_TILES
    heads = IN_TN // LANES

    def run(wsrc_ref, epilogue):
        wb = wsrc_ref[...].astype(BF16)
        for r in range(IN_TM // IN_RB):
            rows = slice(r * IN_RB, (r + 1) * IN_RB)
            epilogue(jnp.dot(h_ref[rows, :], wb, preferred_element_type=F32), rows)

    def na_norm(g_ref):
        def epilogue(y, rows):
            for k in range(heads):
                cols = slice(k * LANES, (k + 1) * LANES)
                o_ref[rows, cols] = _rms_rows(y[:, cols], g_ref[...]).astype(BF16)
        return epilogue

    def df_norm_rope(ca_ref, sb_ref):
        def epilogue(y, rows):
            lane = lax.broadcasted_iota(jnp.int32, (1, LANES), 1)
            comp0 = (lane % 64) < 32
            ca = ca_ref[rows, :]
            sb = sb_ref[rows, :]
            for k in range(heads):
                cols = slice(k * LANES, (k + 1) * LANES)
                yk = y[:, cols]
                pk = pltpu.roll(yk, 64, 1)
                z = yk * yk + pk * pk
                s0 = jnp.sum(jnp.where(comp0, z, 0.0), axis=-1, keepdims=True)
                s1 = jnp.sum(jnp.where(comp0, 0.0, z), axis=-1, keepdims=True)
                r = jnp.where(comp0, lax.rsqrt(s0 * (0.5 / DF_DQ) + EPS), lax.rsqrt(s1 * (0.5 / DF_DQ) + EPS))
                o_ref[rows, cols] = ((yk * ca + pk * sb) * r).astype(BF16)
        return epilogue

    def plain(y, rows):
        o_ref[rows, :] = y.astype(BF16)

    def gate(y, rows):
        o_ref[rows, :] = jax.nn.sigmoid(y).astype(BF16)

    @pl.when(seg == 0)
    def _():
        run(w_ref, na_norm(gq_ref))

    @pl.when(seg == 1)
    def _():
        run(w_ref, na_norm(gk_ref))

    @pl.when(seg == 3)
    def _():
        run(wdf_ref, df_norm_rope(caq_ref, sbq_ref))

    @pl.when(seg == 4)
    def _():
        run(wdf_ref, df_norm_rope(cak_ref, sbk_ref))

    @pl.when((seg == 2) | (seg == 5))
    def _():
        run(w_ref, plain)

    @pl.when(seg >= 6)
    def _():
        run(w_ref, gate)


def _in_proj(x2, g_mix, w_in, w_dfqk, g_na_q, g_na_k, caq, sbq, cak, sbk, seq):
    n = x2.shape[0]
    pos_blocks = seq // IN_TM
    row = lambda i, j: (i, 0)
    const = lambda i, j: (0, 0)
    tab = lambda i, j: (i % pos_blocks, 0)
    is_df = lambda j: (j >= DF_TILE0) & (j < DF_TILE0 + DF_TILES)
    return pl.pallas_call(
        _in_proj_kernel,
        out_shape=jax.ShapeDtypeStruct((n, IN_COLS), BF16),
        grid=(n // IN_TM, IN_COLS // IN_TN),
        in_specs=[
            pl.BlockSpec((IN_TM, D_MODEL), row),
            pl.BlockSpec((1, D_MODEL), const),
            pl.BlockSpec((D_MODEL, IN_TN), lambda i, j: (0, jnp.where(is_df(j), DF_TILE0 - 1, j))),
            pl.BlockSpec((D_MODEL, IN_TN), lambda i, j: (0, jnp.clip(j - DF_TILE0, 0, DF_TILES - 1))),
            pl.BlockSpec((1, LANES), const),
            pl.BlockSpec((1, LANES), const),
            pl.BlockSpec((IN_TM, LANES), tab),
            pl.BlockSpec((IN_TM, LANES), tab),
            pl.BlockSpec((IN_TM, LANES), tab),
            pl.BlockSpec((IN_TM, LANES), tab),
        ],
        out_specs=pl.BlockSpec((IN_TM, IN_TN), lambda i, j: (i, j)),
        scratch_shapes=[pltpu.VMEM((IN_TM, D_MODEL), BF16)],
        compiler_params=_params(("parallel", "arbitrary")),
        name="in_proj",
    )(x2, g_mix, w_in, w_dfqk, g_na_q, g_na_k, caq, sbq, cak, sbk)


NA_QROWS = 8
NA_WROWS = 16
NA_TQ = NA_QROWS * GRID_W
NA_TK = NA_WROWS * GRID_W
NA_PAIR = LANES // GRID_W
NA_NPAIR = NA_QROWS // NA_PAIR
NA_PROWS = NA_KR + NA_PAIR - 1
NA_PK = NA_PROWS * GRID_W
NA_NTAB = 1 + 2 * NA_NPAIR
NA_NEG_SLAB = 2 * NA_KR - 1


def _na_geometry(rows):
    nt = rows // NA_QROWS
    assert nt >= 3 and rows >= NA_WROWS
    ws = np.zeros((3, NA_NPAIR), np.int64)
    idx = np.full((NA_NTAB, NA_PROWS, NA_PAIR), NA_NEG_SLAB, np.int64)
    for ci, t in enumerate((0, 1, nt - 1)):
        w0 = int(np.clip(t * NA_QROWS - NA_KR // 2, 0, rows - NA_WROWS))
        for pi in range(NA_NPAIR):
            r = [t * NA_QROWS + NA_PAIR * pi + hb for hb in range(NA_PAIR)]
            rs = [int(np.clip(ri - NA_KR // 2, 0, rows - NA_KR)) for ri in r]
            start = min(min(rs) - w0, NA_WROWS - NA_PROWS)
            assert 0 <= start and max(rs) + NA_KR <= w0 + start + NA_PROWS and min(rs) >= w0 + start
            ws[ci, pi] = start
            tab = 0 if ci == 1 else 1 + (0 if ci == 0 else NA_NPAIR) + pi
            for wp in range(NA_PROWS):
                key_row = w0 + start + wp
                for hb in range(NA_PAIR):
                    if rs[hb] <= key_row < rs[hb] + NA_KR:
                        idx[tab, wp, hb] = key_row - r[hb] + NA_KR - 1
    combos = sorted({tuple(int(v) for v in pair) for pair in idx.reshape(-1, NA_PAIR)})
    slab = [[combos.index(tuple(int(v) for v in idx[tab, wp])) for wp in range(NA_PROWS)] for tab in range(NA_NTAB)]
    return [[int(v) for v in row] for row in ws], combos, slab


def _na_kernel(q_ref, k_ref, v_ref, tab_ref, o_ref, p_ref, *, rows):
    nt = rows // NA_QROWS
    ws, _, slab = _na_geometry(rows)

    def window(t):
        return int(np.clip(t * NA_QROWS - NA_KR // 2, 0, rows - NA_WROWS)) * GRID_W

    def probs(t):
        cls = 0 if t == 0 else (2 if t == nt - 1 else 1)
        buf = t
        p_ref[buf] = jnp.zeros(p_ref.shape[1:], BF16)
        for pi in range(NA_NPAIR):
            lanes = slice(pi * LANES, (pi + 1) * LANES)
            tab = 0 if cls == 1 else 1 + (0 if cls == 0 else NA_NPAIR) + pi
            r0 = ws[cls][pi] * GRID_W
            kp = k_ref[window(t) + r0:window(t) + r0 + NA_PK, :]
            qp = q_ref[t * NA_TQ + pi * LANES:t * NA_TQ + (pi + 1) * LANES, :]
            sp = lax.dot_general(kp, qp, (((1,), (1,)), ((), ())), preferred_element_type=F32)
            bias = jnp.concatenate([tab_ref[0, ci] for ci in slab[tab]], axis=0)
            sp = sp * (NA_DH ** -0.5) + bias
            m = jnp.max(sp, axis=0, keepdims=True)
            p_ref[buf, r0:r0 + NA_PK, lanes] = jnp.exp(sp - m).astype(BF16)

    def outputs(t):
        vw = _with_ones_rows(v_ref[window(t):window(t) + NA_TK, :].T)
        ov = jnp.dot(vw, p_ref[t], preferred_element_type=F32)
        o_ref[t * NA_TQ:(t + 1) * NA_TQ, :] = (ov[0:NA_DH, :] / ov[NA_DH:NA_DH + 1, :]).T.astype(BF16)

    for t in range(nt + 1):
        if t < nt:
            probs(t)
        if t >= 1:
            outputs(t - 1)


def _na_attn(proj, tabs, batch, seq):
    rows = seq // GRID_W
    return pl.pallas_call(
        functools.partial(_na_kernel, rows=rows),
        out_shape=jax.ShapeDtypeStruct((batch * seq, NA_W), BF16),
        grid=(batch, NA_HEADS),
        in_specs=[
            pl.BlockSpec((seq, LANES), lambda b, h: (b, CB_NA_Q + h)),
            pl.BlockSpec((seq, LANES), lambda b, h: (b, CB_NA_K + h)),
            pl.BlockSpec((seq, LANES), lambda b, h: (b, CB_NA_V + h)),
            pl.BlockSpec((1,) + tabs.shape[1:], lambda b, h: (h, 0, 0, 0)),
        ],
        out_specs=pl.BlockSpec((seq, LANES), lambda b, h: (b, h)),
        scratch_shapes=[pltpu.VMEM((rows // NA_QROWS, NA_TK, NA_TQ), BF16)],
        compiler_params=_params(("parallel", "arbitrary")),
        name="na_attn",
    )(proj, proj, proj, tabs)


def _na_tables(rpb, rows):
    c = np.arange(GRID_W)
    cs = np.clip(c - NA_KW // 2, 0, GRID_W - NA_KW)
    valid_c = (c[None, :] >= cs[:, None]) & (c[None, :] < cs[:, None] + NA_KW)
    dc = c[:, None] - c[None, :] + NA_KW - 1
    onehot = (dc[None] == np.arange(2 * NA_KW - 1)[:, None, None]) & valid_c.T[None]
    mask = np.where(valid_c.T, 0.0, NEG).astype(np.float32)
    half = jnp.sum(rpb[:, :, :, None, None] * jnp.asarray(onehot, F32)[None, None], axis=2) + jnp.asarray(mask)
    half = jnp.concatenate([half, jnp.full((NA_HEADS, 1, GRID_W, GRID_W), NEG, F32)], axis=1)
    _, combos, _ = _na_geometry(rows)
    left = jnp.take(half, jnp.asarray([cb[0] for cb in combos], jnp.int32), axis=1)
    right = jnp.take(half, jnp.asarray([cb[1] for cb in combos], jnp.int32), axis=1)
    return jnp.concatenate([left, right], axis=-1)


DF_TQ, DF_TK = 1024, 512
DF_TG = 256
DF_VROWS = DF_DV + ONES_ROWS


def _df_kernel(q_ref, k_ref, v_ref, lam_ref, gsub_ref, o_ref, q12_ref, s_ref, acc_ref, *, seq):
    lane = lax.broadcasted_iota(jnp.int32, (1, LANES), 1)
    comp0 = (lane % 64) < 32
    q = q_ref[...]
    zero = jnp.zeros_like(q)
    q12_ref[0:DF_TQ, :] = jnp.where(comp0, q, zero)
    q12_ref[DF_TQ:2 * DF_TQ, :] = jnp.where(comp0, zero, q)
    acc_ref[...] = jnp.zeros(acc_ref.shape, F32)

    groups = 2 * DF_TQ // DF_TG
    n_chunks = seq // DF_TK

    def scores(kc, g):
        cols = slice(g * DF_TG, (g + 1) * DF_TG)
        return lax.dot_general(kc, q12_ref[cols, :], (((1,), (1,)), ((), ())), preferred_element_type=F32)

    def keys(c):
        return k_ref[c * DF_TK:(c + 1) * DF_TK, :]

    def step(c, carry):
        cur = c % 2
        vc = _with_ones_rows(v_ref[c * DF_TK:(c + 1) * DF_TK, :].T)
        kn = keys(c + 1) if c + 1 < n_chunks else None
        out = []
        for g in range(groups):
            m_prev = carry[g]
            cols = slice(g * DF_TG, (g + 1) * DF_TG)
            if kn is not None:
                s_ref[1 - cur, :, cols] = scores(kn, g)
            s = s_ref[cur, :, cols]
            m_new = jnp.maximum(m_prev, jnp.max(s, axis=0, keepdims=True))
            alpha = jnp.exp2(m_prev - m_new)
            p = jnp.exp2(s - m_new).astype(BF16)
            pv = jnp.dot(vc, p, preferred_element_type=F32)
            acc_ref[:, cols] = alpha * acc_ref[:, cols] + pv
            out.append(m_new)
        return tuple(out)

    k0 = keys(0)
    for g in range(groups):
        s_ref[0, :, g * DF_TG:(g + 1) * DF_TG] = scores(k0, g)

    carry = tuple(jnp.full((1, DF_TG), -jnp.inf, F32) for _ in range(groups))
    for c in range(n_chunks):
        carry = step(c, carry)

    lp = lam_ref[...]
    lam = (jnp.exp(jnp.sum(lp[0:1] * lp[1:2], axis=-1, keepdims=True))
           - jnp.exp(jnp.sum(lp[2:3] * lp[3:4], axis=-1, keepdims=True)) + LAM_INIT)
    o12 = acc_ref[0:DF_DV, :] / acc_ref[DF_DV:DF_DV + 1, :]
    o_t = o12[:, 0:DF_TQ] - lam * o12[:, DF_TQ:2 * DF_TQ]
    o_ref[...] = (_rms_rows(o_t.T, gsub_ref[...]) * (1.0 - LAM_INIT)).astype(BF16)


def _df_attn(proj, lam_p, g_sub, batch, seq):
    nq = seq // DF_TQ
    const = lambda b, h, i: (0, 0)
    return pl.pallas_call(
        functools.partial(_df_kernel, seq=seq),
        out_shape=jax.ShapeDtypeStruct((batch * seq, DF_VW), BF16),
        grid=(batch, DF_HEADS, nq),
        in_specs=[
            pl.BlockSpec((DF_TQ, LANES), lambda b, h, i: (b * nq + i, CB_DF_Q + h)),
            pl.BlockSpec((seq, LANES), lambda b, h, i: (b, CB_DF_K + h)),
            pl.BlockSpec((seq, LANES), lambda b, h, i: (b, CB_DF_V + h)),
            pl.BlockSpec((4, DF_DQ), const),
            pl.BlockSpec((1, DF_DV), const),
        ],
        out_specs=pl.BlockSpec((DF_TQ, LANES), lambda b, h, i: (b * nq + i, h)),
        scratch_shapes=[pltpu.VMEM((2 * DF_TQ, LANES), BF16), pltpu.VMEM((2, DF_TK, 2 * DF_TQ), F32),
                        pltpu.VMEM((DF_VROWS, 2 * DF_TQ), F32)],
        compiler_params=_params(("parallel", "parallel", "arbitrary")),
        name="df_attn",
    )(proj, proj, proj, lam_p, g_sub)


MG_TM, MG_TN = 1024, 512


def _merge_kernel(oa_ref, ob_ref, wa_ref, wb_ref, sa_ref, sb_ref, o_ref):
    ya = jnp.dot(oa_ref[...], wa_ref[...].astype(BF16), preferred_element_type=F32)
    yb = jnp.dot(ob_ref[...], wb_ref[...].astype(BF16), preferred_element_type=F32)
    o_ref[...] = (sa_ref[...].astype(F32) * ya + sb_ref[...].astype(F32) * yb).astype(BF16)


def _merge(o_a, o_b, w_na_out, w_df_out, proj):
    n = o_a.shape[0]
    ga, gb = COL_GATE_A // MG_TN, COL_GATE_B // MG_TN
    return pl.pallas_call(
        _merge_kernel,
        out_shape=jax.ShapeDtypeStruct((n, D_MODEL), BF16),
        grid=(n // MG_TM, D_MODEL // MG_TN),
        in_specs=[
            pl.BlockSpec((MG_TM, NA_W), lambda i, j: (i, 0)),
            pl.BlockSpec((MG_TM, DF_VW), lambda i, j: (i, 0)),
            pl.BlockSpec((NA_W, MG_TN), lambda i, j: (0, j)),
            pl.BlockSpec((DF_VW, MG_TN), lambda i, j: (0, j)),
            pl.BlockSpec((MG_TM, MG_TN), lambda i, j: (i, ga + j)),
            pl.BlockSpec((MG_TM, MG_TN), lambda i, j: (i, gb + j)),
        ],
        out_specs=pl.BlockSpec((MG_TM, MG_TN), lambda i, j: (i, j)),
        compiler_params=_params(("parallel", "arbitrary")),
        name="merge",
    )(o_a, o_b, w_na_out, w_df_out, proj, proj)


RS_TM, RS_TN = 1024, 512


def _residual_matmul_kernel(a_ref, w_ref, x_ref, o_ref):
    o_ref[...] = x_ref[...] + jnp.dot(a_ref[...], w_ref[...].astype(BF16), preferred_element_type=F32)


def _residual_matmul(a, w, x, name, tm=RS_TM, tn=RS_TN):
    n, k = a.shape
    d = w.shape[1]
    return pl.pallas_call(
        _residual_matmul_kernel,
        out_shape=jax.ShapeDtypeStruct((n, d), F32),
        grid=(n // tm, d // tn),
        in_specs=[
            pl.BlockSpec((tm, k), lambda i, j: (i, 0)),
            pl.BlockSpec((k, tn), lambda i, j: (0, j)),
            pl.BlockSpec((tm, tn), lambda i, j: (i, j)),
        ],
        out_specs=pl.BlockSpec((tm, tn), lambda i, j: (i, j)),
        compiler_params=_params(("parallel", "arbitrary")),
        name=name,
    )(a, w, x)


UP_TM, UP_TN = 1024, 512


def _ffn_up_kernel(x_ref, g_ref, wg_ref, wu_ref, o_ref, h_ref):
    @pl.when(pl.program_id(1) == 0)
    def _():
        _norm_block(x_ref, g_ref, h_ref)

    h = h_ref[...]
    gate = jnp.dot(h, wg_ref[...].astype(BF16), preferred_element_type=F32)
    up = jnp.dot(h, wu_ref[...].astype(BF16), preferred_element_type=F32)
    o_ref[...] = (gate * jax.nn.sigmoid(gate) * up).astype(BF16)


def _ffn_up(x, g, w_gate, w_up):
    n = x.shape[0]
    return pl.pallas_call(
        _ffn_up_kernel,
        out_shape=jax.ShapeDtypeStruct((n, FFN_HID), BF16),
        grid=(n // UP_TM, FFN_HID // UP_TN),
        in_specs=[
            pl.BlockSpec((UP_TM, D_MODEL), lambda i, j: (i, 0)),
            pl.BlockSpec((1, D_MODEL), lambda i, j: (0, 0)),
            pl.BlockSpec((D_MODEL, UP_TN), lambda i, j: (0, j)),
            pl.BlockSpec((D_MODEL, UP_TN), lambda i, j: (0, j)),
        ],
        out_specs=pl.BlockSpec((UP_TM, UP_TN), lambda i, j: (i, j)),
        scratch_shapes=[pltpu.VMEM((UP_TM, D_MODEL), BF16)],
        compiler_params=_params(("parallel", "arbitrary")),
        name="ffn_up",
    )(x, g, w_gate, w_up)


PLE_TM, PLE_TN = 1024, 512


def _ple_kernel(x_ref, g_ref, wg_ref, p_ref, wp_ref, xt_ref, o_ref, h_ref):
    @pl.when(pl.program_id(1) == 0)
    def _():
        _norm_block(x_ref, g_ref, h_ref)

    gate = jnp.dot(h_ref[...], wg_ref[...].astype(BF16), preferred_element_type=F32)
    emb = jnp.dot(p_ref[...].astype(BF16), wp_ref[...].astype(BF16), preferred_element_type=F32)
    o_ref[...] = xt_ref[...] + jax.nn.sigmoid(gate) * emb


def _ple(x, g, w_gate, p, w_proj):
    n = x.shape[0]
    return pl.pallas_call(
        _ple_kernel,
        out_shape=jax.ShapeDtypeStruct((n, D_MODEL), F32),
        grid=(n // PLE_TM, D_MODEL // PLE_TN),
        in_specs=[
            pl.BlockSpec((PLE_TM, D_MODEL), lambda i, j: (i, 0)),
            pl.BlockSpec((1, D_MODEL), lambda i, j: (0, 0)),
            pl.BlockSpec((D_MODEL, PLE_TN), lambda i, j: (0, j)),
            pl.BlockSpec((PLE_TM, PLE_DIM), lambda i, j: (i, 0)),
            pl.BlockSpec((PLE_DIM, PLE_TN), lambda i, j: (0, j)),
            pl.BlockSpec((PLE_TM, PLE_TN), lambda i, j: (i, j)),
        ],
        out_specs=pl.BlockSpec((PLE_TM, PLE_TN), lambda i, j: (i, j)),
        scratch_shapes=[pltpu.VMEM((PLE_TM, D_MODEL), BF16)],
        compiler_params=_params(("parallel", "arbitrary")),
        name="ple",
    )(x, g, w_gate, p, w_proj, x)


def _df_cols(w):
    k = w.shape[0]
    return w.reshape(k, DF_HEADS, 2, 2, DF_DQ // 2).transpose(0, 1, 3, 2, 4).reshape(k, DF_QW)


def _rope_tables(seq, g, scale):
    half = DF_DQ // 2
    inv = 1.0 / (ROPE_THETA ** (jnp.arange(0, DF_DQ, 2, dtype=F32) / DF_DQ))
    ang = jnp.arange(seq, dtype=F32)[:, None] * inv[None, :]
    cos, sin = jnp.cos(ang), jnp.sin(ang)
    g1, g2 = g[:half], g[half:]
    ca = jnp.concatenate([cos * g1, cos * g1, cos * g2, cos * g2], axis=1) * scale
    sb = jnp.concatenate([-sin * g2, -sin * g2, sin * g1, sin * g1], axis=1) * scale
    return ca, sb


def kernel(x, p, g_mix, w_in, g_na_q, g_na_k, na_rpb, g_df_q, g_df_k, lam_q1, lam_k1, lam_q2, lam_k2,
           g_df_sub, w_na_out, w_df_out, w_o, g_ffn, w_gate, w_up, w_down, g_ple, w_ple_gate, w_ple_proj):
    batch, seq, d = x.shape
    n = batch * seq
    rows = seq // GRID_W
    depth = w_in.shape[0]
    xf = x.reshape(n, d)
    for i in range(depth):
        s0 = 3 * NA_W
        w_dfqk = jnp.concatenate([_df_cols(w_in[i][:, s0:s0 + DF_QW]), _df_cols(w_in[i][:, s0 + DF_QW:s0 + 2 * DF_QW])],
                                 axis=1)
        caq, sbq = _rope_tables(seq, g_df_q[i], DF_DQ ** -0.5 * math.log2(math.e))
        cak, sbk = _rope_tables(seq, g_df_k[i], 1.0)
        proj = _in_proj(xf, g_mix[i][None], w_in[i], w_dfqk, g_na_q[i][None], g_na_k[i][None], caq, sbq, cak, sbk,
                        seq)

        o_a = _na_attn(proj, _na_tables(na_rpb[i], rows), batch, seq)
        lam_p = jnp.stack([lam_q1[i], lam_k1[i], lam_q2[i], lam_k2[i]], axis=0)
        o_b = _df_attn(proj, lam_p, g_df_sub[i][None], batch, seq)

        merged = _merge(o_a, o_b, w_na_out[i], w_df_out[i], proj)
        xf = _residual_matmul(merged, w_o[i], xf, "out_proj")

        act = _ffn_up(xf, g_ffn[i][None], w_gate[i], w_up[i])
        xf = _residual_matmul(act, w_down[i], xf, "ffn_down", tn=256)

        xf = _ple(xf, g_ple[i][None], w_ple_gate[i], p[i].reshape(n, PLE_DIM), w_ple_proj[i])
    return xf.reshape(batch, seq, d)
```

```python
import functools
import math

import numpy as np
import jax
import jax.numpy as jnp
from jax import lax
from jax.experimental import pallas as pl
from jax.experimental.pallas import tpu as pltpu

D_MODEL = 2048
GRID_W = 64
NA_HEADS = 8
NA_DH = 128
NA_KR = 8
NA_KW = 16
DF_HEADS = 8
DF_DQ = 64
DF_DV = 128
FFN_HID = 5632
PLE_DIM = 256
ROPE_THETA = 10000.0
EPS = 1e-6
LAM_INIT = 0.8 - 0.6 * math.exp(-0.3 * 0)

NA_W = NA_HEADS * NA_DH
DF_QW = DF_HEADS * 2 * DF_DQ
DF_VW = DF_HEADS * DF_DV
IN_COLS = 3 * NA_W + 2 * DF_QW + DF_VW + 2 * D_MODEL

LANES = 128
NEG = -1e30
VMEM_LIMIT = 56 * 1024 * 1024

F32 = jnp.float32
BF16 = jnp.bfloat16

CB_NA_Q, CB_NA_K, CB_NA_V = 0, 8, 16
CB_DF_Q, CB_DF_K, CB_DF_V = 24, 32, 40
COL_GATE_A, COL_GATE_B = 6144, 8192


def _params(sem):
    return pltpu.CompilerParams(dimension_semantics=sem, vmem_limit_bytes=VMEM_LIMIT)


def _rms_rows(x, g):
    ms = jnp.mean(x * x, axis=-1, keepdims=True)
    return x * lax.rsqrt(ms + EPS) * g


ONES_ROWS = 16


def _with_ones_rows(v_t):
    row = lax.broadcasted_iota(jnp.int32, (ONES_ROWS, v_t.shape[1]), 0)
    return jnp.concatenate([v_t, jnp.where(row == 0, 1.0, 0.0).astype(v_t.dtype)], axis=0)


NORM_ROWS = 256


def _norm_block(x_ref, g_ref, h_ref):
    def body(c, carry):
        r = pl.ds(pl.multiple_of(c * NORM_ROWS, NORM_ROWS), NORM_ROWS)
        h_ref[r, :] = _rms_rows(x_ref[r, :], g_ref[...]).astype(BF16)
        return carry

    lax.fori_loop(0, x_ref.shape[0] // NORM_ROWS, body, 0)


def _rms_norm_kernel(x_ref, g_ref, o_ref):
    _norm_block(x_ref, g_ref, o_ref)


def _rms_norm(x, g, tm=512):
    n, d = x.shape
    return pl.pallas_call(
        _rms_norm_kernel,
        out_shape=jax.ShapeDtypeStruct((n, d), BF16),
        grid=(n // tm,),
        in_specs=[pl.BlockSpec((tm, d), lambda i: (i, 0)), pl.BlockSpec((1, d), lambda i: (0, 0))],
        out_specs=pl.BlockSpec((tm, d), lambda i: (i, 0)),
        compiler_params=_params(("parallel",)),
        name="rms_norm",
    )(x, g)


IN_TM, IN_TN = 2048, 512
IN_RB = 256
SEG_TILES = 1024 // IN_TN
DF_TILE0 = 3 * SEG_TILES
DF_TILES = 2 * SEG_TILES


def _in_proj_kernel(h_ref, w_ref, wdf_ref, gq_ref, gk_ref, caq_ref, sbq_ref, cak_ref, sbk_ref, o_ref):
    seg = pl.program_id(1) // SEG_TILES
    heads = IN_TN // LANES

    def run(wsrc_ref, epilogue):
        wb = wsrc_ref[...].astype(BF16)
        for r in range(IN_TM // IN_RB):
            rows = slice(r * IN_RB, (r + 1) * IN_RB)
            epilogue(jnp.dot(h_ref[rows, :], wb, preferred_element_type=F32), rows)

    def na_norm(g_ref):
        def epilogue(y, rows):
            for k in range(heads):
                cols = slice(k * LANES, (k + 1) * LANES)
                o_ref[rows, cols] = _rms_rows(y[:, cols], g_ref[...]).astype(BF16)
        return epilogue

    def df_norm_rope(ca_ref, sb_ref):
        def epilogue(y, rows):
            lane = lax.broadcasted_iota(jnp.int32, (1, LANES), 1)
            comp0 = (lane % 64) < 32
            ca = ca_ref[rows, :]
            sb = sb_ref[rows, :]
            for k in range(heads):
                cols = slice(k * LANES, (k + 1) * LANES)
                yk = y[:, cols]
                pk = pltpu.roll(yk, 64, 1)
                z = yk * yk + pk * pk
                s0 = jnp.sum(jnp.where(comp0, z, 0.0), axis=-1, keepdims=True)
                s1 = jnp.sum(jnp.where(comp0, 0.0, z), axis=-1, keepdims=True)
                r = jnp.where(comp0, lax.rsqrt(s0 * (0.5 / DF_DQ) + EPS), lax.rsqrt(s1 * (0.5 / DF_DQ) + EPS))
                o_ref[rows, cols] = ((yk * ca + pk * sb) * r).astype(BF16)
        return epilogue

    def plain(y, rows):
        o_ref[rows, :] = y.astype(BF16)

    def gate(y, rows):
        o_ref[rows, :] = jax.nn.sigmoid(y).astype(BF16)

    @pl.when(seg == 0)
    def _():
        run(w_ref, na_norm(gq_ref))

    @pl.when(seg == 1)
    def _():
        run(w_ref, na_norm(gk_ref))

    @pl.when(seg == 3)
    def _():
        run(wdf_ref, df_norm_rope(caq_ref, sbq_ref))

    @pl.when(seg == 4)
    def _():
        run(wdf_ref, df_norm_rope(cak_ref, sbk_ref))

    @pl.when((seg == 2) | (seg == 5))
    def _():
        run(w_ref, plain)

    @pl.when(seg >= 6)
    def _():
        run(w_ref, gate)


def _in_proj(h, w_in, w_dfqk, g_na_q, g_na_k, caq, sbq, cak, sbk, seq):
    n = h.shape[0]
    pos_blocks = seq // IN_TM
    row = lambda i, j: (i, 0)
    const = lambda i, j: (0, 0)
    tab = lambda i, j: (i % pos_blocks, 0)
    is_df = lambda j: (j >= DF_TILE0) & (j < DF_TILE0 + DF_TILES)
    return pl.pallas_call(
        _in_proj_kernel,
        out_shape=jax.ShapeDtypeStruct((n, IN_COLS), BF16),
        grid=(n // IN_TM, IN_COLS // IN_TN),
        in_specs=[
            pl.BlockSpec((IN_TM, D_MODEL), row),
            pl.BlockSpec((D_MODEL, IN_TN), lambda i, j: (0, jnp.where(is_df(j), DF_TILE0 - 1, j))),
            pl.BlockSpec((D_MODEL, IN_TN), lambda i, j: (0, jnp.clip(j - DF_TILE0, 0, DF_TILES - 1))),
            pl.BlockSpec((1, LANES), const),
            pl.BlockSpec((1, LANES), const),
            pl.BlockSpec((IN_TM, LANES), tab),
            pl.BlockSpec((IN_TM, LANES), tab),
            pl.BlockSpec((IN_TM, LANES), tab),
            pl.BlockSpec((IN_TM, LANES), tab),
        ],
        out_specs=pl.BlockSpec((IN_TM, IN_TN), lambda i, j: (i, j)),
        compiler_params=_params(("parallel", "arbitrary")),
        name="in_proj",
    )(h, w_in, w_dfqk, g_na_q, g_na_k, caq, sbq, cak, sbk)


NA_QROWS = 8
NA_WROWS = 16
NA_TQ = NA_QROWS * GRID_W
NA_TK = NA_WROWS * GRID_W
NA_PAIR = LANES // GRID_W
NA_NPAIR = NA_QROWS // NA_PAIR
NA_PROWS = NA_KR + NA_PAIR - 1
NA_PK = NA_PROWS * GRID_W
NA_NTAB = 1 + 2 * NA_NPAIR
NA_NEG_SLAB = 2 * NA_KR - 1


def _na_geometry(rows):
    nt = rows // NA_QROWS
    assert nt >= 3 and rows >= NA_WROWS
    ws = np.zeros((3, NA_NPAIR), np.int64)
    idx = np.full((NA_NTAB, NA_PROWS, NA_PAIR), NA_NEG_SLAB, np.int64)
    for ci, t in enumerate((0, 1, nt - 1)):
        w0 = int(np.clip(t * NA_QROWS - NA_KR // 2, 0, rows - NA_WROWS))
        for pi in range(NA_NPAIR):
            r = [t * NA_QROWS + NA_PAIR * pi + hb for hb in range(NA_PAIR)]
            rs = [int(np.clip(ri - NA_KR // 2, 0, rows - NA_KR)) for ri in r]
            start = min(min(rs) - w0, NA_WROWS - NA_PROWS)
            assert 0 <= start and max(rs) + NA_KR <= w0 + start + NA_PROWS and min(rs) >= w0 + start
            ws[ci, pi] = start
            tab = 0 if ci == 1 else 1 + (0 if ci == 0 else NA_NPAIR) + pi
            for wp in range(NA_PROWS):
                key_row = w0 + start + wp
                for hb in range(NA_PAIR):
                    if rs[hb] <= key_row < rs[hb] + NA_KR:
                        idx[tab, wp, hb] = key_row - r[hb] + NA_KR - 1
    combos = sorted({tuple(int(v) for v in pair) for pair in idx.reshape(-1, NA_PAIR)})
    slab = [[combos.index(tuple(int(v) for v in idx[tab, wp])) for wp in range(NA_PROWS)] for tab in range(NA_NTAB)]
    return [[int(v) for v in row] for row in ws], combos, slab


def _na_kernel(q_ref, k_ref, v_ref, tab_ref, o_ref, p_ref, *, rows):
    nt = rows // NA_QROWS
    ws, _, slab = _na_geometry(rows)

    def window(t):
        return int(np.clip(t * NA_QROWS - NA_KR // 2, 0, rows - NA_WROWS)) * GRID_W

    def probs(t):
        cls = 0 if t == 0 else (2 if t == nt - 1 else 1)
        buf = t
        p_ref[buf] = jnp.zeros(p_ref.shape[1:], BF16)
        for pi in range(NA_NPAIR):
            lanes = slice(pi * LANES, (pi + 1) * LANES)
            tab = 0 if cls == 1 else 1 + (0 if cls == 0 else NA_NPAIR) + pi
            r0 = ws[cls][pi] * GRID_W
            kp = k_ref[window(t) + r0:window(t) + r0 + NA_PK, :]
            qp = q_ref[t * NA_TQ + pi * LANES:t * NA_TQ + (pi + 1) * LANES, :]
            sp = lax.dot_general(kp, qp, (((1,), (1,)), ((), ())), preferred_element_type=F32)
            bias = jnp.concatenate([tab_ref[0, ci] for ci in slab[tab]], axis=0)
            sp = sp * (NA_DH ** -0.5) + bias
            m = jnp.max(sp, axis=0, keepdims=True)
            p_ref[buf, r0:r0 + NA_PK, lanes] = jnp.exp(sp - m).astype(BF16)

    def outputs(t):
        vw = _with_ones_rows(v_ref[window(t):window(t) + NA_TK, :].T)
        ov = jnp.dot(vw, p_ref[t], preferred_element_type=F32)
        o_ref[t * NA_TQ:(t + 1) * NA_TQ, :] = (ov[0:NA_DH, :] / ov[NA_DH:NA_DH + 1, :]).T.astype(BF16)

    for t in range(nt + 1):
        if t < nt:
            probs(t)
        if t >= 1:
            outputs(t - 1)


def _na_attn(proj, tabs, batch, seq):
    rows = seq // GRID_W
    return pl.pallas_call(
        functools.partial(_na_kernel, rows=rows),
        out_shape=jax.ShapeDtypeStruct((batch * seq, NA_W), BF16),
        grid=(batch, NA_HEADS),
        in_specs=[
            pl.BlockSpec((seq, LANES), lambda b, h: (b, CB_NA_Q + h)),
            pl.BlockSpec((seq, LANES), lambda b, h: (b, CB_NA_K + h)),
            pl.BlockSpec((seq, LANES), lambda b, h: (b, CB_NA_V + h)),
            pl.BlockSpec((1,) + tabs.shape[1:], lambda b, h: (h, 0, 0, 0)),
        ],
        out_specs=pl.BlockSpec((seq, LANES), lambda b, h: (b, h)),
        scratch_shapes=[pltpu.VMEM((rows // NA_QROWS, NA_TK, NA_TQ), BF16)],
        compiler_params=_params(("parallel", "arbitrary")),
        name="na_attn",
    )(proj, proj, proj, tabs)


def _na_tables(rpb, rows):
    c = np.arange(GRID_W)
    cs = np.clip(c - NA_KW // 2, 0, GRID_W - NA_KW)
    valid_c = (c[None, :] >= cs[:, None]) & (c[None, :] < cs[:, None] + NA_KW)
    dc = c[:, None] - c[None, :] + NA_KW - 1
    onehot = (dc[None] == np.arange(2 * NA_KW - 1)[:, None, None]) & valid_c.T[None]
    mask = np.where(valid_c.T, 0.0, NEG).astype(np.float32)
    half = jnp.sum(rpb[:, :, :, None, None] * jnp.asarray(onehot, F32)[None, None], axis=2) + jnp.asarray(mask)
    half = jnp.concatenate([half, jnp.full((NA_HEADS, 1, GRID_W, GRID_W), NEG, F32)], axis=1)
    _, combos, _ = _na_geometry(rows)
    left = jnp.take(half, jnp.asarray([cb[0] for cb in combos], jnp.int32), axis=1)
    right = jnp.take(half, jnp.asarray([cb[1] for cb in combos], jnp.int32), axis=1)
    return jnp.concatenate([left, right], axis=-1)


DF_TQ, DF_TK = 1024, 512
DF_TG = 256
DF_VROWS = DF_DV + ONES_ROWS


def _df_kernel(q_ref, k_ref, v_ref, lam_ref, gsub_ref, o_ref, q12_ref, s_ref, acc_ref, *, seq):
    lane = lax.broadcasted_iota(jnp.int32, (1, LANES), 1)
    comp0 = (lane % 64) < 32
    q = q_ref[...]
    zero = jnp.zeros_like(q)
    q12_ref[0:DF_TQ, :] = jnp.where(comp0, q, zero)
    q12_ref[DF_TQ:2 * DF_TQ, :] = jnp.where(comp0, zero, q)
    acc_ref[...] = jnp.zeros(acc_ref.shape, F32)

    groups = 2 * DF_TQ // DF_TG
    n_chunks = seq // DF_TK

    def scores(kc, g):
        cols = slice(g * DF_TG, (g + 1) * DF_TG)
        return lax.dot_general(kc, q12_ref[cols, :], (((1,), (1,)), ((), ())), preferred_element_type=F32)

    def keys(c):
        return k_ref[c * DF_TK:(c + 1) * DF_TK, :]

    def step(c, carry):
        cur = c % 2
        vc = _with_ones_rows(v_ref[c * DF_TK:(c + 1) * DF_TK, :].T)
        kn = keys(c + 1) if c + 1 < n_chunks else None
        out = []
        for g in range(groups):
            m_prev = carry[g]
            cols = slice(g * DF_TG, (g + 1) * DF_TG)
            if kn is not None:
                s_ref[1 - cur, :, cols] = scores(kn, g)
            s = s_ref[cur, :, cols]
            m_new = jnp.maximum(m_prev, jnp.max(s, axis=0, keepdims=True))
            alpha = jnp.exp2(m_prev - m_new)
            p = jnp.exp2(s - m_new).astype(BF16)
            pv = jnp.dot(vc, p, preferred_element_type=F32)
            acc_ref[:, cols] = alpha * acc_ref[:, cols] + pv
            out.append(m_new)
        return tuple(out)

    k0 = keys(0)
    for g in range(groups):
        s_ref[0, :, g * DF_TG:(g + 1) * DF_TG] = scores(k0, g)

    carry = tuple(jnp.full((1, DF_TG), -jnp.inf, F32) for _ in range(groups))
    for c in range(n_chunks):
        carry = step(c, carry)

    lp = lam_ref[...]
    lam = (jnp.exp(jnp.sum(lp[0:1] * lp[1:2], axis=-1, keepdims=True))
           - jnp.exp(jnp.sum(lp[2:3] * lp[3:4], axis=-1, keepdims=True)) + LAM_INIT)
    o12 = acc_ref[0:DF_DV, :] / acc_ref[DF_DV:DF_DV + 1, :]
    o_t = o12[:, 0:DF_TQ] - lam * o12[:, DF_TQ:2 * DF_TQ]
    o_ref[...] = (_rms_rows(o_t.T, gsub_ref[...]) * (1.0 - LAM_INIT)).astype(BF16)


def _df_attn(proj, lam_p, g_sub, batch, seq):
    nq = seq // DF_TQ
    const = lambda b, h, i: (0, 0)
    return pl.pallas_call(
        functools.partial(_df_kernel, seq=seq),
        out_shape=jax.ShapeDtypeStruct((batch * seq, DF_VW), BF16),
        grid=(batch, DF_HEADS, nq),
        in_specs=[
            pl.BlockSpec((DF_TQ, LANES), lambda b, h, i: (b * nq + i, CB_DF_Q + h)),
            pl.BlockSpec((seq, LANES), lambda b, h, i: (b, CB_DF_K + h)),
            pl.BlockSpec((seq, LANES), lambda b, h, i: (b, CB_DF_V + h)),
            pl.BlockSpec((4, DF_DQ), const),
            pl.BlockSpec((1, DF_DV), const),
        ],
        out_specs=pl.BlockSpec((DF_TQ, LANES), lambda b, h, i: (b * nq + i, h)),
        scratch_shapes=[pltpu.VMEM((2 * DF_TQ, LANES), BF16), pltpu.VMEM((2, DF_TK, 2 * DF_TQ), F32),
                        pltpu.VMEM((DF_VROWS, 2 * DF_TQ), F32)],
        compiler_params=_params(("parallel", "parallel", "arbitrary")),
        name="df_attn",
    )(proj, proj, proj, lam_p, g_sub)


MG_TM, MG_TN = 2048, 512


def _merge_kernel(oa_ref, ob_ref, wa_ref, wb_ref, sa_ref, sb_ref, o_ref):
    ya = jnp.dot(oa_ref[...], wa_ref[...].astype(BF16), preferred_element_type=F32)
    yb = jnp.dot(ob_ref[...], wb_ref[...].astype(BF16), preferred_element_type=F32)
    o_ref[...] = (sa_ref[...].astype(F32) * ya + sb_ref[...].astype(F32) * yb).astype(BF16)


def _merge(o_a, o_b, w_na_out, w_df_out, proj):
    n = o_a.shape[0]
    ga, gb = COL_GATE_A // MG_TN, COL_GATE_B // MG_TN
    return pl.pallas_call(
        _merge_kernel,
        out_shape=jax.ShapeDtypeStruct((n, D_MODEL), BF16),
        grid=(n // MG_TM, D_MODEL // MG_TN),
        in_specs=[
            pl.BlockSpec((MG_TM, NA_W), lambda i, j: (i, 0)),
            pl.BlockSpec((MG_TM, DF_VW), lambda i, j: (i, 0)),
            pl.BlockSpec((NA_W, MG_TN), lambda i, j: (0, j)),
            pl.BlockSpec((DF_VW, MG_TN), lambda i, j: (0, j)),
            pl.BlockSpec((MG_TM, MG_TN), lambda i, j: (i, ga + j)),
            pl.BlockSpec((MG_TM, MG_TN), lambda i, j: (i, gb + j)),
        ],
        out_specs=pl.BlockSpec((MG_TM, MG_TN), lambda i, j: (i, j)),
        compiler_params=_params(("parallel", "arbitrary")),
        name="merge",
    )(o_a, o_b, w_na_out, w_df_out, proj, proj)


RS_TM, RS_TN = 1024, 512


def _residual_matmul_kernel(a_ref, w_ref, x_ref, o_ref):
    o_ref[...] = x_ref[...] + jnp.dot(a_ref[...], w_ref[...].astype(BF16), preferred_element_type=F32)


def _residual_matmul(a, w, x, name, tm=RS_TM, tn=RS_TN):
    n, k = a.shape
    d = w.shape[1]
    return pl.pallas_call(
        _residual_matmul_kernel,
        out_shape=jax.ShapeDtypeStruct((n, d), F32),
        grid=(n // tm, d // tn),
        in_specs=[
            pl.BlockSpec((tm, k), lambda i, j: (i, 0)),
            pl.BlockSpec((k, tn), lambda i, j: (0, j)),
            pl.BlockSpec((tm, tn), lambda i, j: (i, j)),
        ],
        out_specs=pl.BlockSpec((tm, tn), lambda i, j: (i, j)),
        compiler_params=_params(("parallel", "arbitrary")),
        name=name,
    )(a, w, x)


UP_TM, UP_TN = 1024, 512


def _ffn_up_kernel(x_ref, g_ref, wg_ref, wu_ref, o_ref, h_ref):
    @pl.when(pl.program_id(1) == 0)
    def _():
        _norm_block(x_ref, g_ref, h_ref)

    h = h_ref[...]
    gate = jnp.dot(h, wg_ref[...].astype(BF16), preferred_element_type=F32)
    up = jnp.dot(h, wu_ref[...].astype(BF16), preferred_element_type=F32)
    o_ref[...] = (gate * jax.nn.sigmoid(gate) * up).astype(BF16)


def _ffn_up(x, g, w_gate, w_up):
    n = x.shape[0]
    return pl.pallas_call(
        _ffn_up_kernel,
        out_shape=jax.ShapeDtypeStruct((n, FFN_HID), BF16),
        grid=(n // UP_TM, FFN_HID // UP_TN),
        in_specs=[
            pl.BlockSpec((UP_TM, D_MODEL), lambda i, j: (i, 0)),
            pl.BlockSpec((1, D_MODEL), lambda i, j: (0, 0)),
            pl.BlockSpec((D_MODEL, UP_TN), lambda i, j: (0, j)),
            pl.BlockSpec((D_MODEL, UP_TN), lambda i, j: (0, j)),
        ],
        out_specs=pl.BlockSpec((UP_TM, UP_TN), lambda i, j: (i, j)),
        scratch_shapes=[pltpu.VMEM((UP_TM, D_MODEL), BF16)],
        compiler_params=_params(("parallel", "arbitrary")),
        name="ffn_up",
    )(x, g, w_gate, w_up)


PLE_TM, PLE_TN = 1024, 512


def _ple_kernel(x_ref, g_ref, wg_ref, p_ref, wp_ref, o_ref, h_ref):
    @pl.when(pl.program_id(1) == 0)
    def _():
        _norm_block(x_ref, g_ref, h_ref)

    gate = jnp.dot(h_ref[...], wg_ref[...].astype(BF16), preferred_element_type=F32)
    emb = jnp.dot(p_ref[...].astype(BF16), wp_ref[...].astype(BF16), preferred_element_type=F32)
    cols = pl.ds(pl.multiple_of(pl.program_id(1) * PLE_TN, PLE_TN), PLE_TN)
    o_ref[...] = x_ref[:, cols] + jax.nn.sigmoid(gate) * emb


def _ple(x, g, w_gate, p, w_proj):
    n = x.shape[0]
    return pl.pallas_call(
        _ple_kernel,
        out_shape=jax.ShapeDtypeStruct((n, D_MODEL), F32),
        grid=(n // PLE_TM, D_MODEL // PLE_TN),
        in_specs=[
            pl.BlockSpec((PLE_TM, D_MODEL), lambda i, j: (i, 0)),
            pl.BlockSpec((1, D_MODEL), lambda i, j: (0, 0)),
            pl.BlockSpec((D_MODEL, PLE_TN), lambda i, j: (0, j)),
            pl.BlockSpec((PLE_TM, PLE_DIM), lambda i, j: (i, 0)),
            pl.BlockSpec((PLE_DIM, PLE_TN), lambda i, j: (0, j)),
        ],
        out_specs=pl.BlockSpec((PLE_TM, PLE_TN), lambda i, j: (i, j)),
        scratch_shapes=[pltpu.VMEM((PLE_TM, D_MODEL), BF16)],
        compiler_params=_params(("parallel", "arbitrary")),
        name="ple",
    )(x, g, w_gate, p, w_proj)


def _df_cols(w):
    k = w.shape[0]
    return w.reshape(k, DF_HEADS, 2, 2, DF_DQ // 2).transpose(0, 1, 3, 2, 4).reshape(k, DF_QW)


def _rope_tables(seq, g, scale):
    half = DF_DQ // 2
    inv = 1.0 / (ROPE_THETA ** (jnp.arange(0, DF_DQ, 2, dtype=F32) / DF_DQ))
    ang = jnp.arange(seq, dtype=F32)[:, None] * inv[None, :]
    cos, sin = jnp.cos(ang), jnp.sin(ang)
    g1, g2 = g[:half], g[half:]
    ca = jnp.concatenate([cos * g1, cos * g1, cos * g2, cos * g2], axis=1) * scale
    sb = jnp.concatenate([-sin * g2, -sin * g2, sin * g1, sin * g1], axis=1) * scale
    return ca, sb


def kernel(x, p, g_mix, w_in, g_na_q, g_na_k, na_rpb, g_df_q, g_df_k, lam_q1, lam_k1, lam_q2, lam_k2,
           g_df_sub, w_na_out, w_df_out, w_o, g_ffn, w_gate, w_up, w_down, g_ple, w_ple_gate, w_ple_proj):
    batch, seq, d = x.shape
    n = batch * seq
    rows = seq // GRID_W
    depth = w_in.shape[0]
    xf = x.reshape(n, d)
    for i in range(depth):
        s0 = 3 * NA_W
        w_dfqk = jnp.concatenate([_df_cols(w_in[i][:, s0:s0 + DF_QW]), _df_cols(w_in[i][:, s0 + DF_QW:s0 + 2 * DF_QW])],
                                 axis=1)
        caq, sbq = _rope_tables(seq, g_df_q[i], DF_DQ ** -0.5 * math.log2(math.e))
        cak, sbk = _rope_tables(seq, g_df_k[i], 1.0)
        proj = _in_proj(_rms_norm(xf, g_mix[i][None]), w_in[i], w_dfqk, g_na_q[i][None], g_na_k[i][None],
                        caq, sbq, cak, sbk, seq)

        o_a = _na_attn(proj, _na_tables(na_rpb[i], rows), batch, seq)
        lam_p = jnp.stack([lam_q1[i], lam_k1[i], lam_q2[i], lam_k2[i]], axis=0)
        o_b = _df_attn(proj, lam_p, g_df_sub[i][None], batch, seq)

        merged = _merge(o_a, o_b, w_na_out[i], w_df_out[i], proj)
        xf = _residual_matmul(merged, w_o[i], xf, "out_proj", tm=2048)

        act = _ffn_up(xf, g_ffn[i][None], w_gate[i], w_up[i])
        xf = _residual_matmul(act, w_down[i], xf, "ffn_down", tn=256)

        xf = _ple(xf, g_ple[i][None], w_ple_gate[i], p[i].reshape(n, PLE_DIM), w_ple_proj[i])
    return xf.reshape(batch, seq, d)
```

```python
import functools
import math

import numpy as np
import jax
import jax.numpy as jnp
from jax import lax
from jax.experimental import pallas as pl
from jax.experimental.pallas import tpu as pltpu

D_MODEL = 2048
GRID_W = 64
NA_HEADS = 8
NA_DH = 128
NA_KR = 8
NA_KW = 16
DF_HEADS = 8
DF_DQ = 64
DF_DV = 128
FFN_HID = 5632
PLE_DIM = 256
ROPE_THETA = 10000.0
EPS = 1e-6
LAM_INIT = 0.8 - 0.6 * math.exp(-0.3 * 0)

NA_W = NA_HEADS * NA_DH
DF_QW = DF_HEADS * 2 * DF_DQ
DF_VW = DF_HEADS * DF_DV
IN_COLS = 3 * NA_W + 2 * DF_QW + DF_VW + 2 * D_MODEL

LANES = 128
NEG = -1e30
VMEM_LIMIT = 56 * 1024 * 1024

F32 = jnp.float32
BF16 = jnp.bfloat16

CB_NA_Q, CB_NA_K, CB_NA_V = 0, 8, 16
CB_DF_Q, CB_DF_K, CB_DF_V = 24, 32, 40
COL_GATE_A, COL_GATE_B = 6144, 8192


def _params(sem):
    return pltpu.CompilerParams(dimension_semantics=sem, vmem_limit_bytes=VMEM_LIMIT)


def _rms_rows(x, g):
    ms = jnp.mean(x * x, axis=-1, keepdims=True)
    return x * lax.rsqrt(ms + EPS) * g


ONES_ROWS = 16


def _with_ones_rows(v_t):
    row = lax.broadcasted_iota(jnp.int32, (ONES_ROWS, v_t.shape[1]), 0)
    return jnp.concatenate([v_t, jnp.where(row == 0, 1.0, 0.0).astype(v_t.dtype)], axis=0)


NORM_ROWS = 256


def _norm_block(x_ref, g_ref, h_ref):
    def body(c, carry):
        r = pl.ds(pl.multiple_of(c * NORM_ROWS, NORM_ROWS), NORM_ROWS)
        h_ref[r, :] = _rms_rows(x_ref[r, :], g_ref[...]).astype(BF16)
        return carry

    lax.fori_loop(0, x_ref.shape[0] // NORM_ROWS, body, 0)


def _rms_norm_kernel(x_ref, g_ref, o_ref):
    _norm_block(x_ref, g_ref, o_ref)


def _rms_norm(x, g, tm=512):
    n, d = x.shape
    return pl.pallas_call(
        _rms_norm_kernel,
        out_shape=jax.ShapeDtypeStruct((n, d), BF16),
        grid=(n // tm,),
        in_specs=[pl.BlockSpec((tm, d), lambda i: (i, 0)), pl.BlockSpec((1, d), lambda i: (0, 0))],
        out_specs=pl.BlockSpec((tm, d), lambda i: (i, 0)),
        compiler_params=_params(("parallel",)),
        name="rms_norm",
    )(x, g)


IN_TM, IN_TN = 2048, 512
IN_RB = 256
SEG_TILES = 1024 // IN_TN
DF_TILE0 = 3 * SEG_TILES
DF_TILES = 2 * SEG_TILES


def _in_proj_kernel(h_ref, w_ref, wdf_ref, gq_ref, gk_ref, caq_ref, sbq_ref, cak_ref, sbk_ref, o_ref):
    seg = pl.program_id(1) // SEG_TILES
    heads = IN_TN // LANES

    def run(wsrc_ref, epilogue):
        wb = wsrc_ref[...].astype(BF16)
        for r in range(IN_TM // IN_RB):
            rows = slice(r * IN_RB, (r + 1) * IN_RB)
            epilogue(jnp.dot(h_ref[rows, :], wb, preferred_element_type=F32), rows)

    def na_norm(g_ref):
        def epilogue(y, rows):
            for k in range(heads):
                cols = slice(k * LANES, (k + 1) * LANES)
                o_ref[rows, cols] = _rms_rows(y[:, cols], g_ref[...]).astype(BF16)
        return epilogue

    def df_norm_rope(ca_ref, sb_ref):
        def epilogue(y, rows):
            lane = lax.broadcasted_iota(jnp.int32, (1, LANES), 1)
            comp0 = (lane % 64) < 32
            ca = ca_ref[rows, :]
            sb = sb_ref[rows, :]
            for k in range(heads):
                cols = slice(k * LANES, (k + 1) * LANES)
                yk = y[:, cols]
                pk = pltpu.roll(yk, 64, 1)
                z = yk * yk + pk * pk
                s0 = jnp.sum(jnp.where(comp0, z, 0.0), axis=-1, keepdims=True)
                s1 = jnp.sum(jnp.where(comp0, 0.0, z), axis=-1, keepdims=True)
                r = jnp.where(comp0, lax.rsqrt(s0 * (0.5 / DF_DQ) + EPS), lax.rsqrt(s1 * (0.5 / DF_DQ) + EPS))
                o_ref[rows, cols] = ((yk * ca + pk * sb) * r).astype(BF16)
        return epilogue

    def plain(y, rows):
        o_ref[rows, :] = y.astype(BF16)

    def gate(y, rows):
        o_ref[rows, :] = jax.nn.sigmoid(y).astype(BF16)

    @pl.when(seg == 0)
    def _():
        run(w_ref, na_norm(gq_ref))

    @pl.when(seg == 1)
    def _():
        run(w_ref, na_norm(gk_ref))

    @pl.when(seg == 3)
    def _():
        run(wdf_ref, df_norm_rope(caq_ref, sbq_ref))

    @pl.when(seg == 4)
    def _():
        run(wdf_ref, df_norm_rope(cak_ref, sbk_ref))

    @pl.when((seg == 2) | (seg == 5))
    def _():
        run(w_ref, plain)

    @pl.when(seg >= 6)
    def _():
        run(w_ref, gate)


def _in_proj(h, w_in, w_dfqk, g_na_q, g_na_k, caq, sbq, cak, sbk, seq):
    n = h.shape[0]
    pos_blocks = seq // IN_TM
    row = lambda i, j: (i, 0)
    const = lambda i, j: (0, 0)
    tab = lambda i, j: (i % pos_blocks, 0)
    is_df = lambda j: (j >= DF_TILE0) & (j < DF_TILE0 + DF_TILES)
    return pl.pallas_call(
        _in_proj_kernel,
        out_shape=jax.ShapeDtypeStruct((n, IN_COLS), BF16),
        grid=(n // IN_TM, IN_COLS // IN_TN),
        in_specs=[
            pl.BlockSpec((IN_TM, D_MODEL), row),
            pl.BlockSpec((D_MODEL, IN_TN), lambda i, j: (0, jnp.where(is_df(j), DF_TILE0 - 1, j))),
            pl.BlockSpec((D_MODEL, IN_TN), lambda i, j: (0, jnp.clip(j - DF_TILE0, 0, DF_TILES - 1))),
            pl.BlockSpec((1, LANES), const),
            pl.BlockSpec((1, LANES), const),
            pl.BlockSpec((IN_TM, LANES), tab),
            pl.BlockSpec((IN_TM, LANES), tab),
            pl.BlockSpec((IN_TM, LANES), tab),
            pl.BlockSpec((IN_TM, LANES), tab),
        ],
        out_specs=pl.BlockSpec((IN_TM, IN_TN), lambda i, j: (i, j)),
        compiler_params=_params(("parallel", "arbitrary")),
        name="in_proj",
    )(h, w_in, w_dfqk, g_na_q, g_na_k, caq, sbq, cak, sbk)


NA_QROWS = 8
NA_WROWS = 16
NA_TQ = NA_QROWS * GRID_W
NA_TK = NA_WROWS * GRID_W
NA_PAIR = LANES // GRID_W
NA_NPAIR = NA_QROWS // NA_PAIR
NA_PROWS = NA_KR + NA_PAIR - 1
NA_PK = NA_PROWS * GRID_W
NA_NTAB = 1 + 2 * NA_NPAIR
NA_NEG_SLAB = 2 * NA_KR - 1


def _na_geometry(rows):
    nt = rows // NA_QROWS
    assert nt >= 3 and rows >= NA_WROWS
    ws = np.zeros((3, NA_NPAIR), np.int64)
    idx = np.full((NA_NTAB, NA_PROWS, NA_PAIR), NA_NEG_SLAB, np.int64)
    for ci, t in enumerate((0, 1, nt - 1)):
        w0 = int(np.clip(t * NA_QROWS - NA_KR // 2, 0, rows - NA_WROWS))
        for pi in range(NA_NPAIR):
            r = [t * NA_QROWS + NA_PAIR * pi + hb for hb in range(NA_PAIR)]
            rs = [int(np.clip(ri - NA_KR // 2, 0, rows - NA_KR)) for ri in r]
            start = min(min(rs) - w0, NA_WROWS - NA_PROWS)
            assert 0 <= start and max(rs) + NA_KR <= w0 + start + NA_PROWS and min(rs) >= w0 + start
            ws[ci, pi] = start
            tab = 0 if ci == 1 else 1 + (0 if ci == 0 else NA_NPAIR) + pi
            for wp in range(NA_PROWS):
                key_row = w0 + start + wp
                for hb in range(NA_PAIR):
                    if rs[hb] <= key_row < rs[hb] + NA_KR:
                        idx[tab, wp, hb] = key_row - r[hb] + NA_KR - 1
    combos = sorted({tuple(int(v) for v in pair) for pair in idx.reshape(-1, NA_PAIR)})
    slab = [[combos.index(tuple(int(v) for v in idx[tab, wp])) for wp in range(NA_PROWS)] for tab in range(NA_NTAB)]
    return [[int(v) for v in row] for row in ws], combos, slab


def _na_kernel(q_ref, k_ref, v_ref, tab_ref, o_ref, p_ref, *, rows):
    nt = rows // NA_QROWS
    ws, _, slab = _na_geometry(rows)

    def window(t):
        return int(np.clip(t * NA_QROWS - NA_KR // 2, 0, rows - NA_WROWS)) * GRID_W

    def probs(t):
        cls = 0 if t == 0 else (2 if t == nt - 1 else 1)
        buf = t
        p_ref[buf] = jnp.zeros(p_ref.shape[1:], BF16)
        for pi in range(NA_NPAIR):
            lanes = slice(pi * LANES, (pi + 1) * LANES)
            tab = 0 if cls == 1 else 1 + (0 if cls == 0 else NA_NPAIR) + pi
            r0 = ws[cls][pi] * GRID_W
            kp = k_ref[window(t) + r0:window(t) + r0 + NA_PK, :]
            qp = q_ref[t * NA_TQ + pi * LANES:t * NA_TQ + (pi + 1) * LANES, :]
            sp = lax.dot_general(kp, qp, (((1,), (1,)), ((), ())), preferred_element_type=F32)
            bias = jnp.concatenate([tab_ref[0, ci] for ci in slab[tab]], axis=0)
            sp = sp * (NA_DH ** -0.5) + bias
            m = jnp.max(sp, axis=0, keepdims=True)
            p_ref[buf, r0:r0 + NA_PK, lanes] = jnp.exp(sp - m).astype(BF16)

    def outputs(t):
        vw = _with_ones_rows(v_ref[window(t):window(t) + NA_TK, :].T)
        ov = jnp.dot(vw, p_ref[t], preferred_element_type=F32)
        o_ref[t * NA_TQ:(t + 1) * NA_TQ, :] = (ov[0:NA_DH, :] / ov[NA_DH:NA_DH + 1, :]).T.astype(BF16)

    for t in range(nt + 1):
        if t < nt:
            probs(t)
        if t >= 1:
            outputs(t - 1)


def _na_attn(proj, tabs, batch, seq):
    rows = seq // GRID_W
    return pl.pallas_call(
        functools.partial(_na_kernel, rows=rows),
        out_shape=jax.ShapeDtypeStruct((batch * seq, NA_W), BF16),
        grid=(batch, NA_HEADS),
        in_specs=[
            pl.BlockSpec((seq, LANES), lambda b, h: (b, CB_NA_Q + h)),
            pl.BlockSpec((seq, LANES), lambda b, h: (b, CB_NA_K + h)),
            pl.BlockSpec((seq, LANES), lambda b, h: (b, CB_NA_V + h)),
            pl.BlockSpec((1,) + tabs.shape[1:], lambda b, h: (h, 0, 0, 0)),
        ],
        out_specs=pl.BlockSpec((seq, LANES), lambda b, h: (b, h)),
        scratch_shapes=[pltpu.VMEM((rows // NA_QROWS, NA_TK, NA_TQ), BF16)],
        compiler_params=_params(("parallel", "arbitrary")),
        name="na_attn",
    )(proj, proj, proj, tabs)


def _na_tables(rpb, rows):
    c = np.arange(GRID_W)
    cs = np.clip(c - NA_KW // 2, 0, GRID_W - NA_KW)
    valid_c = (c[None, :] >= cs[:, None]) & (c[None, :] < cs[:, None] + NA_KW)
    dc = c[:, None] - c[None, :] + NA_KW - 1
    onehot = (dc[None] == np.arange(2 * NA_KW - 1)[:, None, None]) & valid_c.T[None]
    mask = np.where(valid_c.T, 0.0, NEG).astype(np.float32)
    half = jnp.sum(rpb[:, :, :, None, None] * jnp.asarray(onehot, F32)[None, None], axis=2) + jnp.asarray(mask)
    half = jnp.concatenate([half, jnp.full((NA_HEADS, 1, GRID_W, GRID_W), NEG, F32)], axis=1)
    _, combos, _ = _na_geometry(rows)
    left = jnp.take(half, jnp.asarray([cb[0] for cb in combos], jnp.int32), axis=1)
    right = jnp.take(half, jnp.asarray([cb[1] for cb in combos], jnp.int32), axis=1)
    return jnp.concatenate([left, right], axis=-1)


DF_TQ, DF_TK = 1024, 512
DF_TG = 256
DF_VROWS = DF_DV + ONES_ROWS


def _df_kernel(q_ref, k_ref, v_ref, lam_ref, gsub_ref, o_ref, q12_ref, s_ref, acc_ref, vt_ref, *, seq):
    n_chunks = seq // DF_TK

    @pl.when(pl.program_id(2) == 0)
    def _():
        for c in range(n_chunks):
            cols = slice(c * DF_TK, (c + 1) * DF_TK)
            vt_ref[:, cols] = _with_ones_rows(v_ref[cols, :].T)

    lane = lax.broadcasted_iota(jnp.int32, (1, LANES), 1)
    comp0 = (lane % 64) < 32
    q = q_ref[...]
    zero = jnp.zeros_like(q)
    q12_ref[0:DF_TQ, :] = jnp.where(comp0, q, zero)
    q12_ref[DF_TQ:2 * DF_TQ, :] = jnp.where(comp0, zero, q)
    acc_ref[...] = jnp.zeros(acc_ref.shape, F32)

    groups = 2 * DF_TQ // DF_TG

    def scores(kc, g):
        cols = slice(g * DF_TG, (g + 1) * DF_TG)
        return lax.dot_general(kc, q12_ref[cols, :], (((1,), (1,)), ((), ())), preferred_element_type=F32)

    def keys(c):
        return k_ref[c * DF_TK:(c + 1) * DF_TK, :]

    def step(c, carry):
        cur = c % 2
        vc = vt_ref[:, c * DF_TK:(c + 1) * DF_TK]
        kn = keys(c + 1) if c + 1 < n_chunks else None
        out = []
        for g in range(groups):
            m_prev = carry[g]
            cols = slice(g * DF_TG, (g + 1) * DF_TG)
            if kn is not None:
                s_ref[1 - cur, :, cols] = scores(kn, g)
            s = s_ref[cur, :, cols]
            m_new = jnp.maximum(m_prev, jnp.max(s, axis=0, keepdims=True))
            alpha = jnp.exp2(m_prev - m_new)
            p = jnp.exp2(s - m_new).astype(BF16)
            pv = jnp.dot(vc, p, preferred_element_type=F32)
            acc_ref[:, cols] = alpha * acc_ref[:, cols] + pv
            out.append(m_new)
        return tuple(out)

    k0 = keys(0)
    for g in range(groups):
        s_ref[0, :, g * DF_TG:(g + 1) * DF_TG] = scores(k0, g)

    carry = tuple(jnp.full((1, DF_TG), -jnp.inf, F32) for _ in range(groups))
    for c in range(n_chunks):
        carry = step(c, carry)

    lp = lam_ref[...]
    lam = (jnp.exp(jnp.sum(lp[0:1] * lp[1:2], axis=-1, keepdims=True))
           - jnp.exp(jnp.sum(lp[2:3] * lp[3:4], axis=-1, keepdims=True)) + LAM_INIT)
    o12 = acc_ref[0:DF_DV, :] / acc_ref[DF_DV:DF_DV + 1, :]
    o_t = o12[:, 0:DF_TQ] - lam * o12[:, DF_TQ:2 * DF_TQ]
    o_ref[...] = (_rms_rows(o_t.T, gsub_ref[...]) * (1.0 - LAM_INIT)).astype(BF16)


def _df_attn(proj, lam_p, g_sub, batch, seq):
    nq = seq // DF_TQ
    const = lambda b, h, i: (0, 0)
    return pl.pallas_call(
        functools.partial(_df_kernel, seq=seq),
        out_shape=jax.ShapeDtypeStruct((batch * seq, DF_VW), BF16),
        grid=(batch, DF_HEADS, nq),
        in_specs=[
            pl.BlockSpec((DF_TQ, LANES), lambda b, h, i: (b * nq + i, CB_DF_Q + h)),
            pl.BlockSpec((seq, LANES), lambda b, h, i: (b, CB_DF_K + h)),
            pl.BlockSpec((seq, LANES), lambda b, h, i: (b, CB_DF_V + h)),
            pl.BlockSpec((4, DF_DQ), const),
            pl.BlockSpec((1, DF_DV), const),
        ],
        out_specs=pl.BlockSpec((DF_TQ, LANES), lambda b, h, i: (b * nq + i, h)),
        scratch_shapes=[pltpu.VMEM((2 * DF_TQ, LANES), BF16), pltpu.VMEM((2, DF_TK, 2 * DF_TQ), F32),
                        pltpu.VMEM((DF_VROWS, 2 * DF_TQ), F32), pltpu.VMEM((DF_VROWS, seq), BF16)],
        compiler_params=_params(("parallel", "parallel", "arbitrary")),
        name="df_attn",
    )(proj, proj, proj, lam_p, g_sub)


MG_TM, MG_TN = 2048, 512


def _merge_kernel(oa_ref, ob_ref, wa_ref, wb_ref, sa_ref, sb_ref, o_ref):
    ya = jnp.dot(oa_ref[...], wa_ref[...].astype(BF16), preferred_element_type=F32)
    yb = jnp.dot(ob_ref[...], wb_ref[...].astype(BF16), preferred_element_type=F32)
    o_ref[...] = (sa_ref[...].astype(F32) * ya + sb_ref[...].astype(F32) * yb).astype(BF16)


def _merge(o_a, o_b, w_na_out, w_df_out, proj):
    n = o_a.shape[0]
    ga, gb = COL_GATE_A // MG_TN, COL_GATE_B // MG_TN
    return pl.pallas_call(
        _merge_kernel,
        out_shape=jax.ShapeDtypeStruct((n, D_MODEL), BF16),
        grid=(n // MG_TM, D_MODEL // MG_TN),
        in_specs=[
            pl.BlockSpec((MG_TM, NA_W), lambda i, j: (i, 0)),
            pl.BlockSpec((MG_TM, DF_VW), lambda i, j: (i, 0)),
            pl.BlockSpec((NA_W, MG_TN), lambda i, j: (0, j)),
            pl.BlockSpec((DF_VW, MG_TN), lambda i, j: (0, j)),
            pl.BlockSpec((MG_TM, MG_TN), lambda i, j: (i, ga + j)),
            pl.BlockSpec((MG_TM, MG_TN), lambda i, j: (i, gb + j)),
        ],
        out_specs=pl.BlockSpec((MG_TM, MG_TN), lambda i, j: (i, j)),
        compiler_params=_params(("parallel", "arbitrary")),
        name="merge",
    )(o_a, o_b, w_na_out, w_df_out, proj, proj)


RS_TM, RS_TN = 1024, 512


def _residual_matmul_kernel(a_ref, w_ref, x_ref, o_ref):
    o_ref[...] = x_ref[...] + jnp.dot(a_ref[...], w_ref[...].astype(BF16), preferred_element_type=F32)


def _residual_matmul(a, w, x, name, tm=RS_TM, tn=RS_TN):
    n, k = a.shape
    d = w.shape[1]
    return pl.pallas_call(
        _residual_matmul_kernel,
        out_shape=jax.ShapeDtypeStruct((n, d), F32),
        grid=(n // tm, d // tn),
        in_specs=[
            pl.BlockSpec((tm, k), lambda i, j: (i, 0)),
            pl.BlockSpec((k, tn), lambda i, j: (0, j)),
            pl.BlockSpec((tm, tn), lambda i, j: (i, j)),
        ],
        out_specs=pl.BlockSpec((tm, tn), lambda i, j: (i, j)),
        compiler_params=_params(("parallel", "arbitrary")),
        name=name,
    )(a, w, x)


OPN_TM = 256
CAST_ROWS = 256


def _out_proj_norm_kernel(a_ref, w_ref, x_ref, g_ref, o_ref, h_ref, wb_ref):
    @pl.when(pl.program_id(0) == 0)
    def _():
        def body(c, carry):
            r = pl.ds(pl.multiple_of(c * CAST_ROWS, CAST_ROWS), CAST_ROWS)
            wb_ref[r, :] = w_ref[r, :].astype(BF16)
            return carry

        lax.fori_loop(0, w_ref.shape[0] // CAST_ROWS, body, 0)

    y = x_ref[...] + jnp.dot(a_ref[...], wb_ref[...], preferred_element_type=F32)
    o_ref[...] = y
    h_ref[...] = _rms_rows(y, g_ref[...]).astype(BF16)


def _out_proj_norm(a, w, x, g):
    n, k = a.shape
    d = w.shape[1]
    return pl.pallas_call(
        _out_proj_norm_kernel,
        out_shape=(jax.ShapeDtypeStruct((n, d), F32), jax.ShapeDtypeStruct((n, d), BF16)),
        grid=(n // OPN_TM,),
        in_specs=[
            pl.BlockSpec((OPN_TM, k), lambda i: (i, 0)),
            pl.BlockSpec((k, d), lambda i: (0, 0), pipeline_mode=pl.Buffered(1)),
            pl.BlockSpec((OPN_TM, d), lambda i: (i, 0)),
            pl.BlockSpec((1, d), lambda i: (0, 0)),
        ],
        out_specs=(pl.BlockSpec((OPN_TM, d), lambda i: (i, 0)), pl.BlockSpec((OPN_TM, d), lambda i: (i, 0))),
        scratch_shapes=[pltpu.VMEM((k, d), BF16)],
        compiler_params=_params(("arbitrary",)),
        name="out_proj",
    )(a, w, x, g)


UP_TM, UP_TN = 2048, 512
UP_RB = 512


def _ffn_up_kernel(h_ref, wg_ref, wu_ref, o_ref):
    wg = wg_ref[...].astype(BF16)
    wu = wu_ref[...].astype(BF16)
    for r in range(UP_TM // UP_RB):
        rows = slice(r * UP_RB, (r + 1) * UP_RB)
        h = h_ref[rows, :]
        gate = jnp.dot(h, wg, preferred_element_type=F32)
        up = jnp.dot(h, wu, preferred_element_type=F32)
        o_ref[rows, :] = (gate * jax.nn.sigmoid(gate) * up).astype(BF16)


def _ffn_up(h, w_gate, w_up):
    n = h.shape[0]
    return pl.pallas_call(
        _ffn_up_kernel,
        out_shape=jax.ShapeDtypeStruct((n, FFN_HID), BF16),
        grid=(n // UP_TM, FFN_HID // UP_TN),
        in_specs=[
            pl.BlockSpec((UP_TM, D_MODEL), lambda i, j: (i, 0)),
            pl.BlockSpec((D_MODEL, UP_TN), lambda i, j: (0, j)),
            pl.BlockSpec((D_MODEL, UP_TN), lambda i, j: (0, j)),
        ],
        out_specs=pl.BlockSpec((UP_TM, UP_TN), lambda i, j: (i, j)),
        compiler_params=_params(("parallel", "arbitrary")),
        name="ffn_up",
    )(h, w_gate, w_up)


PLE_TM, PLE_TN = 1024, 512


def _ple_kernel(x_ref, g_ref, wg_ref, p_ref, wp_ref, o_ref, h_ref):
    @pl.when(pl.program_id(1) == 0)
    def _():
        _norm_block(x_ref, g_ref, h_ref)

    gate = jnp.dot(h_ref[...], wg_ref[...].astype(BF16), preferred_element_type=F32)
    emb = jnp.dot(p_ref[...].astype(BF16), wp_ref[...].astype(BF16), preferred_element_type=F32)
    cols = pl.ds(pl.multiple_of(pl.program_id(1) * PLE_TN, PLE_TN), PLE_TN)
    o_ref[...] = x_ref[:, cols] + jax.nn.sigmoid(gate) * emb


def _ple(x, g, w_gate, p, w_proj):
    n = x.shape[0]
    return pl.pallas_call(
        _ple_kernel,
        out_shape=jax.ShapeDtypeStruct((n, D_MODEL), F32),
        grid=(n // PLE_TM, D_MODEL // PLE_TN),
        in_specs=[
            pl.BlockSpec((PLE_TM, D_MODEL), lambda i, j: (i, 0)),
            pl.BlockSpec((1, D_MODEL), lambda i, j: (0, 0)),
            pl.BlockSpec((D_MODEL, PLE_TN), lambda i, j: (0, j)),
            pl.BlockSpec((PLE_TM, PLE_DIM), lambda i, j: (i, 0)),
            pl.BlockSpec((PLE_DIM, PLE_TN), lambda i, j: (0, j)),
        ],
        out_specs=pl.BlockSpec((PLE_TM, PLE_TN), lambda i, j: (i, j)),
        scratch_shapes=[pltpu.VMEM((PLE_TM, D_MODEL), BF16)],
        compiler_params=_params(("parallel", "arbitrary")),
        name="ple",
    )(x, g, w_gate, p, w_proj)


def _df_cols(w):
    k = w.shape[0]
    return w.reshape(k, DF_HEADS, 2, 2, DF_DQ // 2).transpose(0, 1, 3, 2, 4).reshape(k, DF_QW)


def _rope_tables(seq, g, scale):
    half = DF_DQ // 2
    inv = 1.0 / (ROPE_THETA ** (jnp.arange(0, DF_DQ, 2, dtype=F32) / DF_DQ))
    ang = jnp.arange(seq, dtype=F32)[:, None] * inv[None, :]
    cos, sin = jnp.cos(ang), jnp.sin(ang)
    g1, g2 = g[:half], g[half:]
    ca = jnp.concatenate([cos * g1, cos * g1, cos * g2, cos * g2], axis=1) * scale
    sb = jnp.concatenate([-sin * g2, -sin * g2, sin * g1, sin * g1], axis=1) * scale
    return ca, sb


def kernel(x, p, g_mix, w_in, g_na_q, g_na_k, na_rpb, g_df_q, g_df_k, lam_q1, lam_k1, lam_q2, lam_k2,
           g_df_sub, w_na_out, w_df_out, w_o, g_ffn, w_gate, w_up, w_down, g_ple, w_ple_gate, w_ple_proj):
    batch, seq, d = x.shape
    n = batch * seq
    rows = seq // GRID_W
    depth = w_in.shape[0]
    xf = x.reshape(n, d)
    for i in range(depth):
        s0 = 3 * NA_W
        w_dfqk = jnp.concatenate([_df_cols(w_in[i][:, s0:s0 + DF_QW]), _df_cols(w_in[i][:, s0 + DF_QW:s0 + 2 * DF_QW])],
                                 axis=1)
        caq, sbq = _rope_tables(seq, g_df_q[i], DF_DQ ** -0.5 * math.log2(math.e))
        cak, sbk = _rope_tables(seq, g_df_k[i], 1.0)
        proj = _in_proj(_rms_norm(xf, g_mix[i][None]), w_in[i], w_dfqk, g_na_q[i][None], g_na_k[i][None],
                        caq, sbq, cak, sbk, seq)

        o_a = _na_attn(proj, _na_tables(na_rpb[i], rows), batch, seq)
        lam_p = jnp.stack([lam_q1[i], lam_k1[i], lam_q2[i], lam_k2[i]], axis=0)
        o_b = _df_attn(proj, lam_p, g_df_sub[i][None], batch, seq)

        merged = _merge(o_a, o_b, w_na_out[i], w_df_out[i], proj)
        xf, hf = _out_proj_norm(merged, w_o[i], xf, g_ffn[i][None])

        act = _ffn_up(hf, w_gate[i], w_up[i])
        xf = _residual_matmul(act, w_down[i], xf, "ffn_down", tn=256)

        xf = _ple(xf, g_ple[i][None], w_ple_gate[i], p[i].reshape(n, PLE_DIM), w_ple_proj[i])
    return xf.reshape(batch, seq, d)
```

```python
import functools
import math

import numpy as np
import jax
import jax.numpy as jnp
from jax import lax
from jax.experimental import pallas as pl
from jax.experimental.pallas import tpu as pltpu

D_MODEL = 2048
GRID_W = 64
NA_HEADS = 8
NA_DH = 128
NA_KR = 8
NA_KW = 16
DF_HEADS = 8
DF_DQ = 64
DF_DV = 128
FFN_HID = 5632
PLE_DIM = 256
ROPE_THETA = 10000.0
EPS = 1e-6
LAM_INIT = 0.8 - 0.6 * math.exp(-0.3 * 0)

NA_W = NA_HEADS * NA_DH
DF_QW = DF_HEADS * 2 * DF_DQ
DF_VW = DF_HEADS * DF_DV
IN_COLS = 3 * NA_W + 2 * DF_QW + DF_VW + 2 * D_MODEL

LANES = 128
NEG = -1e30
VMEM_LIMIT = 56 * 1024 * 1024

F32 = jnp.float32
BF16 = jnp.bfloat16

CB_NA_Q, CB_NA_K, CB_NA_V = 0, 8, 16
CB_DF_Q, CB_DF_K, CB_DF_V = 24, 32, 40
COL_GATE_A, COL_GATE_B = 6144, 8192


def _params(sem):
    return pltpu.CompilerParams(dimension_semantics=sem, vmem_limit_bytes=VMEM_LIMIT)


def _rms_rows(x, g):
    ms = jnp.mean(x * x, axis=-1, keepdims=True)
    return x * lax.rsqrt(ms + EPS) * g


ONES_ROWS = 16


def _with_ones_rows(v_t):
    row = lax.broadcasted_iota(jnp.int32, (ONES_ROWS, v_t.shape[1]), 0)
    return jnp.concatenate([v_t, jnp.where(row == 0, 1.0, 0.0).astype(v_t.dtype)], axis=0)


NORM_ROWS = 256


def _norm_block(x_ref, g_ref, h_ref):
    def body(c, carry):
        r = pl.ds(pl.multiple_of(c * NORM_ROWS, NORM_ROWS), NORM_ROWS)
        h_ref[r, :] = _rms_rows(x_ref[r, :], g_ref[...]).astype(BF16)
        return carry

    lax.fori_loop(0, x_ref.shape[0] // NORM_ROWS, body, 0)


def _rms_norm_kernel(x_ref, g_ref, o_ref):
    _norm_block(x_ref, g_ref, o_ref)


def _rms_norm(x, g, tm=512):
    n, d = x.shape
    return pl.pallas_call(
        _rms_norm_kernel,
        out_shape=jax.ShapeDtypeStruct((n, d), BF16),
        grid=(n // tm,),
        in_specs=[pl.BlockSpec((tm, d), lambda i: (i, 0)), pl.BlockSpec((1, d), lambda i: (0, 0))],
        out_specs=pl.BlockSpec((tm, d), lambda i: (i, 0)),
        compiler_params=_params(("parallel",)),
        name="rms_norm",
    )(x, g)


IN_TM, IN_TN = 2048, 512
IN_RB = 256
SEG_TILES = 1024 // IN_TN
DF_TILE0 = 3 * SEG_TILES
DF_TILES = 2 * SEG_TILES


def _in_proj_kernel(h_ref, w_ref, wdf_ref, gq_ref, gk_ref, caq_ref, sbq_ref, cak_ref, sbk_ref, o_ref):
    seg = pl.program_id(1) // SEG_TILES
    heads = IN_TN // LANES

    def run(wsrc_ref, epilogue):
        wb = wsrc_ref[...].astype(BF16)
        for r in range(IN_TM // IN_RB):
            rows = slice(r * IN_RB, (r + 1) * IN_RB)
            epilogue(jnp.dot(h_ref[rows, :], wb, preferred_element_type=F32), rows)

    def na_norm(g_ref):
        def epilogue(y, rows):
            for k in range(heads):
                cols = slice(k * LANES, (k + 1) * LANES)
                o_ref[rows, cols] = _rms_rows(y[:, cols], g_ref[...]).astype(BF16)
        return epilogue

    def df_norm_rope(ca_ref, sb_ref):
        def epilogue(y, rows):
            lane = lax.broadcasted_iota(jnp.int32, (1, LANES), 1)
            comp0 = (lane % 64) < 32
            ca = ca_ref[rows, :]
            sb = sb_ref[rows, :]
            for k in range(heads):
                cols = slice(k * LANES, (k + 1) * LANES)
                yk = y[:, cols]
                pk = pltpu.roll(yk, 64, 1)
                z = yk * yk + pk * pk
                s0 = jnp.sum(jnp.where(comp0, z, 0.0), axis=-1, keepdims=True)
                s1 = jnp.sum(jnp.where(comp0, 0.0, z), axis=-1, keepdims=True)
                r = jnp.where(comp0, lax.rsqrt(s0 * (0.5 / DF_DQ) + EPS), lax.rsqrt(s1 * (0.5 / DF_DQ) + EPS))
                o_ref[rows, cols] = ((yk * ca + pk * sb) * r).astype(BF16)
        return epilogue

    def plain(y, rows):
        o_ref[rows, :] = y.astype(BF16)

    def gate(y, rows):
        o_ref[rows, :] = jax.nn.sigmoid(y).astype(BF16)

    @pl.when(seg == 0)
    def _():
        run(w_ref, na_norm(gq_ref))

    @pl.when(seg == 1)
    def _():
        run(w_ref, na_norm(gk_ref))

    @pl.when(seg == 3)
    def _():
        run(wdf_ref, df_norm_rope(caq_ref, sbq_ref))

    @pl.when(seg == 4)
    def _():
        run(wdf_ref, df_norm_rope(cak_ref, sbk_ref))

    @pl.when((seg == 2) | (seg == 5))
    def _():
        run(w_ref, plain)

    @pl.when(seg >= 6)
    def _():
        run(w_ref, gate)


def _in_proj(h, w_in, w_dfqk, g_na_q, g_na_k, caq, sbq, cak, sbk, seq):
    n = h.shape[0]
    pos_blocks = seq // IN_TM
    row = lambda i, j: (i, 0)
    const = lambda i, j: (0, 0)
    tab = lambda i, j: (i % pos_blocks, 0)
    is_df = lambda j: (j >= DF_TILE0) & (j < DF_TILE0 + DF_TILES)
    return pl.pallas_call(
        _in_proj_kernel,
        out_shape=jax.ShapeDtypeStruct((n, IN_COLS), BF16),
        grid=(n // IN_TM, IN_COLS // IN_TN),
        in_specs=[
            pl.BlockSpec((IN_TM, D_MODEL), row),
            pl.BlockSpec((D_MODEL, IN_TN), lambda i, j: (0, jnp.where(is_df(j), DF_TILE0 - 1, j))),
            pl.BlockSpec((D_MODEL, IN_TN), lambda i, j: (0, jnp.clip(j - DF_TILE0, 0, DF_TILES - 1))),
            pl.BlockSpec((1, LANES), const),
            pl.BlockSpec((1, LANES), const),
            pl.BlockSpec((IN_TM, LANES), tab),
            pl.BlockSpec((IN_TM, LANES), tab),
            pl.BlockSpec((IN_TM, LANES), tab),
            pl.BlockSpec((IN_TM, LANES), tab),
        ],
        out_specs=pl.BlockSpec((IN_TM, IN_TN), lambda i, j: (i, j)),
        compiler_params=_params(("parallel", "arbitrary")),
        name="in_proj",
    )(h, w_in, w_dfqk, g_na_q, g_na_k, caq, sbq, cak, sbk)


NA_QROWS = 8
NA_WROWS = 16
NA_TQ = NA_QROWS * GRID_W
NA_TK = NA_WROWS * GRID_W
NA_PAIR = LANES // GRID_W
NA_NPAIR = NA_QROWS // NA_PAIR
NA_PROWS = NA_KR + NA_PAIR - 1
NA_PK = NA_PROWS * GRID_W
NA_NTAB = 1 + 2 * NA_NPAIR
NA_NEG_SLAB = 2 * NA_KR - 1


def _na_geometry(rows):
    nt = rows // NA_QROWS
    assert nt >= 3 and rows >= NA_WROWS
    ws = np.zeros((3, NA_NPAIR), np.int64)
    idx = np.full((NA_NTAB, NA_PROWS, NA_PAIR), NA_NEG_SLAB, np.int64)
    for ci, t in enumerate((0, 1, nt - 1)):
        w0 = int(np.clip(t * NA_QROWS - NA_KR // 2, 0, rows - NA_WROWS))
        for pi in range(NA_NPAIR):
            r = [t * NA_QROWS + NA_PAIR * pi + hb for hb in range(NA_PAIR)]
            rs = [int(np.clip(ri - NA_KR // 2, 0, rows - NA_KR)) for ri in r]
            start = min(min(rs) - w0, NA_WROWS - NA_PROWS)
            assert 0 <= start and max(rs) + NA_KR <= w0 + start + NA_PROWS and min(rs) >= w0 + start
            ws[ci, pi] = start
            tab = 0 if ci == 1 else 1 + (0 if ci == 0 else NA_NPAIR) + pi
            for wp in range(NA_PROWS):
                key_row = w0 + start + wp
                for hb in range(NA_PAIR):
                    if rs[hb] <= key_row < rs[hb] + NA_KR:
                        idx[tab, wp, hb] = key_row - r[hb] + NA_KR - 1
    combos = sorted({tuple(int(v) for v in pair) for pair in idx.reshape(-1, NA_PAIR)})
    slab = [[combos.index(tuple(int(v) for v in idx[tab, wp])) for wp in range(NA_PROWS)] for tab in range(NA_NTAB)]
    return [[int(v) for v in row] for row in ws], combos, slab


def _na_kernel(q_ref, k_ref, v_ref, row_ref, o_ref, p_ref, slab_ref, *, rows):
    nt = rows // NA_QROWS
    ws, combos, slab = _na_geometry(rows)

    kc = lax.broadcasted_iota(jnp.int32, (GRID_W, LANES), 0)
    lane = lax.broadcasted_iota(jnp.int32, (GRID_W, LANES), 1)
    c = lane % GRID_W
    cs = jnp.clip(c - NA_KW // 2, 0, GRID_W - NA_KW)
    in_window = (kc >= cs) & (kc < cs + NA_KW)
    src = jnp.clip(kc - c + NA_KW - 1, 0, 2 * NA_KW - 2) + jnp.where(lane < GRID_W, 0, GRID_W)
    for ci, (left, right) in enumerate(combos):
        ok = in_window
        if left == NA_NEG_SLAB:
            ok = ok & (lane >= GRID_W)
        if right == NA_NEG_SLAB:
            ok = ok & (lane < GRID_W)
        row = jnp.broadcast_to(row_ref[0, ci:ci + 1, :], (GRID_W, LANES))
        slab_ref[ci] = jnp.where(ok, jnp.take_along_axis(row, src, axis=1), NEG)

    def window(t):
        return int(np.clip(t * NA_QROWS - NA_KR // 2, 0, rows - NA_WROWS)) * GRID_W

    def probs(t):
        cls = 0 if t == 0 else (2 if t == nt - 1 else 1)
        buf = t
        p_ref[buf] = jnp.zeros(p_ref.shape[1:], BF16)
        for pi in range(NA_NPAIR):
            lanes = slice(pi * LANES, (pi + 1) * LANES)
            tab = 0 if cls == 1 else 1 + (0 if cls == 0 else NA_NPAIR) + pi
            r0 = ws[cls][pi] * GRID_W
            kp = k_ref[window(t) + r0:window(t) + r0 + NA_PK, :]
            qp = q_ref[t * NA_TQ + pi * LANES:t * NA_TQ + (pi + 1) * LANES, :]
            sp = lax.dot_general(kp, qp, (((1,), (1,)), ((), ())), preferred_element_type=F32)
            bias = jnp.concatenate([slab_ref[ci] for ci in slab[tab]], axis=0)
            sp = sp * (NA_DH ** -0.5) + bias
            m = jnp.max(sp, axis=0, keepdims=True)
            p_ref[buf, r0:r0 + NA_PK, lanes] = jnp.exp(sp - m).astype(BF16)

    def outputs(t):
        vw = _with_ones_rows(v_ref[window(t):window(t) + NA_TK, :].T)
        ov = jnp.dot(vw, p_ref[t], preferred_element_type=F32)
        o_ref[t * NA_TQ:(t + 1) * NA_TQ, :] = (ov[0:NA_DH, :] / ov[NA_DH:NA_DH + 1, :]).T.astype(BF16)

    for t in range(nt + 1):
        if t < nt:
            probs(t)
        if t >= 1:
            outputs(t - 1)


def _na_attn(proj, bias_rows, batch, seq):
    rows = seq // GRID_W
    n_slabs = bias_rows.shape[1]
    return pl.pallas_call(
        functools.partial(_na_kernel, rows=rows),
        out_shape=jax.ShapeDtypeStruct((batch * seq, NA_W), BF16),
        grid=(batch, NA_HEADS),
        in_specs=[
            pl.BlockSpec((seq, LANES), lambda b, h: (b, CB_NA_Q + h)),
            pl.BlockSpec((seq, LANES), lambda b, h: (b, CB_NA_K + h)),
            pl.BlockSpec((seq, LANES), lambda b, h: (b, CB_NA_V + h)),
            pl.BlockSpec((1, n_slabs, LANES), lambda b, h: (h, 0, 0)),
        ],
        out_specs=pl.BlockSpec((seq, LANES), lambda b, h: (b, h)),
        scratch_shapes=[pltpu.VMEM((rows // NA_QROWS, NA_TK, NA_TQ), BF16),
                        pltpu.VMEM((n_slabs, GRID_W, LANES), F32)],
        compiler_params=_params(("parallel", "arbitrary")),
        name="na_attn",
    )(proj, proj, proj, bias_rows)


def _na_bias_rows(rpb, rows):
    _, combos, _ = _na_geometry(rows)
    padded = jnp.pad(rpb, ((0, 0), (0, 1), (0, GRID_W - (2 * NA_KW - 1))))
    left = jnp.take(padded, jnp.asarray([cb[0] for cb in combos], jnp.int32), axis=1)
    right = jnp.take(padded, jnp.asarray([cb[1] for cb in combos], jnp.int32), axis=1)
    return jnp.concatenate([left, right], axis=-1)


DF_TQ, DF_TK = 1024, 512
DF_TG = 256
DF_SBUF = 2
DF_VROWS = DF_DV + ONES_ROWS


def _df_kernel(q_ref, k_ref, v_ref, lam_ref, gsub_ref, o_ref, q12_ref, s_ref, acc_ref, vt_ref, *, seq):
    n_chunks = seq // DF_TK

    @pl.when(pl.program_id(2) == 0)
    def _():
        for c in range(n_chunks):
            cols = slice(c * DF_TK, (c + 1) * DF_TK)
            vt_ref[:, cols] = _with_ones_rows(v_ref[cols, :].T)

    lane = lax.broadcasted_iota(jnp.int32, (1, LANES), 1)
    comp0 = (lane % 64) < 32
    q = q_ref[...]
    zero = jnp.zeros_like(q)
    q12_ref[0:DF_TQ, :] = jnp.where(comp0, q, zero)
    q12_ref[DF_TQ:2 * DF_TQ, :] = jnp.where(comp0, zero, q)
    acc_ref[...] = jnp.zeros(acc_ref.shape, F32)

    groups = 2 * DF_TQ // DF_TG

    def scores(kc, g):
        cols = slice(g * DF_TG, (g + 1) * DF_TG)
        return lax.dot_general(kc, q12_ref[cols, :], (((1,), (1,)), ((), ())), preferred_element_type=F32)

    def keys(c):
        return k_ref[c * DF_TK:(c + 1) * DF_TK, :]

    def step(c, carry):
        cur, nxt = c % DF_SBUF, (c + 1) % DF_SBUF
        vc = vt_ref[:, c * DF_TK:(c + 1) * DF_TK]
        kn = keys(c + 1) if c + 1 < n_chunks else None
        out = []
        for g in range(groups):
            m_prev = carry[g]
            cols = slice(g * DF_TG, (g + 1) * DF_TG)
            if kn is not None:
                s_ref[nxt, :, cols] = scores(kn, g)
            s = s_ref[cur, :, cols]
            m_new = jnp.maximum(m_prev, jnp.max(s, axis=0, keepdims=True))
            alpha = jnp.exp2(m_prev - m_new)
            p = jnp.exp2(s - m_new).astype(BF16)
            pv = jnp.dot(vc, p, preferred_element_type=F32)
            acc_ref[:, cols] = alpha * acc_ref[:, cols] + pv
            out.append(m_new)
        return tuple(out)

    k0 = keys(0)
    for g in range(groups):
        s_ref[0, :, g * DF_TG:(g + 1) * DF_TG] = scores(k0, g)

    carry = tuple(jnp.full((1, DF_TG), -jnp.inf, F32) for _ in range(groups))
    for c in range(n_chunks):
        carry = step(c, carry)

    lp = lam_ref[...]
    lam = (jnp.exp(jnp.sum(lp[0:1] * lp[1:2], axis=-1, keepdims=True))
           - jnp.exp(jnp.sum(lp[2:3] * lp[3:4], axis=-1, keepdims=True)) + LAM_INIT)
    o12 = acc_ref[0:DF_DV, :] / acc_ref[DF_DV:DF_DV + 1, :]
    o_t = o12[:, 0:DF_TQ] - lam * o12[:, DF_TQ:2 * DF_TQ]
    o_ref[...] = (_rms_rows(o_t.T, gsub_ref[...]) * (1.0 - LAM_INIT)).astype(BF16)


def _df_attn(proj, lam_p, g_sub, batch, seq):
    nq = seq // DF_TQ
    const = lambda b, h, i: (0, 0)
    return pl.pallas_call(
        functools.partial(_df_kernel, seq=seq),
        out_shape=jax.ShapeDtypeStruct((batch * seq, DF_VW), BF16),
        grid=(batch, DF_HEADS, nq),
        in_specs=[
            pl.BlockSpec((DF_TQ, LANES), lambda b, h, i: (b * nq + i, CB_DF_Q + h)),
            pl.BlockSpec((seq, LANES), lambda b, h, i: (b, CB_DF_K + h)),
            pl.BlockSpec((seq, LANES), lambda b, h, i: (b, CB_DF_V + h)),
            pl.BlockSpec((4, DF_DQ), const),
            pl.BlockSpec((1, DF_DV), const),
        ],
        out_specs=pl.BlockSpec((DF_TQ, LANES), lambda b, h, i: (b * nq + i, h)),
        scratch_shapes=[pltpu.VMEM((2 * DF_TQ, LANES), BF16), pltpu.VMEM((DF_SBUF, DF_TK, 2 * DF_TQ), F32),
                        pltpu.VMEM((DF_VROWS, 2 * DF_TQ), F32), pltpu.VMEM((DF_VROWS, seq), BF16)],
        compiler_params=_params(("parallel", "parallel", "arbitrary")),
        name="df_attn",
    )(proj, proj, proj, lam_p, g_sub)


MG_TM, MG_TN = 2048, 512


def _merge_kernel(oa_ref, ob_ref, wa_ref, wb_ref, sa_ref, sb_ref, o_ref):
    ya = jnp.dot(oa_ref[...], wa_ref[...].astype(BF16), preferred_element_type=F32)
    yb = jnp.dot(ob_ref[...], wb_ref[...].astype(BF16), preferred_element_type=F32)
    o_ref[...] = (sa_ref[...].astype(F32) * ya + sb_ref[...].astype(F32) * yb).astype(BF16)


def _merge(o_a, o_b, w_na_out, w_df_out, proj):
    n = o_a.shape[0]
    ga, gb = COL_GATE_A // MG_TN, COL_GATE_B // MG_TN
    return pl.pallas_call(
        _merge_kernel,
        out_shape=jax.ShapeDtypeStruct((n, D_MODEL), BF16),
        grid=(n // MG_TM, D_MODEL // MG_TN),
        in_specs=[
            pl.BlockSpec((MG_TM, NA_W), lambda i, j: (i, 0)),
            pl.BlockSpec((MG_TM, DF_VW), lambda i, j: (i, 0)),
            pl.BlockSpec((NA_W, MG_TN), lambda i, j: (0, j)),
            pl.BlockSpec((DF_VW, MG_TN), lambda i, j: (0, j)),
            pl.BlockSpec((MG_TM, MG_TN), lambda i, j: (i, ga + j)),
            pl.BlockSpec((MG_TM, MG_TN), lambda i, j: (i, gb + j)),
        ],
        out_specs=pl.BlockSpec((MG_TM, MG_TN), lambda i, j: (i, j)),
        compiler_params=_params(("parallel", "arbitrary")),
        name="merge",
    )(o_a, o_b, w_na_out, w_df_out, proj, proj)


DN_TM, DN_TN = 2048, 256


def _ffn_down_kernel(a_ref, w_ref, x_ref, o_ref):
    o_ref[...] = x_ref[...] + jnp.dot(a_ref[...], w_ref[...].astype(BF16), preferred_element_type=F32)


def _ffn_down(a, w, x):
    n, k = a.shape
    d = w.shape[1]
    return pl.pallas_call(
        _ffn_down_kernel,
        out_shape=jax.ShapeDtypeStruct((n, d), F32),
        grid=(n // DN_TM, d // DN_TN),
        in_specs=[
            pl.BlockSpec((DN_TM, k), lambda i, j: (i, 0), pipeline_mode=pl.Buffered(1)),
            pl.BlockSpec((k, DN_TN), lambda i, j: (0, j)),
            pl.BlockSpec((DN_TM, DN_TN), lambda i, j: (i, j)),
        ],
        out_specs=pl.BlockSpec((DN_TM, DN_TN), lambda i, j: (i, j)),
        compiler_params=_params(("parallel", "arbitrary")),
        name="ffn_down",
    )(a, w, x)


OPN_TM = 256
CAST_ROWS = 256


def _out_proj_norm_kernel(a_ref, w_ref, x_ref, g_ref, o_ref, h_ref, wb_ref):
    @pl.when(pl.program_id(0) == 0)
    def _():
        def body(c, carry):
            r = pl.ds(pl.multiple_of(c * CAST_ROWS, CAST_ROWS), CAST_ROWS)
            wb_ref[r, :] = w_ref[r, :].astype(BF16)
            return carry

        lax.fori_loop(0, w_ref.shape[0] // CAST_ROWS, body, 0)

    y = x_ref[...] + jnp.dot(a_ref[...], wb_ref[...], preferred_element_type=F32)
    o_ref[...] = y
    h_ref[...] = _rms_rows(y, g_ref[...]).astype(BF16)


def _out_proj_norm(a, w, x, g):
    n, k = a.shape
    d = w.shape[1]
    return pl.pallas_call(
        _out_proj_norm_kernel,
        out_shape=(jax.ShapeDtypeStruct((n, d), F32), jax.ShapeDtypeStruct((n, d), BF16)),
        grid=(n // OPN_TM,),
        in_specs=[
            pl.BlockSpec((OPN_TM, k), lambda i: (i, 0)),
            pl.BlockSpec((k, d), lambda i: (0, 0), pipeline_mode=pl.Buffered(1)),
            pl.BlockSpec((OPN_TM, d), lambda i: (i, 0)),
            pl.BlockSpec((1, d), lambda i: (0, 0)),
        ],
        out_specs=(pl.BlockSpec((OPN_TM, d), lambda i: (i, 0)), pl.BlockSpec((OPN_TM, d), lambda i: (i, 0))),
        scratch_shapes=[pltpu.VMEM((k, d), BF16)],
        compiler_params=_params(("arbitrary",)),
        name="out_proj",
    )(a, w, x, g)


UP_TM, UP_TN = 2048, 512
UP_RB = 512


def _ffn_up_kernel(h_ref, wg_ref, wu_ref, o_ref):
    wg = wg_ref[...].astype(BF16)
    wu = wu_ref[...].astype(BF16)
    for r in range(UP_TM // UP_RB):
        rows = slice(r * UP_RB, (r + 1) * UP_RB)
        h = h_ref[rows, :]
        gate = jnp.dot(h, wg, preferred_element_type=F32)
        up = jnp.dot(h, wu, preferred_element_type=F32)
        o_ref[rows, :] = (gate * jax.nn.sigmoid(gate) * up).astype(BF16)


def _ffn_up(h, w_gate, w_up):
    n = h.shape[0]
    return pl.pallas_call(
        _ffn_up_kernel,
        out_shape=jax.ShapeDtypeStruct((n, FFN_HID), BF16),
        grid=(n // UP_TM, FFN_HID // UP_TN),
        in_specs=[
            pl.BlockSpec((UP_TM, D_MODEL), lambda i, j: (i, 0)),
            pl.BlockSpec((D_MODEL, UP_TN), lambda i, j: (0, j)),
            pl.BlockSpec((D_MODEL, UP_TN), lambda i, j: (0, j)),
        ],
        out_specs=pl.BlockSpec((UP_TM, UP_TN), lambda i, j: (i, j)),
        compiler_params=_params(("parallel", "arbitrary")),
        name="ffn_up",
    )(h, w_gate, w_up)


PLE_TM, PLE_TN = 1024, 512


def _ple_kernel(x_ref, g_ref, wg_ref, p_ref, wp_ref, o_ref, h_ref):
    @pl.when(pl.program_id(1) == 0)
    def _():
        _norm_block(x_ref, g_ref, h_ref)

    gate = jnp.dot(h_ref[...], wg_ref[...].astype(BF16), preferred_element_type=F32)
    emb = jnp.dot(p_ref[...].astype(BF16), wp_ref[...].astype(BF16), preferred_element_type=F32)
    cols = pl.ds(pl.multiple_of(pl.program_id(1) * PLE_TN, PLE_TN), PLE_TN)
    o_ref[...] = x_ref[:, cols] + jax.nn.sigmoid(gate) * emb


def _ple(x, g, w_gate, p, w_proj):
    n = x.shape[0]
    return pl.pallas_call(
        _ple_kernel,
        out_shape=jax.ShapeDtypeStruct((n, D_MODEL), F32),
        grid=(n // PLE_TM, D_MODEL // PLE_TN),
        in_specs=[
            pl.BlockSpec((PLE_TM, D_MODEL), lambda i, j: (i, 0)),
            pl.BlockSpec((1, D_MODEL), lambda i, j: (0, 0)),
            pl.BlockSpec((D_MODEL, PLE_TN), lambda i, j: (0, j)),
            pl.BlockSpec((PLE_TM, PLE_DIM), lambda i, j: (i, 0)),
            pl.BlockSpec((PLE_DIM, PLE_TN), lambda i, j: (0, j)),
        ],
        out_specs=pl.BlockSpec((PLE_TM, PLE_TN), lambda i, j: (i, j)),
        scratch_shapes=[pltpu.VMEM((PLE_TM, D_MODEL), BF16)],
        compiler_params=_params(("parallel", "arbitrary")),
        name="ple",
    )(x, g, w_gate, p, w_proj)


def _df_cols(w):
    k = w.shape[0]
    return w.reshape(k, DF_HEADS, 2, 2, DF_DQ // 2).transpose(0, 1, 3, 2, 4).reshape(k, DF_QW)


def _rope_tables(seq, g, scale):
    half = DF_DQ // 2
    inv = 1.0 / (ROPE_THETA ** (jnp.arange(0, DF_DQ, 2, dtype=F32) / DF_DQ))
    ang = jnp.arange(seq, dtype=F32)[:, None] * inv[None, :]
    cos, sin = jnp.cos(ang), jnp.sin(ang)
    g1, g2 = g[:half], g[half:]
    ca = jnp.concatenate([cos * g1, cos * g1, cos * g2, cos * g2], axis=1) * scale
    sb = jnp.concatenate([-sin * g2, -sin * g2, sin * g1, sin * g1], axis=1) * scale
    return ca, sb


def kernel(x, p, g_mix, w_in, g_na_q, g_na_k, na_rpb, g_df_q, g_df_k, lam_q1, lam_k1, lam_q2, lam_k2,
           g_df_sub, w_na_out, w_df_out, w_o, g_ffn, w_gate, w_up, w_down, g_ple, w_ple_gate, w_ple_proj):
    batch, seq, d = x.shape
    n = batch * seq
    rows = seq // GRID_W
    depth = w_in.shape[0]
    xf = x.reshape(n, d)
    for i in range(depth):
        s0 = 3 * NA_W
        w_dfqk = jnp.concatenate([_df_cols(w_in[i][:, s0:s0 + DF_QW]), _df_cols(w_in[i][:, s0 + DF_QW:s0 + 2 * DF_QW])],
                                 axis=1)
        caq, sbq = _rope_tables(seq, g_df_q[i], DF_DQ ** -0.5 * math.log2(math.e))
        cak, sbk = _rope_tables(seq, g_df_k[i], 1.0)
        proj = _in_proj(_rms_norm(xf, g_mix[i][None]), w_in[i], w_dfqk, g_na_q[i][None], g_na_k[i][None],
                        caq, sbq, cak, sbk, seq)

        o_a = _na_attn(proj, _na_bias_rows(na_rpb[i], rows), batch, seq)
        lam_p = jnp.stack([lam_q1[i], lam_k1[i], lam_q2[i], lam_k2[i]], axis=0)
        o_b = _df_attn(proj, lam_p, g_df_sub[i][None], batch, seq)

        merged = _merge(o_a, o_b, w_na_out[i], w_df_out[i], proj)
        xf, hf = _out_proj_norm(merged, w_o[i], xf, g_ffn[i][None])

        act = _ffn_up(hf, w_gate[i], w_up[i])
        xf = _ffn_down(act, w_down[i], xf)

        xf = _ple(xf, g_ple[i][None], w_ple_gate[i], p[i].reshape(n, PLE_DIM), w_ple_proj[i])
    return xf.reshape(batch, seq, d)
```

```python
import functools
import math

import numpy as np
import jax
import jax.numpy as jnp
from jax import lax
from jax.experimental import pallas as pl
from jax.experimental.pallas import tpu as pltpu

D_MODEL = 2048
GRID_W = 64
NA_HEADS = 8
NA_DH = 128
NA_KR = 8
NA_KW = 16
DF_HEADS = 8
DF_DQ = 64
DF_DV = 128
FFN_HID = 5632
PLE_DIM = 256
ROPE_THETA = 10000.0
EPS = 1e-6
LAM_INIT = 0.8 - 0.6 * math.exp(-0.3 * 0)

NA_W = NA_HEADS * NA_DH
DF_QW = DF_HEADS * 2 * DF_DQ
DF_VW = DF_HEADS * DF_DV
IN_COLS = 3 * NA_W + 2 * DF_QW + DF_VW + 2 * D_MODEL

LANES = 128
NEG = -1e30
VMEM_LIMIT = 56 * 1024 * 1024

F32 = jnp.float32
BF16 = jnp.bfloat16

CB_NA_Q, CB_NA_K, CB_NA_V = 0, 8, 16
CB_DF_Q, CB_DF_K, CB_DF_V = 24, 32, 40
COL_GATE_A, COL_GATE_B = 6144, 8192


def _params(sem):
    return pltpu.CompilerParams(dimension_semantics=sem, vmem_limit_bytes=VMEM_LIMIT)


def _rms_rows(x, g):
    ms = jnp.mean(x * x, axis=-1, keepdims=True)
    return x * lax.rsqrt(ms + EPS) * g


ONES_ROWS = 16


def _with_ones_rows(v_t):
    row = lax.broadcasted_iota(jnp.int32, (ONES_ROWS, v_t.shape[1]), 0)
    return jnp.concatenate([v_t, jnp.where(row == 0, 1.0, 0.0).astype(v_t.dtype)], axis=0)


NORM_ROWS = 256


def _norm_block(x_ref, g_ref, h_ref):
    def body(c, carry):
        r = pl.ds(pl.multiple_of(c * NORM_ROWS, NORM_ROWS), NORM_ROWS)
        h_ref[r, :] = _rms_rows(x_ref[r, :], g_ref[...]).astype(BF16)
        return carry

    lax.fori_loop(0, x_ref.shape[0] // NORM_ROWS, body, 0)


def _rms_norm_kernel(x_ref, g_ref, o_ref):
    _norm_block(x_ref, g_ref, o_ref)


def _rms_norm(x, g, tm=512):
    n, d = x.shape
    return pl.pallas_call(
        _rms_norm_kernel,
        out_shape=jax.ShapeDtypeStruct((n, d), BF16),
        grid=(n // tm,),
        in_specs=[pl.BlockSpec((tm, d), lambda i: (i, 0)), pl.BlockSpec((1, d), lambda i: (0, 0))],
        out_specs=pl.BlockSpec((tm, d), lambda i: (i, 0)),
        compiler_params=_params(("parallel",)),
        name="rms_norm",
    )(x, g)


IN_TM, IN_TN = 2048, 512
IN_RB = 256
SEG_TILES = 1024 // IN_TN


def _in_proj_kernel(h_ref, w_ref, gq_ref, gk_ref, caq_ref, sbq_ref, cak_ref, sbk_ref, o_ref):
    seg = pl.program_id(1) // SEG_TILES
    heads = IN_TN // LANES

    def rotary_layout(w):
        lane = lax.broadcasted_iota(jnp.int32, (w.shape[0], LANES), 1)
        quarter = DF_DQ // 2
        src = (lane // quarter % 2) * DF_DQ + (lane // DF_DQ) * quarter + lane % quarter
        return jnp.concatenate([jnp.take_along_axis(w[:, k * LANES:(k + 1) * LANES], src, axis=1)
                                for k in range(heads)], axis=1)

    def run(w, epilogue):
        wb = w.astype(BF16)
        for r in range(IN_TM // IN_RB):
            rows = slice(r * IN_RB, (r + 1) * IN_RB)
            epilogue(jnp.dot(h_ref[rows, :], wb, preferred_element_type=F32), rows)

    def na_norm(g_ref):
        def epilogue(y, rows):
            for k in range(heads):
                cols = slice(k * LANES, (k + 1) * LANES)
                o_ref[rows, cols] = _rms_rows(y[:, cols], g_ref[...]).astype(BF16)
        return epilogue

    def df_norm_rope(ca_ref, sb_ref):
        def epilogue(y, rows):
            lane = lax.broadcasted_iota(jnp.int32, (1, LANES), 1)
            comp0 = (lane % 64) < 32
            ca = ca_ref[rows, :]
            sb = sb_ref[rows, :]
            for k in range(heads):
                cols = slice(k * LANES, (k + 1) * LANES)
                yk = y[:, cols]
                pk = pltpu.roll(yk, 64, 1)
                z = yk * yk + pk * pk
                s0 = jnp.sum(jnp.where(comp0, z, 0.0), axis=-1, keepdims=True)
                s1 = jnp.sum(jnp.where(comp0, 0.0, z), axis=-1, keepdims=True)
                r = jnp.where(comp0, lax.rsqrt(s0 * (0.5 / DF_DQ) + EPS), lax.rsqrt(s1 * (0.5 / DF_DQ) + EPS))
                o_ref[rows, cols] = ((yk * ca + pk * sb) * r).astype(BF16)
        return epilogue

    def plain(y, rows):
        o_ref[rows, :] = y.astype(BF16)

    def gate(y, rows):
        o_ref[rows, :] = jax.nn.sigmoid(y).astype(BF16)

    @pl.when(seg == 0)
    def _():
        run(w_ref[...], na_norm(gq_ref))

    @pl.when(seg == 1)
    def _():
        run(w_ref[...], na_norm(gk_ref))

    @pl.when(seg == 3)
    def _():
        run(rotary_layout(w_ref[...]), df_norm_rope(caq_ref, sbq_ref))

    @pl.when(seg == 4)
    def _():
        run(rotary_layout(w_ref[...]), df_norm_rope(cak_ref, sbk_ref))

    @pl.when((seg == 2) | (seg == 5))
    def _():
        run(w_ref[...], plain)

    @pl.when(seg >= 6)
    def _():
        run(w_ref[...], gate)


def _in_proj(h, w_in, g_na_q, g_na_k, caq, sbq, cak, sbk, seq):
    n = h.shape[0]
    pos_blocks = seq // IN_TM
    row = lambda i, j: (i, 0)
    const = lambda i, j: (0, 0)
    tab = lambda i, j: (i % pos_blocks, 0)
    return pl.pallas_call(
        _in_proj_kernel,
        out_shape=jax.ShapeDtypeStruct((n, IN_COLS), BF16),
        grid=(n // IN_TM, IN_COLS // IN_TN),
        in_specs=[
            pl.BlockSpec((IN_TM, D_MODEL), row),
            pl.BlockSpec((D_MODEL, IN_TN), lambda i, j: (0, j)),
            pl.BlockSpec((1, LANES), const),
            pl.BlockSpec((1, LANES), const),
            pl.BlockSpec((IN_TM, LANES), tab),
            pl.BlockSpec((IN_TM, LANES), tab),
            pl.BlockSpec((IN_TM, LANES), tab),
            pl.BlockSpec((IN_TM, LANES), tab),
        ],
        out_specs=pl.BlockSpec((IN_TM, IN_TN), lambda i, j: (i, j)),
        compiler_params=_params(("parallel", "arbitrary")),
        name="in_proj",
    )(h, w_in, g_na_q, g_na_k, caq, sbq, cak, sbk)


NA_QROWS = 8
NA_WROWS = 16
NA_TQ = NA_QROWS * GRID_W
NA_TK = NA_WROWS * GRID_W
NA_PAIR = LANES // GRID_W
NA_NPAIR = NA_QROWS // NA_PAIR
NA_PROWS = NA_KR + NA_PAIR - 1
NA_PK = NA_PROWS * GRID_W
NA_NTAB = 1 + 2 * NA_NPAIR
NA_NEG_SLAB = 2 * NA_KR - 1


def _na_geometry(rows):
    nt = rows // NA_QROWS
    assert nt >= 3 and rows >= NA_WROWS
    ws = np.zeros((3, NA_NPAIR), np.int64)
    idx = np.full((NA_NTAB, NA_PROWS, NA_PAIR), NA_NEG_SLAB, np.int64)
    for ci, t in enumerate((0, 1, nt - 1)):
        w0 = int(np.clip(t * NA_QROWS - NA_KR // 2, 0, rows - NA_WROWS))
        for pi in range(NA_NPAIR):
            r = [t * NA_QROWS + NA_PAIR * pi + hb for hb in range(NA_PAIR)]
            rs = [int(np.clip(ri - NA_KR // 2, 0, rows - NA_KR)) for ri in r]
            start = min(min(rs) - w0, NA_WROWS - NA_PROWS)
            assert 0 <= start and max(rs) + NA_KR <= w0 + start + NA_PROWS and min(rs) >= w0 + start
            ws[ci, pi] = start
            tab = 0 if ci == 1 else 1 + (0 if ci == 0 else NA_NPAIR) + pi
            for wp in range(NA_PROWS):
                key_row = w0 + start + wp
                for hb in range(NA_PAIR):
                    if rs[hb] <= key_row < rs[hb] + NA_KR:
                        idx[tab, wp, hb] = key_row - r[hb] + NA_KR - 1
    combos = sorted({tuple(int(v) for v in pair) for pair in idx.reshape(-1, NA_PAIR)})
    slab = [[combos.index(tuple(int(v) for v in idx[tab, wp])) for wp in range(NA_PROWS)] for tab in range(NA_NTAB)]
    return [[int(v) for v in row] for row in ws], combos, slab


def _na_kernel(q_ref, k_ref, v_ref, row_ref, o_ref, p_ref, slab_ref, *, rows):
    nt = rows // NA_QROWS
    ws, combos, slab = _na_geometry(rows)

    kc = lax.broadcasted_iota(jnp.int32, (GRID_W, LANES), 0)
    lane = lax.broadcasted_iota(jnp.int32, (GRID_W, LANES), 1)
    c = lane % GRID_W
    cs = jnp.clip(c - NA_KW // 2, 0, GRID_W - NA_KW)
    in_window = (kc >= cs) & (kc < cs + NA_KW)
    src = jnp.clip(kc - c + NA_KW - 1, 0, 2 * NA_KW - 2) + jnp.where(lane < GRID_W, 0, GRID_W)
    for ci, (left, right) in enumerate(combos):
        ok = in_window
        if left == NA_NEG_SLAB:
            ok = ok & (lane >= GRID_W)
        if right == NA_NEG_SLAB:
            ok = ok & (lane < GRID_W)
        row = jnp.broadcast_to(row_ref[0, ci:ci + 1, :], (GRID_W, LANES))
        slab_ref[ci] = jnp.where(ok, jnp.take_along_axis(row, src, axis=1), NEG)

    def window(t):
        return int(np.clip(t * NA_QROWS - NA_KR // 2, 0, rows - NA_WROWS)) * GRID_W

    def probs(t):
        cls = 0 if t == 0 else (2 if t == nt - 1 else 1)
        buf = t
        p_ref[buf] = jnp.zeros(p_ref.shape[1:], BF16)
        for pi in range(NA_NPAIR):
            lanes = slice(pi * LANES, (pi + 1) * LANES)
            tab = 0 if cls == 1 else 1 + (0 if cls == 0 else NA_NPAIR) + pi
            r0 = ws[cls][pi] * GRID_W
            kp = k_ref[window(t) + r0:window(t) + r0 + NA_PK, :]
            qp = q_ref[t * NA_TQ + pi * LANES:t * NA_TQ + (pi + 1) * LANES, :]
            sp = lax.dot_general(kp, qp, (((1,), (1,)), ((), ())), preferred_element_type=F32)
            bias = jnp.concatenate([slab_ref[ci] for ci in slab[tab]], axis=0)
            sp = sp * (NA_DH ** -0.5) + bias
            m = jnp.max(sp, axis=0, keepdims=True)
            p_ref[buf, r0:r0 + NA_PK, lanes] = jnp.exp(sp - m).astype(BF16)

    def outputs(t):
        vw = _with_ones_rows(v_ref[window(t):window(t) + NA_TK, :].T)
        ov = jnp.dot(vw, p_ref[t], preferred_element_type=F32)
        o_ref[t * NA_TQ:(t + 1) * NA_TQ, :] = (ov[0:NA_DH, :] / ov[NA_DH:NA_DH + 1, :]).T.astype(BF16)

    for t in range(nt + 1):
        if t < nt:
            probs(t)
        if t >= 1:
            outputs(t - 1)


def _na_attn(proj, bias_rows, batch, seq):
    rows = seq // GRID_W
    n_slabs = bias_rows.shape[1]
    return pl.pallas_call(
        functools.partial(_na_kernel, rows=rows),
        out_shape=jax.ShapeDtypeStruct((batch * seq, NA_W), BF16),
        grid=(batch, NA_HEADS),
        in_specs=[
            pl.BlockSpec((seq, LANES), lambda b, h: (b, CB_NA_Q + h)),
            pl.BlockSpec((seq, LANES), lambda b, h: (b, CB_NA_K + h)),
            pl.BlockSpec((seq, LANES), lambda b, h: (b, CB_NA_V + h)),
            pl.BlockSpec((1, n_slabs, LANES), lambda b, h: (h, 0, 0)),
        ],
        out_specs=pl.BlockSpec((seq, LANES), lambda b, h: (b, h)),
        scratch_shapes=[pltpu.VMEM((rows // NA_QROWS, NA_TK, NA_TQ), BF16),
                        pltpu.VMEM((n_slabs, GRID_W, LANES), F32)],
        compiler_params=_params(("parallel", "arbitrary")),
        name="na_attn",
    )(proj, proj, proj, bias_rows)


def _na_bias_rows(rpb, rows):
    _, combos, _ = _na_geometry(rows)
    padded = jnp.pad(rpb, ((0, 0), (0, 1), (0, GRID_W - (2 * NA_KW - 1))))
    left = jnp.take(padded, jnp.asarray([cb[0] for cb in combos], jnp.int32), axis=1)
    right = jnp.take(padded, jnp.asarray([cb[1] for cb in combos], jnp.int32), axis=1)
    return jnp.concatenate([left, right], axis=-1)


DF_TQ, DF_TK = 1024, 512
DF_TG = 256
DF_SBUF = 2
DF_VROWS = DF_DV + ONES_ROWS


def _df_kernel(q_ref, k_ref, v_ref, lam_ref, gsub_ref, o_ref, q12_ref, s_ref, acc_ref, vt_ref, *, seq):
    n_chunks = seq // DF_TK

    @pl.when(pl.program_id(2) == 0)
    def _():
        for c in range(n_chunks):
            cols = slice(c * DF_TK, (c + 1) * DF_TK)
            vt_ref[:, cols] = _with_ones_rows(v_ref[cols, :].T)

    lane = lax.broadcasted_iota(jnp.int32, (1, LANES), 1)
    comp0 = (lane % 64) < 32
    q = q_ref[...]
    zero = jnp.zeros_like(q)
    q12_ref[0:DF_TQ, :] = jnp.where(comp0, q, zero)
    q12_ref[DF_TQ:2 * DF_TQ, :] = jnp.where(comp0, zero, q)
    acc_ref[...] = jnp.zeros(acc_ref.shape, F32)

    groups = 2 * DF_TQ // DF_TG

    def scores(kc, g):
        cols = slice(g * DF_TG, (g + 1) * DF_TG)
        return lax.dot_general(kc, q12_ref[cols, :], (((1,), (1,)), ((), ())), preferred_element_type=F32)

    def keys(c):
        return k_ref[c * DF_TK:(c + 1) * DF_TK, :]

    def step(c, carry):
        cur, nxt = c % DF_SBUF, (c + 1) % DF_SBUF
        vc = vt_ref[:, c * DF_TK:(c + 1) * DF_TK]
        kn = keys(c + 1) if c + 1 < n_chunks else None
        out = []
        for g in range(groups):
            m_prev = carry[g]
            cols = slice(g * DF_TG, (g + 1) * DF_TG)
            if kn is not None:
                s_ref[nxt, :, cols] = scores(kn, g)
            s = s_ref[cur, :, cols]
            m_new = jnp.maximum(m_prev, jnp.max(s, axis=0, keepdims=True))
            alpha = jnp.exp2(m_prev - m_new)
            p = jnp.exp2(s - m_new).astype(BF16)
            pv = jnp.dot(vc, p, preferred_element_type=F32)
            acc_ref[:, cols] = alpha * acc_ref[:, cols] + pv
            out.append(m_new)
        return tuple(out)

    k0 = keys(0)
    for g in range(groups):
        s_ref[0, :, g * DF_TG:(g + 1) * DF_TG] = scores(k0, g)

    carry = tuple(jnp.full((1, DF_TG), -jnp.inf, F32) for _ in range(groups))
    for c in range(n_chunks):
        carry = step(c, carry)

    lp = lam_ref[...]
    lam = (jnp.exp(jnp.sum(lp[0:1] * lp[1:2], axis=-1, keepdims=True))
           - jnp.exp(jnp.sum(lp[2:3] * lp[3:4], axis=-1, keepdims=True)) + LAM_INIT)
    o12 = acc_ref[0:DF_DV, :] / acc_ref[DF_DV:DF_DV + 1, :]
    o_t = o12[:, 0:DF_TQ] - lam * o12[:, DF_TQ:2 * DF_TQ]
    o_ref[...] = (_rms_rows(o_t.T, gsub_ref[...]) * (1.0 - LAM_INIT)).astype(BF16)


def _df_attn(proj, lam_p, g_sub, batch, seq):
    nq = seq // DF_TQ
    const = lambda b, h, i: (0, 0)
    return pl.pallas_call(
        functools.partial(_df_kernel, seq=seq),
        out_shape=jax.ShapeDtypeStruct((batch * seq, DF_VW), BF16),
        grid=(batch, DF_HEADS, nq),
        in_specs=[
            pl.BlockSpec((DF_TQ, LANES), lambda b, h, i: (b * nq + i, CB_DF_Q + h)),
            pl.BlockSpec((seq, LANES), lambda b, h, i: (b, CB_DF_K + h)),
            pl.BlockSpec((seq, LANES), lambda b, h, i: (b, CB_DF_V + h)),
            pl.BlockSpec((4, DF_DQ), const),
            pl.BlockSpec((1, DF_DV), const),
        ],
        out_specs=pl.BlockSpec((DF_TQ, LANES), lambda b, h, i: (b * nq + i, h)),
        scratch_shapes=[pltpu.VMEM((2 * DF_TQ, LANES), BF16), pltpu.VMEM((DF_SBUF, DF_TK, 2 * DF_TQ), F32),
                        pltpu.VMEM((DF_VROWS, 2 * DF_TQ), F32), pltpu.VMEM((DF_VROWS, seq), BF16)],
        compiler_params=_params(("parallel", "parallel", "arbitrary")),
        name="df_attn",
    )(proj, proj, proj, lam_p, g_sub)


MG_TM, MG_TN = 2048, 512


def _merge_kernel(oa_ref, ob_ref, wa_ref, wb_ref, sa_ref, sb_ref, o_ref):
    ya = jnp.dot(oa_ref[...], wa_ref[...].astype(BF16), preferred_element_type=F32)
    yb = jnp.dot(ob_ref[...], wb_ref[...].astype(BF16), preferred_element_type=F32)
    o_ref[...] = (sa_ref[...].astype(F32) * ya + sb_ref[...].astype(F32) * yb).astype(BF16)


def _merge(o_a, o_b, w_na_out, w_df_out, proj):
    n = o_a.shape[0]
    ga, gb = COL_GATE_A // MG_TN, COL_GATE_B // MG_TN
    return pl.pallas_call(
        _merge_kernel,
        out_shape=jax.ShapeDtypeStruct((n, D_MODEL), BF16),
        grid=(n // MG_TM, D_MODEL // MG_TN),
        in_specs=[
            pl.BlockSpec((MG_TM, NA_W), lambda i, j: (i, 0)),
            pl.BlockSpec((MG_TM, DF_VW), lambda i, j: (i, 0)),
            pl.BlockSpec((NA_W, MG_TN), lambda i, j: (0, j)),
            pl.BlockSpec((DF_VW, MG_TN), lambda i, j: (0, j)),
            pl.BlockSpec((MG_TM, MG_TN), lambda i, j: (i, ga + j)),
            pl.BlockSpec((MG_TM, MG_TN), lambda i, j: (i, gb + j)),
        ],
        out_specs=pl.BlockSpec((MG_TM, MG_TN), lambda i, j: (i, j)),
        compiler_params=_params(("parallel", "arbitrary")),
        name="merge",
    )(o_a, o_b, w_na_out, w_df_out, proj, proj)


DN_TM, DN_TN = 1024, 256


def _ffn_down_kernel(a_ref, w_ref, x_ref, o_ref):
    o_ref[...] = x_ref[...] + jnp.dot(a_ref[...], w_ref[...].astype(BF16), preferred_element_type=F32)


def _ffn_down(a, w, x):
    n, k = a.shape
    d = w.shape[1]
    return pl.pallas_call(
        _ffn_down_kernel,
        out_shape=jax.ShapeDtypeStruct((n, d), F32),
        grid=(n // DN_TM, d // DN_TN),
        in_specs=[
            pl.BlockSpec((DN_TM, k), lambda i, j: (i, 0)),
            pl.BlockSpec((k, DN_TN), lambda i, j: (0, j)),
            pl.BlockSpec((DN_TM, DN_TN), lambda i, j: (i, j)),
        ],
        out_specs=pl.BlockSpec((DN_TM, DN_TN), lambda i, j: (i, j)),
        compiler_params=_params(("parallel", "arbitrary")),
        name="ffn_down",
    )(a, w, x)


OPN_TM = 256
CAST_ROWS = 256


def _out_proj_norm_kernel(a_ref, w_ref, x_ref, g_ref, o_ref, h_ref, wb_ref):
    @pl.when(pl.program_id(0) == 0)
    def _():
        def body(c, carry):
            r = pl.ds(pl.multiple_of(c * CAST_ROWS, CAST_ROWS), CAST_ROWS)
            wb_ref[r, :] = w_ref[r, :].astype(BF16)
            return carry

        lax.fori_loop(0, w_ref.shape[0] // CAST_ROWS, body, 0)

    y = x_ref[...] + jnp.dot(a_ref[...], wb_ref[...], preferred_element_type=F32)
    o_ref[...] = y
    h_ref[...] = _rms_rows(y, g_ref[...]).astype(BF16)


def _out_proj_norm(a, w, x, g):
    n, k = a.shape
    d = w.shape[1]
    return pl.pallas_call(
        _out_proj_norm_kernel,
        out_shape=(jax.ShapeDtypeStruct((n, d), F32), jax.ShapeDtypeStruct((n, d), BF16)),
        grid=(n // OPN_TM,),
        in_specs=[
            pl.BlockSpec((OPN_TM, k), lambda i: (i, 0)),
            pl.BlockSpec((k, d), lambda i: (0, 0), pipeline_mode=pl.Buffered(1)),
            pl.BlockSpec((OPN_TM, d), lambda i: (i, 0)),
            pl.BlockSpec((1, d), lambda i: (0, 0)),
        ],
        out_specs=(pl.BlockSpec((OPN_TM, d), lambda i: (i, 0)), pl.BlockSpec((OPN_TM, d), lambda i: (i, 0))),
        scratch_shapes=[pltpu.VMEM((k, d), BF16)],
        compiler_params=_params(("arbitrary",)),
        name="out_proj",
    )(a, w, x, g)


UP_TM, UP_TN = 2048, 512
UP_RB = 512


def _ffn_up_kernel(h_ref, wg_ref, wu_ref, o_ref):
    wg = wg_ref[...].astype(BF16)
    wu = wu_ref[...].astype(BF16)
    for r in range(UP_TM // UP_RB):
        rows = slice(r * UP_RB, (r + 1) * UP_RB)
        h = h_ref[rows, :]
        gate = jnp.dot(h, wg, preferred_element_type=F32)
        up = jnp.dot(h, wu, preferred_element_type=F32)
        o_ref[rows, :] = (gate * jax.nn.sigmoid(gate) * up).astype(BF16)


def _ffn_up(h, w_gate, w_up):
    n = h.shape[0]
    return pl.pallas_call(
        _ffn_up_kernel,
        out_shape=jax.ShapeDtypeStruct((n, FFN_HID), BF16),
        grid=(n // UP_TM, FFN_HID // UP_TN),
        in_specs=[
            pl.BlockSpec((UP_TM, D_MODEL), lambda i, j: (i, 0)),
            pl.BlockSpec((D_MODEL, UP_TN), lambda i, j: (0, j)),
            pl.BlockSpec((D_MODEL, UP_TN), lambda i, j: (0, j)),
        ],
        out_specs=pl.BlockSpec((UP_TM, UP_TN), lambda i, j: (i, j)),
        compiler_params=_params(("parallel", "arbitrary")),
        name="ffn_up",
    )(h, w_gate, w_up)


PLE_TM, PLE_TN = 1024, 512


def _ple_kernel(x_ref, g_ref, wg_ref, p_ref, wp_ref, o_ref, h_ref):
    @pl.when(pl.program_id(1) == 0)
    def _():
        _norm_block(x_ref, g_ref, h_ref)

    gate = jnp.dot(h_ref[...], wg_ref[...].astype(BF16), preferred_element_type=F32)
    emb = jnp.dot(p_ref[...].astype(BF16), wp_ref[...].astype(BF16), preferred_element_type=F32)
    cols = pl.ds(pl.multiple_of(pl.program_id(1) * PLE_TN, PLE_TN), PLE_TN)
    o_ref[...] = x_ref[:, cols] + jax.nn.sigmoid(gate) * emb


def _ple(x, g, w_gate, p, w_proj):
    n = x.shape[0]
    return pl.pallas_call(
        _ple_kernel,
        out_shape=jax.ShapeDtypeStruct((n, D_MODEL), F32),
        grid=(n // PLE_TM, D_MODEL // PLE_TN),
        in_specs=[
            pl.BlockSpec((PLE_TM, D_MODEL), lambda i, j: (i, 0)),
            pl.BlockSpec((1, D_MODEL), lambda i, j: (0, 0)),
            pl.BlockSpec((D_MODEL, PLE_TN), lambda i, j: (0, j)),
            pl.BlockSpec((PLE_TM, PLE_DIM), lambda i, j: (i, 0)),
            pl.BlockSpec((PLE_DIM, PLE_TN), lambda i, j: (0, j)),
        ],
        out_specs=pl.BlockSpec((PLE_TM, PLE_TN), lambda i, j: (i, j)),
        scratch_shapes=[pltpu.VMEM((PLE_TM, D_MODEL), BF16)],
        compiler_params=_params(("parallel", "arbitrary")),
        name="ple",
    )(x, g, w_gate, p, w_proj)


def _rope_tables(seq, g, scale):
    half = DF_DQ // 2
    inv = 1.0 / (ROPE_THETA ** (jnp.arange(0, DF_DQ, 2, dtype=F32) / DF_DQ))
    ang = jnp.arange(seq, dtype=F32)[:, None] * inv[None, :]
    cos, sin = jnp.cos(ang), jnp.sin(ang)
    g1, g2 = g[:half], g[half:]
    ca = jnp.concatenate([cos * g1, cos * g1, cos * g2, cos * g2], axis=1) * scale
    sb = jnp.concatenate([-sin * g2, -sin * g2, sin * g1, sin * g1], axis=1) * scale
    return ca, sb


def kernel(x, p, g_mix, w_in, g_na_q, g_na_k, na_rpb, g_df_q, g_df_k, lam_q1, lam_k1, lam_q2, lam_k2,
           g_df_sub, w_na_out, w_df_out, w_o, g_ffn, w_gate, w_up, w_down, g_ple, w_ple_gate, w_ple_proj):
    batch, seq, d = x.shape
    n = batch * seq
    rows = seq // GRID_W
    depth = w_in.shape[0]
    xf = x.reshape(n, d)
    for i in range(depth):
        caq, sbq = _rope_tables(seq, g_df_q[i], DF_DQ ** -0.5 * math.log2(math.e))
        cak, sbk = _rope_tables(seq, g_df_k[i], 1.0)
        proj = _in_proj(_rms_norm(xf, g_mix[i][None]), w_in[i], g_na_q[i][None], g_na_k[i][None],
                        caq, sbq, cak, sbk, seq)

        o_a = _na_attn(proj, _na_bias_rows(na_rpb[i], rows), batch, seq)
        lam_p = jnp.stack([lam_q1[i], lam_k1[i], lam_q2[i], lam_k2[i]], axis=0)
        o_b = _df_attn(proj, lam_p, g_df_sub[i][None], batch, seq)

        merged = _merge(o_a, o_b, w_na_out[i], w_df_out[i], proj)
        xf, hf = _out_proj_norm(merged, w_o[i], xf, g_ffn[i][None])

        act = _ffn_up(hf, w_gate[i], w_up[i])
        xf = _ffn_down(act, w_down[i], xf)

        xf = _ple(xf, g_ple[i][None], w_ple_gate[i], p[i].reshape(n, PLE_DIM), w_ple_proj[i])
    return xf.reshape(batch, seq, d)
```

```python
import functools
import math

import numpy as np
import jax
import jax.numpy as jnp
from jax import lax
from jax.experimental import pallas as pl
from jax.experimental.pallas import tpu as pltpu

D_MODEL = 2048
GRID_W = 64
NA_HEADS = 8
NA_DH = 128
NA_KR = 8
NA_KW = 16
DF_HEADS = 8
DF_DQ = 64
DF_DV = 128
FFN_HID = 5632
PLE_DIM = 256
ROPE_THETA = 10000.0
EPS = 1e-6
LAM_INIT = 0.8 - 0.6 * math.exp(-0.3 * 0)

NA_W = NA_HEADS * NA_DH
DF_QW = DF_HEADS * 2 * DF_DQ
DF_VW = DF_HEADS * DF_DV
IN_COLS = 3 * NA_W + 2 * DF_QW + DF_VW + 2 * D_MODEL

LANES = 128
NEG = -1e30
VMEM_LIMIT = 56 * 1024 * 1024

F32 = jnp.float32
BF16 = jnp.bfloat16

CB_NA_Q, CB_NA_K, CB_NA_V = 0, 8, 16
CB_DF_Q, CB_DF_K, CB_DF_V = 24, 32, 40
COL_GATE_A, COL_GATE_B = 6144, 8192


def _params(sem):
    return pltpu.CompilerParams(dimension_semantics=sem, vmem_limit_bytes=VMEM_LIMIT)


def _rms_rows(x, g):
    ms = jnp.mean(x * x, axis=-1, keepdims=True)
    return x * lax.rsqrt(ms + EPS) * g


ONES_ROWS = 16


def _with_ones_rows(v_t):
    row = lax.broadcasted_iota(jnp.int32, (ONES_ROWS, v_t.shape[1]), 0)
    return jnp.concatenate([v_t, jnp.where(row == 0, 1.0, 0.0).astype(v_t.dtype)], axis=0)


NORM_ROWS = 256


def _norm_block(x_ref, g_ref, h_ref):
    def body(c, carry):
        r = pl.ds(pl.multiple_of(c * NORM_ROWS, NORM_ROWS), NORM_ROWS)
        h_ref[r, :] = _rms_rows(x_ref[r, :], g_ref[...]).astype(BF16)
        return carry

    lax.fori_loop(0, x_ref.shape[0] // NORM_ROWS, body, 0)


def _rms_norm_kernel(x_ref, g_ref, o_ref):
    _norm_block(x_ref, g_ref, o_ref)


def _rms_norm(x, g, tm=512):
    n, d = x.shape
    return pl.pallas_call(
        _rms_norm_kernel,
        out_shape=jax.ShapeDtypeStruct((n, d), BF16),
        grid=(n // tm,),
        in_specs=[pl.BlockSpec((tm, d), lambda i: (i, 0)), pl.BlockSpec((1, d), lambda i: (0, 0))],
        out_specs=pl.BlockSpec((tm, d), lambda i: (i, 0)),
        compiler_params=_params(("parallel",)),
        name="rms_norm",
    )(x, g)


CAST_ROWS = 256


def _cast_resident(w_ref, wb_ref):
    rows = min(CAST_ROWS, w_ref.shape[0])

    def body(c, carry):
        r = pl.ds(pl.multiple_of(c * rows, rows), rows)
        wb_ref[r, :] = w_ref[r, :].astype(BF16)
        return carry

    lax.fori_loop(0, w_ref.shape[0] // rows, body, 0)


def _resident(shape):
    return pl.BlockSpec(shape, lambda i: (0, 0), pipeline_mode=pl.Buffered(1))


ROW_TM = 256


IN_TM, IN_TN = 2048, 512
IN_RB = 256
SEG_TILES = 1024 // IN_TN


def _in_proj_kernel(h_ref, w_ref, gq_ref, gk_ref, caq_ref, sbq_ref, cak_ref, sbk_ref, o_ref):
    seg = pl.program_id(1) // SEG_TILES
    heads = IN_TN // LANES

    def rotary_layout(w):
        lane = lax.broadcasted_iota(jnp.int32, (w.shape[0], LANES), 1)
        quarter = DF_DQ // 2
        src = (lane // quarter % 2) * DF_DQ + (lane // DF_DQ) * quarter + lane % quarter
        return jnp.concatenate([jnp.take_along_axis(w[:, k * LANES:(k + 1) * LANES], src, axis=1)
                                for k in range(heads)], axis=1)

    def run(w, epilogue):
        wb = w.astype(BF16)
        for r in range(IN_TM // IN_RB):
            rows = slice(r * IN_RB, (r + 1) * IN_RB)
            epilogue(jnp.dot(h_ref[rows, :], wb, preferred_element_type=F32), rows)

    def na_norm(g_ref):
        def epilogue(y, rows):
            for k in range(heads):
                cols = slice(k * LANES, (k + 1) * LANES)
                o_ref[rows, cols] = _rms_rows(y[:, cols], g_ref[...]).astype(BF16)
        return epilogue

    def df_norm_rope(ca_ref, sb_ref):
        def epilogue(y, rows):
            lane = lax.broadcasted_iota(jnp.int32, (1, LANES), 1)
            comp0 = (lane % 64) < 32
            ca = ca_ref[rows, :]
            sb = sb_ref[rows, :]
            for k in range(heads):
                cols = slice(k * LANES, (k + 1) * LANES)
                yk = y[:, cols]
                pk = pltpu.roll(yk, 64, 1)
                z = yk * yk + pk * pk
                s0 = jnp.sum(jnp.where(comp0, z, 0.0), axis=-1, keepdims=True)
                s1 = jnp.sum(jnp.where(comp0, 0.0, z), axis=-1, keepdims=True)
                r = jnp.where(comp0, lax.rsqrt(s0 * (0.5 / DF_DQ) + EPS), lax.rsqrt(s1 * (0.5 / DF_DQ) + EPS))
                o_ref[rows, cols] = ((yk * ca + pk * sb) * r).astype(BF16)
        return epilogue

    def plain(y, rows):
        o_ref[rows, :] = y.astype(BF16)

    def gate(y, rows):
        o_ref[rows, :] = jax.nn.sigmoid(y).astype(BF16)

    @pl.when(seg == 0)
    def _():
        run(w_ref[...], na_norm(gq_ref))

    @pl.when(seg == 1)
    def _():
        run(w_ref[...], na_norm(gk_ref))

    @pl.when(seg == 3)
    def _():
        run(rotary_layout(w_ref[...]), df_norm_rope(caq_ref, sbq_ref))

    @pl.when(seg == 4)
    def _():
        run(rotary_layout(w_ref[...]), df_norm_rope(cak_ref, sbk_ref))

    @pl.when((seg == 2) | (seg == 5))
    def _():
        run(w_ref[...], plain)

    @pl.when(seg >= 6)
    def _():
        run(w_ref[...], gate)


def _in_proj(h, w_in, g_na_q, g_na_k, caq, sbq, cak, sbk, seq):
    n = h.shape[0]
    pos_blocks = seq // IN_TM
    row = lambda i, j: (i, 0)
    const = lambda i, j: (0, 0)
    tab = lambda i, j: (i % pos_blocks, 0)
    return pl.pallas_call(
        _in_proj_kernel,
        out_shape=jax.ShapeDtypeStruct((n, IN_COLS), BF16),
        grid=(n // IN_TM, IN_COLS // IN_TN),
        in_specs=[
            pl.BlockSpec((IN_TM, D_MODEL), row),
            pl.BlockSpec((D_MODEL, IN_TN), lambda i, j: (0, j)),
            pl.BlockSpec((1, LANES), const),
            pl.BlockSpec((1, LANES), const),
            pl.BlockSpec((IN_TM, LANES), tab),
            pl.BlockSpec((IN_TM, LANES), tab),
            pl.BlockSpec((IN_TM, LANES), tab),
            pl.BlockSpec((IN_TM, LANES), tab),
        ],
        out_specs=pl.BlockSpec((IN_TM, IN_TN), lambda i, j: (i, j)),
        compiler_params=_params(("parallel", "arbitrary")),
        name="in_proj",
    )(h, w_in, g_na_q, g_na_k, caq, sbq, cak, sbk)


NA_QROWS = 8
NA_WROWS = 16
NA_TQ = NA_QROWS * GRID_W
NA_TK = NA_WROWS * GRID_W
NA_PAIR = LANES // GRID_W
NA_NPAIR = NA_QROWS // NA_PAIR
NA_PROWS = NA_KR + NA_PAIR - 1
NA_PK = NA_PROWS * GRID_W
NA_NTAB = 1 + 2 * NA_NPAIR
NA_NEG_SLAB = 2 * NA_KR - 1


def _na_geometry(rows):
    nt = rows // NA_QROWS
    assert nt >= 3 and rows >= NA_WROWS
    ws = np.zeros((3, NA_NPAIR), np.int64)
    idx = np.full((NA_NTAB, NA_PROWS, NA_PAIR), NA_NEG_SLAB, np.int64)
    for ci, t in enumerate((0, 1, nt - 1)):
        w0 = int(np.clip(t * NA_QROWS - NA_KR // 2, 0, rows - NA_WROWS))
        for pi in range(NA_NPAIR):
            r = [t * NA_QROWS + NA_PAIR * pi + hb for hb in range(NA_PAIR)]
            rs = [int(np.clip(ri - NA_KR // 2, 0, rows - NA_KR)) for ri in r]
            start = min(min(rs) - w0, NA_WROWS - NA_PROWS)
            assert 0 <= start and max(rs) + NA_KR <= w0 + start + NA_PROWS and min(rs) >= w0 + start
            ws[ci, pi] = start
            tab = 0 if ci == 1 else 1 + (0 if ci == 0 else NA_NPAIR) + pi
            for wp in range(NA_PROWS):
                key_row = w0 + start + wp
                for hb in range(NA_PAIR):
                    if rs[hb] <= key_row < rs[hb] + NA_KR:
                        idx[tab, wp, hb] = key_row - r[hb] + NA_KR - 1
    combos = sorted({tuple(int(v) for v in pair) for pair in idx.reshape(-1, NA_PAIR)})
    slab = [[combos.index(tuple(int(v) for v in idx[tab, wp])) for wp in range(NA_PROWS)] for tab in range(NA_NTAB)]
    return [[int(v) for v in row] for row in ws], combos, slab


def _na_kernel(q_ref, k_ref, v_ref, row_ref, o_ref, p_ref, slab_ref, *, rows):
    nt = rows // NA_QROWS
    ws, combos, slab = _na_geometry(rows)

    kc = lax.broadcasted_iota(jnp.int32, (GRID_W, LANES), 0)
    lane = lax.broadcasted_iota(jnp.int32, (GRID_W, LANES), 1)
    c = lane % GRID_W
    cs = jnp.clip(c - NA_KW // 2, 0, GRID_W - NA_KW)
    in_window = (kc >= cs) & (kc < cs + NA_KW)
    src = jnp.clip(kc - c + NA_KW - 1, 0, 2 * NA_KW - 2) + jnp.where(lane < GRID_W, 0, GRID_W)
    for ci, (left, right) in enumerate(combos):
        ok = in_window
        if left == NA_NEG_SLAB:
            ok = ok & (lane >= GRID_W)
        if right == NA_NEG_SLAB:
            ok = ok & (lane < GRID_W)
        row = jnp.broadcast_to(row_ref[0, ci:ci + 1, :], (GRID_W, LANES))
        slab_ref[ci] = jnp.where(ok, jnp.take_along_axis(row, src, axis=1), NEG)

    def window(t):
        return int(np.clip(t * NA_QROWS - NA_KR // 2, 0, rows - NA_WROWS)) * GRID_W

    def probs(t):
        cls = 0 if t == 0 else (2 if t == nt - 1 else 1)
        buf = t
        p_ref[buf] = jnp.zeros(p_ref.shape[1:], BF16)
        for pi in range(NA_NPAIR):
            lanes = slice(pi * LANES, (pi + 1) * LANES)
            tab = 0 if cls == 1 else 1 + (0 if cls == 0 else NA_NPAIR) + pi
            r0 = ws[cls][pi] * GRID_W
            kp = k_ref[window(t) + r0:window(t) + r0 + NA_PK, :]
            qp = q_ref[t * NA_TQ + pi * LANES:t * NA_TQ + (pi + 1) * LANES, :]
            sp = lax.dot_general(kp, qp, (((1,), (1,)), ((), ())), preferred_element_type=F32)
            bias = jnp.concatenate([slab_ref[ci] for ci in slab[tab]], axis=0)
            sp = sp * (NA_DH ** -0.5) + bias
            m = jnp.max(sp, axis=0, keepdims=True)
            p_ref[buf, r0:r0 + NA_PK, lanes] = jnp.exp(sp - m).astype(BF16)

    def outputs(t):
        vw = _with_ones_rows(v_ref[window(t):window(t) + NA_TK, :].T)
        ov = jnp.dot(vw, p_ref[t], preferred_element_type=F32)
        o_ref[t * NA_TQ:(t + 1) * NA_TQ, :] = (ov[0:NA_DH, :] / ov[NA_DH:NA_DH + 1, :]).T.astype(BF16)

    for t in range(nt + 1):
        if t < nt:
            probs(t)
        if t >= 1:
            outputs(t - 1)


def _na_attn(proj, bias_rows, batch, seq):
    rows = seq // GRID_W
    n_slabs = bias_rows.shape[1]
    return pl.pallas_call(
        functools.partial(_na_kernel, rows=rows),
        out_shape=jax.ShapeDtypeStruct((batch * seq, NA_W), BF16),
        grid=(batch, NA_HEADS),
        in_specs=[
            pl.BlockSpec((seq, LANES), lambda b, h: (b, CB_NA_Q + h)),
            pl.BlockSpec((seq, LANES), lambda b, h: (b, CB_NA_K + h)),
            pl.BlockSpec((seq, LANES), lambda b, h: (b, CB_NA_V + h)),
            pl.BlockSpec((1, n_slabs, LANES), lambda b, h: (h, 0, 0)),
        ],
        out_specs=pl.BlockSpec((seq, LANES), lambda b, h: (b, h)),
        scratch_shapes=[pltpu.VMEM((rows // NA_QROWS, NA_TK, NA_TQ), BF16),
                        pltpu.VMEM((n_slabs, GRID_W, LANES), F32)],
        compiler_params=_params(("parallel", "arbitrary")),
        name="na_attn",
    )(proj, proj, proj, bias_rows)


def _na_bias_rows(rpb, rows):
    _, combos, _ = _na_geometry(rows)
    padded = jnp.pad(rpb, ((0, 0), (0, 1), (0, GRID_W - (2 * NA_KW - 1))))
    left = jnp.take(padded, jnp.asarray([cb[0] for cb in combos], jnp.int32), axis=1)
    right = jnp.take(padded, jnp.asarray([cb[1] for cb in combos], jnp.int32), axis=1)
    return jnp.concatenate([left, right], axis=-1)


DF_TQ, DF_TK = 1024, 512
DF_TG = 256
DF_SBUF = 2
DF_VROWS = DF_DV + ONES_ROWS


def _df_kernel(q_ref, k_ref, v_ref, lam_ref, gsub_ref, o_ref, q12_ref, s_ref, acc_ref, vt_ref, *, seq):
    n_chunks = seq // DF_TK

    @pl.when(pl.program_id(2) == 0)
    def _():
        for c in range(n_chunks):
            cols = slice(c * DF_TK, (c + 1) * DF_TK)
            vt_ref[:, cols] = _with_ones_rows(v_ref[cols, :].T)

    lane = lax.broadcasted_iota(jnp.int32, (1, LANES), 1)
    comp0 = (lane % 64) < 32
    q = q_ref[...]
    zero = jnp.zeros_like(q)
    q12_ref[0:DF_TQ, :] = jnp.where(comp0, q, zero)
    q12_ref[DF_TQ:2 * DF_TQ, :] = jnp.where(comp0, zero, q)
    acc_ref[...] = jnp.zeros(acc_ref.shape, F32)

    groups = 2 * DF_TQ // DF_TG

    def scores(kc, g):
        cols = slice(g * DF_TG, (g + 1) * DF_TG)
        return lax.dot_general(kc, q12_ref[cols, :], (((1,), (1,)), ((), ())), preferred_element_type=F32)

    def keys(c):
        return k_ref[c * DF_TK:(c + 1) * DF_TK, :]

    def step(c, carry):
        cur, nxt = c % DF_SBUF, (c + 1) % DF_SBUF
        vc = vt_ref[:, c * DF_TK:(c + 1) * DF_TK]
        kn = keys(c + 1) if c + 1 < n_chunks else None
        out = []
        for g in range(groups):
            m_prev = carry[g]
            cols = slice(g * DF_TG, (g + 1) * DF_TG)
            if kn is not None:
                s_ref[nxt, :, cols] = scores(kn, g)
            s = s_ref[cur, :, cols]
            m_new = jnp.maximum(m_prev, jnp.max(s, axis=0, keepdims=True))
            alpha = jnp.exp2(m_prev - m_new)
            p = jnp.exp2(s - m_new).astype(BF16)
            pv = jnp.dot(vc, p, preferred_element_type=F32)
            acc_ref[:, cols] = alpha * acc_ref[:, cols] + pv
            out.append(m_new)
        return tuple(out)

    k0 = keys(0)
    for g in range(groups):
        s_ref[0, :, g * DF_TG:(g + 1) * DF_TG] = scores(k0, g)

    carry = tuple(jnp.full((1, DF_TG), -jnp.inf, F32) for _ in range(groups))
    for c in range(n_chunks):
        carry = step(c, carry)

    lp = lam_ref[...]
    lam = (jnp.exp(jnp.sum(lp[0:1] * lp[1:2], axis=-1, keepdims=True))
           - jnp.exp(jnp.sum(lp[2:3] * lp[3:4], axis=-1, keepdims=True)) + LAM_INIT)
    o12 = acc_ref[0:DF_DV, :] / acc_ref[DF_DV:DF_DV + 1, :]
    o_t = o12[:, 0:DF_TQ] - lam * o12[:, DF_TQ:2 * DF_TQ]
    o_ref[...] = (_rms_rows(o_t.T, gsub_ref[...]) * (1.0 - LAM_INIT)).astype(BF16)


def _df_attn(proj, lam_p, g_sub, batch, seq):
    nq = seq // DF_TQ
    const = lambda b, h, i: (0, 0)
    return pl.pallas_call(
        functools.partial(_df_kernel, seq=seq),
        out_shape=jax.ShapeDtypeStruct((batch * seq, DF_VW), BF16),
        grid=(batch, DF_HEADS, nq),
        in_specs=[
            pl.BlockSpec((DF_TQ, LANES), lambda b, h, i: (b * nq + i, CB_DF_Q + h)),
            pl.BlockSpec((seq, LANES), lambda b, h, i: (b, CB_DF_K + h)),
            pl.BlockSpec((seq, LANES), lambda b, h, i: (b, CB_DF_V + h)),
            pl.BlockSpec((4, DF_DQ), const),
            pl.BlockSpec((1, DF_DV), const),
        ],
        out_specs=pl.BlockSpec((DF_TQ, LANES), lambda b, h, i: (b * nq + i, h)),
        scratch_shapes=[pltpu.VMEM((2 * DF_TQ, LANES), BF16), pltpu.VMEM((DF_SBUF, DF_TK, 2 * DF_TQ), F32),
                        pltpu.VMEM((DF_VROWS, 2 * DF_TQ), F32), pltpu.VMEM((DF_VROWS, seq), BF16)],
        compiler_params=_params(("parallel", "parallel", "arbitrary")),
        name="df_attn",
    )(proj, proj, proj, lam_p, g_sub)


def _merge_kernel(oa_ref, ob_ref, wa_ref, wb_ref, sa_ref, sb_ref, o_ref, wab_ref, wbb_ref):
    @pl.when(pl.program_id(0) == 0)
    def _():
        _cast_resident(wa_ref, wab_ref)
        _cast_resident(wb_ref, wbb_ref)

    ya = jnp.dot(oa_ref[...], wab_ref[...], preferred_element_type=F32)
    yb = jnp.dot(ob_ref[...], wbb_ref[...], preferred_element_type=F32)
    o_ref[...] = (sa_ref[...].astype(F32) * ya + sb_ref[...].astype(F32) * yb).astype(BF16)


def _merge(o_a, o_b, w_na_out, w_df_out, proj):
    n = o_a.shape[0]
    ga, gb = COL_GATE_A // D_MODEL, COL_GATE_B // D_MODEL
    return pl.pallas_call(
        _merge_kernel,
        out_shape=jax.ShapeDtypeStruct((n, D_MODEL), BF16),
        grid=(n // ROW_TM,),
        in_specs=[
            pl.BlockSpec((ROW_TM, NA_W), lambda i: (i, 0)),
            pl.BlockSpec((ROW_TM, DF_VW), lambda i: (i, 0)),
            _resident((NA_W, D_MODEL)),
            _resident((DF_VW, D_MODEL)),
            pl.BlockSpec((ROW_TM, D_MODEL), lambda i: (i, ga)),
            pl.BlockSpec((ROW_TM, D_MODEL), lambda i: (i, gb)),
        ],
        out_specs=pl.BlockSpec((ROW_TM, D_MODEL), lambda i: (i, 0)),
        scratch_shapes=[pltpu.VMEM((NA_W, D_MODEL), BF16), pltpu.VMEM((DF_VW, D_MODEL), BF16)],
        compiler_params=_params(("arbitrary",)),
        name="merge",
    )(o_a, o_b, w_na_out, w_df_out, proj, proj)


DN_TM, DN_TN = 1024, 256


def _ffn_down_kernel(a_ref, w_ref, x_ref, o_ref):
    o_ref[...] = x_ref[...] + jnp.dot(a_ref[...], w_ref[...].astype(BF16), preferred_element_type=F32)


def _ffn_down(a, w, x):
    n, k = a.shape
    d = w.shape[1]
    return pl.pallas_call(
        _ffn_down_kernel,
        out_shape=jax.ShapeDtypeStruct((n, d), F32),
        grid=(n // DN_TM, d // DN_TN),
        in_specs=[
            pl.BlockSpec((DN_TM, k), lambda i, j: (i, 0)),
            pl.BlockSpec((k, DN_TN), lambda i, j: (0, j)),
            pl.BlockSpec((DN_TM, DN_TN), lambda i, j: (i, j)),
        ],
        out_specs=pl.BlockSpec((DN_TM, DN_TN), lambda i, j: (i, j)),
        compiler_params=_params(("parallel", "arbitrary")),
        name="ffn_down",
    )(a, w, x)


def _out_proj_norm_kernel(a_ref, w_ref, x_ref, g_ref, o_ref, h_ref, wb_ref):
    @pl.when(pl.program_id(0) == 0)
    def _():
        _cast_resident(w_ref, wb_ref)

    y = x_ref[...] + jnp.dot(a_ref[...], wb_ref[...], preferred_element_type=F32)
    o_ref[...] = y
    h_ref[...] = _rms_rows(y, g_ref[...]).astype(BF16)


def _out_proj_norm(a, w, x, g):
    n, k = a.shape
    d = w.shape[1]
    return pl.pallas_call(
        _out_proj_norm_kernel,
        out_shape=(jax.ShapeDtypeStruct((n, d), F32), jax.ShapeDtypeStruct((n, d), BF16)),
        grid=(n // ROW_TM,),
        in_specs=[
            pl.BlockSpec((ROW_TM, k), lambda i: (i, 0)),
            _resident((k, d)),
            pl.BlockSpec((ROW_TM, d), lambda i: (i, 0)),
            pl.BlockSpec((1, d), lambda i: (0, 0)),
        ],
        out_specs=(pl.BlockSpec((ROW_TM, d), lambda i: (i, 0)), pl.BlockSpec((ROW_TM, d), lambda i: (i, 0))),
        scratch_shapes=[pltpu.VMEM((k, d), BF16)],
        compiler_params=_params(("arbitrary",)),
        name="out_proj",
    )(a, w, x, g)


UP_TM, UP_TN = 2048, 512
UP_RB = 512


def _ffn_up_kernel(h_ref, wg_ref, wu_ref, o_ref):
    wg = wg_ref[...].astype(BF16)
    wu = wu_ref[...].astype(BF16)
    for r in range(UP_TM // UP_RB):
        rows = slice(r * UP_RB, (r + 1) * UP_RB)
        h = h_ref[rows, :]
        gate = jnp.dot(h, wg, preferred_element_type=F32)
        up = jnp.dot(h, wu, preferred_element_type=F32)
        o_ref[rows, :] = (gate * jax.nn.sigmoid(gate) * up).astype(BF16)


def _ffn_up(h, w_gate, w_up):
    n = h.shape[0]
    return pl.pallas_call(
        _ffn_up_kernel,
        out_shape=jax.ShapeDtypeStruct((n, FFN_HID), BF16),
        grid=(n // UP_TM, FFN_HID // UP_TN),
        in_specs=[
            pl.BlockSpec((UP_TM, D_MODEL), lambda i, j: (i, 0)),
            pl.BlockSpec((D_MODEL, UP_TN), lambda i, j: (0, j)),
            pl.BlockSpec((D_MODEL, UP_TN), lambda i, j: (0, j)),
        ],
        out_specs=pl.BlockSpec((UP_TM, UP_TN), lambda i, j: (i, j)),
        compiler_params=_params(("parallel", "arbitrary")),
        name="ffn_up",
    )(h, w_gate, w_up)


def _ple_kernel(x_ref, g_ref, wg_ref, p_ref, wp_ref, o_ref, wgb_ref, wpb_ref):
    @pl.when(pl.program_id(0) == 0)
    def _():
        _cast_resident(wg_ref, wgb_ref)
        _cast_resident(wp_ref, wpb_ref)

    x = x_ref[...]
    h = _rms_rows(x, g_ref[...]).astype(BF16)
    gate = jnp.dot(h, wgb_ref[...], preferred_element_type=F32)
    emb = jnp.dot(p_ref[...].astype(BF16), wpb_ref[...], preferred_element_type=F32)
    o_ref[...] = x + jax.nn.sigmoid(gate) * emb


def _ple(x, g, w_gate, p, w_proj):
    n, d = x.shape
    return pl.pallas_call(
        _ple_kernel,
        out_shape=jax.ShapeDtypeStruct((n, d), F32),
        grid=(n // ROW_TM,),
        in_specs=[
            pl.BlockSpec((ROW_TM, d), lambda i: (i, 0)),
            pl.BlockSpec((1, d), lambda i: (0, 0)),
            _resident((d, d)),
            pl.BlockSpec((ROW_TM, PLE_DIM), lambda i: (i, 0)),
            _resident((PLE_DIM, d)),
        ],
        out_specs=pl.BlockSpec((ROW_TM, d), lambda i: (i, 0)),
        scratch_shapes=[pltpu.VMEM((d, d), BF16), pltpu.VMEM((PLE_DIM, d), BF16)],
        compiler_params=_params(("arbitrary",)),
        name="ple",
    )(x, g, w_gate, p, w_proj)


def _rope_tables(seq, g, scale):
    half = DF_DQ // 2
    inv = 1.0 / (ROPE_THETA ** (jnp.arange(0, DF_DQ, 2, dtype=F32) / DF_DQ))
    ang = jnp.arange(seq, dtype=F32)[:, None] * inv[None, :]
    cos, sin = jnp.cos(ang), jnp.sin(ang)
    g1, g2 = g[:half], g[half:]
    ca = jnp.concatenate([cos * g1, cos * g1, cos * g2, cos * g2], axis=1) * scale
    sb = jnp.concatenate([-sin * g2, -sin * g2, sin * g1, sin * g1], axis=1) * scale
    return ca, sb


def kernel(x, p, g_mix, w_in, g_na_q, g_na_k, na_rpb, g_df_q, g_df_k, lam_q1, lam_k1, lam_q2, lam_k2,
           g_df_sub, w_na_out, w_df_out, w_o, g_ffn, w_gate, w_up, w_down, g_ple, w_ple_gate, w_ple_proj):
    batch, seq, d = x.shape
    n = batch * seq
    rows = seq // GRID_W
    depth = w_in.shape[0]
    xf = x.reshape(n, d)
    for i in range(depth):
        caq, sbq = _rope_tables(seq, g_df_q[i], DF_DQ ** -0.5 * math.log2(math.e))
        cak, sbk = _rope_tables(seq, g_df_k[i], 1.0)
        proj = _in_proj(_rms_norm(xf, g_mix[i][None]), w_in[i], g_na_q[i][None], g_na_k[i][None],
                        caq, sbq, cak, sbk, seq)

        o_a = _na_attn(proj, _na_bias_rows(na_rpb[i], rows), batch, seq)
        lam_p = jnp.stack([lam_q1[i], lam_k1[i], lam_q2[i], lam_k2[i]], axis=0)
        o_b = _df_attn(proj, lam_p, g_df_sub[i][None], batch, seq)

        merged = _merge(o_a, o_b, w_na_out[i], w_df_out[i], proj)
        xf, hf = _out_proj_norm(merged, w_o[i], xf, g_ffn[i][None])

        act = _ffn_up(hf, w_gate[i], w_up[i])
        xf = _ffn_down(act, w_down[i], xf)

        xf = _ple(xf, g_ple[i][None], w_ple_gate[i], p[i].reshape(n, PLE_DIM), w_ple_proj[i])
    return xf.reshape(batch, seq, d)
```

```python
import functools
import math

import numpy as np
import jax
import jax.numpy as jnp
from jax import lax
from jax.experimental import pallas as pl
from jax.experimental.pallas import tpu as pltpu

D_MODEL = 2048
GRID_W = 64
NA_HEADS = 8
NA_DH = 128
NA_KR = 8
NA_KW = 16
DF_HEADS = 8
DF_DQ = 64
DF_DV = 128
FFN_HID = 5632
PLE_DIM = 256
ROPE_THETA = 10000.0
EPS = 1e-6
LAM_INIT = 0.8 - 0.6 * math.exp(-0.3 * 0)

NA_W = NA_HEADS * NA_DH
DF_QW = DF_HEADS * 2 * DF_DQ
DF_VW = DF_HEADS * DF_DV
IN_COLS = 3 * NA_W + 2 * DF_QW + DF_VW + 2 * D_MODEL

LANES = 128
NEG = -1e30
VMEM_LIMIT = 56 * 1024 * 1024

F32 = jnp.float32
BF16 = jnp.bfloat16

CB_NA_Q, CB_NA_K, CB_NA_V = 0, 8, 16
CB_DF_Q, CB_DF_K, CB_DF_V = 24, 32, 40
COL_GATE_A, COL_GATE_B = 6144, 8192


def _params(sem):
    return pltpu.CompilerParams(dimension_semantics=sem, vmem_limit_bytes=VMEM_LIMIT)


def _rms_rows(x, g):
    ms = jnp.mean(x * x, axis=-1, keepdims=True)
    return x * lax.rsqrt(ms + EPS) * g


ONES_ROWS = 16


def _with_ones_rows(v_t):
    row = lax.broadcasted_iota(jnp.int32, (ONES_ROWS, v_t.shape[1]), 0)
    return jnp.concatenate([v_t, jnp.where(row == 0, 1.0, 0.0).astype(v_t.dtype)], axis=0)


NORM_ROWS = 256


def _norm_block(x_ref, g_ref, h_ref):
    def body(c, carry):
        r = pl.ds(pl.multiple_of(c * NORM_ROWS, NORM_ROWS), NORM_ROWS)
        h_ref[r, :] = _rms_rows(x_ref[r, :], g_ref[...]).astype(BF16)
        return carry

    lax.fori_loop(0, x_ref.shape[0] // NORM_ROWS, body, 0)


def _rms_norm_kernel(x_ref, g_ref, o_ref):
    _norm_block(x_ref, g_ref, o_ref)


def _rms_norm(x, g, tm=512):
    n, d = x.shape
    return pl.pallas_call(
        _rms_norm_kernel,
        out_shape=jax.ShapeDtypeStruct((n, d), BF16),
        grid=(n // tm,),
        in_specs=[pl.BlockSpec((tm, d), lambda i: (i, 0)), pl.BlockSpec((1, d), lambda i: (0, 0))],
        out_specs=pl.BlockSpec((tm, d), lambda i: (i, 0)),
        compiler_params=_params(("parallel",)),
        name="rms_norm",
    )(x, g)


CAST_ROWS = 256


def _cast_resident(w_ref, wb_ref):
    rows = min(CAST_ROWS, w_ref.shape[0])

    def body(c, carry):
        r = pl.ds(pl.multiple_of(c * rows, rows), rows)
        wb_ref[r, :] = w_ref[r, :].astype(BF16)
        return carry

    lax.fori_loop(0, w_ref.shape[0] // rows, body, 0)


def _resident(shape):
    return pl.BlockSpec(shape, lambda i: (0, 0), pipeline_mode=pl.Buffered(1))


ROW_TM = 256


IN_TM, IN_TN = 2048, 512
IN_RB = 256
SEG_TILES = 1024 // IN_TN


def _in_proj_kernel(h_ref, w_ref, gq_ref, gk_ref, caq_ref, sbq_ref, cak_ref, sbk_ref, o_ref):
    seg = pl.program_id(1) // SEG_TILES
    heads = IN_TN // LANES

    def rotary_layout(w):
        lane = lax.broadcasted_iota(jnp.int32, (w.shape[0], LANES), 1)
        quarter = DF_DQ // 2
        src = (lane // quarter % 2) * DF_DQ + (lane // DF_DQ) * quarter + lane % quarter
        return jnp.concatenate([jnp.take_along_axis(w[:, k * LANES:(k + 1) * LANES], src, axis=1)
                                for k in range(heads)], axis=1)

    def run(w, epilogue):
        wb = w.astype(BF16)
        for r in range(IN_TM // IN_RB):
            rows = slice(r * IN_RB, (r + 1) * IN_RB)
            epilogue(jnp.dot(h_ref[rows, :], wb, preferred_element_type=F32), rows)

    def na_norm(g_ref):
        def epilogue(y, rows):
            for k in range(heads):
                cols = slice(k * LANES, (k + 1) * LANES)
                o_ref[rows, cols] = _rms_rows(y[:, cols], g_ref[...]).astype(BF16)
        return epilogue

    def df_norm_rope(ca_ref, sb_ref):
        def epilogue(y, rows):
            lane = lax.broadcasted_iota(jnp.int32, (1, LANES), 1)
            comp0 = (lane % 64) < 32
            ca = ca_ref[rows, :]
            sb = sb_ref[rows, :]
            for k in range(heads):
                cols = slice(k * LANES, (k + 1) * LANES)
                yk = y[:, cols]
                pk = pltpu.roll(yk, 64, 1)
                z = yk * yk + pk * pk
                s0 = jnp.sum(jnp.where(comp0, z, 0.0), axis=-1, keepdims=True)
                s1 = jnp.sum(jnp.where(comp0, 0.0, z), axis=-1, keepdims=True)
                r = jnp.where(comp0, lax.rsqrt(s0 * (0.5 / DF_DQ) + EPS), lax.rsqrt(s1 * (0.5 / DF_DQ) + EPS))
                o_ref[rows, cols] = ((yk * ca + pk * sb) * r).astype(BF16)
        return epilogue

    def plain(y, rows):
        o_ref[rows, :] = y.astype(BF16)

    def gate(y, rows):
        o_ref[rows, :] = jax.nn.sigmoid(y).astype(BF16)

    @pl.when(seg == 0)
    def _():
        run(w_ref[...], na_norm(gq_ref))

    @pl.when(seg == 1)
    def _():
        run(w_ref[...], na_norm(gk_ref))

    @pl.when(seg == 3)
    def _():
        run(rotary_layout(w_ref[...]), df_norm_rope(caq_ref, sbq_ref))

    @pl.when(seg == 4)
    def _():
        run(rotary_layout(w_ref[...]), df_norm_rope(cak_ref, sbk_ref))

    @pl.when((seg == 2) | (seg == 5))
    def _():
        run(w_ref[...], plain)

    @pl.when(seg >= 6)
    def _():
        run(w_ref[...], gate)


def _in_proj(h, w_in, g_na_q, g_na_k, caq, sbq, cak, sbk, seq):
    n = h.shape[0]
    pos_blocks = seq // IN_TM
    row = lambda i, j: (i, 0)
    const = lambda i, j: (0, 0)
    tab = lambda i, j: (i % pos_blocks, 0)
    return pl.pallas_call(
        _in_proj_kernel,
        out_shape=jax.ShapeDtypeStruct((n, IN_COLS), BF16),
        grid=(n // IN_TM, IN_COLS // IN_TN),
        in_specs=[
            pl.BlockSpec((IN_TM, D_MODEL), row),
            pl.BlockSpec((D_MODEL, IN_TN), lambda i, j: (0, j)),
            pl.BlockSpec((1, LANES), const),
            pl.BlockSpec((1, LANES), const),
            pl.BlockSpec((IN_TM, LANES), tab),
            pl.BlockSpec((IN_TM, LANES), tab),
            pl.BlockSpec((IN_TM, LANES), tab),
            pl.BlockSpec((IN_TM, LANES), tab),
        ],
        out_specs=pl.BlockSpec((IN_TM, IN_TN), lambda i, j: (i, j)),
        compiler_params=_params(("parallel", "arbitrary")),
        name="in_proj",
    )(h, w_in, g_na_q, g_na_k, caq, sbq, cak, sbk)


NA_QROWS = 8
NA_WROWS = 16
NA_TQ = NA_QROWS * GRID_W
NA_TK = NA_WROWS * GRID_W
NA_PAIR = LANES // GRID_W
NA_NPAIR = NA_QROWS // NA_PAIR
NA_PROWS = NA_KR + NA_PAIR - 1
NA_PK = NA_PROWS * GRID_W
NA_NTAB = 1 + 2 * NA_NPAIR
NA_NEG_SLAB = 2 * NA_KR - 1


def _na_geometry(rows):
    nt = rows // NA_QROWS
    assert nt >= 3 and rows >= NA_WROWS
    ws = np.zeros((3, NA_NPAIR), np.int64)
    idx = np.full((NA_NTAB, NA_PROWS, NA_PAIR), NA_NEG_SLAB, np.int64)
    for ci, t in enumerate((0, 1, nt - 1)):
        w0 = int(np.clip(t * NA_QROWS - NA_KR // 2, 0, rows - NA_WROWS))
        for pi in range(NA_NPAIR):
            r = [t * NA_QROWS + NA_PAIR * pi + hb for hb in range(NA_PAIR)]
            rs = [int(np.clip(ri - NA_KR // 2, 0, rows - NA_KR)) for ri in r]
            start = min(min(rs) - w0, NA_WROWS - NA_PROWS)
            assert 0 <= start and max(rs) + NA_KR <= w0 + start + NA_PROWS and min(rs) >= w0 + start
            ws[ci, pi] = start
            tab = 0 if ci == 1 else 1 + (0 if ci == 0 else NA_NPAIR) + pi
            for wp in range(NA_PROWS):
                key_row = w0 + start + wp
                for hb in range(NA_PAIR):
                    if rs[hb] <= key_row < rs[hb] + NA_KR:
                        idx[tab, wp, hb] = key_row - r[hb] + NA_KR - 1
    combos = sorted({tuple(int(v) for v in pair) for pair in idx.reshape(-1, NA_PAIR)})
    slab = [[combos.index(tuple(int(v) for v in idx[tab, wp])) for wp in range(NA_PROWS)] for tab in range(NA_NTAB)]
    return [[int(v) for v in row] for row in ws], combos, slab


def _na_kernel(q_ref, k_ref, v_ref, row_ref, o_ref, p_ref, slab_ref, *, rows):
    nt = rows // NA_QROWS
    ws, combos, slab = _na_geometry(rows)

    kc = lax.broadcasted_iota(jnp.int32, (GRID_W, LANES), 0)
    lane = lax.broadcasted_iota(jnp.int32, (GRID_W, LANES), 1)
    c = lane % GRID_W
    cs = jnp.clip(c - NA_KW // 2, 0, GRID_W - NA_KW)
    in_window = (kc >= cs) & (kc < cs + NA_KW)
    src = jnp.clip(kc - c + NA_KW - 1, 0, 2 * NA_KW - 2) + jnp.where(lane < GRID_W, 0, GRID_W)
    for ci, (left, right) in enumerate(combos):
        ok = in_window
        if left == NA_NEG_SLAB:
            ok = ok & (lane >= GRID_W)
        if right == NA_NEG_SLAB:
            ok = ok & (lane < GRID_W)
        row = jnp.broadcast_to(row_ref[0, ci:ci + 1, :], (GRID_W, LANES))
        slab_ref[ci] = jnp.where(ok, jnp.take_along_axis(row, src, axis=1), NEG)

    def window(t):
        return int(np.clip(t * NA_QROWS - NA_KR // 2, 0, rows - NA_WROWS)) * GRID_W

    def probs(t):
        cls = 0 if t == 0 else (2 if t == nt - 1 else 1)
        buf = t
        p_ref[buf] = jnp.zeros(p_ref.shape[1:], BF16)
        for pi in range(NA_NPAIR):
            lanes = slice(pi * LANES, (pi + 1) * LANES)
            tab = 0 if cls == 1 else 1 + (0 if cls == 0 else NA_NPAIR) + pi
            r0 = ws[cls][pi] * GRID_W
            kp = k_ref[window(t) + r0:window(t) + r0 + NA_PK, :]
            qp = q_ref[t * NA_TQ + pi * LANES:t * NA_TQ + (pi + 1) * LANES, :]
            sp = lax.dot_general(kp, qp, (((1,), (1,)), ((), ())), preferred_element_type=F32)
            bias = jnp.concatenate([slab_ref[ci] for ci in slab[tab]], axis=0)
            sp = sp * (NA_DH ** -0.5) + bias
            m = jnp.max(sp, axis=0, keepdims=True)
            p_ref[buf, r0:r0 + NA_PK, lanes] = jnp.exp(sp - m).astype(BF16)

    def outputs(t):
        vw = _with_ones_rows(v_ref[window(t):window(t) + NA_TK, :].T)
        ov = jnp.dot(vw, p_ref[t], preferred_element_type=F32)
        o_ref[t * NA_TQ:(t + 1) * NA_TQ, :] = (ov[0:NA_DH, :] / ov[NA_DH:NA_DH + 1, :]).T.astype(BF16)

    for t in range(nt + 1):
        if t < nt:
            probs(t)
        if t >= 1:
            outputs(t - 1)


def _na_attn(proj, bias_rows, batch, seq):
    rows = seq // GRID_W
    n_slabs = bias_rows.shape[1]
    return pl.pallas_call(
        functools.partial(_na_kernel, rows=rows),
        out_shape=jax.ShapeDtypeStruct((batch * seq, NA_W), BF16),
        grid=(batch, NA_HEADS),
        in_specs=[
            pl.BlockSpec((seq, LANES), lambda b, h: (b, CB_NA_Q + h)),
            pl.BlockSpec((seq, LANES), lambda b, h: (b, CB_NA_K + h)),
            pl.BlockSpec((seq, LANES), lambda b, h: (b, CB_NA_V + h)),
            pl.BlockSpec((1, n_slabs, LANES), lambda b, h: (h, 0, 0)),
        ],
        out_specs=pl.BlockSpec((seq, LANES), lambda b, h: (b, h)),
        scratch_shapes=[pltpu.VMEM((rows // NA_QROWS, NA_TK, NA_TQ), BF16),
                        pltpu.VMEM((n_slabs, GRID_W, LANES), F32)],
        compiler_params=_params(("parallel", "arbitrary")),
        name="na_attn",
    )(proj, proj, proj, bias_rows)


def _na_bias_rows(rpb, rows):
    _, combos, _ = _na_geometry(rows)
    padded = jnp.pad(rpb, ((0, 0), (0, 1), (0, GRID_W - (2 * NA_KW - 1))))
    left = jnp.take(padded, jnp.asarray([cb[0] for cb in combos], jnp.int32), axis=1)
    right = jnp.take(padded, jnp.asarray([cb[1] for cb in combos], jnp.int32), axis=1)
    return jnp.concatenate([left, right], axis=-1)


DF_TQ, DF_TK = 1024, 512
DF_TG = 256
DF_SBUF = 2
DF_VROWS = DF_DV + ONES_ROWS


def _df_kernel(q_ref, k_ref, v_ref, lam_ref, gsub_ref, o_ref, q12_ref, q12n_ref, s_ref, acc_ref, vt_ref, *, seq):
    n_chunks = seq // DF_TK
    n_blocks = seq // DF_TQ
    groups = 2 * DF_TQ // DF_TG

    for c in range(n_chunks):
        cols = slice(c * DF_TK, (c + 1) * DF_TK)
        vt_ref[:, cols] = _with_ones_rows(v_ref[cols, :].T)

    lane = lax.broadcasted_iota(jnp.int32, (1, LANES), 1)
    comp0 = (lane % 64) < 32

    def split_maps(block, dst_ref):
        q = q_ref[pl.ds(pl.multiple_of(block * DF_TQ, DF_TQ), DF_TQ), :]
        zero = jnp.zeros_like(q)
        dst_ref[0:DF_TQ, :] = jnp.where(comp0, q, zero)
        dst_ref[DF_TQ:2 * DF_TQ, :] = jnp.where(comp0, zero, q)

    def scores(kc, q_src_ref, g):
        cols = slice(g * DF_TG, (g + 1) * DF_TG)
        return lax.dot_general(kc, q_src_ref[cols, :], (((1,), (1,)), ((), ())), preferred_element_type=F32)

    def keys(c):
        return k_ref[c * DF_TK:(c + 1) * DF_TK, :]

    def step(c, carry):
        cur, nxt = c % DF_SBUF, (c + 1) % DF_SBUF
        vc = vt_ref[:, c * DF_TK:(c + 1) * DF_TK]
        kn, q_next = (keys(c + 1), q12_ref) if c + 1 < n_chunks else (keys(0), q12n_ref)
        out = []
        for g in range(groups):
            m_prev = carry[g]
            cols = slice(g * DF_TG, (g + 1) * DF_TG)
            s_ref[nxt, :, cols] = scores(kn, q_next, g)
            s = s_ref[cur, :, cols]
            m_new = jnp.maximum(m_prev, jnp.max(s, axis=0, keepdims=True))
            alpha = jnp.exp2(m_prev - m_new)
            p = jnp.exp2(s - m_new).astype(BF16)
            pv = jnp.dot(vc, p, preferred_element_type=F32)
            acc_ref[:, cols] = alpha * acc_ref[:, cols] + pv
            out.append(m_new)
        return tuple(out)

    lp = lam_ref[...]
    lam = (jnp.exp(jnp.sum(lp[0:1] * lp[1:2], axis=-1, keepdims=True))
           - jnp.exp(jnp.sum(lp[2:3] * lp[3:4], axis=-1, keepdims=True)) + LAM_INIT)

    def finish(block):
        o12 = acc_ref[0:DF_DV, :] / acc_ref[DF_DV:DF_DV + 1, :]
        o_t = o12[:, 0:DF_TQ] - lam * o12[:, DF_TQ:2 * DF_TQ]
        rows = pl.ds(pl.multiple_of(block * DF_TQ, DF_TQ), DF_TQ)
        o_ref[rows, :] = (_rms_rows(o_t.T, gsub_ref[...]) * (1.0 - LAM_INIT)).astype(BF16)

    assert n_chunks % DF_SBUF == 0
    split_maps(0, q12n_ref)
    k0 = keys(0)
    for g in range(groups):
        s_ref[0, :, g * DF_TG:(g + 1) * DF_TG] = scores(k0, q12n_ref, g)
    acc_ref[...] = jnp.zeros(acc_ref.shape, F32)
    acc_ref[DF_DV:DF_DV + 1, :] = jnp.ones((1, 2 * DF_TQ), F32)

    def body(block, carry):
        finish(jnp.maximum(block - 1, 0))
        acc_ref[...] = jnp.zeros(acc_ref.shape, F32)
        split_maps(block, q12_ref)
        split_maps(jnp.minimum(block + 1, n_blocks - 1), q12n_ref)
        m = tuple(jnp.full((1, DF_TG), -jnp.inf, F32) for _ in range(groups))
        for c in range(n_chunks):
            m = step(c, m)
        return carry

    lax.fori_loop(0, n_blocks, body, 0)
    finish(n_blocks - 1)


def _df_attn(proj, lam_p, g_sub, batch, seq):
    const = lambda b, h: (0, 0)
    return pl.pallas_call(
        functools.partial(_df_kernel, seq=seq),
        out_shape=jax.ShapeDtypeStruct((batch * seq, DF_VW), BF16),
        grid=(batch, DF_HEADS),
        in_specs=[
            pl.BlockSpec((seq, LANES), lambda b, h: (b, CB_DF_Q + h)),
            pl.BlockSpec((seq, LANES), lambda b, h: (b, CB_DF_K + h)),
            pl.BlockSpec((seq, LANES), lambda b, h: (b, CB_DF_V + h)),
            pl.BlockSpec((4, DF_DQ), const),
            pl.BlockSpec((1, DF_DV), const),
        ],
        out_specs=pl.BlockSpec((seq, LANES), lambda b, h: (b, h)),
        scratch_shapes=[pltpu.VMEM((2 * DF_TQ, LANES), BF16), pltpu.VMEM((2 * DF_TQ, LANES), BF16),
                        pltpu.VMEM((DF_SBUF, DF_TK, 2 * DF_TQ), F32),
                        pltpu.VMEM((DF_VROWS, 2 * DF_TQ), F32), pltpu.VMEM((DF_VROWS, seq), BF16)],
        compiler_params=_params(("parallel", "parallel")),
        name="df_attn",
    )(proj, proj, proj, lam_p, g_sub)


def _merge_kernel(oa_ref, ob_ref, wa_ref, wb_ref, sa_ref, sb_ref, o_ref, wab_ref, wbb_ref):
    @pl.when(pl.program_id(0) == 0)
    def _():
        _cast_resident(wa_ref, wab_ref)
        _cast_resident(wb_ref, wbb_ref)

    ya = jnp.dot(oa_ref[...], wab_ref[...], preferred_element_type=F32)
    yb = jnp.dot(ob_ref[...], wbb_ref[...], preferred_element_type=F32)
    o_ref[...] = (sa_ref[...].astype(F32) * ya + sb_ref[...].astype(F32) * yb).astype(BF16)


def _merge(o_a, o_b, w_na_out, w_df_out, proj):
    n = o_a.shape[0]
    ga, gb = COL_GATE_A // D_MODEL, COL_GATE_B // D_MODEL
    return pl.pallas_call(
        _merge_kernel,
        out_shape=jax.ShapeDtypeStruct((n, D_MODEL), BF16),
        grid=(n // ROW_TM,),
        in_specs=[
            pl.BlockSpec((ROW_TM, NA_W), lambda i: (i, 0)),
            pl.BlockSpec((ROW_TM, DF_VW), lambda i: (i, 0)),
            _resident((NA_W, D_MODEL)),
            _resident((DF_VW, D_MODEL)),
            pl.BlockSpec((ROW_TM, D_MODEL), lambda i: (i, ga)),
            pl.BlockSpec((ROW_TM, D_MODEL), lambda i: (i, gb)),
        ],
        out_specs=pl.BlockSpec((ROW_TM, D_MODEL), lambda i: (i, 0)),
        scratch_shapes=[pltpu.VMEM((NA_W, D_MODEL), BF16), pltpu.VMEM((DF_VW, D_MODEL), BF16)],
        compiler_params=_params(("arbitrary",)),
        name="merge",
    )(o_a, o_b, w_na_out, w_df_out, proj, proj)


DN_TM, DN_TN = 1024, 256


def _ffn_down_kernel(a_ref, w_ref, x_ref, o_ref):
    o_ref[...] = x_ref[...] + jnp.dot(a_ref[...], w_ref[...].astype(BF16), preferred_element_type=F32)


def _ffn_down(a, w, x):
    n, k = a.shape
    d = w.shape[1]
    return pl.pallas_call(
        _ffn_down_kernel,
        out_shape=jax.ShapeDtypeStruct((n, d), F32),
        grid=(n // DN_TM, d // DN_TN),
        in_specs=[
            pl.BlockSpec((DN_TM, k), lambda i, j: (i, 0)),
            pl.BlockSpec((k, DN_TN), lambda i, j: (0, j)),
            pl.BlockSpec((DN_TM, DN_TN), lambda i, j: (i, j)),
        ],
        out_specs=pl.BlockSpec((DN_TM, DN_TN), lambda i, j: (i, j)),
        compiler_params=_params(("parallel", "arbitrary")),
        name="ffn_down",
    )(a, w, x)


def _out_proj_norm_kernel(a_ref, w_ref, x_ref, g_ref, o_ref, h_ref, wb_ref):
    @pl.when(pl.program_id(0) == 0)
    def _():
        _cast_resident(w_ref, wb_ref)

    y = x_ref[...] + jnp.dot(a_ref[...], wb_ref[...], preferred_element_type=F32)
    o_ref[...] = y
    h_ref[...] = _rms_rows(y, g_ref[...]).astype(BF16)


def _out_proj_norm(a, w, x, g):
    n, k = a.shape
    d = w.shape[1]
    return pl.pallas_call(
        _out_proj_norm_kernel,
        out_shape=(jax.ShapeDtypeStruct((n, d), F32), jax.ShapeDtypeStruct((n, d), BF16)),
        grid=(n // ROW_TM,),
        in_specs=[
            pl.BlockSpec((ROW_TM, k), lambda i: (i, 0)),
            _resident((k, d)),
            pl.BlockSpec((ROW_TM, d), lambda i: (i, 0)),
            pl.BlockSpec((1, d), lambda i: (0, 0)),
        ],
        out_specs=(pl.BlockSpec((ROW_TM, d), lambda i: (i, 0)), pl.BlockSpec((ROW_TM, d), lambda i: (i, 0))),
        scratch_shapes=[pltpu.VMEM((k, d), BF16)],
        compiler_params=_params(("arbitrary",)),
        name="out_proj",
    )(a, w, x, g)


UP_TM, UP_TN = 2048, 512
UP_RB = 512


def _ffn_up_kernel(h_ref, wg_ref, wu_ref, o_ref):
    wg = wg_ref[...].astype(BF16)
    wu = wu_ref[...].astype(BF16)
    for r in range(UP_TM // UP_RB):
        rows = slice(r * UP_RB, (r + 1) * UP_RB)
        h = h_ref[rows, :]
        gate = jnp.dot(h, wg, preferred_element_type=F32)
        up = jnp.dot(h, wu, preferred_element_type=F32)
        o_ref[rows, :] = (gate * jax.nn.sigmoid(gate) * up).astype(BF16)


def _ffn_up(h, w_gate, w_up):
    n = h.shape[0]
    return pl.pallas_call(
        _ffn_up_kernel,
        out_shape=jax.ShapeDtypeStruct((n, FFN_HID), BF16),
        grid=(n // UP_TM, FFN_HID // UP_TN),
        in_specs=[
            pl.BlockSpec((UP_TM, D_MODEL), lambda i, j: (i, 0)),
            pl.BlockSpec((D_MODEL, UP_TN), lambda i, j: (0, j)),
            pl.BlockSpec((D_MODEL, UP_TN), lambda i, j: (0, j)),
        ],
        out_specs=pl.BlockSpec((UP_TM, UP_TN), lambda i, j: (i, j)),
        compiler_params=_params(("parallel", "arbitrary")),
        name="ffn_up",
    )(h, w_gate, w_up)


def _ple_kernel(x_ref, g_ref, wg_ref, p_ref, wp_ref, o_ref, wgb_ref, wpb_ref):
    @pl.when(pl.program_id(0) == 0)
    def _():
        _cast_resident(wg_ref, wgb_ref)
        _cast_resident(wp_ref, wpb_ref)

    x = x_ref[...]
    h = _rms_rows(x, g_ref[...]).astype(BF16)
    gate = jnp.dot(h, wgb_ref[...], preferred_element_type=F32)
    emb = jnp.dot(p_ref[...].astype(BF16), wpb_ref[...], preferred_element_type=F32)
    o_ref[...] = x + jax.nn.sigmoid(gate) * emb


def _ple(x, g, w_gate, p, w_proj):
    n, d = x.shape
    return pl.pallas_call(
        _ple_kernel,
        out_shape=jax.ShapeDtypeStruct((n, d), F32),
        grid=(n // ROW_TM,),
        in_specs=[
            pl.BlockSpec((ROW_TM, d), lambda i: (i, 0)),
            pl.BlockSpec((1, d), lambda i: (0, 0)),
            _resident((d, d)),
            pl.BlockSpec((ROW_TM, PLE_DIM), lambda i: (i, 0)),
            _resident((PLE_DIM, d)),
        ],
        out_specs=pl.BlockSpec((ROW_TM, d), lambda i: (i, 0)),
        scratch_shapes=[pltpu.VMEM((d, d), BF16), pltpu.VMEM((PLE_DIM, d), BF16)],
        compiler_params=_params(("arbitrary",)),
        name="ple",
    )(x, g, w_gate, p, w_proj)


def _rope_tables(seq, g, scale):
    half = DF_DQ // 2
    inv = 1.0 / (ROPE_THETA ** (jnp.arange(0, DF_DQ, 2, dtype=F32) / DF_DQ))
    ang = jnp.arange(seq, dtype=F32)[:, None] * inv[None, :]
    cos, sin = jnp.cos(ang), jnp.sin(ang)
    g1, g2 = g[:half], g[half:]
    ca = jnp.concatenate([cos * g1, cos * g1, cos * g2, cos * g2], axis=1) * scale
    sb = jnp.concatenate([-sin * g2, -sin * g2, sin * g1, sin * g1], axis=1) * scale
    return ca, sb


def kernel(x, p, g_mix, w_in, g_na_q, g_na_k, na_rpb, g_df_q, g_df_k, lam_q1, lam_k1, lam_q2, lam_k2,
           g_df_sub, w_na_out, w_df_out, w_o, g_ffn, w_gate, w_up, w_down, g_ple, w_ple_gate, w_ple_proj):
    batch, seq, d = x.shape
    n = batch * seq
    rows = seq // GRID_W
    depth = w_in.shape[0]
    xf = x.reshape(n, d)
    for i in range(depth):
        caq, sbq = _rope_tables(seq, g_df_q[i], DF_DQ ** -0.5 * math.log2(math.e))
        cak, sbk = _rope_tables(seq, g_df_k[i], 1.0)
        proj = _in_proj(_rms_norm(xf, g_mix[i][None]), w_in[i], g_na_q[i][None], g_na_k[i][None],
                        caq, sbq, cak, sbk, seq)

        o_a = _na_attn(proj, _na_bias_rows(na_rpb[i], rows), batch, seq)
        lam_p = jnp.stack([lam_q1[i], lam_k1[i], lam_q2[i], lam_k2[i]], axis=0)
        o_b = _df_attn(proj, lam_p, g_df_sub[i][None], batch, seq)

        merged = _merge(o_a, o_b, w_na_out[i], w_df_out[i], proj)
        xf, hf = _out_proj_norm(merged, w_o[i], xf, g_ffn[i][None])

        act = _ffn_up(hf, w_gate[i], w_up[i])
        xf = _ffn_down(act, w_down[i], xf)

        xf = _ple(xf, g_ple[i][None], w_ple_gate[i], p[i].reshape(n, PLE_DIM), w_ple_proj[i])
    return xf.reshape(batch, seq, d)
```

```python
import functools
import math

import numpy as np
import jax
import jax.numpy as jnp
from jax import lax
from jax.experimental import pallas as pl
from jax.experimental.pallas import tpu as pltpu

D_MODEL = 2048
GRID_W = 64
NA_HEADS = 8
NA_DH = 128
NA_KR = 8
NA_KW = 16
DF_HEADS = 8
DF_DQ = 64
DF_DV = 128
FFN_HID = 5632
PLE_DIM = 256
ROPE_THETA = 10000.0
EPS = 1e-6
LAM_INIT = 0.8 - 0.6 * math.exp(-0.3 * 0)

NA_W = NA_HEADS * NA_DH
DF_QW = DF_HEADS * 2 * DF_DQ
DF_VW = DF_HEADS * DF_DV
IN_COLS = 3 * NA_W + 2 * DF_QW + DF_VW + 2 * D_MODEL

LANES = 128
NEG = -1e30
VMEM_LIMIT = 56 * 1024 * 1024

F32 = jnp.float32
BF16 = jnp.bfloat16

CB_NA_Q, CB_NA_K, CB_NA_V = 0, 8, 16
CB_DF_Q, CB_DF_K, CB_DF_V = 24, 32, 40
COL_GATE_A, COL_GATE_B = 6144, 8192


def _params(sem):
    return pltpu.CompilerParams(dimension_semantics=sem, vmem_limit_bytes=VMEM_LIMIT)


def _rms_rows(x, g):
    ms = jnp.mean(x * x, axis=-1, keepdims=True)
    return x * lax.rsqrt(ms + EPS) * g


ONES_ROWS = 16


def _with_ones_rows(v_t):
    row = lax.broadcasted_iota(jnp.int32, (ONES_ROWS, v_t.shape[1]), 0)
    return jnp.concatenate([v_t, jnp.where(row == 0, 1.0, 0.0).astype(v_t.dtype)], axis=0)


NORM_ROWS = 256


def _norm_block(x_ref, g_ref, h_ref):
    def body(c, carry):
        r = pl.ds(pl.multiple_of(c * NORM_ROWS, NORM_ROWS), NORM_ROWS)
        h_ref[r, :] = _rms_rows(x_ref[r, :], g_ref[...]).astype(BF16)
        return carry

    lax.fori_loop(0, x_ref.shape[0] // NORM_ROWS, body, 0)


def _rms_norm_kernel(x_ref, g_ref, o_ref):
    _norm_block(x_ref, g_ref, o_ref)


def _rms_norm(x, g, tm=512):
    n, d = x.shape
    return pl.pallas_call(
        _rms_norm_kernel,
        out_shape=jax.ShapeDtypeStruct((n, d), BF16),
        grid=(n // tm,),
        in_specs=[pl.BlockSpec((tm, d), lambda i: (i, 0)), pl.BlockSpec((1, d), lambda i: (0, 0))],
        out_specs=pl.BlockSpec((tm, d), lambda i: (i, 0)),
        compiler_params=_params(("parallel",)),
        name="rms_norm",
    )(x, g)


CAST_ROWS = 256


def _cast_resident(w_ref, wb_ref):
    rows = min(CAST_ROWS, w_ref.shape[0])

    def body(c, carry):
        r = pl.ds(pl.multiple_of(c * rows, rows), rows)
        wb_ref[r, :] = w_ref[r, :].astype(BF16)
        return carry

    lax.fori_loop(0, w_ref.shape[0] // rows, body, 0)


def _resident(shape):
    return pl.BlockSpec(shape, lambda i: (0, 0), pipeline_mode=pl.Buffered(1))


ROW_TM = 256


IN_TM, IN_TN = 2048, 512
IN_RB = 256
SEG_TILES = 1024 // IN_TN


def _in_proj_kernel(h_ref, w_ref, gq_ref, gk_ref, caq_ref, sbq_ref, cak_ref, sbk_ref, o_ref):
    seg = pl.program_id(1) // SEG_TILES
    heads = IN_TN // LANES

    def rotary_layout(w):
        lane = lax.broadcasted_iota(jnp.int32, (w.shape[0], LANES), 1)
        quarter = DF_DQ // 2
        src = (lane // quarter % 2) * DF_DQ + (lane // DF_DQ) * quarter + lane % quarter
        return jnp.concatenate([jnp.take_along_axis(w[:, k * LANES:(k + 1) * LANES], src, axis=1)
                                for k in range(heads)], axis=1)

    def run(w, epilogue):
        wb = w.astype(BF16)
        for r in range(IN_TM // IN_RB):
            rows = slice(r * IN_RB, (r + 1) * IN_RB)
            epilogue(jnp.dot(h_ref[rows, :], wb, preferred_element_type=F32), rows)

    def na_norm(g_ref):
        def epilogue(y, rows):
            for k in range(heads):
                cols = slice(k * LANES, (k + 1) * LANES)
                o_ref[k, rows, :] = _rms_rows(y[:, cols], g_ref[...]).astype(BF16)
        return epilogue

    def df_norm_rope(ca_ref, sb_ref):
        def epilogue(y, rows):
            lane = lax.broadcasted_iota(jnp.int32, (1, LANES), 1)
            comp0 = (lane % 64) < 32
            ca = ca_ref[rows, :]
            sb = sb_ref[rows, :]
            for k in range(heads):
                cols = slice(k * LANES, (k + 1) * LANES)
                yk = y[:, cols]
                pk = pltpu.roll(yk, 64, 1)
                z = yk * yk + pk * pk
                s0 = jnp.sum(jnp.where(comp0, z, 0.0), axis=-1, keepdims=True)
                s1 = jnp.sum(jnp.where(comp0, 0.0, z), axis=-1, keepdims=True)
                r = jnp.where(comp0, lax.rsqrt(s0 * (0.5 / DF_DQ) + EPS), lax.rsqrt(s1 * (0.5 / DF_DQ) + EPS))
                o_ref[k, rows, :] = ((yk * ca + pk * sb) * r).astype(BF16)
        return epilogue

    def plain(y, rows):
        for k in range(heads):
            o_ref[k, rows, :] = y[:, k * LANES:(k + 1) * LANES].astype(BF16)

    def gate(y, rows):
        for k in range(heads):
            o_ref[k, rows, :] = jax.nn.sigmoid(y[:, k * LANES:(k + 1) * LANES]).astype(BF16)

    @pl.when(seg == 0)
    def _():
        run(w_ref[...], na_norm(gq_ref))

    @pl.when(seg == 1)
    def _():
        run(w_ref[...], na_norm(gk_ref))

    @pl.when(seg == 3)
    def _():
        run(rotary_layout(w_ref[...]), df_norm_rope(caq_ref, sbq_ref))

    @pl.when(seg == 4)
    def _():
        run(rotary_layout(w_ref[...]), df_norm_rope(cak_ref, sbk_ref))

    @pl.when((seg == 2) | (seg == 5))
    def _():
        run(w_ref[...], plain)

    @pl.when(seg >= 6)
    def _():
        run(w_ref[...], gate)


def _in_proj(h, w_in, g_na_q, g_na_k, caq, sbq, cak, sbk, seq):
    n = h.shape[0]
    pos_blocks = seq // IN_TM
    row = lambda i, j: (i, 0)
    const = lambda i, j: (0, 0)
    tab = lambda i, j: (i % pos_blocks, 0)
    return pl.pallas_call(
        _in_proj_kernel,
        out_shape=jax.ShapeDtypeStruct((IN_COLS // LANES, n, LANES), BF16),
        grid=(n // IN_TM, IN_COLS // IN_TN),
        in_specs=[
            pl.BlockSpec((IN_TM, D_MODEL), row),
            pl.BlockSpec((D_MODEL, IN_TN), lambda i, j: (0, j)),
            pl.BlockSpec((1, LANES), const),
            pl.BlockSpec((1, LANES), const),
            pl.BlockSpec((IN_TM, LANES), tab),
            pl.BlockSpec((IN_TM, LANES), tab),
            pl.BlockSpec((IN_TM, LANES), tab),
            pl.BlockSpec((IN_TM, LANES), tab),
        ],
        out_specs=pl.BlockSpec((IN_TN // LANES, IN_TM, LANES), lambda i, j: (j, i, 0)),
        compiler_params=_params(("parallel", "arbitrary")),
        name="in_proj",
    )(h, w_in, g_na_q, g_na_k, caq, sbq, cak, sbk)


NA_QROWS = 8
NA_WROWS = 16
NA_TQ = NA_QROWS * GRID_W
NA_TK = NA_WROWS * GRID_W
NA_PAIR = LANES // GRID_W
NA_NPAIR = NA_QROWS // NA_PAIR
NA_PROWS = NA_KR + NA_PAIR - 1
NA_PK = NA_PROWS * GRID_W
NA_NTAB = 1 + 2 * NA_NPAIR
NA_NEG_SLAB = 2 * NA_KR - 1


def _na_geometry(rows):
    nt = rows // NA_QROWS
    assert nt >= 3 and rows >= NA_WROWS
    ws = np.zeros((3, NA_NPAIR), np.int64)
    idx = np.full((NA_NTAB, NA_PROWS, NA_PAIR), NA_NEG_SLAB, np.int64)
    for ci, t in enumerate((0, 1, nt - 1)):
        w0 = int(np.clip(t * NA_QROWS - NA_KR // 2, 0, rows - NA_WROWS))
        for pi in range(NA_NPAIR):
            r = [t * NA_QROWS + NA_PAIR * pi + hb for hb in range(NA_PAIR)]
            rs = [int(np.clip(ri - NA_KR // 2, 0, rows - NA_KR)) for ri in r]
            start = min(min(rs) - w0, NA_WROWS - NA_PROWS)
            assert 0 <= start and max(rs) + NA_KR <= w0 + start + NA_PROWS and min(rs) >= w0 + start
            ws[ci, pi] = start
            tab = 0 if ci == 1 else 1 + (0 if ci == 0 else NA_NPAIR) + pi
            for wp in range(NA_PROWS):
                key_row = w0 + start + wp
                for hb in range(NA_PAIR):
                    if rs[hb] <= key_row < rs[hb] + NA_KR:
                        idx[tab, wp, hb] = key_row - r[hb] + NA_KR - 1
    combos = sorted({tuple(int(v) for v in pair) for pair in idx.reshape(-1, NA_PAIR)})
    slab = [[combos.index(tuple(int(v) for v in idx[tab, wp])) for wp in range(NA_PROWS)] for tab in range(NA_NTAB)]
    return [[int(v) for v in row] for row in ws], combos, slab


def _na_kernel(q_ref, k_ref, v_ref, row_ref, o_ref, p_ref, slab_ref, *, rows):
    nt = rows // NA_QROWS
    ws, combos, slab = _na_geometry(rows)

    kc = lax.broadcasted_iota(jnp.int32, (GRID_W, LANES), 0)
    lane = lax.broadcasted_iota(jnp.int32, (GRID_W, LANES), 1)
    c = lane % GRID_W
    cs = jnp.clip(c - NA_KW // 2, 0, GRID_W - NA_KW)
    in_window = (kc >= cs) & (kc < cs + NA_KW)
    src = jnp.clip(kc - c + NA_KW - 1, 0, 2 * NA_KW - 2) + jnp.where(lane < GRID_W, 0, GRID_W)
    for ci, (left, right) in enumerate(combos):
        ok = in_window
        if left == NA_NEG_SLAB:
            ok = ok & (lane >= GRID_W)
        if right == NA_NEG_SLAB:
            ok = ok & (lane < GRID_W)
        row = jnp.broadcast_to(row_ref[0, ci:ci + 1, :], (GRID_W, LANES))
        slab_ref[ci] = jnp.where(ok, jnp.take_along_axis(row, src, axis=1), NEG)

    def window(t):
        return int(np.clip(t * NA_QROWS - NA_KR // 2, 0, rows - NA_WROWS)) * GRID_W

    def probs(t):
        cls = 0 if t == 0 else (2 if t == nt - 1 else 1)
        buf = t
        p_ref[buf] = jnp.zeros(p_ref.shape[1:], BF16)
        for pi in range(NA_NPAIR):
            lanes = slice(pi * LANES, (pi + 1) * LANES)
            tab = 0 if cls == 1 else 1 + (0 if cls == 0 else NA_NPAIR) + pi
            r0 = ws[cls][pi] * GRID_W
            kp = k_ref[window(t) + r0:window(t) + r0 + NA_PK, :]
            qp = q_ref[t * NA_TQ + pi * LANES:t * NA_TQ + (pi + 1) * LANES, :]
            sp = lax.dot_general(kp, qp, (((1,), (1,)), ((), ())), preferred_element_type=F32)
            bias = jnp.concatenate([slab_ref[ci] for ci in slab[tab]], axis=0)
            sp = sp * (NA_DH ** -0.5) + bias
            m = jnp.max(sp, axis=0, keepdims=True)
            p_ref[buf, r0:r0 + NA_PK, lanes] = jnp.exp(sp - m).astype(BF16)

    def outputs(t):
        vw = _with_ones_rows(v_ref[window(t):window(t) + NA_TK, :].T)
        ov = jnp.dot(vw, p_ref[t], preferred_element_type=F32)
        o_ref[t * NA_TQ:(t + 1) * NA_TQ, :] = (ov[0:NA_DH, :] / ov[NA_DH:NA_DH + 1, :]).T.astype(BF16)

    for t in range(nt + 1):
        if t < nt:
            probs(t)
        if t >= 1:
            outputs(t - 1)


def _na_attn(proj, bias_rows, batch, seq):
    rows = seq // GRID_W
    n_slabs = bias_rows.shape[1]
    return pl.pallas_call(
        functools.partial(_na_kernel, rows=rows),
        out_shape=jax.ShapeDtypeStruct((NA_HEADS, batch * seq, LANES), BF16),
        grid=(batch, NA_HEADS),
        in_specs=[
            pl.BlockSpec((None, seq, LANES), lambda b, h: (CB_NA_Q + h, b, 0)),
            pl.BlockSpec((None, seq, LANES), lambda b, h: (CB_NA_K + h, b, 0)),
            pl.BlockSpec((None, seq, LANES), lambda b, h: (CB_NA_V + h, b, 0)),
            pl.BlockSpec((1, n_slabs, LANES), lambda b, h: (h, 0, 0)),
        ],
        out_specs=pl.BlockSpec((None, seq, LANES), lambda b, h: (h, b, 0)),
        scratch_shapes=[pltpu.VMEM((rows // NA_QROWS, NA_TK, NA_TQ), BF16),
                        pltpu.VMEM((n_slabs, GRID_W, LANES), F32)],
        compiler_params=_params(("parallel", "arbitrary")),
        name="na_attn",
    )(proj, proj, proj, bias_rows)


def _na_bias_rows(rpb, rows):
    _, combos, _ = _na_geometry(rows)
    padded = jnp.pad(rpb, ((0, 0), (0, 1), (0, GRID_W - (2 * NA_KW - 1))))
    left = jnp.take(padded, jnp.asarray([cb[0] for cb in combos], jnp.int32), axis=1)
    right = jnp.take(padded, jnp.asarray([cb[1] for cb in combos], jnp.int32), axis=1)
    return jnp.concatenate([left, right], axis=-1)


DF_TQ, DF_TK = 1024, 512
DF_TG = 256
DF_SBUF = 2
DF_VROWS = DF_DV + ONES_ROWS


def _df_kernel(q_ref, k_ref, v_ref, lam_ref, gsub_ref, o_ref, q12_ref, q12n_ref, s_ref, acc_ref, vt_ref, *, seq):
    n_chunks = seq // DF_TK
    n_blocks = seq // DF_TQ
    groups = 2 * DF_TQ // DF_TG

    for c in range(n_chunks):
        cols = slice(c * DF_TK, (c + 1) * DF_TK)
        vt_ref[:, cols] = _with_ones_rows(v_ref[cols, :].T)

    lane = lax.broadcasted_iota(jnp.int32, (1, LANES), 1)
    comp0 = (lane % 64) < 32

    def split_maps(block, dst_ref):
        q = q_ref[pl.ds(pl.multiple_of(block * DF_TQ, DF_TQ), DF_TQ), :]
        zero = jnp.zeros_like(q)
        dst_ref[0:DF_TQ, :] = jnp.where(comp0, q, zero)
        dst_ref[DF_TQ:2 * DF_TQ, :] = jnp.where(comp0, zero, q)

    def scores(kc, q_src_ref, g):
        cols = slice(g * DF_TG, (g + 1) * DF_TG)
        return lax.dot_general(kc, q_src_ref[cols, :], (((1,), (1,)), ((), ())), preferred_element_type=F32)

    def keys(c):
        return k_ref[c * DF_TK:(c + 1) * DF_TK, :]

    def step(c, carry):
        cur, nxt = c % DF_SBUF, (c + 1) % DF_SBUF
        vc = vt_ref[:, c * DF_TK:(c + 1) * DF_TK]
        kn, q_next = (keys(c + 1), q12_ref) if c + 1 < n_chunks else (keys(0), q12n_ref)
        out = []
        for g in range(groups):
            m_prev = carry[g]
            cols = slice(g * DF_TG, (g + 1) * DF_TG)
            s_ref[nxt, :, cols] = scores(kn, q_next, g)
            s = s_ref[cur, :, cols]
            m_new = jnp.maximum(m_prev, jnp.max(s, axis=0, keepdims=True))
            alpha = jnp.exp2(m_prev - m_new)
            p = jnp.exp2(s - m_new).astype(BF16)
            pv = jnp.dot(vc, p, preferred_element_type=F32)
            acc_ref[:, cols] = alpha * acc_ref[:, cols] + pv
            out.append(m_new)
        return tuple(out)

    lp = lam_ref[...]
    lam = (jnp.exp(jnp.sum(lp[0:1] * lp[1:2], axis=-1, keepdims=True))
           - jnp.exp(jnp.sum(lp[2:3] * lp[3:4], axis=-1, keepdims=True)) + LAM_INIT)

    def finish(block):
        o12 = acc_ref[0:DF_DV, :] / acc_ref[DF_DV:DF_DV + 1, :]
        o_t = o12[:, 0:DF_TQ] - lam * o12[:, DF_TQ:2 * DF_TQ]
        rows = pl.ds(pl.multiple_of(block * DF_TQ, DF_TQ), DF_TQ)
        o_ref[rows, :] = (_rms_rows(o_t.T, gsub_ref[...]) * (1.0 - LAM_INIT)).astype(BF16)

    assert n_chunks % DF_SBUF == 0
    split_maps(0, q12n_ref)
    k0 = keys(0)
    for g in range(groups):
        s_ref[0, :, g * DF_TG:(g + 1) * DF_TG] = scores(k0, q12n_ref, g)
    acc_ref[...] = jnp.zeros(acc_ref.shape, F32)
    acc_ref[DF_DV:DF_DV + 1, :] = jnp.ones((1, 2 * DF_TQ), F32)

    def body(block, carry):
        finish(jnp.maximum(block - 1, 0))
        acc_ref[...] = jnp.zeros(acc_ref.shape, F32)
        split_maps(block, q12_ref)
        split_maps(jnp.minimum(block + 1, n_blocks - 1), q12n_ref)
        m = tuple(jnp.full((1, DF_TG), -jnp.inf, F32) for _ in range(groups))
        for c in range(n_chunks):
            m = step(c, m)
        return carry

    lax.fori_loop(0, n_blocks, body, 0)
    finish(n_blocks - 1)


def _df_attn(proj, lam_p, g_sub, batch, seq):
    const = lambda b, h: (0, 0)
    return pl.pallas_call(
        functools.partial(_df_kernel, seq=seq),
        out_shape=jax.ShapeDtypeStruct((DF_HEADS, batch * seq, LANES), BF16),
        grid=(batch, DF_HEADS),
        in_specs=[
            pl.BlockSpec((None, seq, LANES), lambda b, h: (CB_DF_Q + h, b, 0)),
            pl.BlockSpec((None, seq, LANES), lambda b, h: (CB_DF_K + h, b, 0)),
            pl.BlockSpec((None, seq, LANES), lambda b, h: (CB_DF_V + h, b, 0)),
            pl.BlockSpec((4, DF_DQ), const),
            pl.BlockSpec((1, DF_DV), const),
        ],
        out_specs=pl.BlockSpec((None, seq, LANES), lambda b, h: (h, b, 0)),
        scratch_shapes=[pltpu.VMEM((2 * DF_TQ, LANES), BF16), pltpu.VMEM((2 * DF_TQ, LANES), BF16),
                        pltpu.VMEM((DF_SBUF, DF_TK, 2 * DF_TQ), F32),
                        pltpu.VMEM((DF_VROWS, 2 * DF_TQ), F32), pltpu.VMEM((DF_VROWS, seq), BF16)],
        compiler_params=_params(("parallel", "parallel")),
        name="df_attn",
    )(proj, proj, proj, lam_p, g_sub)


def _merge_kernel(oa_ref, ob_ref, wa_ref, wb_ref, sa_ref, sb_ref, o_ref, wab_ref, wbb_ref):
    @pl.when(pl.program_id(0) == 0)
    def _():
        _cast_resident(wa_ref, wab_ref)
        _cast_resident(wb_ref, wbb_ref)

    def rows(slab_ref):
        return jnp.concatenate([slab_ref[k] for k in range(slab_ref.shape[0])], axis=1)

    ya = jnp.dot(rows(oa_ref), wab_ref[...], preferred_element_type=F32)
    yb = jnp.dot(rows(ob_ref), wbb_ref[...], preferred_element_type=F32)
    o_ref[...] = (rows(sa_ref).astype(F32) * ya + rows(sb_ref).astype(F32) * yb).astype(BF16)


def _merge(o_a, o_b, w_na_out, w_df_out, proj):
    n = o_a.shape[1]
    gate_slabs = D_MODEL // LANES
    ga, gb = COL_GATE_A // D_MODEL, COL_GATE_B // D_MODEL
    return pl.pallas_call(
        _merge_kernel,
        out_shape=jax.ShapeDtypeStruct((n, D_MODEL), BF16),
        grid=(n // ROW_TM,),
        in_specs=[
            pl.BlockSpec((NA_HEADS, ROW_TM, LANES), lambda i: (0, i, 0)),
            pl.BlockSpec((DF_HEADS, ROW_TM, LANES), lambda i: (0, i, 0)),
            _resident((NA_W, D_MODEL)),
            _resident((DF_VW, D_MODEL)),
            pl.BlockSpec((gate_slabs, ROW_TM, LANES), lambda i: (ga, i, 0)),
            pl.BlockSpec((gate_slabs, ROW_TM, LANES), lambda i: (gb, i, 0)),
        ],
        out_specs=pl.BlockSpec((ROW_TM, D_MODEL), lambda i: (i, 0)),
        scratch_shapes=[pltpu.VMEM((NA_W, D_MODEL), BF16), pltpu.VMEM((DF_VW, D_MODEL), BF16)],
        compiler_params=_params(("arbitrary",)),
        name="merge",
    )(o_a, o_b, w_na_out, w_df_out, proj, proj)


DN_TM, DN_TN = 1024, 256


def _ffn_down_kernel(a_ref, w_ref, x_ref, o_ref):
    o_ref[...] = x_ref[...] + jnp.dot(a_ref[...], w_ref[...].astype(BF16), preferred_element_type=F32)


def _ffn_down(a, w, x):
    n, k = a.shape
    d = w.shape[1]
    return pl.pallas_call(
        _ffn_down_kernel,
        out_shape=jax.ShapeDtypeStruct((n, d), F32),
        grid=(n // DN_TM, d // DN_TN),
        in_specs=[
            pl.BlockSpec((DN_TM, k), lambda i, j: (i, 0)),
            pl.BlockSpec((k, DN_TN), lambda i, j: (0, j)),
            pl.BlockSpec((DN_TM, DN_TN), lambda i, j: (i, j)),
        ],
        out_specs=pl.BlockSpec((DN_TM, DN_TN), lambda i, j: (i, j)),
        compiler_params=_params(("parallel", "arbitrary")),
        name="ffn_down",
    )(a, w, x)


def _out_proj_norm_kernel(a_ref, w_ref, x_ref, g_ref, o_ref, h_ref, wb_ref):
    @pl.when(pl.program_id(0) == 0)
    def _():
        _cast_resident(w_ref, wb_ref)

    y = x_ref[...] + jnp.dot(a_ref[...], wb_ref[...], preferred_element_type=F32)
    o_ref[...] = y
    h_ref[...] = _rms_rows(y, g_ref[...]).astype(BF16)


def _out_proj_norm(a, w, x, g):
    n, k = a.shape
    d = w.shape[1]
    return pl.pallas_call(
        _out_proj_norm_kernel,
        out_shape=(jax.ShapeDtypeStruct((n, d), F32), jax.ShapeDtypeStruct((n, d), BF16)),
        grid=(n // ROW_TM,),
        in_specs=[
            pl.BlockSpec((ROW_TM, k), lambda i: (i, 0)),
            _resident((k, d)),
            pl.BlockSpec((ROW_TM, d), lambda i: (i, 0)),
            pl.BlockSpec((1, d), lambda i: (0, 0)),
        ],
        out_specs=(pl.BlockSpec((ROW_TM, d), lambda i: (i, 0)), pl.BlockSpec((ROW_TM, d), lambda i: (i, 0))),
        scratch_shapes=[pltpu.VMEM((k, d), BF16)],
        compiler_params=_params(("arbitrary",)),
        name="out_proj",
    )(a, w, x, g)


UP_TM, UP_TN = 2048, 512
UP_RB = 512


def _ffn_up_kernel(h_ref, wg_ref, wu_ref, o_ref):
    wg = wg_ref[...].astype(BF16)
    wu = wu_ref[...].astype(BF16)
    for r in range(UP_TM // UP_RB):
        rows = slice(r * UP_RB, (r + 1) * UP_RB)
        h = h_ref[rows, :]
        gate = jnp.dot(h, wg, preferred_element_type=F32)
        up = jnp.dot(h, wu, preferred_element_type=F32)
        o_ref[rows, :] = (gate * jax.nn.sigmoid(gate) * up).astype(BF16)


def _ffn_up(h, w_gate, w_up):
    n = h.shape[0]
    return pl.pallas_call(
        _ffn_up_kernel,
        out_shape=jax.ShapeDtypeStruct((n, FFN_HID), BF16),
        grid=(n // UP_TM, FFN_HID // UP_TN),
        in_specs=[
            pl.BlockSpec((UP_TM, D_MODEL), lambda i, j: (i, 0)),
            pl.BlockSpec((D_MODEL, UP_TN), lambda i, j: (0, j)),
            pl.BlockSpec((D_MODEL, UP_TN), lambda i, j: (0, j)),
        ],
        out_specs=pl.BlockSpec((UP_TM, UP_TN), lambda i, j: (i, j)),
        compiler_params=_params(("parallel", "arbitrary")),
        name="ffn_up",
    )(h, w_gate, w_up)


def _ple_kernel(x_ref, g_ref, wg_ref, p_ref, wp_ref, o_ref, wgb_ref, wpb_ref):
    @pl.when(pl.program_id(0) == 0)
    def _():
        _cast_resident(wg_ref, wgb_ref)
        _cast_resident(wp_ref, wpb_ref)

    x = x_ref[...]
    h = _rms_rows(x, g_ref[...]).astype(BF16)
    gate = jnp.dot(h, wgb_ref[...], preferred_element_type=F32)
    emb = jnp.dot(p_ref[...].astype(BF16), wpb_ref[...], preferred_element_type=F32)
    o_ref[...] = x + jax.nn.sigmoid(gate) * emb


def _ple(x, g, w_gate, p, w_proj):
    n, d = x.shape
    return pl.pallas_call(
        _ple_kernel,
        out_shape=jax.ShapeDtypeStruct((n, d), F32),
        grid=(n // ROW_TM,),
        in_specs=[
            pl.BlockSpec((ROW_TM, d), lambda i: (i, 0)),
            pl.BlockSpec((1, d), lambda i: (0, 0)),
            _resident((d, d)),
            pl.BlockSpec((ROW_TM, PLE_DIM), lambda i: (i, 0)),
            _resident((PLE_DIM, d)),
        ],
        out_specs=pl.BlockSpec((ROW_TM, d), lambda i: (i, 0)),
        scratch_shapes=[pltpu.VMEM((d, d), BF16), pltpu.VMEM((PLE_DIM, d), BF16)],
        compiler_params=_params(("arbitrary",)),
        name="ple",
    )(x, g, w_gate, p, w_proj)


def _rope_tables(seq, g, scale):
    half = DF_DQ // 2
    inv = 1.0 / (ROPE_THETA ** (jnp.arange(0, DF_DQ, 2, dtype=F32) / DF_DQ))
    ang = jnp.arange(seq, dtype=F32)[:, None] * inv[None, :]
    cos, sin = jnp.cos(ang), jnp.sin(ang)
    g1, g2 = g[:half], g[half:]
    ca = jnp.concatenate([cos * g1, cos * g1, cos * g2, cos * g2], axis=1) * scale
    sb = jnp.concatenate([-sin * g2, -sin * g2, sin * g1, sin * g1], axis=1) * scale
    return ca, sb


def kernel(x, p, g_mix, w_in, g_na_q, g_na_k, na_rpb, g_df_q, g_df_k, lam_q1, lam_k1, lam_q2, lam_k2,
           g_df_sub, w_na_out, w_df_out, w_o, g_ffn, w_gate, w_up, w_down, g_ple, w_ple_gate, w_ple_proj):
    batch, seq, d = x.shape
    n = batch * seq
    rows = seq // GRID_W
    depth = w_in.shape[0]
    xf = x.reshape(n, d)
    for i in range(depth):
        caq, sbq = _rope_tables(seq, g_df_q[i], DF_DQ ** -0.5 * math.log2(math.e))
        cak, sbk = _rope_tables(seq, g_df_k[i], 1.0)
        proj = _in_proj(_rms_norm(xf, g_mix[i][None]), w_in[i], g_na_q[i][None], g_na_k[i][None],
                        caq, sbq, cak, sbk, seq)

        o_a = _na_attn(proj, _na_bias_rows(na_rpb[i], rows), batch, seq)
        lam_p = jnp.stack([lam_q1[i], lam_k1[i], lam_q2[i], lam_k2[i]], axis=0)
        o_b = _df_attn(proj, lam_p, g_df_sub[i][None], batch, seq)

        merged = _merge(o_a, o_b, w_na_out[i], w_df_out[i], proj)
        xf, hf = _out_proj_norm(merged, w_o[i], xf, g_ffn[i][None])

        act = _ffn_up(hf, w_gate[i], w_up[i])
        xf = _ffn_down(act, w_down[i], xf)

        xf = _ple(xf, g_ple[i][None], w_ple_gate[i], p[i].reshape(n, PLE_DIM), w_ple_proj[i])
    return xf.reshape(batch, seq, d)
```

```python
import functools
import math

import numpy as np
import jax
import jax.numpy as jnp
from jax import lax
from jax.experimental import pallas as pl
from jax.experimental.pallas import tpu as pltpu

D_MODEL = 2048
GRID_W = 64
NA_HEADS = 8
NA_DH = 128
NA_KR = 8
NA_KW = 16
DF_HEADS = 8
DF_DQ = 64
DF_DV = 128
FFN_HID = 5632
PLE_DIM = 256
ROPE_THETA = 10000.0
EPS = 1e-6
LAM_INIT = 0.8 - 0.6 * math.exp(-0.3 * 0)

NA_W = NA_HEADS * NA_DH
DF_QW = DF_HEADS * 2 * DF_DQ
DF_VW = DF_HEADS * DF_DV
IN_COLS = 3 * NA_W + 2 * DF_QW + DF_VW + 2 * D_MODEL

LANES = 128
NEG = -1e30
VMEM_LIMIT = 56 * 1024 * 1024

F32 = jnp.float32
BF16 = jnp.bfloat16

CB_NA_Q, CB_NA_K, CB_NA_V = 0, 8, 16
CB_DF_Q, CB_DF_K, CB_DF_V = 24, 32, 40
COL_GATE_A, COL_GATE_B = 6144, 8192


def _params(sem):
    return pltpu.CompilerParams(dimension_semantics=sem, vmem_limit_bytes=VMEM_LIMIT)


def _rms_rows(x, g):
    ms = jnp.mean(x * x, axis=-1, keepdims=True)
    return x * lax.rsqrt(ms + EPS) * g


ONES_ROWS = 16


def _with_ones_rows(v_t):
    row = lax.broadcasted_iota(jnp.int32, (ONES_ROWS, v_t.shape[1]), 0)
    return jnp.concatenate([v_t, jnp.where(row == 0, 1.0, 0.0).astype(v_t.dtype)], axis=0)


NORM_ROWS = 256


def _norm_block(x_ref, g_ref, h_ref):
    def body(c, carry):
        r = pl.ds(pl.multiple_of(c * NORM_ROWS, NORM_ROWS), NORM_ROWS)
        h_ref[r, :] = _rms_rows(x_ref[r, :], g_ref[...]).astype(BF16)
        return carry

    lax.fori_loop(0, x_ref.shape[0] // NORM_ROWS, body, 0)


def _rms_norm_kernel(x_ref, g_ref, o_ref):
    _norm_block(x_ref, g_ref, o_ref)


def _rms_norm(x, g, tm=512):
    n, d = x.shape
    return pl.pallas_call(
        _rms_norm_kernel,
        out_shape=jax.ShapeDtypeStruct((n, d), BF16),
        grid=(n // tm,),
        in_specs=[pl.BlockSpec((tm, d), lambda i: (i, 0)), pl.BlockSpec((1, d), lambda i: (0, 0))],
        out_specs=pl.BlockSpec((tm, d), lambda i: (i, 0)),
        compiler_params=_params(("parallel",)),
        name="rms_norm",
    )(x, g)


CAST_ROWS = 256


def _cast_resident(w_ref, wb_ref):
    rows = min(CAST_ROWS, w_ref.shape[0])

    def body(c, carry):
        r = pl.ds(pl.multiple_of(c * rows, rows), rows)
        wb_ref[r, :] = w_ref[r, :].astype(BF16)
        return carry

    lax.fori_loop(0, w_ref.shape[0] // rows, body, 0)


def _resident(shape):
    return pl.BlockSpec(shape, lambda i: (0, 0), pipeline_mode=pl.Buffered(1))


ROW_TM = 256


IN_TM, IN_TN = 2048, 512
IN_RB = 256
SEG_TILES = 1024 // IN_TN


def _in_proj_kernel(h_ref, w_ref, gq_ref, gk_ref, caq_ref, sbq_ref, cak_ref, sbk_ref, o_ref):
    seg = pl.program_id(1) // SEG_TILES
    heads = IN_TN // LANES

    def rotary_layout(w):
        lane = lax.broadcasted_iota(jnp.int32, (w.shape[0], LANES), 1)
        quarter = DF_DQ // 2
        src = (lane // quarter % 2) * DF_DQ + (lane // DF_DQ) * quarter + lane % quarter
        return jnp.concatenate([jnp.take_along_axis(w[:, k * LANES:(k + 1) * LANES], src, axis=1)
                                for k in range(heads)], axis=1)

    def run(w, epilogue):
        wb = w.astype(BF16)
        for r in range(IN_TM // IN_RB):
            rows = slice(r * IN_RB, (r + 1) * IN_RB)
            epilogue(jnp.dot(h_ref[rows, :], wb, preferred_element_type=F32), rows)

    def na_norm(g_ref):
        def epilogue(y, rows):
            for k in range(heads):
                cols = slice(k * LANES, (k + 1) * LANES)
                o_ref[k, rows, :] = _rms_rows(y[:, cols], g_ref[...]).astype(BF16)
        return epilogue

    def df_norm_rope(ca_ref, sb_ref):
        def epilogue(y, rows):
            lane = lax.broadcasted_iota(jnp.int32, (1, LANES), 1)
            comp0 = (lane % 64) < 32
            ca = ca_ref[rows, :]
            sb = sb_ref[rows, :]
            for k in range(heads):
                cols = slice(k * LANES, (k + 1) * LANES)
                yk = y[:, cols]
                pk = pltpu.roll(yk, 64, 1)
                z = yk * yk + pk * pk
                s0 = jnp.sum(jnp.where(comp0, z, 0.0), axis=-1, keepdims=True)
                s1 = jnp.sum(jnp.where(comp0, 0.0, z), axis=-1, keepdims=True)
                r = jnp.where(comp0, lax.rsqrt(s0 * (0.5 / DF_DQ) + EPS), lax.rsqrt(s1 * (0.5 / DF_DQ) + EPS))
                o_ref[k, rows, :] = ((yk * ca + pk * sb) * r).astype(BF16)
        return epilogue

    def plain(y, rows):
        for k in range(heads):
            o_ref[k, rows, :] = y[:, k * LANES:(k + 1) * LANES].astype(BF16)

    def gate(y, rows):
        for k in range(heads):
            o_ref[k, rows, :] = jax.nn.sigmoid(y[:, k * LANES:(k + 1) * LANES]).astype(BF16)

    @pl.when(seg == 0)
    def _():
        run(w_ref[...], na_norm(gq_ref))

    @pl.when(seg == 1)
    def _():
        run(w_ref[...], na_norm(gk_ref))

    @pl.when(seg == 3)
    def _():
        run(rotary_layout(w_ref[...]), df_norm_rope(caq_ref, sbq_ref))

    @pl.when(seg == 4)
    def _():
        run(rotary_layout(w_ref[...]), df_norm_rope(cak_ref, sbk_ref))

    @pl.when((seg == 2) | (seg == 5))
    def _():
        run(w_ref[...], plain)

    @pl.when(seg >= 6)
    def _():
        run(w_ref[...], gate)


def _in_proj(h, w_in, g_na_q, g_na_k, caq, sbq, cak, sbk, seq):
    n = h.shape[0]
    pos_blocks = seq // IN_TM
    row = lambda i, j: (i, 0)
    const = lambda i, j: (0, 0)
    tab = lambda i, j: (i % pos_blocks, 0)
    return pl.pallas_call(
        _in_proj_kernel,
        out_shape=jax.ShapeDtypeStruct((IN_COLS // LANES, n, LANES), BF16),
        grid=(n // IN_TM, IN_COLS // IN_TN),
        in_specs=[
            pl.BlockSpec((IN_TM, D_MODEL), row),
            pl.BlockSpec((D_MODEL, IN_TN), lambda i, j: (0, j)),
            pl.BlockSpec((1, LANES), const),
            pl.BlockSpec((1, LANES), const),
            pl.BlockSpec((IN_TM, LANES), tab),
            pl.BlockSpec((IN_TM, LANES), tab),
            pl.BlockSpec((IN_TM, LANES), tab),
            pl.BlockSpec((IN_TM, LANES), tab),
        ],
        out_specs=pl.BlockSpec((IN_TN // LANES, IN_TM, LANES), lambda i, j: (j, i, 0)),
        compiler_params=_params(("parallel", "arbitrary")),
        name="in_proj",
    )(h, w_in, g_na_q, g_na_k, caq, sbq, cak, sbk)


NA_QROWS = 8
NA_WROWS = 16
NA_TQ = NA_QROWS * GRID_W
NA_TK = NA_WROWS * GRID_W
NA_PAIR = LANES // GRID_W
NA_NPAIR = NA_QROWS // NA_PAIR
NA_PROWS = NA_KR + NA_PAIR - 1
NA_PK = NA_PROWS * GRID_W
NA_NTAB = 1 + 2 * NA_NPAIR
NA_NEG_SLAB = 2 * NA_KR - 1


def _na_geometry(rows):
    nt = rows // NA_QROWS
    assert nt >= 3 and rows >= NA_WROWS
    ws = np.zeros((3, NA_NPAIR), np.int64)
    idx = np.full((NA_NTAB, NA_PROWS, NA_PAIR), NA_NEG_SLAB, np.int64)
    for ci, t in enumerate((0, 1, nt - 1)):
        w0 = int(np.clip(t * NA_QROWS - NA_KR // 2, 0, rows - NA_WROWS))
        for pi in range(NA_NPAIR):
            r = [t * NA_QROWS + NA_PAIR * pi + hb for hb in range(NA_PAIR)]
            rs = [int(np.clip(ri - NA_KR // 2, 0, rows - NA_KR)) for ri in r]
            start = min(min(rs) - w0, NA_WROWS - NA_PROWS)
            assert 0 <= start and max(rs) + NA_KR <= w0 + start + NA_PROWS and min(rs) >= w0 + start
            ws[ci, pi] = start
            tab = 0 if ci == 1 else 1 + (0 if ci == 0 else NA_NPAIR) + pi
            for wp in range(NA_PROWS):
                key_row = w0 + start + wp
                for hb in range(NA_PAIR):
                    if rs[hb] <= key_row < rs[hb] + NA_KR:
                        idx[tab, wp, hb] = key_row - r[hb] + NA_KR - 1
    combos = sorted({tuple(int(v) for v in pair) for pair in idx.reshape(-1, NA_PAIR)})
    slab = [[combos.index(tuple(int(v) for v in idx[tab, wp])) for wp in range(NA_PROWS)] for tab in range(NA_NTAB)]
    return [[int(v) for v in row] for row in ws], combos, slab


def _na_kernel(q_ref, k_ref, v_ref, row_ref, o_ref, p_ref, slab_ref, *, rows):
    nt = rows // NA_QROWS
    ws, combos, slab = _na_geometry(rows)

    kc = lax.broadcasted_iota(jnp.int32, (GRID_W, LANES), 0)
    lane = lax.broadcasted_iota(jnp.int32, (GRID_W, LANES), 1)
    c = lane % GRID_W
    cs = jnp.clip(c - NA_KW // 2, 0, GRID_W - NA_KW)
    in_window = (kc >= cs) & (kc < cs + NA_KW)
    src = jnp.clip(kc - c + NA_KW - 1, 0, 2 * NA_KW - 2) + jnp.where(lane < GRID_W, 0, GRID_W)
    for ci, (left, right) in enumerate(combos):
        ok = in_window
        if left == NA_NEG_SLAB:
            ok = ok & (lane >= GRID_W)
        if right == NA_NEG_SLAB:
            ok = ok & (lane < GRID_W)
        row = jnp.broadcast_to(row_ref[0, ci:ci + 1, :], (GRID_W, LANES))
        slab_ref[ci] = jnp.where(ok, jnp.take_along_axis(row, src, axis=1), NEG)

    def window(t):
        return int(np.clip(t * NA_QROWS - NA_KR // 2, 0, rows - NA_WROWS)) * GRID_W

    def probs(t):
        cls = 0 if t == 0 else (2 if t == nt - 1 else 1)
        buf = t
        p_ref[buf] = jnp.zeros(p_ref.shape[1:], BF16)
        for pi in range(NA_NPAIR):
            tab = 0 if cls == 1 else 1 + (0 if cls == 0 else NA_NPAIR) + pi
            r0 = ws[cls][pi] * GRID_W
            kp = k_ref[window(t) + r0:window(t) + r0 + NA_PK, :]
            qp = q_ref[t * NA_TQ + pi * LANES:t * NA_TQ + (pi + 1) * LANES, :]
            sp = lax.dot_general(kp, qp, (((1,), (1,)), ((), ())), preferred_element_type=F32)
            bias = jnp.concatenate([slab_ref[ci] for ci in slab[tab]], axis=0)
            sp = sp * (NA_DH ** -0.5) + bias
            m = jnp.max(sp, axis=0, keepdims=True)
            p_ref[buf, pi, r0:r0 + NA_PK, :] = jnp.exp(sp - m).astype(BF16)

    def outputs(t):
        vw = _with_ones_rows(v_ref[window(t):window(t) + NA_TK, :].T)
        pt = jnp.concatenate([p_ref[t, pi] for pi in range(NA_NPAIR)], axis=1)
        ov = jnp.dot(vw, pt, preferred_element_type=F32)
        o_ref[t * NA_TQ:(t + 1) * NA_TQ, :] = (ov[0:NA_DH, :] / ov[NA_DH:NA_DH + 1, :]).T.astype(BF16)

    for t in range(nt + 1):
        if t < nt:
            probs(t)
        if t >= 1:
            outputs(t - 1)


def _na_attn(proj, bias_rows, batch, seq):
    rows = seq // GRID_W
    n_slabs = bias_rows.shape[1]
    return pl.pallas_call(
        functools.partial(_na_kernel, rows=rows),
        out_shape=jax.ShapeDtypeStruct((NA_HEADS, batch * seq, LANES), BF16),
        grid=(batch, NA_HEADS),
        in_specs=[
            pl.BlockSpec((None, seq, LANES), lambda b, h: (CB_NA_Q + h, b, 0)),
            pl.BlockSpec((None, seq, LANES), lambda b, h: (CB_NA_K + h, b, 0)),
            pl.BlockSpec((None, seq, LANES), lambda b, h: (CB_NA_V + h, b, 0)),
            pl.BlockSpec((1, n_slabs, LANES), lambda b, h: (h, 0, 0)),
        ],
        out_specs=pl.BlockSpec((None, seq, LANES), lambda b, h: (h, b, 0)),
        scratch_shapes=[pltpu.VMEM((rows // NA_QROWS, NA_NPAIR, NA_TK, LANES), BF16),
                        pltpu.VMEM((n_slabs, GRID_W, LANES), F32)],
        compiler_params=_params(("parallel", "arbitrary")),
        name="na_attn",
    )(proj, proj, proj, bias_rows)


def _na_bias_rows(rpb, rows):
    _, combos, _ = _na_geometry(rows)
    padded = jnp.pad(rpb, ((0, 0), (0, 1), (0, GRID_W - (2 * NA_KW - 1))))
    left = jnp.take(padded, jnp.asarray([cb[0] for cb in combos], jnp.int32), axis=1)
    right = jnp.take(padded, jnp.asarray([cb[1] for cb in combos], jnp.int32), axis=1)
    return jnp.concatenate([left, right], axis=-1)


DF_TQ, DF_TK = 1024, 512
DF_TG = 256
DF_SBUF = 2
DF_VROWS = DF_DV + ONES_ROWS


def _df_kernel(q_ref, k_ref, v_ref, lam_ref, gsub_ref, o_ref, q12_ref, q12n_ref, s_ref, acc_ref, vt_ref, *, seq):
    n_chunks = seq // DF_TK
    n_blocks = seq // DF_TQ
    groups = 2 * DF_TQ // DF_TG

    for c in range(n_chunks):
        cols = slice(c * DF_TK, (c + 1) * DF_TK)
        vt_ref[:, cols] = _with_ones_rows(v_ref[cols, :].T)

    lane = lax.broadcasted_iota(jnp.int32, (1, LANES), 1)
    comp0 = (lane % 64) < 32

    def split_maps(block, dst_ref):
        q = q_ref[pl.ds(pl.multiple_of(block * DF_TQ, DF_TQ), DF_TQ), :]
        zero = jnp.zeros_like(q)
        dst_ref[0:DF_TQ, :] = jnp.where(comp0, q, zero)
        dst_ref[DF_TQ:2 * DF_TQ, :] = jnp.where(comp0, zero, q)

    def scores(kc, q_src_ref, g):
        cols = slice(g * DF_TG, (g + 1) * DF_TG)
        return lax.dot_general(kc, q_src_ref[cols, :], (((1,), (1,)), ((), ())), preferred_element_type=F32)

    def keys(c):
        return k_ref[c * DF_TK:(c + 1) * DF_TK, :]

    def step(c, carry):
        cur, nxt = c % DF_SBUF, (c + 1) % DF_SBUF
        vc = vt_ref[:, c * DF_TK:(c + 1) * DF_TK]
        kn, q_next = (keys(c + 1), q12_ref) if c + 1 < n_chunks else (keys(0), q12n_ref)
        out = []
        for g in range(groups):
            m_prev = carry[g]
            cols = slice(g * DF_TG, (g + 1) * DF_TG)
            s_ref[nxt, :, cols] = scores(kn, q_next, g)
            s = s_ref[cur, :, cols]
            m_new = jnp.maximum(m_prev, jnp.max(s, axis=0, keepdims=True))
            alpha = jnp.exp2(m_prev - m_new)
            p = jnp.exp2(s - m_new).astype(BF16)
            pv = jnp.dot(vc, p, preferred_element_type=F32)
            acc_ref[:, cols] = alpha * acc_ref[:, cols] + pv
            out.append(m_new)
        return tuple(out)

    lp = lam_ref[...]
    lam = (jnp.exp(jnp.sum(lp[0:1] * lp[1:2], axis=-1, keepdims=True))
           - jnp.exp(jnp.sum(lp[2:3] * lp[3:4], axis=-1, keepdims=True)) + LAM_INIT)

    def finish(block):
        o12 = acc_ref[0:DF_DV, :] / acc_ref[DF_DV:DF_DV + 1, :]
        o_t = o12[:, 0:DF_TQ] - lam * o12[:, DF_TQ:2 * DF_TQ]
        rows = pl.ds(pl.multiple_of(block * DF_TQ, DF_TQ), DF_TQ)
        o_ref[rows, :] = (_rms_rows(o_t.T, gsub_ref[...]) * (1.0 - LAM_INIT)).astype(BF16)

    assert n_chunks % DF_SBUF == 0
    split_maps(0, q12n_ref)
    k0 = keys(0)
    for g in range(groups):
        s_ref[0, :, g * DF_TG:(g + 1) * DF_TG] = scores(k0, q12n_ref, g)
    acc_ref[...] = jnp.zeros(acc_ref.shape, F32)
    acc_ref[DF_DV:DF_DV + 1, :] = jnp.ones((1, 2 * DF_TQ), F32)

    def body(block, carry):
        finish(jnp.maximum(block - 1, 0))
        acc_ref[...] = jnp.zeros(acc_ref.shape, F32)
        split_maps(block, q12_ref)
        split_maps(jnp.minimum(block + 1, n_blocks - 1), q12n_ref)
        m = tuple(jnp.full((1, DF_TG), -jnp.inf, F32) for _ in range(groups))
        for c in range(n_chunks):
            m = step(c, m)
        return carry

    lax.fori_loop(0, n_blocks, body, 0)
    finish(n_blocks - 1)


def _df_attn(proj, lam_p, g_sub, batch, seq):
    const = lambda b, h: (0, 0)
    return pl.pallas_call(
        functools.partial(_df_kernel, seq=seq),
        out_shape=jax.ShapeDtypeStruct((DF_HEADS, batch * seq, LANES), BF16),
        grid=(batch, DF_HEADS),
        in_specs=[
            pl.BlockSpec((None, seq, LANES), lambda b, h: (CB_DF_Q + h, b, 0)),
            pl.BlockSpec((None, seq, LANES), lambda b, h: (CB_DF_K + h, b, 0)),
            pl.BlockSpec((None, seq, LANES), lambda b, h: (CB_DF_V + h, b, 0)),
            pl.BlockSpec((4, DF_DQ), const),
            pl.BlockSpec((1, DF_DV), const),
        ],
        out_specs=pl.BlockSpec((None, seq, LANES), lambda b, h: (h, b, 0)),
        scratch_shapes=[pltpu.VMEM((2 * DF_TQ, LANES), BF16), pltpu.VMEM((2 * DF_TQ, LANES), BF16),
                        pltpu.VMEM((DF_SBUF, DF_TK, 2 * DF_TQ), F32),
                        pltpu.VMEM((DF_VROWS, 2 * DF_TQ), F32), pltpu.VMEM((DF_VROWS, seq), BF16)],
        compiler_params=_params(("parallel", "parallel")),
        name="df_attn",
    )(proj, proj, proj, lam_p, g_sub)


def _merge_kernel(oa_ref, ob_ref, wa_ref, wb_ref, sa_ref, sb_ref, o_ref, wab_ref, wbb_ref):
    @pl.when(pl.program_id(0) == 0)
    def _():
        _cast_resident(wa_ref, wab_ref)
        _cast_resident(wb_ref, wbb_ref)

    def rows(slab_ref):
        return jnp.concatenate([slab_ref[k] for k in range(slab_ref.shape[0])], axis=1)

    ya = jnp.dot(rows(oa_ref), wab_ref[...], preferred_element_type=F32)
    yb = jnp.dot(rows(ob_ref), wbb_ref[...], preferred_element_type=F32)
    o_ref[...] = (rows(sa_ref).astype(F32) * ya + rows(sb_ref).astype(F32) * yb).astype(BF16)


def _merge(o_a, o_b, w_na_out, w_df_out, proj):
    n = o_a.shape[1]
    gate_slabs = D_MODEL // LANES
    ga, gb = COL_GATE_A // D_MODEL, COL_GATE_B // D_MODEL
    return pl.pallas_call(
        _merge_kernel,
        out_shape=jax.ShapeDtypeStruct((n, D_MODEL), BF16),
        grid=(n // ROW_TM,),
        in_specs=[
            pl.BlockSpec((NA_HEADS, ROW_TM, LANES), lambda i: (0, i, 0)),
            pl.BlockSpec((DF_HEADS, ROW_TM, LANES), lambda i: (0, i, 0)),
            _resident((NA_W, D_MODEL)),
            _resident((DF_VW, D_MODEL)),
            pl.BlockSpec((gate_slabs, ROW_TM, LANES), lambda i: (ga, i, 0)),
            pl.BlockSpec((gate_slabs, ROW_TM, LANES), lambda i: (gb, i, 0)),
        ],
        out_specs=pl.BlockSpec((ROW_TM, D_MODEL), lambda i: (i, 0)),
        scratch_shapes=[pltpu.VMEM((NA_W, D_MODEL), BF16), pltpu.VMEM((DF_VW, D_MODEL), BF16)],
        compiler_params=_params(("arbitrary",)),
        name="merge",
    )(o_a, o_b, w_na_out, w_df_out, proj, proj)


DN_TM, DN_TN = 1024, 256


def _ffn_down_kernel(a_ref, w_ref, x_ref, o_ref):
    o_ref[...] = x_ref[...] + jnp.dot(a_ref[...], w_ref[...].astype(BF16), preferred_element_type=F32)


def _ffn_down(a, w, x):
    n, k = a.shape
    d = w.shape[1]
    return pl.pallas_call(
        _ffn_down_kernel,
        out_shape=jax.ShapeDtypeStruct((n, d), F32),
        grid=(n // DN_TM, d // DN_TN),
        in_specs=[
            pl.BlockSpec((DN_TM, k), lambda i, j: (i, 0)),
            pl.BlockSpec((k, DN_TN), lambda i, j: (0, j)),
            pl.BlockSpec((DN_TM, DN_TN), lambda i, j: (i, j)),
        ],
        out_specs=pl.BlockSpec((DN_TM, DN_TN), lambda i, j: (i, j)),
        compiler_params=_params(("parallel", "arbitrary")),
        name="ffn_down",
    )(a, w, x)


def _out_proj_norm_kernel(a_ref, w_ref, x_ref, g_ref, o_ref, h_ref, wb_ref):
    @pl.when(pl.program_id(0) == 0)
    def _():
        _cast_resident(w_ref, wb_ref)

    y = x_ref[...] + jnp.dot(a_ref[...], wb_ref[...], preferred_element_type=F32)
    o_ref[...] = y
    h_ref[...] = _rms_rows(y, g_ref[...]).astype(BF16)


def _out_proj_norm(a, w, x, g):
    n, k = a.shape
    d = w.shape[1]
    return pl.pallas_call(
        _out_proj_norm_kernel,
        out_shape=(jax.ShapeDtypeStruct((n, d), F32), jax.ShapeDtypeStruct((n, d), BF16)),
        grid=(n // ROW_TM,),
        in_specs=[
            pl.BlockSpec((ROW_TM, k), lambda i: (i, 0)),
            _resident((k, d)),
            pl.BlockSpec((ROW_TM, d), lambda i: (i, 0)),
            pl.BlockSpec((1, d), lambda i: (0, 0)),
        ],
        out_specs=(pl.BlockSpec((ROW_TM, d), lambda i: (i, 0)), pl.BlockSpec((ROW_TM, d), lambda i: (i, 0))),
        scratch_shapes=[pltpu.VMEM((k, d), BF16)],
        compiler_params=_params(("arbitrary",)),
        name="out_proj",
    )(a, w, x, g)


UP_TM, UP_TN = 2048, 512
UP_RB = 512


def _ffn_up_kernel(h_ref, wg_ref, wu_ref, o_ref):
    wg = wg_ref[...].astype(BF16)
    wu = wu_ref[...].astype(BF16)
    for r in range(UP_TM // UP_RB):
        rows = slice(r * UP_RB, (r + 1) * UP_RB)
        h = h_ref[rows, :]
        gate = jnp.dot(h, wg, preferred_element_type=F32)
        up = jnp.dot(h, wu, preferred_element_type=F32)
        o_ref[rows, :] = (gate * jax.nn.sigmoid(gate) * up).astype(BF16)


def _ffn_up(h, w_gate, w_up):
    n = h.shape[0]
    return pl.pallas_call(
        _ffn_up_kernel,
        out_shape=jax.ShapeDtypeStruct((n, FFN_HID), BF16),
        grid=(n // UP_TM, FFN_HID // UP_TN),
        in_specs=[
            pl.BlockSpec((UP_TM, D_MODEL), lambda i, j: (i, 0)),
            pl.BlockSpec((D_MODEL, UP_TN), lambda i, j: (0, j)),
            pl.BlockSpec((D_MODEL, UP_TN), lambda i, j: (0, j)),
        ],
        out_specs=pl.BlockSpec((UP_TM, UP_TN), lambda i, j: (i, j)),
        compiler_params=_params(("parallel", "arbitrary")),
        name="ffn_up",
    )(h, w_gate, w_up)


def _ple_kernel(x_ref, g_ref, wg_ref, p_ref, wp_ref, o_ref, wgb_ref, wpb_ref):
    @pl.when(pl.program_id(0) == 0)
    def _():
        _cast_resident(wg_ref, wgb_ref)
        _cast_resident(wp_ref, wpb_ref)

    x = x_ref[...]
    h = _rms_rows(x, g_ref[...]).astype(BF16)
    gate = jnp.dot(h, wgb_ref[...], preferred_element_type=F32)
    emb = jnp.dot(p_ref[...].astype(BF16), wpb_ref[...], preferred_element_type=F32)
    o_ref[...] = x + jax.nn.sigmoid(gate) * emb


def _ple(x, g, w_gate, p, w_proj):
    n, d = x.shape
    return pl.pallas_call(
        _ple_kernel,
        out_shape=jax.ShapeDtypeStruct((n, d), F32),
        grid=(n // ROW_TM,),
        in_specs=[
            pl.BlockSpec((ROW_TM, d), lambda i: (i, 0)),
            pl.BlockSpec((1, d), lambda i: (0, 0)),
            _resident((d, d)),
            pl.BlockSpec((ROW_TM, PLE_DIM), lambda i: (i, 0)),
            _resident((PLE_DIM, d)),
        ],
        out_specs=pl.BlockSpec((ROW_TM, d), lambda i: (i, 0)),
        scratch_shapes=[pltpu.VMEM((d, d), BF16), pltpu.VMEM((PLE_DIM, d), BF16)],
        compiler_params=_params(("arbitrary",)),
        name="ple",
    )(x, g, w_gate, p, w_proj)


def _rope_tables(seq, g, scale):
    half = DF_DQ // 2
    inv = 1.0 / (ROPE_THETA ** (jnp.arange(0, DF_DQ, 2, dtype=F32) / DF_DQ))
    ang = jnp.arange(seq, dtype=F32)[:, None] * inv[None, :]
    cos, sin = jnp.cos(ang), jnp.sin(ang)
    g1, g2 = g[:half], g[half:]
    ca = jnp.concatenate([cos * g1, cos * g1, cos * g2, cos * g2], axis=1) * scale
    sb = jnp.concatenate([-sin * g2, -sin * g2, sin * g1, sin * g1], axis=1) * scale
    return ca, sb


def kernel(x, p, g_mix, w_in, g_na_q, g_na_k, na_rpb, g_df_q, g_df_k, lam_q1, lam_k1, lam_q2, lam_k2,
           g_df_sub, w_na_out, w_df_out, w_o, g_ffn, w_gate, w_up, w_down, g_ple, w_ple_gate, w_ple_proj):
    batch, seq, d = x.shape
    n = batch * seq
    rows = seq // GRID_W
    depth = w_in.shape[0]
    xf = x.reshape(n, d)
    for i in range(depth):
        caq, sbq = _rope_tables(seq, g_df_q[i], DF_DQ ** -0.5 * math.log2(math.e))
        cak, sbk = _rope_tables(seq, g_df_k[i], 1.0)
        proj = _in_proj(_rms_norm(xf, g_mix[i][None]), w_in[i], g_na_q[i][None], g_na_k[i][None],
                        caq, sbq, cak, sbk, seq)

        o_a = _na_attn(proj, _na_bias_rows(na_rpb[i], rows), batch, seq)
        lam_p = jnp.stack([lam_q1[i], lam_k1[i], lam_q2[i], lam_k2[i]], axis=0)
        o_b = _df_attn(proj, lam_p, g_df_sub[i][None], batch, seq)

        merged = _merge(o_a, o_b, w_na_out[i], w_df_out[i], proj)
        xf, hf = _out_proj_norm(merged, w_o[i], xf, g_ffn[i][None])

        act = _ffn_up(hf, w_gate[i], w_up[i])
        xf = _ffn_down(act, w_down[i], xf)

        xf = _ple(xf, g_ple[i][None], w_ple_gate[i], p[i].reshape(n, PLE_DIM), w_ple_proj[i])
    return xf.reshape(batch, seq, d)
```

```python
import functools
import math

import numpy as np
import jax
import jax.numpy as jnp
from jax import lax
from jax.experimental import pallas as pl
from jax.experimental.pallas import tpu as pltpu

D_MODEL = 2048
GRID_W = 64
NA_HEADS = 8
NA_DH = 128
NA_KR = 8
NA_KW = 16
DF_HEADS = 8
DF_DQ = 64
DF_DV = 128
FFN_HID = 5632
PLE_DIM = 256
ROPE_THETA = 10000.0
EPS = 1e-6
LAM_INIT = 0.8 - 0.6 * math.exp(-0.3 * 0)

NA_W = NA_HEADS * NA_DH
DF_QW = DF_HEADS * 2 * DF_DQ
DF_VW = DF_HEADS * DF_DV
IN_COLS = 3 * NA_W + 2 * DF_QW + DF_VW + 2 * D_MODEL

LANES = 128
NEG = -1e30
VMEM_LIMIT = 56 * 1024 * 1024

F32 = jnp.float32
BF16 = jnp.bfloat16

CB_NA_Q, CB_NA_K, CB_NA_V = 0, 8, 16
CB_DF_Q, CB_DF_K, CB_DF_V = 24, 32, 40
COL_GATE_A, COL_GATE_B = 6144, 8192


def _params(sem):
    return pltpu.CompilerParams(dimension_semantics=sem, vmem_limit_bytes=VMEM_LIMIT)


def _rms_rows(x, g):
    ms = jnp.mean(x * x, axis=-1, keepdims=True)
    return x * lax.rsqrt(ms + EPS) * g


ONES_ROWS = 16


def _with_ones_rows(v_t):
    row = lax.broadcasted_iota(jnp.int32, (ONES_ROWS, v_t.shape[1]), 0)
    return jnp.concatenate([v_t, jnp.where(row == 0, 1.0, 0.0).astype(v_t.dtype)], axis=0)


NORM_ROWS = 256


def _norm_block(x_ref, g_ref, h_ref):
    def body(c, carry):
        r = pl.ds(pl.multiple_of(c * NORM_ROWS, NORM_ROWS), NORM_ROWS)
        h_ref[r, :] = _rms_rows(x_ref[r, :], g_ref[...]).astype(BF16)
        return carry

    lax.fori_loop(0, x_ref.shape[0] // NORM_ROWS, body, 0)


def _rms_norm_kernel(x_ref, g_ref, o_ref):
    _norm_block(x_ref, g_ref, o_ref)


def _rms_norm(x, g, tm=512):
    n, d = x.shape
    return pl.pallas_call(
        _rms_norm_kernel,
        out_shape=jax.ShapeDtypeStruct((n, d), BF16),
        grid=(n // tm,),
        in_specs=[pl.BlockSpec((tm, d), lambda i: (i, 0)), pl.BlockSpec((1, d), lambda i: (0, 0))],
        out_specs=pl.BlockSpec((tm, d), lambda i: (i, 0)),
        compiler_params=_params(("parallel",)),
        name="rms_norm",
    )(x, g)


CAST_ROWS = 256


def _cast_resident(w_ref, wb_ref):
    rows = min(CAST_ROWS, w_ref.shape[0])

    def body(c, carry):
        r = pl.ds(pl.multiple_of(c * rows, rows), rows)
        wb_ref[r, :] = w_ref[r, :].astype(BF16)
        return carry

    lax.fori_loop(0, w_ref.shape[0] // rows, body, 0)


def _resident(shape):
    return pl.BlockSpec(shape, lambda i: (0, 0), pipeline_mode=pl.Buffered(1))


ROW_TM = 256


IN_TM, IN_TN = 2048, 512
IN_RB = 256
SEG_TILES = 1024 // IN_TN


def _in_proj_kernel(h_ref, w_ref, gq_ref, gk_ref, caq_ref, sbq_ref, cak_ref, sbk_ref, o_ref):
    seg = pl.program_id(1) // SEG_TILES
    heads = IN_TN // LANES

    def rotary_layout(w):
        lane = lax.broadcasted_iota(jnp.int32, (w.shape[0], LANES), 1)
        quarter = DF_DQ // 2
        src = (lane // quarter % 2) * DF_DQ + (lane // DF_DQ) * quarter + lane % quarter
        return jnp.concatenate([jnp.take_along_axis(w[:, k * LANES:(k + 1) * LANES], src, axis=1)
                                for k in range(heads)], axis=1)

    def run(w, epilogue):
        wb = w.astype(BF16)
        for r in range(IN_TM // IN_RB):
            rows = slice(r * IN_RB, (r + 1) * IN_RB)
            epilogue(jnp.dot(h_ref[rows, :], wb, preferred_element_type=F32), rows)

    def na_norm(g_ref):
        def epilogue(y, rows):
            for k in range(heads):
                cols = slice(k * LANES, (k + 1) * LANES)
                o_ref[k, rows, :] = _rms_rows(y[:, cols], g_ref[...]).astype(BF16)
        return epilogue

    def df_norm_rope(ca_ref, sb_ref):
        def epilogue(y, rows):
            lane = lax.broadcasted_iota(jnp.int32, (1, LANES), 1)
            comp0 = (lane % 64) < 32
            ca = ca_ref[rows, :]
            sb = sb_ref[rows, :]
            for k in range(heads):
                cols = slice(k * LANES, (k + 1) * LANES)
                yk = y[:, cols]
                pk = pltpu.roll(yk, 64, 1)
                z = yk * yk + pk * pk
                s0 = jnp.sum(jnp.where(comp0, z, 0.0), axis=-1, keepdims=True)
                s1 = jnp.sum(jnp.where(comp0, 0.0, z), axis=-1, keepdims=True)
                r = jnp.where(comp0, lax.rsqrt(s0 * (0.5 / DF_DQ) + EPS), lax.rsqrt(s1 * (0.5 / DF_DQ) + EPS))
                o_ref[k, rows, :] = ((yk * ca + pk * sb) * r).astype(BF16)
        return epilogue

    def plain(y, rows):
        for k in range(heads):
            o_ref[k, rows, :] = y[:, k * LANES:(k + 1) * LANES].astype(BF16)

    def gate(y, rows):
        for k in range(heads):
            o_ref[k, rows, :] = jax.nn.sigmoid(y[:, k * LANES:(k + 1) * LANES]).astype(BF16)

    @pl.when(seg == 0)
    def _():
        run(w_ref[...], na_norm(gq_ref))

    @pl.when(seg == 1)
    def _():
        run(w_ref[...], na_norm(gk_ref))

    @pl.when(seg == 3)
    def _():
        run(rotary_layout(w_ref[...]), df_norm_rope(caq_ref, sbq_ref))

    @pl.when(seg == 4)
    def _():
        run(rotary_layout(w_ref[...]), df_norm_rope(cak_ref, sbk_ref))

    @pl.when((seg == 2) | (seg == 5))
    def _():
        run(w_ref[...], plain)

    @pl.when(seg >= 6)
    def _():
        run(w_ref[...], gate)


def _in_proj(h, w_in, g_na_q, g_na_k, caq, sbq, cak, sbk, seq):
    n = h.shape[0]
    pos_blocks = seq // IN_TM
    row = lambda i, j: (i, 0)
    const = lambda i, j: (0, 0)
    tab = lambda i, j: (i % pos_blocks, 0)
    return pl.pallas_call(
        _in_proj_kernel,
        out_shape=jax.ShapeDtypeStruct((IN_COLS // LANES, n, LANES), BF16),
        grid=(n // IN_TM, IN_COLS // IN_TN),
        in_specs=[
            pl.BlockSpec((IN_TM, D_MODEL), row),
            pl.BlockSpec((D_MODEL, IN_TN), lambda i, j: (0, j)),
            pl.BlockSpec((1, LANES), const),
            pl.BlockSpec((1, LANES), const),
            pl.BlockSpec((IN_TM, LANES), tab),
            pl.BlockSpec((IN_TM, LANES), tab),
            pl.BlockSpec((IN_TM, LANES), tab),
            pl.BlockSpec((IN_TM, LANES), tab),
        ],
        out_specs=pl.BlockSpec((IN_TN // LANES, IN_TM, LANES), lambda i, j: (j, i, 0)),
        compiler_params=_params(("parallel", "arbitrary")),
        name="in_proj",
    )(h, w_in, g_na_q, g_na_k, caq, sbq, cak, sbk)


NA_QROWS = 8
NA_WROWS = 16
NA_TQ = NA_QROWS * GRID_W
NA_TK = NA_WROWS * GRID_W
NA_PAIR = LANES // GRID_W
NA_NPAIR = NA_QROWS // NA_PAIR
NA_PROWS = NA_KR + NA_PAIR - 1
NA_PK = NA_PROWS * GRID_W
NA_NTAB = 1 + 2 * NA_NPAIR
NA_NEG_SLAB = 2 * NA_KR - 1
NA_PPAD = LANES


def _na_geometry(rows):
    nt = rows // NA_QROWS
    assert nt >= 3 and rows >= NA_WROWS
    ws = np.zeros((3, NA_NPAIR), np.int64)
    idx = np.full((NA_NTAB, NA_PROWS, NA_PAIR), NA_NEG_SLAB, np.int64)
    for ci, t in enumerate((0, 1, nt - 1)):
        w0 = int(np.clip(t * NA_QROWS - NA_KR // 2, 0, rows - NA_WROWS))
        for pi in range(NA_NPAIR):
            r = [t * NA_QROWS + NA_PAIR * pi + hb for hb in range(NA_PAIR)]
            rs = [int(np.clip(ri - NA_KR // 2, 0, rows - NA_KR)) for ri in r]
            start = min(min(rs) - w0, NA_WROWS - NA_PROWS)
            assert 0 <= start and max(rs) + NA_KR <= w0 + start + NA_PROWS and min(rs) >= w0 + start
            ws[ci, pi] = start
            tab = 0 if ci == 1 else 1 + (0 if ci == 0 else NA_NPAIR) + pi
            for wp in range(NA_PROWS):
                key_row = w0 + start + wp
                for hb in range(NA_PAIR):
                    if rs[hb] <= key_row < rs[hb] + NA_KR:
                        idx[tab, wp, hb] = key_row - r[hb] + NA_KR - 1
    combos = sorted({tuple(int(v) for v in pair) for pair in idx.reshape(-1, NA_PAIR)})
    slab = [[combos.index(tuple(int(v) for v in idx[tab, wp])) for wp in range(NA_PROWS)] for tab in range(NA_NTAB)]
    return [[int(v) for v in row] for row in ws], combos, slab


def _na_kernel(q_ref, k_ref, v_ref, row_ref, o_ref, p_ref, slab_ref, *, rows):
    nt = rows // NA_QROWS
    ws, combos, slab = _na_geometry(rows)

    kc = lax.broadcasted_iota(jnp.int32, (GRID_W, LANES), 0)
    lane = lax.broadcasted_iota(jnp.int32, (GRID_W, LANES), 1)
    c = lane % GRID_W
    cs = jnp.clip(c - NA_KW // 2, 0, GRID_W - NA_KW)
    in_window = (kc >= cs) & (kc < cs + NA_KW)
    src = jnp.clip(kc - c + NA_KW - 1, 0, 2 * NA_KW - 2) + jnp.where(lane < GRID_W, 0, GRID_W)
    for ci, (left, right) in enumerate(combos):
        ok = in_window
        if left == NA_NEG_SLAB:
            ok = ok & (lane >= GRID_W)
        if right == NA_NEG_SLAB:
            ok = ok & (lane < GRID_W)
        row = jnp.broadcast_to(row_ref[0, ci:ci + 1, :], (GRID_W, LANES))
        slab_ref[ci] = jnp.where(ok, jnp.take_along_axis(row, src, axis=1), NEG)

    def window(t):
        return int(np.clip(t * NA_QROWS - NA_KR // 2, 0, rows - NA_WROWS)) * GRID_W

    def probs(t):
        cls = 0 if t == 0 else (2 if t == nt - 1 else 1)
        buf = t
        p_ref[buf, :, 0:NA_TQ] = jnp.zeros((NA_TK, NA_TQ), BF16)
        for pi in range(NA_NPAIR):
            lanes = slice(pi * LANES, (pi + 1) * LANES)
            tab = 0 if cls == 1 else 1 + (0 if cls == 0 else NA_NPAIR) + pi
            r0 = ws[cls][pi] * GRID_W
            kp = k_ref[window(t) + r0:window(t) + r0 + NA_PK, :]
            qp = q_ref[t * NA_TQ + pi * LANES:t * NA_TQ + (pi + 1) * LANES, :]
            sp = lax.dot_general(kp, qp, (((1,), (1,)), ((), ())), preferred_element_type=F32)
            bias = jnp.concatenate([slab_ref[ci] for ci in slab[tab]], axis=0)
            sp = sp * (NA_DH ** -0.5) + bias
            m = jnp.max(sp, axis=0, keepdims=True)
            p_ref[buf, r0:r0 + NA_PK, lanes] = jnp.exp(sp - m).astype(BF16)

    def outputs(t):
        vw = _with_ones_rows(v_ref[window(t):window(t) + NA_TK, :].T)
        ov = jnp.dot(vw, p_ref[t, :, 0:NA_TQ], preferred_element_type=F32)
        o_ref[t * NA_TQ:(t + 1) * NA_TQ, :] = (ov[0:NA_DH, :] / ov[NA_DH:NA_DH + 1, :]).T.astype(BF16)

    for t in range(nt + 1):
        if t < nt:
            probs(t)
        if t >= 1:
            outputs(t - 1)


def _na_attn(proj, bias_rows, batch, seq):
    rows = seq // GRID_W
    n_slabs = bias_rows.shape[1]
    return pl.pallas_call(
        functools.partial(_na_kernel, rows=rows),
        out_shape=jax.ShapeDtypeStruct((NA_HEADS, batch * seq, LANES), BF16),
        grid=(batch, NA_HEADS),
        in_specs=[
            pl.BlockSpec((None, seq, LANES), lambda b, h: (CB_NA_Q + h, b, 0)),
            pl.BlockSpec((None, seq, LANES), lambda b, h: (CB_NA_K + h, b, 0)),
            pl.BlockSpec((None, seq, LANES), lambda b, h: (CB_NA_V + h, b, 0)),
            pl.BlockSpec((1, n_slabs, LANES), lambda b, h: (h, 0, 0)),
        ],
        out_specs=pl.BlockSpec((None, seq, LANES), lambda b, h: (h, b, 0)),
        scratch_shapes=[pltpu.VMEM((rows // NA_QROWS, NA_TK, NA_TQ + NA_PPAD), BF16),
                        pltpu.VMEM((n_slabs, GRID_W, LANES), F32)],
        compiler_params=_params(("parallel", "arbitrary")),
        name="na_attn",
    )(proj, proj, proj, bias_rows)


def _na_bias_rows(rpb, rows):
    _, combos, _ = _na_geometry(rows)
    padded = jnp.pad(rpb, ((0, 0), (0, 1), (0, GRID_W - (2 * NA_KW - 1))))
    left = jnp.take(padded, jnp.asarray([cb[0] for cb in combos], jnp.int32), axis=1)
    right = jnp.take(padded, jnp.asarray([cb[1] for cb in combos], jnp.int32), axis=1)
    return jnp.concatenate([left, right], axis=-1)


DF_TQ, DF_TK = 1024, 512
DF_TG = 256
DF_SBUF = 2
DF_VROWS = DF_DV + ONES_ROWS


def _df_kernel(q_ref, k_ref, v_ref, lam_ref, gsub_ref, o_ref, q12_ref, q12n_ref, s_ref, acc_ref, vt_ref, *, seq):
    n_chunks = seq // DF_TK
    n_blocks = seq // DF_TQ
    groups = 2 * DF_TQ // DF_TG

    for c in range(n_chunks):
        cols = slice(c * DF_TK, (c + 1) * DF_TK)
        vt_ref[:, cols] = _with_ones_rows(v_ref[cols, :].T)

    lane = lax.broadcasted_iota(jnp.int32, (1, LANES), 1)
    comp0 = (lane % 64) < 32

    def split_maps(block, dst_ref):
        q = q_ref[pl.ds(pl.multiple_of(block * DF_TQ, DF_TQ), DF_TQ), :]
        zero = jnp.zeros_like(q)
        dst_ref[0:DF_TQ, :] = jnp.where(comp0, q, zero)
        dst_ref[DF_TQ:2 * DF_TQ, :] = jnp.where(comp0, zero, q)

    def scores(kc, q_src_ref, g):
        cols = slice(g * DF_TG, (g + 1) * DF_TG)
        return lax.dot_general(kc, q_src_ref[cols, :], (((1,), (1,)), ((), ())), preferred_element_type=F32)

    def keys(c):
        return k_ref[c * DF_TK:(c + 1) * DF_TK, :]

    def step(c, carry):
        cur, nxt = c % DF_SBUF, (c + 1) % DF_SBUF
        vc = vt_ref[:, c * DF_TK:(c + 1) * DF_TK]
        kn, q_next = (keys(c + 1), q12_ref) if c + 1 < n_chunks else (keys(0), q12n_ref)
        out = []
        for g in range(groups):
            m_prev = carry[g]
            cols = slice(g * DF_TG, (g + 1) * DF_TG)
            s_ref[nxt, :, cols] = scores(kn, q_next, g)
            s = s_ref[cur, :, cols]
            m_new = jnp.maximum(m_prev, jnp.max(s, axis=0, keepdims=True))
            alpha = jnp.exp2(m_prev - m_new)
            p = jnp.exp2(s - m_new).astype(BF16)
            pv = jnp.dot(vc, p, preferred_element_type=F32)
            acc_ref[:, cols] = alpha * acc_ref[:, cols] + pv
            out.append(m_new)
        return tuple(out)

    lp = lam_ref[...]
    lam = (jnp.exp(jnp.sum(lp[0:1] * lp[1:2], axis=-1, keepdims=True))
           - jnp.exp(jnp.sum(lp[2:3] * lp[3:4], axis=-1, keepdims=True)) + LAM_INIT)

    def finish(block):
        o12 = acc_ref[0:DF_DV, :] / acc_ref[DF_DV:DF_DV + 1, :]
        o_t = o12[:, 0:DF_TQ] - lam * o12[:, DF_TQ:2 * DF_TQ]
        rows = pl.ds(pl.multiple_of(block * DF_TQ, DF_TQ), DF_TQ)
        o_ref[rows, :] = (_rms_rows(o_t.T, gsub_ref[...]) * (1.0 - LAM_INIT)).astype(BF16)

    assert n_chunks % DF_SBUF == 0
    split_maps(0, q12n_ref)
    k0 = keys(0)
    for g in range(groups):
        s_ref[0, :, g * DF_TG:(g + 1) * DF_TG] = scores(k0, q12n_ref, g)
    acc_ref[...] = jnp.zeros(acc_ref.shape, F32)
    acc_ref[DF_DV:DF_DV + 1, :] = jnp.ones((1, 2 * DF_TQ), F32)

    def body(block, carry):
        finish(jnp.maximum(block - 1, 0))
        acc_ref[...] = jnp.zeros(acc_ref.shape, F32)
        split_maps(block, q12_ref)
        split_maps(jnp.minimum(block + 1, n_blocks - 1), q12n_ref)
        m = tuple(jnp.full((1, DF_TG), -jnp.inf, F32) for _ in range(groups))
        for c in range(n_chunks):
            m = step(c, m)
        return carry

    lax.fori_loop(0, n_blocks, body, 0)
    finish(n_blocks - 1)


def _df_attn(proj, lam_p, g_sub, batch, seq):
    const = lambda b, h: (0, 0)
    return pl.pallas_call(
        functools.partial(_df_kernel, seq=seq),
        out_shape=jax.ShapeDtypeStruct((DF_HEADS, batch * seq, LANES), BF16),
        grid=(batch, DF_HEADS),
        in_specs=[
            pl.BlockSpec((None, seq, LANES), lambda b, h: (CB_DF_Q + h, b, 0)),
            pl.BlockSpec((None, seq, LANES), lambda b, h: (CB_DF_K + h, b, 0)),
            pl.BlockSpec((None, seq, LANES), lambda b, h: (CB_DF_V + h, b, 0)),
            pl.BlockSpec((4, DF_DQ), const),
            pl.BlockSpec((1, DF_DV), const),
        ],
        out_specs=pl.BlockSpec((None, seq, LANES), lambda b, h: (h, b, 0)),
        scratch_shapes=[pltpu.VMEM((2 * DF_TQ, LANES), BF16), pltpu.VMEM((2 * DF_TQ, LANES), BF16),
                        pltpu.VMEM((DF_SBUF, DF_TK, 2 * DF_TQ), F32),
                        pltpu.VMEM((DF_VROWS, 2 * DF_TQ), F32), pltpu.VMEM((DF_VROWS, seq), BF16)],
        compiler_params=_params(("parallel", "parallel")),
        name="df_attn",
    )(proj, proj, proj, lam_p, g_sub)


MG_TM = 512


def _merge_kernel(oa_ref, ob_ref, wa_ref, wb_ref, sa_ref, sb_ref, o_ref, wab_ref, wbb_ref):
    @pl.when(pl.program_id(0) == 0)
    def _():
        _cast_resident(wa_ref, wab_ref)
        _cast_resident(wb_ref, wbb_ref)

    def rows(slab_ref):
        return jnp.concatenate([slab_ref[k] for k in range(slab_ref.shape[0])], axis=1)

    ya = jnp.dot(rows(oa_ref), wab_ref[...], preferred_element_type=F32)
    yb = jnp.dot(rows(ob_ref), wbb_ref[...], preferred_element_type=F32)
    o_ref[...] = (rows(sa_ref).astype(F32) * ya + rows(sb_ref).astype(F32) * yb).astype(BF16)


def _merge(o_a, o_b, w_na_out, w_df_out, proj):
    n = o_a.shape[1]
    gate_slabs = D_MODEL // LANES
    ga, gb = COL_GATE_A // D_MODEL, COL_GATE_B // D_MODEL
    return pl.pallas_call(
        _merge_kernel,
        out_shape=jax.ShapeDtypeStruct((n, D_MODEL), BF16),
        grid=(n // MG_TM,),
        in_specs=[
            pl.BlockSpec((NA_HEADS, MG_TM, LANES), lambda i: (0, i, 0)),
            pl.BlockSpec((DF_HEADS, MG_TM, LANES), lambda i: (0, i, 0)),
            _resident((NA_W, D_MODEL)),
            _resident((DF_VW, D_MODEL)),
            pl.BlockSpec((gate_slabs, MG_TM, LANES), lambda i: (ga, i, 0)),
            pl.BlockSpec((gate_slabs, MG_TM, LANES), lambda i: (gb, i, 0)),
        ],
        out_specs=pl.BlockSpec((MG_TM, D_MODEL), lambda i: (i, 0)),
        scratch_shapes=[pltpu.VMEM((NA_W, D_MODEL), BF16), pltpu.VMEM((DF_VW, D_MODEL), BF16)],
        compiler_params=_params(("arbitrary",)),
        name="merge",
    )(o_a, o_b, w_na_out, w_df_out, proj, proj)


DN_TM, DN_TN = 1024, 256


def _ffn_down_kernel(a_ref, w_ref, x_ref, o_ref):
    o_ref[...] = x_ref[...] + jnp.dot(a_ref[...], w_ref[...].astype(BF16), preferred_element_type=F32)


def _ffn_down(a, w, x):
    n, k = a.shape
    d = w.shape[1]
    return pl.pallas_call(
        _ffn_down_kernel,
        out_shape=jax.ShapeDtypeStruct((n, d), F32),
        grid=(n // DN_TM, d // DN_TN),
        in_specs=[
            pl.BlockSpec((DN_TM, k), lambda i, j: (i, 0)),
            pl.BlockSpec((k, DN_TN), lambda i, j: (0, j)),
            pl.BlockSpec((DN_TM, DN_TN), lambda i, j: (i, j)),
        ],
        out_specs=pl.BlockSpec((DN_TM, DN_TN), lambda i, j: (i, j)),
        compiler_params=_params(("parallel", "arbitrary")),
        name="ffn_down",
    )(a, w, x)


def _out_proj_norm_kernel(a_ref, w_ref, x_ref, g_ref, o_ref, h_ref, wb_ref):
    @pl.when(pl.program_id(0) == 0)
    def _():
        _cast_resident(w_ref, wb_ref)

    y = x_ref[...] + jnp.dot(a_ref[...], wb_ref[...], preferred_element_type=F32)
    o_ref[...] = y
    h_ref[...] = _rms_rows(y, g_ref[...]).astype(BF16)


def _out_proj_norm(a, w, x, g):
    n, k = a.shape
    d = w.shape[1]
    return pl.pallas_call(
        _out_proj_norm_kernel,
        out_shape=(jax.ShapeDtypeStruct((n, d), F32), jax.ShapeDtypeStruct((n, d), BF16)),
        grid=(n // ROW_TM,),
        in_specs=[
            pl.BlockSpec((ROW_TM, k), lambda i: (i, 0)),
            _resident((k, d)),
            pl.BlockSpec((ROW_TM, d), lambda i: (i, 0)),
            pl.BlockSpec((1, d), lambda i: (0, 0)),
        ],
        out_specs=(pl.BlockSpec((ROW_TM, d), lambda i: (i, 0)), pl.BlockSpec((ROW_TM, d), lambda i: (i, 0))),
        scratch_shapes=[pltpu.VMEM((k, d), BF16)],
        compiler_params=_params(("arbitrary",)),
        name="out_proj",
    )(a, w, x, g)


UP_TM, UP_TN = 2048, 512
UP_RB = 512


def _ffn_up_kernel(h_ref, wg_ref, wu_ref, o_ref):
    wg = wg_ref[...].astype(BF16)
    wu = wu_ref[...].astype(BF16)
    for r in range(UP_TM // UP_RB):
        rows = slice(r * UP_RB, (r + 1) * UP_RB)
        h = h_ref[rows, :]
        gate = jnp.dot(h, wg, preferred_element_type=F32)
        up = jnp.dot(h, wu, preferred_element_type=F32)
        o_ref[rows, :] = (gate * jax.nn.sigmoid(gate) * up).astype(BF16)


def _ffn_up(h, w_gate, w_up):
    n = h.shape[0]
    return pl.pallas_call(
        _ffn_up_kernel,
        out_shape=jax.ShapeDtypeStruct((n, FFN_HID), BF16),
        grid=(n // UP_TM, FFN_HID // UP_TN),
        in_specs=[
            pl.BlockSpec((UP_TM, D_MODEL), lambda i, j: (i, 0)),
            pl.BlockSpec((D_MODEL, UP_TN), lambda i, j: (0, j)),
            pl.BlockSpec((D_MODEL, UP_TN), lambda i, j: (0, j)),
        ],
        out_specs=pl.BlockSpec((UP_TM, UP_TN), lambda i, j: (i, j)),
        compiler_params=_params(("parallel", "arbitrary")),
        name="ffn_up",
    )(h, w_gate, w_up)


def _ple_kernel(x_ref, g_ref, wg_ref, p_ref, wp_ref, o_ref, wgb_ref, wpb_ref):
    @pl.when(pl.program_id(0) == 0)
    def _():
        _cast_resident(wg_ref, wgb_ref)
        _cast_resident(wp_ref, wpb_ref)

    x = x_ref[...]
    h = _rms_rows(x, g_ref[...]).astype(BF16)
    gate = jnp.dot(h, wgb_ref[...], preferred_element_type=F32)
    emb = jnp.dot(p_ref[...].astype(BF16), wpb_ref[...], preferred_element_type=F32)
    o_ref[...] = x + jax.nn.sigmoid(gate) * emb


def _ple(x, g, w_gate, p, w_proj):
    n, d = x.shape
    return pl.pallas_call(
        _ple_kernel,
        out_shape=jax.ShapeDtypeStruct((n, d), F32),
        grid=(n // ROW_TM,),
        in_specs=[
            pl.BlockSpec((ROW_TM, d), lambda i: (i, 0)),
            pl.BlockSpec((1, d), lambda i: (0, 0)),
            _resident((d, d)),
            pl.BlockSpec((ROW_TM, PLE_DIM), lambda i: (i, 0)),
            _resident((PLE_DIM, d)),
        ],
        out_specs=pl.BlockSpec((ROW_TM, d), lambda i: (i, 0)),
        scratch_shapes=[pltpu.VMEM((d, d), BF16), pltpu.VMEM((PLE_DIM, d), BF16)],
        compiler_params=_params(("arbitrary",)),
        name="ple",
    )(x, g, w_gate, p, w_proj)


def _rope_tables(seq, g, scale):
    half = DF_DQ // 2
    inv = 1.0 / (ROPE_THETA ** (jnp.arange(0, DF_DQ, 2, dtype=F32) / DF_DQ))
    ang = jnp.arange(seq, dtype=F32)[:, None] * inv[None, :]
    cos, sin = jnp.cos(ang), jnp.sin(ang)
    g1, g2 = g[:half], g[half:]
    ca = jnp.concatenate([cos * g1, cos * g1, cos * g2, cos * g2], axis=1) * scale
    sb = jnp.concatenate([-sin * g2, -sin * g2, sin * g1, sin * g1], axis=1) * scale
    return ca, sb


def kernel(x, p, g_mix, w_in, g_na_q, g_na_k, na_rpb, g_df_q, g_df_k, lam_q1, lam_k1, lam_q2, lam_k2,
           g_df_sub, w_na_out, w_df_out, w_o, g_ffn, w_gate, w_up, w_down, g_ple, w_ple_gate, w_ple_proj):
    batch, seq, d = x.shape
    n = batch * seq
    rows = seq // GRID_W
    depth = w_in.shape[0]
    xf = x.reshape(n, d)
    for i in range(depth):
        caq, sbq = _rope_tables(seq, g_df_q[i], DF_DQ ** -0.5 * math.log2(math.e))
        cak, sbk = _rope_tables(seq, g_df_k[i], 1.0)
        proj = _in_proj(_rms_norm(xf, g_mix[i][None]), w_in[i], g_na_q[i][None], g_na_k[i][None],
                        caq, sbq, cak, sbk, seq)

        o_a = _na_attn(proj, _na_bias_rows(na_rpb[i], rows), batch, seq)
        lam_p = jnp.stack([lam_q1[i], lam_k1[i], lam_q2[i], lam_k2[i]], axis=0)
        o_b = _df_attn(proj, lam_p, g_df_sub[i][None], batch, seq)

        merged = _merge(o_a, o_b, w_na_out[i], w_df_out[i], proj)
        xf, hf = _out_proj_norm(merged, w_o[i], xf, g_ffn[i][None])

        act = _ffn_up(hf, w_gate[i], w_up[i])
        xf = _ffn_down(act, w_down[i], xf)

        xf = _ple(xf, g_ple[i][None], w_ple_gate[i], p[i].reshape(n, PLE_DIM), w_ple_proj[i])
    return xf.reshape(batch, seq, d)
```

```python
import functools
import math

import numpy as np
import jax
import jax.numpy as jnp
from jax import lax
from jax.experimental import pallas as pl
from jax.experimental.pallas import tpu as pltpu

D_MODEL = 2048
GRID_W = 64
NA_HEADS = 8
NA_DH = 128
NA_KR = 8
NA_KW = 16
DF_HEADS = 8
DF_DQ = 64
DF_DV = 128
FFN_HID = 5632
PLE_DIM = 256
ROPE_THETA = 10000.0
EPS = 1e-6
LAM_INIT = 0.8 - 0.6 * math.exp(-0.3 * 0)

NA_W = NA_HEADS * NA_DH
DF_QW = DF_HEADS * 2 * DF_DQ
DF_VW = DF_HEADS * DF_DV
IN_COLS = 3 * NA_W + 2 * DF_QW + DF_VW + 2 * D_MODEL

LANES = 128
NEG = -1e30
VMEM_LIMIT = 56 * 1024 * 1024

F32 = jnp.float32
BF16 = jnp.bfloat16

CB_NA_Q, CB_NA_K, CB_NA_V = 0, 8, 16
CB_DF_Q, CB_DF_K, CB_DF_V = 24, 32, 40
COL_GATE_A, COL_GATE_B = 6144, 8192


def _params(sem):
    return pltpu.CompilerParams(dimension_semantics=sem, vmem_limit_bytes=VMEM_LIMIT)


def _rms_rows(x, g):
    ms = jnp.mean(x * x, axis=-1, keepdims=True)
    return x * lax.rsqrt(ms + EPS) * g


ONES_ROWS = 16


def _with_ones_rows(v_t):
    row = lax.broadcasted_iota(jnp.int32, (ONES_ROWS, v_t.shape[1]), 0)
    return jnp.concatenate([v_t, jnp.where(row == 0, 1.0, 0.0).astype(v_t.dtype)], axis=0)


NORM_ROWS = 256


def _norm_block(x_ref, g_ref, h_ref):
    def body(c, carry):
        r = pl.ds(pl.multiple_of(c * NORM_ROWS, NORM_ROWS), NORM_ROWS)
        h_ref[r, :] = _rms_rows(x_ref[r, :], g_ref[...]).astype(BF16)
        return carry

    lax.fori_loop(0, x_ref.shape[0] // NORM_ROWS, body, 0)


def _rms_norm_kernel(x_ref, g_ref, o_ref):
    _norm_block(x_ref, g_ref, o_ref)


def _rms_norm(x, g, tm=512):
    n, d = x.shape
    return pl.pallas_call(
        _rms_norm_kernel,
        out_shape=jax.ShapeDtypeStruct((n, d), BF16),
        grid=(n // tm,),
        in_specs=[pl.BlockSpec((tm, d), lambda i: (i, 0)), pl.BlockSpec((1, d), lambda i: (0, 0))],
        out_specs=pl.BlockSpec((tm, d), lambda i: (i, 0)),
        compiler_params=_params(("parallel",)),
        name="rms_norm",
    )(x, g)


CAST_ROWS = 256


def _cast_resident(w_ref, wb_ref):
    rows = min(CAST_ROWS, w_ref.shape[0])

    def body(c, carry):
        r = pl.ds(pl.multiple_of(c * rows, rows), rows)
        wb_ref[r, :] = w_ref[r, :].astype(BF16)
        return carry

    lax.fori_loop(0, w_ref.shape[0] // rows, body, 0)


def _resident(shape):
    return pl.BlockSpec(shape, lambda i: (0, 0), pipeline_mode=pl.Buffered(1))


ROW_TM = 512


IN_TM, IN_TN = 2048, 512
IN_RB = 256
SEG_TILES = 1024 // IN_TN


def _in_proj_kernel(h_ref, w_ref, gq_ref, gk_ref, caq_ref, sbq_ref, cak_ref, sbk_ref, o_ref):
    seg = pl.program_id(1) // SEG_TILES
    heads = IN_TN // LANES

    def rotary_layout(w):
        lane = lax.broadcasted_iota(jnp.int32, (w.shape[0], LANES), 1)
        quarter = DF_DQ // 2
        src = (lane // quarter % 2) * DF_DQ + (lane // DF_DQ) * quarter + lane % quarter
        return jnp.concatenate([jnp.take_along_axis(w[:, k * LANES:(k + 1) * LANES], src, axis=1)
                                for k in range(heads)], axis=1)

    def run(w, epilogue):
        wb = w.astype(BF16)
        for r in range(IN_TM // IN_RB):
            rows = slice(r * IN_RB, (r + 1) * IN_RB)
            epilogue(jnp.dot(h_ref[rows, :], wb, preferred_element_type=F32), rows)

    def na_norm(g_ref):
        def epilogue(y, rows):
            for k in range(heads):
                cols = slice(k * LANES, (k + 1) * LANES)
                o_ref[k, rows, :] = _rms_rows(y[:, cols], g_ref[...]).astype(BF16)
        return epilogue

    def df_norm_rope(ca_ref, sb_ref):
        def epilogue(y, rows):
            lane = lax.broadcasted_iota(jnp.int32, (1, LANES), 1)
            comp0 = (lane % 64) < 32
            ca = ca_ref[rows, :]
            sb = sb_ref[rows, :]
            for k in range(heads):
                cols = slice(k * LANES, (k + 1) * LANES)
                yk = y[:, cols]
                pk = pltpu.roll(yk, 64, 1)
                z = yk * yk + pk * pk
                s0 = jnp.sum(jnp.where(comp0, z, 0.0), axis=-1, keepdims=True)
                s1 = jnp.sum(jnp.where(comp0, 0.0, z), axis=-1, keepdims=True)
                r = jnp.where(comp0, lax.rsqrt(s0 * (0.5 / DF_DQ) + EPS), lax.rsqrt(s1 * (0.5 / DF_DQ) + EPS))
                o_ref[k, rows, :] = ((yk * ca + pk * sb) * r).astype(BF16)
        return epilogue

    def plain(y, rows):
        for k in range(heads):
            o_ref[k, rows, :] = y[:, k * LANES:(k + 1) * LANES].astype(BF16)

    def gate(y, rows):
        for k in range(heads):
            o_ref[k, rows, :] = jax.nn.sigmoid(y[:, k * LANES:(k + 1) * LANES]).astype(BF16)

    @pl.when(seg == 0)
    def _():
        run(w_ref[...], na_norm(gq_ref))

    @pl.when(seg == 1)
    def _():
        run(w_ref[...], na_norm(gk_ref))

    @pl.when(seg == 3)
    def _():
        run(rotary_layout(w_ref[...]), df_norm_rope(caq_ref, sbq_ref))

    @pl.when(seg == 4)
    def _():
        run(rotary_layout(w_ref[...]), df_norm_rope(cak_ref, sbk_ref))

    @pl.when((seg == 2) | (seg == 5))
    def _():
        run(w_ref[...], plain)

    @pl.when(seg >= 6)
    def _():
        run(w_ref[...], gate)


def _in_proj(h, w_in, g_na_q, g_na_k, caq, sbq, cak, sbk, seq):
    n = h.shape[0]
    pos_blocks = seq // IN_TM
    row = lambda i, j: (i, 0)
    const = lambda i, j: (0, 0)
    tab = lambda i, j: (i % pos_blocks, 0)
    return pl.pallas_call(
        _in_proj_kernel,
        out_shape=jax.ShapeDtypeStruct((IN_COLS // LANES, n, LANES), BF16),
        grid=(n // IN_TM, IN_COLS // IN_TN),
        in_specs=[
            pl.BlockSpec((IN_TM, D_MODEL), row),
            pl.BlockSpec((D_MODEL, IN_TN), lambda i, j: (0, j)),
            pl.BlockSpec((1, LANES), const),
            pl.BlockSpec((1, LANES), const),
            pl.BlockSpec((IN_TM, LANES), tab),
            pl.BlockSpec((IN_TM, LANES), tab),
            pl.BlockSpec((IN_TM, LANES), tab),
            pl.BlockSpec((IN_TM, LANES), tab),
        ],
        out_specs=pl.BlockSpec((IN_TN // LANES, IN_TM, LANES), lambda i, j: (j, i, 0)),
        compiler_params=_params(("parallel", "arbitrary")),
        name="in_proj",
    )(h, w_in, g_na_q, g_na_k, caq, sbq, cak, sbk)


NA_QROWS = 8
NA_WROWS = 16
NA_TQ = NA_QROWS * GRID_W
NA_TK = NA_WROWS * GRID_W
NA_PAIR = LANES // GRID_W
NA_NPAIR = NA_QROWS // NA_PAIR
NA_PROWS = NA_KR + NA_PAIR - 1
NA_PK = NA_PROWS * GRID_W
NA_NTAB = 1 + 2 * NA_NPAIR
NA_NEG_SLAB = 2 * NA_KR - 1


def _na_geometry(rows):
    nt = rows // NA_QROWS
    assert nt >= 3 and rows >= NA_WROWS
    ws = np.zeros((3, NA_NPAIR), np.int64)
    idx = np.full((NA_NTAB, NA_PROWS, NA_PAIR), NA_NEG_SLAB, np.int64)
    for ci, t in enumerate((0, 1, nt - 1)):
        w0 = int(np.clip(t * NA_QROWS - NA_KR // 2, 0, rows - NA_WROWS))
        for pi in range(NA_NPAIR):
            r = [t * NA_QROWS + NA_PAIR * pi + hb for hb in range(NA_PAIR)]
            rs = [int(np.clip(ri - NA_KR // 2, 0, rows - NA_KR)) for ri in r]
            start = min(min(rs) - w0, NA_WROWS - NA_PROWS)
            assert 0 <= start and max(rs) + NA_KR <= w0 + start + NA_PROWS and min(rs) >= w0 + start
            ws[ci, pi] = start
            tab = 0 if ci == 1 else 1 + (0 if ci == 0 else NA_NPAIR) + pi
            for wp in range(NA_PROWS):
                key_row = w0 + start + wp
                for hb in range(NA_PAIR):
                    if rs[hb] <= key_row < rs[hb] + NA_KR:
                        idx[tab, wp, hb] = key_row - r[hb] + NA_KR - 1
    combos = sorted({tuple(int(v) for v in pair) for pair in idx.reshape(-1, NA_PAIR)})
    slab = [[combos.index(tuple(int(v) for v in idx[tab, wp])) for wp in range(NA_PROWS)] for tab in range(NA_NTAB)]
    return [[int(v) for v in row] for row in ws], combos, slab


def _na_kernel(q_ref, k_ref, v_ref, row_ref, o_ref, p_ref, slab_ref, *, rows):
    nt = rows // NA_QROWS
    ws, combos, slab = _na_geometry(rows)

    kc = lax.broadcasted_iota(jnp.int32, (GRID_W, LANES), 0)
    lane = lax.broadcasted_iota(jnp.int32, (GRID_W, LANES), 1)
    c = lane % GRID_W
    cs = jnp.clip(c - NA_KW // 2, 0, GRID_W - NA_KW)
    in_window = (kc >= cs) & (kc < cs + NA_KW)
    src = jnp.clip(kc - c + NA_KW - 1, 0, 2 * NA_KW - 2) + jnp.where(lane < GRID_W, 0, GRID_W)
    for ci, (left, right) in enumerate(combos):
        ok = in_window
        if left == NA_NEG_SLAB:
            ok = ok & (lane >= GRID_W)
        if right == NA_NEG_SLAB:
            ok = ok & (lane < GRID_W)
        row = jnp.broadcast_to(row_ref[0, ci:ci + 1, :], (GRID_W, LANES))
        slab_ref[ci] = jnp.where(ok, jnp.take_along_axis(row, src, axis=1), NEG)

    def window(t):
        return int(np.clip(t * NA_QROWS - NA_KR // 2, 0, rows - NA_WROWS)) * GRID_W

    def probs(t):
        cls = 0 if t == 0 else (2 if t == nt - 1 else 1)
        buf = t
        p_ref[buf] = jnp.zeros(p_ref.shape[1:], BF16)
        for pi in range(NA_NPAIR):
            lanes = slice(pi * LANES, (pi + 1) * LANES)
            tab = 0 if cls == 1 else 1 + (0 if cls == 0 else NA_NPAIR) + pi
            r0 = ws[cls][pi] * GRID_W
            kp = k_ref[window(t) + r0:window(t) + r0 + NA_PK, :]
            qp = q_ref[t * NA_TQ + pi * LANES:t * NA_TQ + (pi + 1) * LANES, :]
            sp = lax.dot_general(kp, qp, (((1,), (1,)), ((), ())), preferred_element_type=F32)
            bias = jnp.concatenate([slab_ref[ci] for ci in slab[tab]], axis=0)
            sp = sp * (NA_DH ** -0.5) + bias
            m = jnp.max(sp, axis=0, keepdims=True)
            p_ref[buf, r0:r0 + NA_PK, lanes] = jnp.exp(sp - m).astype(BF16)

    def outputs(t):
        vw = _with_ones_rows(v_ref[window(t):window(t) + NA_TK, :].T)
        ov = jnp.dot(vw, p_ref[t], preferred_element_type=F32)
        o_ref[t * NA_TQ:(t + 1) * NA_TQ, :] = (ov[0:NA_DH, :] / ov[NA_DH:NA_DH + 1, :]).T.astype(BF16)

    for t in range(nt + 1):
        if t < nt:
            probs(t)
        if t >= 1:
            outputs(t - 1)


def _na_attn(proj, bias_rows, batch, seq):
    rows = seq // GRID_W
    n_slabs = bias_rows.shape[1]
    return pl.pallas_call(
        functools.partial(_na_kernel, rows=rows),
        out_shape=jax.ShapeDtypeStruct((NA_HEADS, batch * seq, LANES), BF16),
        grid=(batch, NA_HEADS),
        in_specs=[
            pl.BlockSpec((None, seq, LANES), lambda b, h: (CB_NA_Q + h, b, 0)),
            pl.BlockSpec((None, seq, LANES), lambda b, h: (CB_NA_K + h, b, 0)),
            pl.BlockSpec((None, seq, LANES), lambda b, h: (CB_NA_V + h, b, 0)),
            pl.BlockSpec((1, n_slabs, LANES), lambda b, h: (h, 0, 0)),
        ],
        out_specs=pl.BlockSpec((None, seq, LANES), lambda b, h: (h, b, 0)),
        scratch_shapes=[pltpu.VMEM((rows // NA_QROWS, NA_TK, NA_TQ), BF16),
                        pltpu.VMEM((n_slabs, GRID_W, LANES), F32)],
        compiler_params=_params(("parallel", "arbitrary")),
        name="na_attn",
    )(proj, proj, proj, bias_rows)


def _na_bias_rows(rpb, rows):
    _, combos, _ = _na_geometry(rows)
    padded = jnp.pad(rpb, ((0, 0), (0, 1), (0, GRID_W - (2 * NA_KW - 1))))
    left = jnp.take(padded, jnp.asarray([cb[0] for cb in combos], jnp.int32), axis=1)
    right = jnp.take(padded, jnp.asarray([cb[1] for cb in combos], jnp.int32), axis=1)
    return jnp.concatenate([left, right], axis=-1)


DF_TQ, DF_TK = 1024, 512
DF_TG = 256
DF_SBUF = 2
DF_VROWS = DF_DV + ONES_ROWS


def _df_kernel(q_ref, k_ref, v_ref, lam_ref, gsub_ref, o_ref, q12_ref, q12n_ref, s_ref, acc_ref, vt_ref, *, seq):
    n_chunks = seq // DF_TK
    n_blocks = seq // DF_TQ
    groups = 2 * DF_TQ // DF_TG

    for c in range(n_chunks):
        cols = slice(c * DF_TK, (c + 1) * DF_TK)
        vt_ref[:, cols] = _with_ones_rows(v_ref[cols, :].T)

    lane = lax.broadcasted_iota(jnp.int32, (1, LANES), 1)
    comp0 = (lane % 64) < 32

    def split_maps(block, dst_ref):
        q = q_ref[pl.ds(pl.multiple_of(block * DF_TQ, DF_TQ), DF_TQ), :]
        zero = jnp.zeros_like(q)
        dst_ref[0:DF_TQ, :] = jnp.where(comp0, q, zero)
        dst_ref[DF_TQ:2 * DF_TQ, :] = jnp.where(comp0, zero, q)

    def scores(kc, q_src_ref, g):
        cols = slice(g * DF_TG, (g + 1) * DF_TG)
        return lax.dot_general(kc, q_src_ref[cols, :], (((1,), (1,)), ((), ())), preferred_element_type=F32)

    def keys(c):
        return k_ref[c * DF_TK:(c + 1) * DF_TK, :]

    def step(c, carry):
        cur, nxt = c % DF_SBUF, (c + 1) % DF_SBUF
        vc = vt_ref[:, c * DF_TK:(c + 1) * DF_TK]
        kn, q_next = (keys(c + 1), q12_ref) if c + 1 < n_chunks else (keys(0), q12n_ref)
        out = []
        for g in range(groups):
            m_prev = carry[g]
            cols = slice(g * DF_TG, (g + 1) * DF_TG)
            s_ref[nxt, :, cols] = scores(kn, q_next, g)
            s = s_ref[cur, :, cols]
            m_new = jnp.maximum(m_prev, jnp.max(s, axis=0, keepdims=True))
            alpha = jnp.exp2(m_prev - m_new)
            p = jnp.exp2(s - m_new).astype(BF16)
            pv = jnp.dot(vc, p, preferred_element_type=F32)
            acc_ref[:, cols] = alpha * acc_ref[:, cols] + pv
            out.append(m_new)
        return tuple(out)

    lp = lam_ref[...]
    lam = (jnp.exp(jnp.sum(lp[0:1] * lp[1:2], axis=-1, keepdims=True))
           - jnp.exp(jnp.sum(lp[2:3] * lp[3:4], axis=-1, keepdims=True)) + LAM_INIT)

    def finish(block):
        o12 = acc_ref[0:DF_DV, :] / acc_ref[DF_DV:DF_DV + 1, :]
        o_t = o12[:, 0:DF_TQ] - lam * o12[:, DF_TQ:2 * DF_TQ]
        rows = pl.ds(pl.multiple_of(block * DF_TQ, DF_TQ), DF_TQ)
        o_ref[rows, :] = (_rms_rows(o_t.T, gsub_ref[...]) * (1.0 - LAM_INIT)).astype(BF16)

    assert n_chunks % DF_SBUF == 0
    split_maps(0, q12n_ref)
    k0 = keys(0)
    for g in range(groups):
        s_ref[0, :, g * DF_TG:(g + 1) * DF_TG] = scores(k0, q12n_ref, g)
    acc_ref[...] = jnp.zeros(acc_ref.shape, F32)
    acc_ref[DF_DV:DF_DV + 1, :] = jnp.ones((1, 2 * DF_TQ), F32)

    def body(block, carry):
        finish(jnp.maximum(block - 1, 0))
        acc_ref[...] = jnp.zeros(acc_ref.shape, F32)
        split_maps(block, q12_ref)
        split_maps(jnp.minimum(block + 1, n_blocks - 1), q12n_ref)
        m = tuple(jnp.full((1, DF_TG), -jnp.inf, F32) for _ in range(groups))
        for c in range(n_chunks):
            m = step(c, m)
        return carry

    lax.fori_loop(0, n_blocks, body, 0)
    finish(n_blocks - 1)


def _df_attn(proj, lam_p, g_sub, batch, seq):
    const = lambda b, h: (0, 0)
    return pl.pallas_call(
        functools.partial(_df_kernel, seq=seq),
        out_shape=jax.ShapeDtypeStruct((DF_HEADS, batch * seq, LANES), BF16),
        grid=(batch, DF_HEADS),
        in_specs=[
            pl.BlockSpec((None, seq, LANES), lambda b, h: (CB_DF_Q + h, b, 0)),
            pl.BlockSpec((None, seq, LANES), lambda b, h: (CB_DF_K + h, b, 0)),
            pl.BlockSpec((None, seq, LANES), lambda b, h: (CB_DF_V + h, b, 0)),
            pl.BlockSpec((4, DF_DQ), const),
            pl.BlockSpec((1, DF_DV), const),
        ],
        out_specs=pl.BlockSpec((None, seq, LANES), lambda b, h: (h, b, 0)),
        scratch_shapes=[pltpu.VMEM((2 * DF_TQ, LANES), BF16), pltpu.VMEM((2 * DF_TQ, LANES), BF16),
                        pltpu.VMEM((DF_SBUF, DF_TK, 2 * DF_TQ), F32),
                        pltpu.VMEM((DF_VROWS, 2 * DF_TQ), F32), pltpu.VMEM((DF_VROWS, seq), BF16)],
        compiler_params=_params(("parallel", "parallel")),
        name="df_attn",
    )(proj, proj, proj, lam_p, g_sub)


def _merge_kernel(oa_ref, ob_ref, wa_ref, wb_ref, sa_ref, sb_ref, o_ref, wab_ref, wbb_ref):
    @pl.when(pl.program_id(0) == 0)
    def _():
        _cast_resident(wa_ref, wab_ref)
        _cast_resident(wb_ref, wbb_ref)

    def rows(slab_ref):
        return jnp.concatenate([slab_ref[k] for k in range(slab_ref.shape[0])], axis=1)

    ya = jnp.dot(rows(oa_ref), wab_ref[...], preferred_element_type=F32)
    yb = jnp.dot(rows(ob_ref), wbb_ref[...], preferred_element_type=F32)
    o_ref[...] = (rows(sa_ref).astype(F32) * ya + rows(sb_ref).astype(F32) * yb).astype(BF16)


def _merge(o_a, o_b, w_na_out, w_df_out, proj):
    n = o_a.shape[1]
    gate_slabs = D_MODEL // LANES
    ga, gb = COL_GATE_A // D_MODEL, COL_GATE_B // D_MODEL
    return pl.pallas_call(
        _merge_kernel,
        out_shape=jax.ShapeDtypeStruct((n, D_MODEL), BF16),
        grid=(n // ROW_TM,),
        in_specs=[
            pl.BlockSpec((NA_HEADS, ROW_TM, LANES), lambda i: (0, i, 0)),
            pl.BlockSpec((DF_HEADS, ROW_TM, LANES), lambda i: (0, i, 0)),
            _resident((NA_W, D_MODEL)),
            _resident((DF_VW, D_MODEL)),
            pl.BlockSpec((gate_slabs, ROW_TM, LANES), lambda i: (ga, i, 0)),
            pl.BlockSpec((gate_slabs, ROW_TM, LANES), lambda i: (gb, i, 0)),
        ],
        out_specs=pl.BlockSpec((ROW_TM, D_MODEL), lambda i: (i, 0)),
        scratch_shapes=[pltpu.VMEM((NA_W, D_MODEL), BF16), pltpu.VMEM((DF_VW, D_MODEL), BF16)],
        compiler_params=_params(("arbitrary",)),
        name="merge",
    )(o_a, o_b, w_na_out, w_df_out, proj, proj)


DN_TM, DN_TN = 1024, 256


def _ffn_down_kernel(a_ref, w_ref, x_ref, o_ref):
    o_ref[...] = x_ref[...] + jnp.dot(a_ref[...], w_ref[...].astype(BF16), preferred_element_type=F32)


def _ffn_down(a, w, x):
    n, k = a.shape
    d = w.shape[1]
    return pl.pallas_call(
        _ffn_down_kernel,
        out_shape=jax.ShapeDtypeStruct((n, d), F32),
        grid=(n // DN_TM, d // DN_TN),
        in_specs=[
            pl.BlockSpec((DN_TM, k), lambda i, j: (i, 0)),
            pl.BlockSpec((k, DN_TN), lambda i, j: (0, j)),
            pl.BlockSpec((DN_TM, DN_TN), lambda i, j: (i, j)),
        ],
        out_specs=pl.BlockSpec((DN_TM, DN_TN), lambda i, j: (i, j)),
        compiler_params=_params(("parallel", "arbitrary")),
        name="ffn_down",
    )(a, w, x)


def _out_proj_norm_kernel(a_ref, w_ref, x_ref, g_ref, o_ref, h_ref, wb_ref):
    @pl.when(pl.program_id(0) == 0)
    def _():
        _cast_resident(w_ref, wb_ref)

    y = x_ref[...] + jnp.dot(a_ref[...], wb_ref[...], preferred_element_type=F32)
    o_ref[...] = y
    h_ref[...] = _rms_rows(y, g_ref[...]).astype(BF16)


def _out_proj_norm(a, w, x, g):
    n, k = a.shape
    d = w.shape[1]
    return pl.pallas_call(
        _out_proj_norm_kernel,
        out_shape=(jax.ShapeDtypeStruct((n, d), F32), jax.ShapeDtypeStruct((n, d), BF16)),
        grid=(n // ROW_TM,),
        in_specs=[
            pl.BlockSpec((ROW_TM, k), lambda i: (i, 0)),
            _resident((k, d)),
            pl.BlockSpec((ROW_TM, d), lambda i: (i, 0)),
            pl.BlockSpec((1, d), lambda i: (0, 0)),
        ],
        out_specs=(pl.BlockSpec((ROW_TM, d), lambda i: (i, 0)), pl.BlockSpec((ROW_TM, d), lambda i: (i, 0))),
        scratch_shapes=[pltpu.VMEM((k, d), BF16)],
        compiler_params=_params(("arbitrary",)),
        name="out_proj",
    )(a, w, x, g)


UP_TM, UP_TN = 2048, 512
UP_RB = 512


def _ffn_up_kernel(h_ref, wg_ref, wu_ref, o_ref):
    wg = wg_ref[...].astype(BF16)
    wu = wu_ref[...].astype(BF16)
    for r in range(UP_TM // UP_RB):
        rows = slice(r * UP_RB, (r + 1) * UP_RB)
        h = h_ref[rows, :]
        gate = jnp.dot(h, wg, preferred_element_type=F32)
        up = jnp.dot(h, wu, preferred_element_type=F32)
        o_ref[rows, :] = (gate * jax.nn.sigmoid(gate) * up).astype(BF16)


def _ffn_up(h, w_gate, w_up):
    n = h.shape[0]
    return pl.pallas_call(
        _ffn_up_kernel,
        out_shape=jax.ShapeDtypeStruct((n, FFN_HID), BF16),
        grid=(n // UP_TM, FFN_HID // UP_TN),
        in_specs=[
            pl.BlockSpec((UP_TM, D_MODEL), lambda i, j: (i, 0)),
            pl.BlockSpec((D_MODEL, UP_TN), lambda i, j: (0, j)),
            pl.BlockSpec((D_MODEL, UP_TN), lambda i, j: (0, j)),
        ],
        out_specs=pl.BlockSpec((UP_TM, UP_TN), lambda i, j: (i, j)),
        compiler_params=_params(("parallel", "arbitrary")),
        name="ffn_up",
    )(h, w_gate, w_up)


def _ple_kernel(x_ref, g_ref, wg_ref, p_ref, wp_ref, o_ref, wgb_ref, wpb_ref):
    @pl.when(pl.program_id(0) == 0)
    def _():
        _cast_resident(wg_ref, wgb_ref)
        _cast_resident(wp_ref, wpb_ref)

    x = x_ref[...]
    h = _rms_rows(x, g_ref[...]).astype(BF16)
    gate = jnp.dot(h, wgb_ref[...], preferred_element_type=F32)
    emb = jnp.dot(p_ref[...].astype(BF16), wpb_ref[...], preferred_element_type=F32)
    o_ref[...] = x + jax.nn.sigmoid(gate) * emb


def _ple(x, g, w_gate, p, w_proj):
    n, d = x.shape
    return pl.pallas_call(
        _ple_kernel,
        out_shape=jax.ShapeDtypeStruct((n, d), F32),
        grid=(n // ROW_TM,),
        in_specs=[
            pl.BlockSpec((ROW_TM, d), lambda i: (i, 0)),
            pl.BlockSpec((1, d), lambda i: (0, 0)),
            _resident((d, d)),
            pl.BlockSpec((ROW_TM, PLE_DIM), lambda i: (i, 0)),
            _resident((PLE_DIM, d)),
        ],
        out_specs=pl.BlockSpec((ROW_TM, d), lambda i: (i, 0)),
        scratch_shapes=[pltpu.VMEM((d, d), BF16), pltpu.VMEM((PLE_DIM, d), BF16)],
        compiler_params=_params(("arbitrary",)),
        name="ple",
    )(x, g, w_gate, p, w_proj)


def _rope_tables(seq, g, scale):
    half = DF_DQ // 2
    inv = 1.0 / (ROPE_THETA ** (jnp.arange(0, DF_DQ, 2, dtype=F32) / DF_DQ))
    ang = jnp.arange(seq, dtype=F32)[:, None] * inv[None, :]
    cos, sin = jnp.cos(ang), jnp.sin(ang)
    g1, g2 = g[:half], g[half:]
    ca = jnp.concatenate([cos * g1, cos * g1, cos * g2, cos * g2], axis=1) * scale
    sb = jnp.concatenate([-sin * g2, -sin * g2, sin * g1, sin * g1], axis=1) * scale
    return ca, sb


def kernel(x, p, g_mix, w_in, g_na_q, g_na_k, na_rpb, g_df_q, g_df_k, lam_q1, lam_k1, lam_q2, lam_k2,
           g_df_sub, w_na_out, w_df_out, w_o, g_ffn, w_gate, w_up, w_down, g_ple, w_ple_gate, w_ple_proj):
    batch, seq, d = x.shape
    n = batch * seq
    rows = seq // GRID_W
    depth = w_in.shape[0]
    xf = x.reshape(n, d)
    for i in range(depth):
        caq, sbq = _rope_tables(seq, g_df_q[i], DF_DQ ** -0.5 * math.log2(math.e))
        cak, sbk = _rope_tables(seq, g_df_k[i], 1.0)
        proj = _in_proj(_rms_norm(xf, g_mix[i][None]), w_in[i], g_na_q[i][None], g_na_k[i][None],
                        caq, sbq, cak, sbk, seq)

        o_a = _na_attn(proj, _na_bias_rows(na_rpb[i], rows), batch, seq)
        lam_p = jnp.stack([lam_q1[i], lam_k1[i], lam_q2[i], lam_k2[i]], axis=0)
        o_b = _df_attn(proj, lam_p, g_df_sub[i][None], batch, seq)

        merged = _merge(o_a, o_b, w_na_out[i], w_df_out[i], proj)
        xf, hf = _out_proj_norm(merged, w_o[i], xf, g_ffn[i][None])

        act = _ffn_up(hf, w_gate[i], w_up[i])
        xf = _ffn_down(act, w_down[i], xf)

        xf = _ple(xf, g_ple[i][None], w_ple_gate[i], p[i].reshape(n, PLE_DIM), w_ple_proj[i])
    return xf.reshape(batch, seq, d)
```

```python
import functools
import math

import numpy as np
import jax
import jax.numpy as jnp
from jax import lax
from jax.experimental import pallas as pl
from jax.experimental.pallas import tpu as pltpu

D_MODEL = 2048
GRID_W = 64
NA_HEADS = 8
NA_DH = 128
NA_KR = 8
NA_KW = 16
DF_HEADS = 8
DF_DQ = 64
DF_DV = 128
FFN_HID = 5632
PLE_DIM = 256
ROPE_THETA = 10000.0
EPS = 1e-6
LAM_INIT = 0.8 - 0.6 * math.exp(-0.3 * 0)

NA_W = NA_HEADS * NA_DH
DF_QW = DF_HEADS * 2 * DF_DQ
DF_VW = DF_HEADS * DF_DV
IN_COLS = 3 * NA_W + 2 * DF_QW + DF_VW + 2 * D_MODEL

LANES = 128
NEG = -1e30
VMEM_LIMIT = 56 * 1024 * 1024

F32 = jnp.float32
BF16 = jnp.bfloat16

CB_NA_Q, CB_NA_K, CB_NA_V = 0, 8, 16
CB_DF_Q, CB_DF_K, CB_DF_V = 24, 32, 40
COL_GATE_A, COL_GATE_B = 6144, 8192


def _params(sem):
    return pltpu.CompilerParams(dimension_semantics=sem, vmem_limit_bytes=VMEM_LIMIT)


def _rms_rows(x, g):
    ms = jnp.mean(x * x, axis=-1, keepdims=True)
    return x * lax.rsqrt(ms + EPS) * g


ONES_ROWS = 16


def _with_ones_rows(v_t):
    row = lax.broadcasted_iota(jnp.int32, (ONES_ROWS, v_t.shape[1]), 0)
    return jnp.concatenate([v_t, jnp.where(row == 0, 1.0, 0.0).astype(v_t.dtype)], axis=0)


NORM_ROWS = 256


def _norm_block(x_ref, g_ref, h_ref):
    def body(c, carry):
        r = pl.ds(pl.multiple_of(c * NORM_ROWS, NORM_ROWS), NORM_ROWS)
        h_ref[r, :] = _rms_rows(x_ref[r, :], g_ref[...]).astype(BF16)
        return carry

    lax.fori_loop(0, x_ref.shape[0] // NORM_ROWS, body, 0)


def _rms_norm_kernel(x_ref, g_ref, o_ref):
    _norm_block(x_ref, g_ref, o_ref)


def _rms_norm(x, g, tm=512):
    n, d = x.shape
    return pl.pallas_call(
        _rms_norm_kernel,
        out_shape=jax.ShapeDtypeStruct((n, d), BF16),
        grid=(n // tm,),
        in_specs=[pl.BlockSpec((tm, d), lambda i: (i, 0)), pl.BlockSpec((1, d), lambda i: (0, 0))],
        out_specs=pl.BlockSpec((tm, d), lambda i: (i, 0)),
        compiler_params=_params(("parallel",)),
        name="rms_norm",
    )(x, g)


CAST_ROWS = 256


def _cast_resident(w_ref, wb_ref):
    rows = min(CAST_ROWS, w_ref.shape[0])

    def body(c, carry):
        r = pl.ds(pl.multiple_of(c * rows, rows), rows)
        wb_ref[r, :] = w_ref[r, :].astype(BF16)
        return carry

    lax.fori_loop(0, w_ref.shape[0] // rows, body, 0)


def _resident(shape):
    return pl.BlockSpec(shape, lambda i: (0, 0), pipeline_mode=pl.Buffered(1))


ROW_TM = 512


IN_TM, IN_TN = 2048, 512
IN_RB = 256
SEG_TILES = 1024 // IN_TN


def _in_proj_kernel(h_ref, w_ref, gq_ref, gk_ref, cos_ref, sin_ref, rq_ref, rk_ref, o_ref):
    seg = pl.program_id(1) // SEG_TILES
    heads = IN_TN // LANES

    def rotary_layout(w):
        lane = lax.broadcasted_iota(jnp.int32, (w.shape[0], LANES), 1)
        quarter = DF_DQ // 2
        src = (lane // quarter % 2) * DF_DQ + (lane // DF_DQ) * quarter + lane % quarter
        return jnp.concatenate([jnp.take_along_axis(w[:, k * LANES:(k + 1) * LANES], src, axis=1)
                                for k in range(heads)], axis=1)

    def run(w, epilogue):
        wb = w.astype(BF16)
        for r in range(IN_TM // IN_RB):
            rows = slice(r * IN_RB, (r + 1) * IN_RB)
            epilogue(jnp.dot(h_ref[rows, :], wb, preferred_element_type=F32), rows)

    def na_norm(g_ref):
        def epilogue(y, rows):
            for k in range(heads):
                cols = slice(k * LANES, (k + 1) * LANES)
                o_ref[k, rows, :] = _rms_rows(y[:, cols], g_ref[...]).astype(BF16)
        return epilogue

    def df_norm_rope(r_ref):
        def epilogue(y, rows):
            lane = lax.broadcasted_iota(jnp.int32, (1, LANES), 1)
            comp0 = (lane % 64) < 32
            ca = cos_ref[rows, :] * r_ref[0:1, :]
            sb = sin_ref[rows, :] * r_ref[1:2, :]
            for k in range(heads):
                cols = slice(k * LANES, (k + 1) * LANES)
                yk = y[:, cols]
                pk = pltpu.roll(yk, 64, 1)
                z = yk * yk + pk * pk
                s0 = jnp.sum(jnp.where(comp0, z, 0.0), axis=-1, keepdims=True)
                s1 = jnp.sum(jnp.where(comp0, 0.0, z), axis=-1, keepdims=True)
                r = jnp.where(comp0, lax.rsqrt(s0 * (0.5 / DF_DQ) + EPS), lax.rsqrt(s1 * (0.5 / DF_DQ) + EPS))
                o_ref[k, rows, :] = ((yk * ca + pk * sb) * r).astype(BF16)
        return epilogue

    def plain(y, rows):
        for k in range(heads):
            o_ref[k, rows, :] = y[:, k * LANES:(k + 1) * LANES].astype(BF16)

    def gate(y, rows):
        for k in range(heads):
            o_ref[k, rows, :] = jax.nn.sigmoid(y[:, k * LANES:(k + 1) * LANES]).astype(BF16)

    @pl.when(seg == 0)
    def _():
        run(w_ref[...], na_norm(gq_ref))

    @pl.when(seg == 1)
    def _():
        run(w_ref[...], na_norm(gk_ref))

    @pl.when(seg == 3)
    def _():
        run(rotary_layout(w_ref[...]), df_norm_rope(rq_ref))

    @pl.when(seg == 4)
    def _():
        run(rotary_layout(w_ref[...]), df_norm_rope(rk_ref))

    @pl.when((seg == 2) | (seg == 5))
    def _():
        run(w_ref[...], plain)

    @pl.when(seg >= 6)
    def _():
        run(w_ref[...], gate)


def _in_proj(h, w_in, g_na_q, g_na_k, cos, sin, rq, rk, seq):
    n = h.shape[0]
    pos_blocks = seq // IN_TM
    row = lambda i, j: (i, 0)
    const = lambda i, j: (0, 0)
    tab = lambda i, j: (i % pos_blocks, 0)
    return pl.pallas_call(
        _in_proj_kernel,
        out_shape=jax.ShapeDtypeStruct((IN_COLS // LANES, n, LANES), BF16),
        grid=(n // IN_TM, IN_COLS // IN_TN),
        in_specs=[
            pl.BlockSpec((IN_TM, D_MODEL), row),
            pl.BlockSpec((D_MODEL, IN_TN), lambda i, j: (0, j)),
            pl.BlockSpec((1, LANES), const),
            pl.BlockSpec((1, LANES), const),
            pl.BlockSpec((IN_TM, LANES), tab),
            pl.BlockSpec((IN_TM, LANES), tab),
            pl.BlockSpec((2, LANES), const),
            pl.BlockSpec((2, LANES), const),
        ],
        out_specs=pl.BlockSpec((IN_TN // LANES, IN_TM, LANES), lambda i, j: (j, i, 0)),
        compiler_params=_params(("parallel", "arbitrary")),
        name="in_proj",
    )(h, w_in, g_na_q, g_na_k, cos, sin, rq, rk)


NA_QROWS = 8
NA_WROWS = 16
NA_TQ = NA_QROWS * GRID_W
NA_TK = NA_WROWS * GRID_W
NA_PAIR = LANES // GRID_W
NA_NPAIR = NA_QROWS // NA_PAIR
NA_PROWS = NA_KR + NA_PAIR - 1
NA_PK = NA_PROWS * GRID_W
NA_NTAB = 1 + 2 * NA_NPAIR
NA_NEG_SLAB = 2 * NA_KR - 1


def _na_geometry(rows):
    nt = rows // NA_QROWS
    assert nt >= 3 and rows >= NA_WROWS
    ws = np.zeros((3, NA_NPAIR), np.int64)
    idx = np.full((NA_NTAB, NA_PROWS, NA_PAIR), NA_NEG_SLAB, np.int64)
    for ci, t in enumerate((0, 1, nt - 1)):
        w0 = int(np.clip(t * NA_QROWS - NA_KR // 2, 0, rows - NA_WROWS))
        for pi in range(NA_NPAIR):
            r = [t * NA_QROWS + NA_PAIR * pi + hb for hb in range(NA_PAIR)]
            rs = [int(np.clip(ri - NA_KR // 2, 0, rows - NA_KR)) for ri in r]
            start = min(min(rs) - w0, NA_WROWS - NA_PROWS)
            assert 0 <= start and max(rs) + NA_KR <= w0 + start + NA_PROWS and min(rs) >= w0 + start
            ws[ci, pi] = start
            tab = 0 if ci == 1 else 1 + (0 if ci == 0 else NA_NPAIR) + pi
            for wp in range(NA_PROWS):
                key_row = w0 + start + wp
                for hb in range(NA_PAIR):
                    if rs[hb] <= key_row < rs[hb] + NA_KR:
                        idx[tab, wp, hb] = key_row - r[hb] + NA_KR - 1
    combos = sorted({tuple(int(v) for v in pair) for pair in idx.reshape(-1, NA_PAIR)})
    slab = [[combos.index(tuple(int(v) for v in idx[tab, wp])) for wp in range(NA_PROWS)] for tab in range(NA_NTAB)]
    return [[int(v) for v in row] for row in ws], combos, slab


def _na_kernel(q_ref, k_ref, v_ref, row_ref, o_ref, p_ref, slab_ref, *, rows):
    nt = rows // NA_QROWS
    ws, combos, slab = _na_geometry(rows)

    kc = lax.broadcasted_iota(jnp.int32, (GRID_W, LANES), 0)
    lane = lax.broadcasted_iota(jnp.int32, (GRID_W, LANES), 1)
    c = lane % GRID_W
    cs = jnp.clip(c - NA_KW // 2, 0, GRID_W - NA_KW)
    in_window = (kc >= cs) & (kc < cs + NA_KW)
    src = jnp.clip(kc - c + NA_KW - 1, 0, 2 * NA_KW - 2) + jnp.where(lane < GRID_W, 0, GRID_W)
    for ci, (left, right) in enumerate(combos):
        ok = in_window
        if left == NA_NEG_SLAB:
            ok = ok & (lane >= GRID_W)
        if right == NA_NEG_SLAB:
            ok = ok & (lane < GRID_W)
        row = jnp.broadcast_to(row_ref[0, ci:ci + 1, :], (GRID_W, LANES))
        slab_ref[ci] = jnp.where(ok, jnp.take_along_axis(row, src, axis=1), NEG)

    def window(t):
        return int(np.clip(t * NA_QROWS - NA_KR // 2, 0, rows - NA_WROWS)) * GRID_W

    def probs(t):
        cls = 0 if t == 0 else (2 if t == nt - 1 else 1)
        buf = t
        for pi in range(NA_NPAIR):
            lanes = slice(pi * LANES, (pi + 1) * LANES)
            tab = 0 if cls == 1 else 1 + (0 if cls == 0 else NA_NPAIR) + pi
            r0 = ws[cls][pi] * GRID_W
            kp = k_ref[window(t) + r0:window(t) + r0 + NA_PK, :]
            qp = q_ref[t * NA_TQ + pi * LANES:t * NA_TQ + (pi + 1) * LANES, :]
            sp = lax.dot_general(kp, qp, (((1,), (1,)), ((), ())), preferred_element_type=F32)
            bias = jnp.concatenate([slab_ref[ci] for ci in slab[tab]], axis=0)
            sp = sp * (NA_DH ** -0.5) + bias
            m = jnp.max(sp, axis=0, keepdims=True)
            p_ref[buf, r0:r0 + NA_PK, lanes] = jnp.exp(sp - m).astype(BF16)
            if r0 > 0:
                p_ref[buf, 0:r0, lanes] = jnp.zeros((r0, LANES), BF16)
            if r0 + NA_PK < NA_TK:
                p_ref[buf, r0 + NA_PK:NA_TK, lanes] = jnp.zeros((NA_TK - r0 - NA_PK, LANES), BF16)

    def outputs(t):
        vw = _with_ones_rows(v_ref[window(t):window(t) + NA_TK, :].T)
        ov = jnp.dot(vw, p_ref[t], preferred_element_type=F32)
        o_ref[t * NA_TQ:(t + 1) * NA_TQ, :] = (ov[0:NA_DH, :] / ov[NA_DH:NA_DH + 1, :]).T.astype(BF16)

    for t in range(nt + 1):
        if t < nt:
            probs(t)
        if t >= 1:
            outputs(t - 1)


def _na_attn(proj, bias_rows, batch, seq):
    rows = seq // GRID_W
    n_slabs = bias_rows.shape[1]
    return pl.pallas_call(
        functools.partial(_na_kernel, rows=rows),
        out_shape=jax.ShapeDtypeStruct((NA_HEADS, batch * seq, LANES), BF16),
        grid=(batch, NA_HEADS),
        in_specs=[
            pl.BlockSpec((None, seq, LANES), lambda b, h: (CB_NA_Q + h, b, 0)),
            pl.BlockSpec((None, seq, LANES), lambda b, h: (CB_NA_K + h, b, 0)),
            pl.BlockSpec((None, seq, LANES), lambda b, h: (CB_NA_V + h, b, 0)),
            pl.BlockSpec((1, n_slabs, LANES), lambda b, h: (h, 0, 0)),
        ],
        out_specs=pl.BlockSpec((None, seq, LANES), lambda b, h: (h, b, 0)),
        scratch_shapes=[pltpu.VMEM((rows // NA_QROWS, NA_TK, NA_TQ), BF16),
                        pltpu.VMEM((n_slabs, GRID_W, LANES), F32)],
        compiler_params=_params(("parallel", "arbitrary")),
        name="na_attn",
    )(proj, proj, proj, bias_rows)


def _na_bias_rows(rpb, rows):
    _, combos, _ = _na_geometry(rows)
    padded = jnp.pad(rpb, ((0, 0), (0, 1), (0, GRID_W - (2 * NA_KW - 1))))
    left = jnp.take(padded, jnp.asarray([cb[0] for cb in combos], jnp.int32), axis=1)
    right = jnp.take(padded, jnp.asarray([cb[1] for cb in combos], jnp.int32), axis=1)
    return jnp.concatenate([left, right], axis=-1)


DF_TQ, DF_TK = 1024, 512
DF_TG = 256
DF_SBUF = 2
DF_VROWS = DF_DV + ONES_ROWS


def _df_kernel(q_ref, k_ref, v_ref, lam_ref, gsub_ref, o_ref, q12_ref, q12n_ref, s_ref, acc_ref, vt_ref, *, seq):
    n_chunks = seq // DF_TK
    n_blocks = seq // DF_TQ
    groups = 2 * DF_TQ // DF_TG

    for c in range(n_chunks):
        cols = slice(c * DF_TK, (c + 1) * DF_TK)
        vt_ref[:, cols] = _with_ones_rows(v_ref[cols, :].T)

    lane = lax.broadcasted_iota(jnp.int32, (1, LANES), 1)
    comp0 = (lane % 64) < 32

    def split_maps(block, dst_ref):
        q = q_ref[pl.ds(pl.multiple_of(block * DF_TQ, DF_TQ), DF_TQ), :]
        zero = jnp.zeros_like(q)
        dst_ref[0:DF_TQ, :] = jnp.where(comp0, q, zero)
        dst_ref[DF_TQ:2 * DF_TQ, :] = jnp.where(comp0, zero, q)

    def scores(kc, q_src_ref, g):
        cols = slice(g * DF_TG, (g + 1) * DF_TG)
        return lax.dot_general(kc, q_src_ref[cols, :], (((1,), (1,)), ((), ())), preferred_element_type=F32)

    def keys(c):
        return k_ref[c * DF_TK:(c + 1) * DF_TK, :]

    def step(c, carry):
        cur, nxt = c % DF_SBUF, (c + 1) % DF_SBUF
        vc = vt_ref[:, c * DF_TK:(c + 1) * DF_TK]
        kn, q_next = (keys(c + 1), q12_ref) if c + 1 < n_chunks else (keys(0), q12n_ref)
        out = []
        for g in range(groups):
            m_prev = carry[g]
            cols = slice(g * DF_TG, (g + 1) * DF_TG)
            s_ref[nxt, :, cols] = scores(kn, q_next, g)
            s = s_ref[cur, :, cols]
            m_new = jnp.maximum(m_prev, jnp.max(s, axis=0, keepdims=True))
            alpha = jnp.exp2(m_prev - m_new)
            p = jnp.exp2(s - m_new).astype(BF16)
            pv = jnp.dot(vc, p, preferred_element_type=F32)
            acc_ref[:, cols] = alpha * acc_ref[:, cols] + pv
            out.append(m_new)
        return tuple(out)

    lp = lam_ref[...]
    lam = (jnp.exp(jnp.sum(lp[0:1] * lp[1:2], axis=-1, keepdims=True))
           - jnp.exp(jnp.sum(lp[2:3] * lp[3:4], axis=-1, keepdims=True)) + LAM_INIT)

    def finish(block):
        o12 = acc_ref[0:DF_DV, :] / acc_ref[DF_DV:DF_DV + 1, :]
        o_t = o12[:, 0:DF_TQ] - lam * o12[:, DF_TQ:2 * DF_TQ]
        rows = pl.ds(pl.multiple_of(block * DF_TQ, DF_TQ), DF_TQ)
        o_ref[rows, :] = (_rms_rows(o_t.T, gsub_ref[...]) * (1.0 - LAM_INIT)).astype(BF16)

    assert n_chunks % DF_SBUF == 0
    split_maps(0, q12n_ref)
    k0 = keys(0)
    for g in range(groups):
        s_ref[0, :, g * DF_TG:(g + 1) * DF_TG] = scores(k0, q12n_ref, g)
    acc_ref[...] = jnp.zeros(acc_ref.shape, F32)
    acc_ref[DF_DV:DF_DV + 1, :] = jnp.ones((1, 2 * DF_TQ), F32)

    def body(block, carry):
        finish(jnp.maximum(block - 1, 0))
        acc_ref[...] = jnp.zeros(acc_ref.shape, F32)
        split_maps(block, q12_ref)
        split_maps(jnp.minimum(block + 1, n_blocks - 1), q12n_ref)
        m = tuple(jnp.full((1, DF_TG), -jnp.inf, F32) for _ in range(groups))
        for c in range(n_chunks):
            m = step(c, m)
        return carry

    lax.fori_loop(0, n_blocks, body, 0)
    finish(n_blocks - 1)


def _df_attn(proj, lam_p, g_sub, batch, seq):
    const = lambda b, h: (0, 0)
    return pl.pallas_call(
        functools.partial(_df_kernel, seq=seq),
        out_shape=jax.ShapeDtypeStruct((DF_HEADS, batch * seq, LANES), BF16),
        grid=(batch, DF_HEADS),
        in_specs=[
            pl.BlockSpec((None, seq, LANES), lambda b, h: (CB_DF_Q + h, b, 0)),
            pl.BlockSpec((None, seq, LANES), lambda b, h: (CB_DF_K + h, b, 0)),
            pl.BlockSpec((None, seq, LANES), lambda b, h: (CB_DF_V + h, b, 0)),
            pl.BlockSpec((4, DF_DQ), const),
            pl.BlockSpec((1, DF_DV), const),
        ],
        out_specs=pl.BlockSpec((None, seq, LANES), lambda b, h: (h, b, 0)),
        scratch_shapes=[pltpu.VMEM((2 * DF_TQ, LANES), BF16), pltpu.VMEM((2 * DF_TQ, LANES), BF16),
                        pltpu.VMEM((DF_SBUF, DF_TK, 2 * DF_TQ), F32),
                        pltpu.VMEM((DF_VROWS, 2 * DF_TQ), F32), pltpu.VMEM((DF_VROWS, seq), BF16)],
        compiler_params=_params(("parallel", "parallel")),
        name="df_attn",
    )(proj, proj, proj, lam_p, g_sub)


def _merge_kernel(oa_ref, ob_ref, wa_ref, wb_ref, sa_ref, sb_ref, o_ref, wab_ref, wbb_ref):
    @pl.when(pl.program_id(0) == 0)
    def _():
        _cast_resident(wa_ref, wab_ref)
        _cast_resident(wb_ref, wbb_ref)

    def rows(slab_ref):
        return jnp.concatenate([slab_ref[k] for k in range(slab_ref.shape[0])], axis=1)

    ya = jnp.dot(rows(oa_ref), wab_ref[...], preferred_element_type=F32)
    yb = jnp.dot(rows(ob_ref), wbb_ref[...], preferred_element_type=F32)
    o_ref[...] = (rows(sa_ref).astype(F32) * ya + rows(sb_ref).astype(F32) * yb).astype(BF16)


def _merge(o_a, o_b, w_na_out, w_df_out, proj):
    n = o_a.shape[1]
    gate_slabs = D_MODEL // LANES
    ga, gb = COL_GATE_A // D_MODEL, COL_GATE_B // D_MODEL
    return pl.pallas_call(
        _merge_kernel,
        out_shape=jax.ShapeDtypeStruct((n, D_MODEL), BF16),
        grid=(n // ROW_TM,),
        in_specs=[
            pl.BlockSpec((NA_HEADS, ROW_TM, LANES), lambda i: (0, i, 0)),
            pl.BlockSpec((DF_HEADS, ROW_TM, LANES), lambda i: (0, i, 0)),
            _resident((NA_W, D_MODEL)),
            _resident((DF_VW, D_MODEL)),
            pl.BlockSpec((gate_slabs, ROW_TM, LANES), lambda i: (ga, i, 0)),
            pl.BlockSpec((gate_slabs, ROW_TM, LANES), lambda i: (gb, i, 0)),
        ],
        out_specs=pl.BlockSpec((ROW_TM, D_MODEL), lambda i: (i, 0)),
        scratch_shapes=[pltpu.VMEM((NA_W, D_MODEL), BF16), pltpu.VMEM((DF_VW, D_MODEL), BF16)],
        compiler_params=_params(("arbitrary",)),
        name="merge",
    )(o_a, o_b, w_na_out, w_df_out, proj, proj)


DN_TM, DN_TN = 1024, 256


def _ffn_down_kernel(a_ref, w_ref, x_ref, o_ref):
    o_ref[...] = x_ref[...] + jnp.dot(a_ref[...], w_ref[...].astype(BF16), preferred_element_type=F32)


def _ffn_down(a, w, x):
    n, k = a.shape
    d = w.shape[1]
    return pl.pallas_call(
        _ffn_down_kernel,
        out_shape=jax.ShapeDtypeStruct((n, d), F32),
        grid=(n // DN_TM, d // DN_TN),
        in_specs=[
            pl.BlockSpec((DN_TM, k), lambda i, j: (i, 0)),
            pl.BlockSpec((k, DN_TN), lambda i, j: (0, j)),
            pl.BlockSpec((DN_TM, DN_TN), lambda i, j: (i, j)),
        ],
        out_specs=pl.BlockSpec((DN_TM, DN_TN), lambda i, j: (i, j)),
        compiler_params=_params(("parallel", "arbitrary")),
        name="ffn_down",
    )(a, w, x)


def _out_proj_norm_kernel(a_ref, w_ref, x_ref, g_ref, o_ref, h_ref, wb_ref):
    @pl.when(pl.program_id(0) == 0)
    def _():
        _cast_resident(w_ref, wb_ref)

    y = x_ref[...] + jnp.dot(a_ref[...], wb_ref[...], preferred_element_type=F32)
    o_ref[...] = y
    h_ref[...] = _rms_rows(y, g_ref[...]).astype(BF16)


def _out_proj_norm(a, w, x, g):
    n, k = a.shape
    d = w.shape[1]
    return pl.pallas_call(
        _out_proj_norm_kernel,
        out_shape=(jax.ShapeDtypeStruct((n, d), F32), jax.ShapeDtypeStruct((n, d), BF16)),
        grid=(n // ROW_TM,),
        in_specs=[
            pl.BlockSpec((ROW_TM, k), lambda i: (i, 0)),
            _resident((k, d)),
            pl.BlockSpec((ROW_TM, d), lambda i: (i, 0)),
            pl.BlockSpec((1, d), lambda i: (0, 0)),
        ],
        out_specs=(pl.BlockSpec((ROW_TM, d), lambda i: (i, 0)), pl.BlockSpec((ROW_TM, d), lambda i: (i, 0))),
        scratch_shapes=[pltpu.VMEM((k, d), BF16)],
        compiler_params=_params(("arbitrary",)),
        name="out_proj",
    )(a, w, x, g)


UP_TM, UP_TN = 2048, 512
UP_RB = 512


def _ffn_up_kernel(h_ref, wg_ref, wu_ref, o_ref):
    wg = wg_ref[...].astype(BF16)
    wu = wu_ref[...].astype(BF16)
    for r in range(UP_TM // UP_RB):
        rows = slice(r * UP_RB, (r + 1) * UP_RB)
        h = h_ref[rows, :]
        gate = jnp.dot(h, wg, preferred_element_type=F32)
        up = jnp.dot(h, wu, preferred_element_type=F32)
        o_ref[rows, :] = (gate * jax.nn.sigmoid(gate) * up).astype(BF16)


def _ffn_up(h, w_gate, w_up):
    n = h.shape[0]
    return pl.pallas_call(
        _ffn_up_kernel,
        out_shape=jax.ShapeDtypeStruct((n, FFN_HID), BF16),
        grid=(n // UP_TM, FFN_HID // UP_TN),
        in_specs=[
            pl.BlockSpec((UP_TM, D_MODEL), lambda i, j: (i, 0)),
            pl.BlockSpec((D_MODEL, UP_TN), lambda i, j: (0, j)),
            pl.BlockSpec((D_MODEL, UP_TN), lambda i, j: (0, j)),
        ],
        out_specs=pl.BlockSpec((UP_TM, UP_TN), lambda i, j: (i, j)),
        compiler_params=_params(("parallel", "arbitrary")),
        name="ffn_up",
    )(h, w_gate, w_up)


def _ple_kernel(x_ref, g_ref, wg_ref, p_ref, wp_ref, o_ref, wgb_ref, wpb_ref):
    @pl.when(pl.program_id(0) == 0)
    def _():
        _cast_resident(wg_ref, wgb_ref)
        _cast_resident(wp_ref, wpb_ref)

    x = x_ref[...]
    h = _rms_rows(x, g_ref[...]).astype(BF16)
    gate = jnp.dot(h, wgb_ref[...], preferred_element_type=F32)
    emb = jnp.dot(p_ref[...].astype(BF16), wpb_ref[...], preferred_element_type=F32)
    o_ref[...] = x + jax.nn.sigmoid(gate) * emb


def _ple(x, g, w_gate, p, w_proj):
    n, d = x.shape
    return pl.pallas_call(
        _ple_kernel,
        out_shape=jax.ShapeDtypeStruct((n, d), F32),
        grid=(n // ROW_TM,),
        in_specs=[
            pl.BlockSpec((ROW_TM, d), lambda i: (i, 0)),
            pl.BlockSpec((1, d), lambda i: (0, 0)),
            _resident((d, d)),
            pl.BlockSpec((ROW_TM, PLE_DIM), lambda i: (i, 0)),
            _resident((PLE_DIM, d)),
        ],
        out_specs=pl.BlockSpec((ROW_TM, d), lambda i: (i, 0)),
        scratch_shapes=[pltpu.VMEM((d, d), BF16), pltpu.VMEM((PLE_DIM, d), BF16)],
        compiler_params=_params(("arbitrary",)),
        name="ple",
    )(x, g, w_gate, p, w_proj)


def _rope_tables(seq):
    inv = 1.0 / (ROPE_THETA ** (jnp.arange(0, DF_DQ, 2, dtype=F32) / DF_DQ))
    ang = jnp.arange(seq, dtype=F32)[:, None] * inv[None, :]
    cos, sin = jnp.cos(ang), jnp.sin(ang)
    return jnp.concatenate([cos, cos, cos, cos], axis=1), jnp.concatenate([-sin, -sin, sin, sin], axis=1)


def _rope_gains(g, scale):
    half = DF_DQ // 2
    g1, g2 = g[:half], g[half:]
    return jnp.stack([jnp.concatenate([g1, g1, g2, g2]), jnp.concatenate([g2, g2, g1, g1])]) * scale


def kernel(x, p, g_mix, w_in, g_na_q, g_na_k, na_rpb, g_df_q, g_df_k, lam_q1, lam_k1, lam_q2, lam_k2,
           g_df_sub, w_na_out, w_df_out, w_o, g_ffn, w_gate, w_up, w_down, g_ple, w_ple_gate, w_ple_proj):
    batch, seq, d = x.shape
    n = batch * seq
    rows = seq // GRID_W
    depth = w_in.shape[0]
    xf = x.reshape(n, d)
    for i in range(depth):
        cos, sin = _rope_tables(seq)
        rq = _rope_gains(g_df_q[i], DF_DQ ** -0.5 * math.log2(math.e))
        rk = _rope_gains(g_df_k[i], 1.0)
        proj = _in_proj(_rms_norm(xf, g_mix[i][None]), w_in[i], g_na_q[i][None], g_na_k[i][None],
                        cos, sin, rq, rk, seq)

        o_a = _na_attn(proj, _na_bias_rows(na_rpb[i], rows), batch, seq)
        lam_p = jnp.stack([lam_q1[i], lam_k1[i], lam_q2[i], lam_k2[i]], axis=0)
        o_b = _df_attn(proj, lam_p, g_df_sub[i][None], batch, seq)

        merged = _merge(o_a, o_b, w_na_out[i], w_df_out[i], proj)
        xf, hf = _out_proj_norm(merged, w_o[i], xf, g_ffn[i][None])

        act = _ffn_up(hf, w_gate[i], w_up[i])
        xf = _ffn_down(act, w_down[i], xf)

        xf = _ple(xf, g_ple[i][None], w_ple_gate[i], p[i].reshape(n, PLE_DIM), w_ple_proj[i])
    return xf.reshape(batch, seq, d)
```

```python
import functools
import math

import numpy as np
import jax
import jax.numpy as jnp
from jax import lax
from jax.experimental import pallas as pl
from jax.experimental.pallas import tpu as pltpu

D_MODEL = 2048
GRID_W = 64
NA_HEADS = 8
NA_DH = 128
NA_KR = 8
NA_KW = 16
DF_HEADS = 8
DF_DQ = 64
DF_DV = 128
FFN_HID = 5632
PLE_DIM = 256
ROPE_THETA = 10000.0
EPS = 1e-6
LAM_INIT = 0.8 - 0.6 * math.exp(-0.3 * 0)

NA_W = NA_HEADS * NA_DH
DF_QW = DF_HEADS * 2 * DF_DQ
DF_VW = DF_HEADS * DF_DV
IN_COLS = 3 * NA_W + 2 * DF_QW + DF_VW + 2 * D_MODEL

LANES = 128
NEG = -1e30
VMEM_LIMIT = 56 * 1024 * 1024

F32 = jnp.float32
BF16 = jnp.bfloat16

CB_NA_Q, CB_NA_K, CB_NA_V = 0, 8, 16
CB_DF_Q, CB_DF_K, CB_DF_V = 24, 32, 40
COL_GATE_A, COL_GATE_B = 6144, 8192


def _params(sem):
    return pltpu.CompilerParams(dimension_semantics=sem, vmem_limit_bytes=VMEM_LIMIT)


def _rms_rows(x, g):
    ms = jnp.mean(x * x, axis=-1, keepdims=True)
    return x * lax.rsqrt(ms + EPS) * g


ONES_ROWS = 16


def _with_ones_rows(v_t):
    row = lax.broadcasted_iota(jnp.int32, (ONES_ROWS, v_t.shape[1]), 0)
    return jnp.concatenate([v_t, jnp.where(row == 0, 1.0, 0.0).astype(v_t.dtype)], axis=0)


NORM_ROWS = 256


def _norm_block(x_ref, g_ref, h_ref):
    def body(c, carry):
        r = pl.ds(pl.multiple_of(c * NORM_ROWS, NORM_ROWS), NORM_ROWS)
        h_ref[r, :] = _rms_rows(x_ref[r, :], g_ref[...]).astype(BF16)
        return carry

    lax.fori_loop(0, x_ref.shape[0] // NORM_ROWS, body, 0)


def _rms_norm_kernel(x_ref, g_ref, o_ref):
    _norm_block(x_ref, g_ref, o_ref)


def _rms_norm(x, g, tm=512):
    n, d = x.shape
    return pl.pallas_call(
        _rms_norm_kernel,
        out_shape=jax.ShapeDtypeStruct((n, d), BF16),
        grid=(n // tm,),
        in_specs=[pl.BlockSpec((tm, d), lambda i: (i, 0)), pl.BlockSpec((1, d), lambda i: (0, 0))],
        out_specs=pl.BlockSpec((tm, d), lambda i: (i, 0)),
        compiler_params=_params(("parallel",)),
        name="rms_norm",
    )(x, g)


CAST_ROWS = 256


def _cast_resident(w_ref, wb_ref):
    rows = min(CAST_ROWS, w_ref.shape[0])

    def body(c, carry):
        r = pl.ds(pl.multiple_of(c * rows, rows), rows)
        wb_ref[r, :] = w_ref[r, :].astype(BF16)
        return carry

    lax.fori_loop(0, w_ref.shape[0] // rows, body, 0)


def _resident(shape):
    return pl.BlockSpec(shape, lambda i: (0, 0), pipeline_mode=pl.Buffered(1))


ROW_TM = 512


IN_TM, IN_TN = 2048, 512
IN_RB = 256
SEG_TILES = 1024 // IN_TN


def _in_proj_kernel(h_ref, w_ref, gq_ref, gk_ref, cos_ref, sin_ref, rq_ref, rk_ref, o_ref):
    seg = pl.program_id(1) // SEG_TILES
    heads = IN_TN // LANES

    def rotary_layout(w):
        lane = lax.broadcasted_iota(jnp.int32, (w.shape[0], LANES), 1)
        quarter = DF_DQ // 2
        src = (lane // quarter % 2) * DF_DQ + (lane // DF_DQ) * quarter + lane % quarter
        return jnp.concatenate([jnp.take_along_axis(w[:, k * LANES:(k + 1) * LANES], src, axis=1)
                                for k in range(heads)], axis=1)

    def run(w, epilogue):
        wb = w.astype(BF16)
        for r in range(IN_TM // IN_RB):
            rows = slice(r * IN_RB, (r + 1) * IN_RB)
            epilogue(jnp.dot(h_ref[rows, :], wb, preferred_element_type=F32), rows)

    def na_norm(g_ref):
        def epilogue(y, rows):
            for k in range(heads):
                cols = slice(k * LANES, (k + 1) * LANES)
                o_ref[k, rows, :] = _rms_rows(y[:, cols], g_ref[...]).astype(BF16)
        return epilogue

    def df_norm_rope(r_ref):
        def epilogue(y, rows):
            lane = lax.broadcasted_iota(jnp.int32, (1, LANES), 1)
            comp0 = (lane % 64) < 32
            ca = cos_ref[rows, :] * r_ref[0:1, :]
            sb = sin_ref[rows, :] * r_ref[1:2, :]
            for k in range(heads):
                cols = slice(k * LANES, (k + 1) * LANES)
                yk = y[:, cols]
                pk = pltpu.roll(yk, 64, 1)
                z = yk * yk + pk * pk
                s0 = jnp.sum(jnp.where(comp0, z, 0.0), axis=-1, keepdims=True)
                s1 = jnp.sum(jnp.where(comp0, 0.0, z), axis=-1, keepdims=True)
                r = jnp.where(comp0, lax.rsqrt(s0 * (0.5 / DF_DQ) + EPS), lax.rsqrt(s1 * (0.5 / DF_DQ) + EPS))
                o_ref[k, rows, :] = ((yk * ca + pk * sb) * r).astype(BF16)
        return epilogue

    def plain(y, rows):
        for k in range(heads):
            o_ref[k, rows, :] = y[:, k * LANES:(k + 1) * LANES].astype(BF16)

    def gate(y, rows):
        for k in range(heads):
            o_ref[k, rows, :] = jax.nn.sigmoid(y[:, k * LANES:(k + 1) * LANES]).astype(BF16)

    @pl.when(seg == 0)
    def _():
        run(w_ref[...], na_norm(gq_ref))

    @pl.when(seg == 1)
    def _():
        run(w_ref[...], na_norm(gk_ref))

    @pl.when(seg == 3)
    def _():
        run(rotary_layout(w_ref[...]), df_norm_rope(rq_ref))

    @pl.when(seg == 4)
    def _():
        run(rotary_layout(w_ref[...]), df_norm_rope(rk_ref))

    @pl.when((seg == 2) | (seg == 5))
    def _():
        run(w_ref[...], plain)

    @pl.when(seg >= 6)
    def _():
        run(w_ref[...], gate)


def _in_proj(h, w_in, g_na_q, g_na_k, cos, sin, rq, rk, seq):
    n = h.shape[0]
    pos_blocks = seq // IN_TM
    row = lambda i, j: (i, 0)
    const = lambda i, j: (0, 0)
    tab = lambda i, j: (i % pos_blocks, 0)
    return pl.pallas_call(
        _in_proj_kernel,
        out_shape=jax.ShapeDtypeStruct((IN_COLS // LANES, n, LANES), BF16),
        grid=(n // IN_TM, IN_COLS // IN_TN),
        in_specs=[
            pl.BlockSpec((IN_TM, D_MODEL), row),
            pl.BlockSpec((D_MODEL, IN_TN), lambda i, j: (0, j)),
            pl.BlockSpec((1, LANES), const),
            pl.BlockSpec((1, LANES), const),
            pl.BlockSpec((IN_TM, LANES), tab),
            pl.BlockSpec((IN_TM, LANES), tab),
            pl.BlockSpec((2, LANES), const),
            pl.BlockSpec((2, LANES), const),
        ],
        out_specs=pl.BlockSpec((IN_TN // LANES, IN_TM, LANES), lambda i, j: (j, i, 0)),
        compiler_params=_params(("parallel", "arbitrary")),
        name="in_proj",
    )(h, w_in, g_na_q, g_na_k, cos, sin, rq, rk)


NA_QROWS = 8
NA_WROWS = 16
NA_TQ = NA_QROWS * GRID_W
NA_TK = NA_WROWS * GRID_W
NA_PAIR = LANES // GRID_W
NA_NPAIR = NA_QROWS // NA_PAIR
NA_PROWS = NA_KR + NA_PAIR - 1
NA_PK = NA_PROWS * GRID_W
NA_NTAB = 1 + 2 * NA_NPAIR
NA_NEG_SLAB = 2 * NA_KR - 1


def _na_geometry(rows):
    nt = rows // NA_QROWS
    assert nt >= 3 and rows >= NA_WROWS
    ws = np.zeros((3, NA_NPAIR), np.int64)
    idx = np.full((NA_NTAB, NA_PROWS, NA_PAIR), NA_NEG_SLAB, np.int64)
    for ci, t in enumerate((0, 1, nt - 1)):
        w0 = int(np.clip(t * NA_QROWS - NA_KR // 2, 0, rows - NA_WROWS))
        for pi in range(NA_NPAIR):
            r = [t * NA_QROWS + NA_PAIR * pi + hb for hb in range(NA_PAIR)]
            rs = [int(np.clip(ri - NA_KR // 2, 0, rows - NA_KR)) for ri in r]
            start = min(min(rs) - w0, NA_WROWS - NA_PROWS)
            assert 0 <= start and max(rs) + NA_KR <= w0 + start + NA_PROWS and min(rs) >= w0 + start
            ws[ci, pi] = start
            tab = 0 if ci == 1 else 1 + (0 if ci == 0 else NA_NPAIR) + pi
            for wp in range(NA_PROWS):
                key_row = w0 + start + wp
                for hb in range(NA_PAIR):
                    if rs[hb] <= key_row < rs[hb] + NA_KR:
                        idx[tab, wp, hb] = key_row - r[hb] + NA_KR - 1
    combos = sorted({tuple(int(v) for v in pair) for pair in idx.reshape(-1, NA_PAIR)})
    slab = [[combos.index(tuple(int(v) for v in idx[tab, wp])) for wp in range(NA_PROWS)] for tab in range(NA_NTAB)]
    return [[int(v) for v in row] for row in ws], combos, slab


def _na_kernel(q_ref, k_ref, v_ref, row_ref, o_ref, p_ref, slab_ref, *, rows):
    nt = rows // NA_QROWS
    ws, combos, slab = _na_geometry(rows)

    kc = lax.broadcasted_iota(jnp.int32, (GRID_W, LANES), 0)
    lane = lax.broadcasted_iota(jnp.int32, (GRID_W, LANES), 1)
    c = lane % GRID_W
    cs = jnp.clip(c - NA_KW // 2, 0, GRID_W - NA_KW)
    in_window = (kc >= cs) & (kc < cs + NA_KW)
    src = jnp.clip(kc - c + NA_KW - 1, 0, 2 * NA_KW - 2) + jnp.where(lane < GRID_W, 0, GRID_W)
    for ci, (left, right) in enumerate(combos):
        ok = in_window
        if left == NA_NEG_SLAB:
            ok = ok & (lane >= GRID_W)
        if right == NA_NEG_SLAB:
            ok = ok & (lane < GRID_W)
        row = jnp.broadcast_to(row_ref[0, ci:ci + 1, :], (GRID_W, LANES))
        slab_ref[ci] = jnp.where(ok, jnp.take_along_axis(row, src, axis=1), NEG)

    def window(t):
        return int(np.clip(t * NA_QROWS - NA_KR // 2, 0, rows - NA_WROWS)) * GRID_W

    def probs(t):
        cls = 0 if t == 0 else (2 if t == nt - 1 else 1)
        buf = t
        for pi in range(NA_NPAIR):
            lanes = slice(pi * LANES, (pi + 1) * LANES)
            tab = 0 if cls == 1 else 1 + (0 if cls == 0 else NA_NPAIR) + pi
            r0 = ws[cls][pi] * GRID_W
            kp = k_ref[window(t) + r0:window(t) + r0 + NA_PK, :]
            qp = q_ref[t * NA_TQ + pi * LANES:t * NA_TQ + (pi + 1) * LANES, :]
            sp = lax.dot_general(kp, qp, (((1,), (1,)), ((), ())), preferred_element_type=F32)
            bias = jnp.concatenate([slab_ref[ci] for ci in slab[tab]], axis=0)
            sp = sp * (NA_DH ** -0.5) + bias
            m = jnp.max(sp, axis=0, keepdims=True)
            p_ref[buf, r0:r0 + NA_PK, lanes] = jnp.exp(sp - m).astype(BF16)
            if r0 > 0:
                p_ref[buf, 0:r0, lanes] = jnp.zeros((r0, LANES), BF16)
            if r0 + NA_PK < NA_TK:
                p_ref[buf, r0 + NA_PK:NA_TK, lanes] = jnp.zeros((NA_TK - r0 - NA_PK, LANES), BF16)

    def outputs(t):
        vw = _with_ones_rows(v_ref[window(t):window(t) + NA_TK, :].T)
        ov = jnp.dot(vw, p_ref[t], preferred_element_type=F32)
        o_ref[t * NA_TQ:(t + 1) * NA_TQ, :] = (ov[0:NA_DH, :] / ov[NA_DH:NA_DH + 1, :]).T.astype(BF16)

    for t in range(nt + 1):
        if t < nt:
            probs(t)
        if t >= 1:
            outputs(t - 1)


def _na_attn(proj, bias_rows, batch, seq):
    rows = seq // GRID_W
    n_slabs = bias_rows.shape[1]
    return pl.pallas_call(
        functools.partial(_na_kernel, rows=rows),
        out_shape=jax.ShapeDtypeStruct((NA_HEADS, batch * seq, LANES), BF16),
        grid=(batch, NA_HEADS),
        in_specs=[
            pl.BlockSpec((None, seq, LANES), lambda b, h: (CB_NA_Q + h, b, 0)),
            pl.BlockSpec((None, seq, LANES), lambda b, h: (CB_NA_K + h, b, 0)),
            pl.BlockSpec((None, seq, LANES), lambda b, h: (CB_NA_V + h, b, 0)),
            pl.BlockSpec((1, n_slabs, LANES), lambda b, h: (h, 0, 0)),
        ],
        out_specs=pl.BlockSpec((None, seq, LANES), lambda b, h: (h, b, 0)),
        scratch_shapes=[pltpu.VMEM((rows // NA_QROWS, NA_TK, NA_TQ), BF16),
                        pltpu.VMEM((n_slabs, GRID_W, LANES), F32)],
        compiler_params=_params(("parallel", "arbitrary")),
        name="na_attn",
    )(proj, proj, proj, bias_rows)


def _na_bias_rows(rpb, rows):
    _, combos, _ = _na_geometry(rows)
    padded = jnp.pad(rpb, ((0, 0), (0, 1), (0, GRID_W - (2 * NA_KW - 1))))
    left = jnp.take(padded, jnp.asarray([cb[0] for cb in combos], jnp.int32), axis=1)
    right = jnp.take(padded, jnp.asarray([cb[1] for cb in combos], jnp.int32), axis=1)
    return jnp.concatenate([left, right], axis=-1)


DF_TQ, DF_TK = 1024, 512
DF_TG = 256
DF_SBUF = 2
DF_VROWS = DF_DV + ONES_ROWS


def _df_kernel(q_ref, k_ref, v_ref, lam_ref, gsub_ref, o_ref, q12_ref, q12n_ref, s_ref, acc_ref, vt_ref, *, seq):
    n_chunks = seq // DF_TK
    n_blocks = seq // DF_TQ
    groups = 2 * DF_TQ // DF_TG

    for c in range(n_chunks):
        cols = slice(c * DF_TK, (c + 1) * DF_TK)
        vt_ref[:, cols] = _with_ones_rows(v_ref[cols, :].T)

    lane = lax.broadcasted_iota(jnp.int32, (1, LANES), 1)
    comp0 = (lane % 64) < 32

    def split_maps(block, dst_ref):
        q = q_ref[pl.ds(pl.multiple_of(block * DF_TQ, DF_TQ), DF_TQ), :]
        zero = jnp.zeros_like(q)
        dst_ref[0:DF_TQ, :] = jnp.where(comp0, q, zero)
        dst_ref[DF_TQ:2 * DF_TQ, :] = jnp.where(comp0, zero, q)

    def scores(kc, q_src_ref, g):
        cols = slice(g * DF_TG, (g + 1) * DF_TG)
        return lax.dot_general(kc, q_src_ref[cols, :], (((1,), (1,)), ((), ())), preferred_element_type=F32)

    def keys(c):
        return k_ref[c * DF_TK:(c + 1) * DF_TK, :]

    def step(c, carry):
        cur, nxt = c % DF_SBUF, (c + 1) % DF_SBUF
        vc = vt_ref[:, c * DF_TK:(c + 1) * DF_TK]
        kn, q_next = (keys(c + 1), q12_ref) if c + 1 < n_chunks else (keys(0), q12n_ref)
        out = []
        for g in range(groups):
            m_prev = carry[g]
            cols = slice(g * DF_TG, (g + 1) * DF_TG)
            s_ref[nxt, :, cols] = scores(kn, q_next, g)
            s = s_ref[cur, :, cols]
            m_new = jnp.maximum(m_prev, jnp.max(s, axis=0, keepdims=True))
            alpha = jnp.exp2(m_prev - m_new)
            p = jnp.exp2(s - m_new).astype(BF16)
            pv = jnp.dot(vc, p, preferred_element_type=F32)
            acc_ref[:, cols] = alpha * acc_ref[:, cols] + pv
            out.append(m_new)
        return tuple(out)

    lp = lam_ref[...]
    lam = (jnp.exp(jnp.sum(lp[0:1] * lp[1:2], axis=-1, keepdims=True))
           - jnp.exp(jnp.sum(lp[2:3] * lp[3:4], axis=-1, keepdims=True)) + LAM_INIT)

    def finish(block):
        o12 = acc_ref[0:DF_DV, :] / acc_ref[DF_DV:DF_DV + 1, :]
        o_t = o12[:, 0:DF_TQ] - lam * o12[:, DF_TQ:2 * DF_TQ]
        rows = pl.ds(pl.multiple_of(block * DF_TQ, DF_TQ), DF_TQ)
        o_ref[rows, :] = (_rms_rows(o_t.T, gsub_ref[...]) * (1.0 - LAM_INIT)).astype(BF16)

    assert n_chunks % DF_SBUF == 0
    split_maps(0, q12n_ref)
    k0 = keys(0)
    for g in range(groups):
        s_ref[0, :, g * DF_TG:(g + 1) * DF_TG] = scores(k0, q12n_ref, g)
    acc_ref[...] = jnp.zeros(acc_ref.shape, F32)
    acc_ref[DF_DV:DF_DV + 1, :] = jnp.ones((1, 2 * DF_TQ), F32)

    def body(block, carry):
        finish(jnp.maximum(block - 1, 0))
        acc_ref[...] = jnp.zeros(acc_ref.shape, F32)
        split_maps(block, q12_ref)
        split_maps(jnp.minimum(block + 1, n_blocks - 1), q12n_ref)
        m = tuple(jnp.full((1, DF_TG), -jnp.inf, F32) for _ in range(groups))
        for c in range(n_chunks):
            m = step(c, m)
        return carry

    lax.fori_loop(0, n_blocks, body, 0)
    finish(n_blocks - 1)


def _df_attn(proj, lam_p, g_sub, batch, seq):
    const = lambda b, h: (0, 0)
    return pl.pallas_call(
        functools.partial(_df_kernel, seq=seq),
        out_shape=jax.ShapeDtypeStruct((DF_HEADS, batch * seq, LANES), BF16),
        grid=(batch, DF_HEADS),
        in_specs=[
            pl.BlockSpec((None, seq, LANES), lambda b, h: (CB_DF_Q + h, b, 0)),
            pl.BlockSpec((None, seq, LANES), lambda b, h: (CB_DF_K + h, b, 0)),
            pl.BlockSpec((None, seq, LANES), lambda b, h: (CB_DF_V + h, b, 0)),
            pl.BlockSpec((4, DF_DQ), const),
            pl.BlockSpec((1, DF_DV), const),
        ],
        out_specs=pl.BlockSpec((None, seq, LANES), lambda b, h: (h, b, 0)),
        scratch_shapes=[pltpu.VMEM((2 * DF_TQ, LANES), BF16), pltpu.VMEM((2 * DF_TQ, LANES), BF16),
                        pltpu.VMEM((DF_SBUF, DF_TK, 2 * DF_TQ), F32),
                        pltpu.VMEM((DF_VROWS, 2 * DF_TQ), F32), pltpu.VMEM((DF_VROWS, seq), BF16)],
        compiler_params=_params(("parallel", "parallel")),
        name="df_attn",
    )(proj, proj, proj, lam_p, g_sub)


def _merge_kernel(oa_ref, ob_ref, wa_ref, wb_ref, sa_ref, sb_ref, o_ref, wab_ref, wbb_ref):
    @pl.when(pl.program_id(0) == 0)
    def _():
        _cast_resident(wa_ref, wab_ref)
        _cast_resident(wb_ref, wbb_ref)

    def rows(slab_ref):
        return jnp.concatenate([slab_ref[k] for k in range(slab_ref.shape[0])], axis=1)

    ya = jnp.dot(rows(oa_ref), wab_ref[...], preferred_element_type=F32)
    yb = jnp.dot(rows(ob_ref), wbb_ref[...], preferred_element_type=F32)
    o_ref[...] = (rows(sa_ref).astype(F32) * ya + rows(sb_ref).astype(F32) * yb).astype(BF16)


def _merge(o_a, o_b, w_na_out, w_df_out, proj):
    n = o_a.shape[1]
    gate_slabs = D_MODEL // LANES
    ga, gb = COL_GATE_A // D_MODEL, COL_GATE_B // D_MODEL
    return pl.pallas_call(
        _merge_kernel,
        out_shape=jax.ShapeDtypeStruct((n, D_MODEL), BF16),
        grid=(n // ROW_TM,),
        in_specs=[
            pl.BlockSpec((NA_HEADS, ROW_TM, LANES), lambda i: (0, i, 0)),
            pl.BlockSpec((DF_HEADS, ROW_TM, LANES), lambda i: (0, i, 0)),
            _resident((NA_W, D_MODEL)),
            _resident((DF_VW, D_MODEL)),
            pl.BlockSpec((gate_slabs, ROW_TM, LANES), lambda i: (ga, i, 0)),
            pl.BlockSpec((gate_slabs, ROW_TM, LANES), lambda i: (gb, i, 0)),
        ],
        out_specs=pl.BlockSpec((ROW_TM, D_MODEL), lambda i: (i, 0)),
        scratch_shapes=[pltpu.VMEM((NA_W, D_MODEL), BF16), pltpu.VMEM((DF_VW, D_MODEL), BF16)],
        compiler_params=_params(("arbitrary",)),
        name="merge",
    )(o_a, o_b, w_na_out, w_df_out, proj, proj)


DN_TM = 256
DN_KC = 512


def _ffn_down_kernel(a_ref, w_ref, x_ref, o_ref, wb_ref, *, n_load):
    s = pl.program_id(0)

    @pl.when(s < n_load)
    def _():
        r = pl.ds(pl.multiple_of(s * DN_KC, DN_KC), DN_KC)
        wb_ref[r, :] = w_ref[...].astype(BF16)

    @pl.when(s >= n_load)
    def _():
        o_ref[...] = x_ref[...] + jnp.dot(a_ref[...], wb_ref[...], preferred_element_type=F32)


def _ffn_down(a, w, x):
    n, k = a.shape
    d = w.shape[1]
    n_load = k // DN_KC
    row = lambda s: (jnp.maximum(s - n_load, 0), 0)
    return pl.pallas_call(
        functools.partial(_ffn_down_kernel, n_load=n_load),
        out_shape=jax.ShapeDtypeStruct((n, d), F32),
        grid=(n_load + n // DN_TM,),
        in_specs=[
            pl.BlockSpec((DN_TM, k), row),
            pl.BlockSpec((DN_KC, d), lambda s: (jnp.minimum(s, n_load - 1), 0)),
            pl.BlockSpec((DN_TM, d), row),
        ],
        out_specs=pl.BlockSpec((DN_TM, d), row),
        scratch_shapes=[pltpu.VMEM((k, d), BF16)],
        compiler_params=_params(("arbitrary",)),
        name="ffn_down",
    )(a, w, x)


def _out_proj_norm_kernel(a_ref, w_ref, x_ref, g_ref, o_ref, h_ref, wb_ref):
    @pl.when(pl.program_id(0) == 0)
    def _():
        _cast_resident(w_ref, wb_ref)

    y = x_ref[...] + jnp.dot(a_ref[...], wb_ref[...], preferred_element_type=F32)
    o_ref[...] = y
    h_ref[...] = _rms_rows(y, g_ref[...]).astype(BF16)


def _out_proj_norm(a, w, x, g):
    n, k = a.shape
    d = w.shape[1]
    return pl.pallas_call(
        _out_proj_norm_kernel,
        out_shape=(jax.ShapeDtypeStruct((n, d), F32), jax.ShapeDtypeStruct((n, d), BF16)),
        grid=(n // ROW_TM,),
        in_specs=[
            pl.BlockSpec((ROW_TM, k), lambda i: (i, 0)),
            _resident((k, d)),
            pl.BlockSpec((ROW_TM, d), lambda i: (i, 0)),
            pl.BlockSpec((1, d), lambda i: (0, 0)),
        ],
        out_specs=(pl.BlockSpec((ROW_TM, d), lambda i: (i, 0)), pl.BlockSpec((ROW_TM, d), lambda i: (i, 0))),
        scratch_shapes=[pltpu.VMEM((k, d), BF16)],
        compiler_params=_params(("arbitrary",)),
        name="out_proj",
    )(a, w, x, g)


UP_TM, UP_TN = 2048, 512
UP_RB = 256


def _ffn_up_kernel(h_ref, wg_ref, wu_ref, o_ref):
    wg = wg_ref[...].astype(BF16)
    wu = wu_ref[...].astype(BF16)
    for r in range(UP_TM // UP_RB):
        rows = slice(r * UP_RB, (r + 1) * UP_RB)
        h = h_ref[rows, :]
        gate = jnp.dot(h, wg, preferred_element_type=F32)
        up = jnp.dot(h, wu, preferred_element_type=F32)
        o_ref[rows, :] = (gate * jax.nn.sigmoid(gate) * up).astype(BF16)


def _ffn_up(h, w_gate, w_up):
    n = h.shape[0]
    return pl.pallas_call(
        _ffn_up_kernel,
        out_shape=jax.ShapeDtypeStruct((n, FFN_HID), BF16),
        grid=(n // UP_TM, FFN_HID // UP_TN),
        in_specs=[
            pl.BlockSpec((UP_TM, D_MODEL), lambda i, j: (i, 0)),
            pl.BlockSpec((D_MODEL, UP_TN), lambda i, j: (0, j)),
            pl.BlockSpec((D_MODEL, UP_TN), lambda i, j: (0, j)),
        ],
        out_specs=pl.BlockSpec((UP_TM, UP_TN), lambda i, j: (i, j)),
        compiler_params=_params(("parallel", "arbitrary")),
        name="ffn_up",
    )(h, w_gate, w_up)


def _ple_kernel(x_ref, g_ref, wg_ref, p_ref, wp_ref, o_ref, wgb_ref, wpb_ref):
    @pl.when(pl.program_id(0) == 0)
    def _():
        _cast_resident(wg_ref, wgb_ref)
        _cast_resident(wp_ref, wpb_ref)

    x = x_ref[...]
    h = _rms_rows(x, g_ref[...]).astype(BF16)
    gate = jnp.dot(h, wgb_ref[...], preferred_element_type=F32)
    emb = jnp.dot(p_ref[...].astype(BF16), wpb_ref[...], preferred_element_type=F32)
    o_ref[...] = x + jax.nn.sigmoid(gate) * emb


def _ple(x, g, w_gate, p, w_proj):
    n, d = x.shape
    return pl.pallas_call(
        _ple_kernel,
        out_shape=jax.ShapeDtypeStruct((n, d), F32),
        grid=(n // ROW_TM,),
        in_specs=[
            pl.BlockSpec((ROW_TM, d), lambda i: (i, 0)),
            pl.BlockSpec((1, d), lambda i: (0, 0)),
            _resident((d, d)),
            pl.BlockSpec((ROW_TM, PLE_DIM), lambda i: (i, 0)),
            _resident((PLE_DIM, d)),
        ],
        out_specs=pl.BlockSpec((ROW_TM, d), lambda i: (i, 0)),
        scratch_shapes=[pltpu.VMEM((d, d), BF16), pltpu.VMEM((PLE_DIM, d), BF16)],
        compiler_params=_params(("arbitrary",)),
        name="ple",
    )(x, g, w_gate, p, w_proj)


def _rope_tables(seq):
    inv = 1.0 / (ROPE_THETA ** (jnp.arange(0, DF_DQ, 2, dtype=F32) / DF_DQ))
    ang = jnp.arange(seq, dtype=F32)[:, None] * inv[None, :]
    cos, sin = jnp.cos(ang), jnp.sin(ang)
    return jnp.concatenate([cos, cos, cos, cos], axis=1), jnp.concatenate([-sin, -sin, sin, sin], axis=1)


def _rope_gains(g, scale):
    half = DF_DQ // 2
    g1, g2 = g[:half], g[half:]
    return jnp.stack([jnp.concatenate([g1, g1, g2, g2]), jnp.concatenate([g2, g2, g1, g1])]) * scale


def kernel(x, p, g_mix, w_in, g_na_q, g_na_k, na_rpb, g_df_q, g_df_k, lam_q1, lam_k1, lam_q2, lam_k2,
           g_df_sub, w_na_out, w_df_out, w_o, g_ffn, w_gate, w_up, w_down, g_ple, w_ple_gate, w_ple_proj):
    batch, seq, d = x.shape
    n = batch * seq
    rows = seq // GRID_W
    depth = w_in.shape[0]
    xf = x.reshape(n, d)
    for i in range(depth):
        cos, sin = _rope_tables(seq)
        rq = _rope_gains(g_df_q[i], DF_DQ ** -0.5 * math.log2(math.e))
        rk = _rope_gains(g_df_k[i], 1.0)
        proj = _in_proj(_rms_norm(xf, g_mix[i][None]), w_in[i], g_na_q[i][None], g_na_k[i][None],
                        cos, sin, rq, rk, seq)

        o_a = _na_attn(proj, _na_bias_rows(na_rpb[i], rows), batch, seq)
        lam_p = jnp.stack([lam_q1[i], lam_k1[i], lam_q2[i], lam_k2[i]], axis=0)
        o_b = _df_attn(proj, lam_p, g_df_sub[i][None], batch, seq)

        merged = _merge(o_a, o_b, w_na_out[i], w_df_out[i], proj)
        xf, hf = _out_proj_norm(merged, w_o[i], xf, g_ffn[i][None])

        act = _ffn_up(hf, w_gate[i], w_up[i])
        xf = _ffn_down(act, w_down[i], xf)

        xf = _ple(xf, g_ple[i][None], w_ple_gate[i], p[i].reshape(n, PLE_DIM), w_ple_proj[i])
    return xf.reshape(batch, seq, d)
```

```python
import functools
import math

import numpy as np
import jax
import jax.numpy as jnp
from jax import lax
from jax.experimental import pallas as pl
from jax.experimental.pallas import tpu as pltpu

D_MODEL = 2048
GRID_W = 64
NA_HEADS = 8
NA_DH = 128
NA_KR = 8
NA_KW = 16
DF_HEADS = 8
DF_DQ = 64
DF_DV = 128
FFN_HID = 5632
PLE_DIM = 256
ROPE_THETA = 10000.0
EPS = 1e-6
LAM_INIT = 0.8 - 0.6 * math.exp(-0.3 * 0)

NA_W = NA_HEADS * NA_DH
DF_QW = DF_HEADS * 2 * DF_DQ
DF_VW = DF_HEADS * DF_DV
IN_COLS = 3 * NA_W + 2 * DF_QW + DF_VW + 2 * D_MODEL

LANES = 128
NEG = -1e30
VMEM_LIMIT = 56 * 1024 * 1024

F32 = jnp.float32
BF16 = jnp.bfloat16

CB_NA_Q, CB_NA_K, CB_NA_V = 0, 8, 16
CB_DF_Q, CB_DF_K, CB_DF_V = 24, 32, 40
COL_GATE_A, COL_GATE_B = 6144, 8192


def _params(sem):
    return pltpu.CompilerParams(dimension_semantics=sem, vmem_limit_bytes=VMEM_LIMIT)


def _rms_rows(x, g):
    ms = jnp.mean(x * x, axis=-1, keepdims=True)
    return x * lax.rsqrt(ms + EPS) * g


ONES_ROWS = 16


def _with_ones_rows(v_t):
    row = lax.broadcasted_iota(jnp.int32, (ONES_ROWS, v_t.shape[1]), 0)
    return jnp.concatenate([v_t, jnp.where(row == 0, 1.0, 0.0).astype(v_t.dtype)], axis=0)


NORM_ROWS = 256


def _norm_block(x_ref, g_ref, h_ref):
    def body(c, carry):
        r = pl.ds(pl.multiple_of(c * NORM_ROWS, NORM_ROWS), NORM_ROWS)
        h_ref[r, :] = _rms_rows(x_ref[r, :], g_ref[...]).astype(BF16)
        return carry

    lax.fori_loop(0, x_ref.shape[0] // NORM_ROWS, body, 0)


def _rms_norm_kernel(x_ref, g_ref, o_ref):
    _norm_block(x_ref, g_ref, o_ref)


def _rms_norm(x, g, tm=512):
    n, d = x.shape
    return pl.pallas_call(
        _rms_norm_kernel,
        out_shape=jax.ShapeDtypeStruct((n, d), BF16),
        grid=(n // tm,),
        in_specs=[pl.BlockSpec((tm, d), lambda i: (i, 0)), pl.BlockSpec((1, d), lambda i: (0, 0))],
        out_specs=pl.BlockSpec((tm, d), lambda i: (i, 0)),
        compiler_params=_params(("parallel",)),
        name="rms_norm",
    )(x, g)


CAST_ROWS = 256


def _cast_resident(w_ref, wb_ref):
    rows = min(CAST_ROWS, w_ref.shape[0])

    def body(c, carry):
        r = pl.ds(pl.multiple_of(c * rows, rows), rows)
        wb_ref[r, :] = w_ref[r, :].astype(BF16)
        return carry

    lax.fori_loop(0, w_ref.shape[0] // rows, body, 0)


def _resident(shape):
    return pl.BlockSpec(shape, lambda i: (0, 0), pipeline_mode=pl.Buffered(1))


ROW_TM = 512


IN_TM, IN_TN = 2048, 512
IN_RB = 256
SEG_TILES = 1024 // IN_TN


def _in_proj_kernel(h_ref, w_ref, gq_ref, gk_ref, cos_ref, sin_ref, rq_ref, rk_ref, o_ref):
    seg = pl.program_id(1) // SEG_TILES
    heads = IN_TN // LANES

    def rotary_layout(w):
        lane = lax.broadcasted_iota(jnp.int32, (w.shape[0], LANES), 1)
        quarter = DF_DQ // 2
        src = (lane // quarter % 2) * DF_DQ + (lane // DF_DQ) * quarter + lane % quarter
        return jnp.concatenate([jnp.take_along_axis(w[:, k * LANES:(k + 1) * LANES], src, axis=1)
                                for k in range(heads)], axis=1)

    def run(w, epilogue):
        wb = w.astype(BF16)
        for r in range(IN_TM // IN_RB):
            rows = slice(r * IN_RB, (r + 1) * IN_RB)
            epilogue(jnp.dot(h_ref[rows, :], wb, preferred_element_type=F32), rows)

    def na_norm(g_ref):
        def epilogue(y, rows):
            for k in range(heads):
                cols = slice(k * LANES, (k + 1) * LANES)
                o_ref[k, rows, :] = _rms_rows(y[:, cols], g_ref[...]).astype(BF16)
        return epilogue

    def df_norm_rope(r_ref):
        def epilogue(y, rows):
            lane = lax.broadcasted_iota(jnp.int32, (1, LANES), 1)
            comp0 = (lane % 64) < 32
            ca = cos_ref[rows, :] * r_ref[0:1, :]
            sb = sin_ref[rows, :] * r_ref[1:2, :]
            for k in range(heads):
                cols = slice(k * LANES, (k + 1) * LANES)
                yk = y[:, cols]
                pk = pltpu.roll(yk, 64, 1)
                z = yk * yk + pk * pk
                s0 = jnp.sum(jnp.where(comp0, z, 0.0), axis=-1, keepdims=True)
                s1 = jnp.sum(jnp.where(comp0, 0.0, z), axis=-1, keepdims=True)
                r = jnp.where(comp0, lax.rsqrt(s0 * (0.5 / DF_DQ) + EPS), lax.rsqrt(s1 * (0.5 / DF_DQ) + EPS))
                o_ref[k, rows, :] = ((yk * ca + pk * sb) * r).astype(BF16)
        return epilogue

    def plain(y, rows):
        for k in range(heads):
            o_ref[k, rows, :] = y[:, k * LANES:(k + 1) * LANES].astype(BF16)

    def gate(y, rows):
        for k in range(heads):
            o_ref[k, rows, :] = jax.nn.sigmoid(y[:, k * LANES:(k + 1) * LANES]).astype(BF16)

    @pl.when(seg == 0)
    def _():
        run(w_ref[...], na_norm(gq_ref))

    @pl.when(seg == 1)
    def _():
        run(w_ref[...], na_norm(gk_ref))

    @pl.when(seg == 3)
    def _():
        run(rotary_layout(w_ref[...]), df_norm_rope(rq_ref))

    @pl.when(seg == 4)
    def _():
        run(rotary_layout(w_ref[...]), df_norm_rope(rk_ref))

    @pl.when((seg == 2) | (seg == 5))
    def _():
        run(w_ref[...], plain)

    @pl.when(seg >= 6)
    def _():
        run(w_ref[...], gate)


def _in_proj(h, w_in, g_na_q, g_na_k, cos, sin, rq, rk, seq):
    n = h.shape[0]
    pos_blocks = seq // IN_TM
    row = lambda i, j: (i, 0)
    const = lambda i, j: (0, 0)
    tab = lambda i, j: (i % pos_blocks, 0)
    return pl.pallas_call(
        _in_proj_kernel,
        out_shape=jax.ShapeDtypeStruct((IN_COLS // LANES, n, LANES), BF16),
        grid=(n // IN_TM, IN_COLS // IN_TN),
        in_specs=[
            pl.BlockSpec((IN_TM, D_MODEL), row),
            pl.BlockSpec((D_MODEL, IN_TN), lambda i, j: (0, j)),
            pl.BlockSpec((1, LANES), const),
            pl.BlockSpec((1, LANES), const),
            pl.BlockSpec((IN_TM, LANES), tab),
            pl.BlockSpec((IN_TM, LANES), tab),
            pl.BlockSpec((2, LANES), const),
            pl.BlockSpec((2, LANES), const),
        ],
        out_specs=pl.BlockSpec((IN_TN // LANES, IN_TM, LANES), lambda i, j: (j, i, 0)),
        compiler_params=_params(("parallel", "arbitrary")),
        name="in_proj",
    )(h, w_in, g_na_q, g_na_k, cos, sin, rq, rk)


NA_QROWS = 8
NA_WROWS = 16
NA_TQ = NA_QROWS * GRID_W
NA_TK = NA_WROWS * GRID_W
NA_PAIR = LANES // GRID_W
NA_NPAIR = NA_QROWS // NA_PAIR
NA_PROWS = NA_KR + NA_PAIR - 1
NA_PK = NA_PROWS * GRID_W
NA_NTAB = 1 + 2 * NA_NPAIR
NA_NEG_SLAB = 2 * NA_KR - 1


def _na_geometry(rows):
    nt = rows // NA_QROWS
    assert nt >= 3 and rows >= NA_WROWS
    ws = np.zeros((3, NA_NPAIR), np.int64)
    idx = np.full((NA_NTAB, NA_PROWS, NA_PAIR), NA_NEG_SLAB, np.int64)
    for ci, t in enumerate((0, 1, nt - 1)):
        w0 = int(np.clip(t * NA_QROWS - NA_KR // 2, 0, rows - NA_WROWS))
        for pi in range(NA_NPAIR):
            r = [t * NA_QROWS + NA_PAIR * pi + hb for hb in range(NA_PAIR)]
            rs = [int(np.clip(ri - NA_KR // 2, 0, rows - NA_KR)) for ri in r]
            start = min(min(rs) - w0, NA_WROWS - NA_PROWS)
            assert 0 <= start and max(rs) + NA_KR <= w0 + start + NA_PROWS and min(rs) >= w0 + start
            ws[ci, pi] = start
            tab = 0 if ci == 1 else 1 + (0 if ci == 0 else NA_NPAIR) + pi
            for wp in range(NA_PROWS):
                key_row = w0 + start + wp
                for hb in range(NA_PAIR):
                    if rs[hb] <= key_row < rs[hb] + NA_KR:
                        idx[tab, wp, hb] = key_row - r[hb] + NA_KR - 1
    combos = sorted({tuple(int(v) for v in pair) for pair in idx.reshape(-1, NA_PAIR)})
    slab = [[combos.index(tuple(int(v) for v in idx[tab, wp])) for wp in range(NA_PROWS)] for tab in range(NA_NTAB)]
    return [[int(v) for v in row] for row in ws], combos, slab


def _na_kernel(q_ref, k_ref, v_ref, row_ref, o_ref, p_ref, slab_ref, vt_ref, *, rows):
    nt = rows // NA_QROWS
    ws, combos, slab = _na_geometry(rows)

    kc = lax.broadcasted_iota(jnp.int32, (GRID_W, LANES), 0)
    lane = lax.broadcasted_iota(jnp.int32, (GRID_W, LANES), 1)
    c = lane % GRID_W
    cs = jnp.clip(c - NA_KW // 2, 0, GRID_W - NA_KW)
    in_window = (kc >= cs) & (kc < cs + NA_KW)
    src = jnp.clip(kc - c + NA_KW - 1, 0, 2 * NA_KW - 2) + jnp.where(lane < GRID_W, 0, GRID_W)
    for ci, (left, right) in enumerate(combos):
        ok = in_window
        if left == NA_NEG_SLAB:
            ok = ok & (lane >= GRID_W)
        if right == NA_NEG_SLAB:
            ok = ok & (lane < GRID_W)
        row = jnp.broadcast_to(row_ref[0, ci:ci + 1, :], (GRID_W, LANES))
        slab_ref[ci] = jnp.where(ok, jnp.take_along_axis(row, src, axis=1), NEG)

    def window(t):
        return int(np.clip(t * NA_QROWS - NA_KR // 2, 0, rows - NA_WROWS)) * GRID_W

    def probs(t):
        cls = 0 if t == 0 else (2 if t == nt - 1 else 1)
        buf = t
        for pi in range(NA_NPAIR):
            lanes = slice(pi * LANES, (pi + 1) * LANES)
            tab = 0 if cls == 1 else 1 + (0 if cls == 0 else NA_NPAIR) + pi
            r0 = ws[cls][pi] * GRID_W
            kp = k_ref[window(t) + r0:window(t) + r0 + NA_PK, :]
            qp = q_ref[t * NA_TQ + pi * LANES:t * NA_TQ + (pi + 1) * LANES, :]
            sp = lax.dot_general(kp, qp, (((1,), (1,)), ((), ())), preferred_element_type=F32)
            bias = jnp.concatenate([slab_ref[ci] for ci in slab[tab]], axis=0)
            sp = sp * (NA_DH ** -0.5) + bias
            m = jnp.max(sp, axis=0, keepdims=True)
            p_ref[buf, r0:r0 + NA_PK, lanes] = jnp.exp(sp - m).astype(BF16)
            if r0 > 0:
                p_ref[buf, 0:r0, lanes] = jnp.zeros((r0, LANES), BF16)
            if r0 + NA_PK < NA_TK:
                p_ref[buf, r0 + NA_PK:NA_TK, lanes] = jnp.zeros((NA_TK - r0 - NA_PK, LANES), BF16)

    for c in range(rows * GRID_W // NA_TK):
        cols = slice(c * NA_TK, (c + 1) * NA_TK)
        vt_ref[:, cols] = _with_ones_rows(v_ref[cols, :].T)

    def outputs(t):
        vw = vt_ref[:, window(t):window(t) + NA_TK]
        ov = jnp.dot(vw, p_ref[t], preferred_element_type=F32)
        o_ref[t * NA_TQ:(t + 1) * NA_TQ, :] = (ov[0:NA_DH, :] / ov[NA_DH:NA_DH + 1, :]).T.astype(BF16)

    for t in range(nt + 1):
        if t < nt:
            probs(t)
        if t >= 1:
            outputs(t - 1)


def _na_attn(proj, bias_rows, batch, seq):
    rows = seq // GRID_W
    n_slabs = bias_rows.shape[1]
    return pl.pallas_call(
        functools.partial(_na_kernel, rows=rows),
        out_shape=jax.ShapeDtypeStruct((NA_HEADS, batch * seq, LANES), BF16),
        grid=(batch, NA_HEADS),
        in_specs=[
            pl.BlockSpec((None, seq, LANES), lambda b, h: (CB_NA_Q + h, b, 0)),
            pl.BlockSpec((None, seq, LANES), lambda b, h: (CB_NA_K + h, b, 0)),
            pl.BlockSpec((None, seq, LANES), lambda b, h: (CB_NA_V + h, b, 0)),
            pl.BlockSpec((1, n_slabs, LANES), lambda b, h: (h, 0, 0)),
        ],
        out_specs=pl.BlockSpec((None, seq, LANES), lambda b, h: (h, b, 0)),
        scratch_shapes=[pltpu.VMEM((rows // NA_QROWS, NA_TK, NA_TQ), BF16),
                        pltpu.VMEM((n_slabs, GRID_W, LANES), F32),
                        pltpu.VMEM((NA_DH + ONES_ROWS, seq), BF16)],
        compiler_params=_params(("parallel", "arbitrary")),
        name="na_attn",
    )(proj, proj, proj, bias_rows)


def _na_bias_rows(rpb, rows):
    _, combos, _ = _na_geometry(rows)
    padded = jnp.pad(rpb, ((0, 0), (0, 1), (0, GRID_W - (2 * NA_KW - 1))))
    left = jnp.take(padded, jnp.asarray([cb[0] for cb in combos], jnp.int32), axis=1)
    right = jnp.take(padded, jnp.asarray([cb[1] for cb in combos], jnp.int32), axis=1)
    return jnp.concatenate([left, right], axis=-1)


DF_TQ, DF_TK = 1024, 512
DF_TG = 256
DF_SBUF = 2
DF_VROWS = DF_DV + ONES_ROWS


def _df_kernel(q_ref, k_ref, v_ref, lam_ref, gsub_ref, o_ref, q12_ref, q12n_ref, s_ref, acc_ref, vt_ref, *, seq):
    n_chunks = seq // DF_TK
    n_blocks = seq // DF_TQ
    groups = 2 * DF_TQ // DF_TG

    for c in range(n_chunks):
        cols = slice(c * DF_TK, (c + 1) * DF_TK)
        vt_ref[:, cols] = _with_ones_rows(v_ref[cols, :].T)

    lane = lax.broadcasted_iota(jnp.int32, (1, LANES), 1)
    comp0 = (lane % 64) < 32

    def split_maps(block, dst_ref):
        q = q_ref[pl.ds(pl.multiple_of(block * DF_TQ, DF_TQ), DF_TQ), :]
        zero = jnp.zeros_like(q)
        dst_ref[0:DF_TQ, :] = jnp.where(comp0, q, zero)
        dst_ref[DF_TQ:2 * DF_TQ, :] = jnp.where(comp0, zero, q)

    def scores(kc, q_src_ref, g):
        cols = slice(g * DF_TG, (g + 1) * DF_TG)
        return lax.dot_general(kc, q_src_ref[cols, :], (((1,), (1,)), ((), ())), preferred_element_type=F32)

    def keys(c):
        return k_ref[c * DF_TK:(c + 1) * DF_TK, :]

    def step(c, carry):
        cur, nxt = c % DF_SBUF, (c + 1) % DF_SBUF
        vc = vt_ref[:, c * DF_TK:(c + 1) * DF_TK]
        kn, q_next = (keys(c + 1), q12_ref) if c + 1 < n_chunks else (keys(0), q12n_ref)
        out = []
        for g in range(groups):
            m_prev = carry[g]
            cols = slice(g * DF_TG, (g + 1) * DF_TG)
            s_ref[nxt, :, cols] = scores(kn, q_next, g)
            s = s_ref[cur, :, cols]
            m_new = jnp.maximum(m_prev, jnp.max(s, axis=0, keepdims=True))
            alpha = jnp.exp2(m_prev - m_new)
            p = jnp.exp2(s - m_new).astype(BF16)
            pv = jnp.dot(vc, p, preferred_element_type=F32)
            acc_ref[:, cols] = alpha * acc_ref[:, cols] + pv
            out.append(m_new)
        return tuple(out)

    lp = lam_ref[...]
    lam = (jnp.exp(jnp.sum(lp[0:1] * lp[1:2], axis=-1, keepdims=True))
           - jnp.exp(jnp.sum(lp[2:3] * lp[3:4], axis=-1, keepdims=True)) + LAM_INIT)

    def finish(block):
        o12 = acc_ref[0:DF_DV, :] / acc_ref[DF_DV:DF_DV + 1, :]
        o_t = o12[:, 0:DF_TQ] - lam * o12[:, DF_TQ:2 * DF_TQ]
        rows = pl.ds(pl.multiple_of(block * DF_TQ, DF_TQ), DF_TQ)
        o_ref[rows, :] = (_rms_rows(o_t.T, gsub_ref[...]) * (1.0 - LAM_INIT)).astype(BF16)

    assert n_chunks % DF_SBUF == 0
    split_maps(0, q12n_ref)
    k0 = keys(0)
    for g in range(groups):
        s_ref[0, :, g * DF_TG:(g + 1) * DF_TG] = scores(k0, q12n_ref, g)
    acc_ref[...] = jnp.zeros(acc_ref.shape, F32)
    acc_ref[DF_DV:DF_DV + 1, :] = jnp.ones((1, 2 * DF_TQ), F32)

    def body(block, carry):
        finish(jnp.maximum(block - 1, 0))
        acc_ref[...] = jnp.zeros(acc_ref.shape, F32)
        split_maps(block, q12_ref)
        split_maps(jnp.minimum(block + 1, n_blocks - 1), q12n_ref)
        m = tuple(jnp.full((1, DF_TG), -jnp.inf, F32) for _ in range(groups))
        for c in range(n_chunks):
            m = step(c, m)
        return carry

    lax.fori_loop(0, n_blocks, body, 0)
    finish(n_blocks - 1)


def _df_attn(proj, lam_p, g_sub, batch, seq):
    const = lambda b, h: (0, 0)
    return pl.pallas_call(
        functools.partial(_df_kernel, seq=seq),
        out_shape=jax.ShapeDtypeStruct((DF_HEADS, batch * seq, LANES), BF16),
        grid=(batch, DF_HEADS),
        in_specs=[
            pl.BlockSpec((None, seq, LANES), lambda b, h: (CB_DF_Q + h, b, 0)),
            pl.BlockSpec((None, seq, LANES), lambda b, h: (CB_DF_K + h, b, 0)),
            pl.BlockSpec((None, seq, LANES), lambda b, h: (CB_DF_V + h, b, 0)),
            pl.BlockSpec((4, DF_DQ), const),
            pl.BlockSpec((1, DF_DV), const),
        ],
        out_specs=pl.BlockSpec((None, seq, LANES), lambda b, h: (h, b, 0)),
        scratch_shapes=[pltpu.VMEM((2 * DF_TQ, LANES), BF16), pltpu.VMEM((2 * DF_TQ, LANES), BF16),
                        pltpu.VMEM((DF_SBUF, DF_TK, 2 * DF_TQ), F32),
                        pltpu.VMEM((DF_VROWS, 2 * DF_TQ), F32), pltpu.VMEM((DF_VROWS, seq), BF16)],
        compiler_params=_params(("parallel", "parallel")),
        name="df_attn",
    )(proj, proj, proj, lam_p, g_sub)


def _merge_kernel(oa_ref, ob_ref, wa_ref, wb_ref, sa_ref, sb_ref, o_ref, wab_ref, wbb_ref):
    @pl.when(pl.program_id(0) == 0)
    def _():
        _cast_resident(wa_ref, wab_ref)
        _cast_resident(wb_ref, wbb_ref)

    def rows(slab_ref):
        return jnp.concatenate([slab_ref[k] for k in range(slab_ref.shape[0])], axis=1)

    ya = jnp.dot(rows(oa_ref), wab_ref[...], preferred_element_type=F32)
    yb = jnp.dot(rows(ob_ref), wbb_ref[...], preferred_element_type=F32)
    o_ref[...] = (rows(sa_ref).astype(F32) * ya + rows(sb_ref).astype(F32) * yb).astype(BF16)


def _merge(o_a, o_b, w_na_out, w_df_out, proj):
    n = o_a.shape[1]
    gate_slabs = D_MODEL // LANES
    ga, gb = COL_GATE_A // D_MODEL, COL_GATE_B // D_MODEL
    return pl.pallas_call(
        _merge_kernel,
        out_shape=jax.ShapeDtypeStruct((n, D_MODEL), BF16),
        grid=(n // ROW_TM,),
        in_specs=[
            pl.BlockSpec((NA_HEADS, ROW_TM, LANES), lambda i: (0, i, 0)),
            pl.BlockSpec((DF_HEADS, ROW_TM, LANES), lambda i: (0, i, 0)),
            _resident((NA_W, D_MODEL)),
            _resident((DF_VW, D_MODEL)),
            pl.BlockSpec((gate_slabs, ROW_TM, LANES), lambda i: (ga, i, 0)),
            pl.BlockSpec((gate_slabs, ROW_TM, LANES), lambda i: (gb, i, 0)),
        ],
        out_specs=pl.BlockSpec((ROW_TM, D_MODEL), lambda i: (i, 0)),
        scratch_shapes=[pltpu.VMEM((NA_W, D_MODEL), BF16), pltpu.VMEM((DF_VW, D_MODEL), BF16)],
        compiler_params=_params(("arbitrary",)),
        name="merge",
    )(o_a, o_b, w_na_out, w_df_out, proj, proj)


DN_TM = 256
DN_KC = 512


def _ffn_down_kernel(a_ref, w_ref, x_ref, o_ref, wb_ref, *, n_load):
    s = pl.program_id(0)

    @pl.when(s < n_load)
    def _():
        r = pl.ds(pl.multiple_of(s * DN_KC, DN_KC), DN_KC)
        wb_ref[r, :] = w_ref[...].astype(BF16)

    @pl.when(s >= n_load)
    def _():
        o_ref[...] = x_ref[...] + jnp.dot(a_ref[...], wb_ref[...], preferred_element_type=F32)


def _ffn_down(a, w, x):
    n, k = a.shape
    d = w.shape[1]
    n_load = k // DN_KC
    row = lambda s: (jnp.maximum(s - n_load, 0), 0)
    return pl.pallas_call(
        functools.partial(_ffn_down_kernel, n_load=n_load),
        out_shape=jax.ShapeDtypeStruct((n, d), F32),
        grid=(n_load + n // DN_TM,),
        in_specs=[
            pl.BlockSpec((DN_TM, k), row),
            pl.BlockSpec((DN_KC, d), lambda s: (jnp.minimum(s, n_load - 1), 0)),
            pl.BlockSpec((DN_TM, d), row),
        ],
        out_specs=pl.BlockSpec((DN_TM, d), row),
        scratch_shapes=[pltpu.VMEM((k, d), BF16)],
        compiler_params=_params(("arbitrary",)),
        name="ffn_down",
    )(a, w, x)


def _out_proj_norm_kernel(a_ref, w_ref, x_ref, g_ref, o_ref, h_ref, wb_ref):
    @pl.when(pl.program_id(0) == 0)
    def _():
        _cast_resident(w_ref, wb_ref)

    y = x_ref[...] + jnp.dot(a_ref[...], wb_ref[...], preferred_element_type=F32)
    o_ref[...] = y
    h_ref[...] = _rms_rows(y, g_ref[...]).astype(BF16)


def _out_proj_norm(a, w, x, g):
    n, k = a.shape
    d = w.shape[1]
    return pl.pallas_call(
        _out_proj_norm_kernel,
        out_shape=(jax.ShapeDtypeStruct((n, d), F32), jax.ShapeDtypeStruct((n, d), BF16)),
        grid=(n // ROW_TM,),
        in_specs=[
            pl.BlockSpec((ROW_TM, k), lambda i: (i, 0)),
            _resident((k, d)),
            pl.BlockSpec((ROW_TM, d), lambda i: (i, 0)),
            pl.BlockSpec((1, d), lambda i: (0, 0)),
        ],
        out_specs=(pl.BlockSpec((ROW_TM, d), lambda i: (i, 0)), pl.BlockSpec((ROW_TM, d), lambda i: (i, 0))),
        scratch_shapes=[pltpu.VMEM((k, d), BF16)],
        compiler_params=_params(("arbitrary",)),
        name="out_proj",
    )(a, w, x, g)


UP_TM, UP_TN = 2048, 512
UP_RB = 256


def _ffn_up_kernel(h_ref, wg_ref, wu_ref, o_ref):
    wg = wg_ref[...].astype(BF16)
    wu = wu_ref[...].astype(BF16)
    for r in range(UP_TM // UP_RB):
        rows = slice(r * UP_RB, (r + 1) * UP_RB)
        h = h_ref[rows, :]
        gate = jnp.dot(h, wg, preferred_element_type=F32)
        up = jnp.dot(h, wu, preferred_element_type=F32)
        o_ref[rows, :] = (gate * jax.nn.sigmoid(gate) * up).astype(BF16)


def _ffn_up(h, w_gate, w_up):
    n = h.shape[0]
    return pl.pallas_call(
        _ffn_up_kernel,
        out_shape=jax.ShapeDtypeStruct((n, FFN_HID), BF16),
        grid=(n // UP_TM, FFN_HID // UP_TN),
        in_specs=[
            pl.BlockSpec((UP_TM, D_MODEL), lambda i, j: (i, 0)),
            pl.BlockSpec((D_MODEL, UP_TN), lambda i, j: (0, j)),
            pl.BlockSpec((D_MODEL, UP_TN), lambda i, j: (0, j)),
        ],
        out_specs=pl.BlockSpec((UP_TM, UP_TN), lambda i, j: (i, j)),
        compiler_params=_params(("parallel", "arbitrary")),
        name="ffn_up",
    )(h, w_gate, w_up)


def _ple_kernel(x_ref, g_ref, wg_ref, p_ref, wp_ref, o_ref, wgb_ref, wpb_ref):
    @pl.when(pl.program_id(0) == 0)
    def _():
        _cast_resident(wg_ref, wgb_ref)
        _cast_resident(wp_ref, wpb_ref)

    x = x_ref[...]
    h = _rms_rows(x, g_ref[...]).astype(BF16)
    gate = jnp.dot(h, wgb_ref[...], preferred_element_type=F32)
    emb = jnp.dot(p_ref[...].astype(BF16), wpb_ref[...], preferred_element_type=F32)
    o_ref[...] = x + jax.nn.sigmoid(gate) * emb


def _ple(x, g, w_gate, p, w_proj):
    n, d = x.shape
    return pl.pallas_call(
        _ple_kernel,
        out_shape=jax.ShapeDtypeStruct((n, d), F32),
        grid=(n // ROW_TM,),
        in_specs=[
            pl.BlockSpec((ROW_TM, d), lambda i: (i, 0)),
            pl.BlockSpec((1, d), lambda i: (0, 0)),
            _resident((d, d)),
            pl.BlockSpec((ROW_TM, PLE_DIM), lambda i: (i, 0)),
            _resident((PLE_DIM, d)),
        ],
        out_specs=pl.BlockSpec((ROW_TM, d), lambda i: (i, 0)),
        scratch_shapes=[pltpu.VMEM((d, d), BF16), pltpu.VMEM((PLE_DIM, d), BF16)],
        compiler_params=_params(("arbitrary",)),
        name="ple",
    )(x, g, w_gate, p, w_proj)


def _rope_tables(seq):
    inv = 1.0 / (ROPE_THETA ** (jnp.arange(0, DF_DQ, 2, dtype=F32) / DF_DQ))
    ang = jnp.arange(seq, dtype=F32)[:, None] * inv[None, :]
    cos, sin = jnp.cos(ang), jnp.sin(ang)
    return jnp.concatenate([cos, cos, cos, cos], axis=1), jnp.concatenate([-sin, -sin, sin, sin], axis=1)


def _rope_gains(g, scale):
    half = DF_DQ // 2
    g1, g2 = g[:half], g[half:]
    return jnp.stack([jnp.concatenate([g1, g1, g2, g2]), jnp.concatenate([g2, g2, g1, g1])]) * scale


def kernel(x, p, g_mix, w_in, g_na_q, g_na_k, na_rpb, g_df_q, g_df_k, lam_q1, lam_k1, lam_q2, lam_k2,
           g_df_sub, w_na_out, w_df_out, w_o, g_ffn, w_gate, w_up, w_down, g_ple, w_ple_gate, w_ple_proj):
    batch, seq, d = x.shape
    n = batch * seq
    rows = seq // GRID_W
    depth = w_in.shape[0]
    xf = x.reshape(n, d)
    for i in range(depth):
        cos, sin = _rope_tables(seq)
        rq = _rope_gains(g_df_q[i], DF_DQ ** -0.5 * math.log2(math.e))
        rk = _rope_gains(g_df_k[i], 1.0)
        proj = _in_proj(_rms_norm(xf, g_mix[i][None]), w_in[i], g_na_q[i][None], g_na_k[i][None],
                        cos, sin, rq, rk, seq)

        o_a = _na_attn(proj, _na_bias_rows(na_rpb[i], rows), batch, seq)
        lam_p = jnp.stack([lam_q1[i], lam_k1[i], lam_q2[i], lam_k2[i]], axis=0)
        o_b = _df_attn(proj, lam_p, g_df_sub[i][None], batch, seq)

        merged = _merge(o_a, o_b, w_na_out[i], w_df_out[i], proj)
        xf, hf = _out_proj_norm(merged, w_o[i], xf, g_ffn[i][None])

        act = _ffn_up(hf, w_gate[i], w_up[i])
        xf = _ffn_down(act, w_down[i], xf)

        xf = _ple(xf, g_ple[i][None], w_ple_gate[i], p[i].reshape(n, PLE_DIM), w_ple_proj[i])
    return xf.reshape(batch, seq, d)
```

```python
import functools
import math

import numpy as np
import jax
import jax.numpy as jnp
from jax import lax
from jax.experimental import pallas as pl
from jax.experimental.pallas import tpu as pltpu

D_MODEL = 2048
GRID_W = 64
NA_HEADS = 8
NA_DH = 128
NA_KR = 8
NA_KW = 16
DF_HEADS = 8
DF_DQ = 64
DF_DV = 128
FFN_HID = 5632
PLE_DIM = 256
ROPE_THETA = 10000.0
EPS = 1e-6
LAM_INIT = 0.8 - 0.6 * math.exp(-0.3 * 0)

NA_W = NA_HEADS * NA_DH
DF_QW = DF_HEADS * 2 * DF_DQ
DF_VW = DF_HEADS * DF_DV
IN_COLS = 3 * NA_W + 2 * DF_QW + DF_VW + 2 * D_MODEL

LANES = 128
NEG = -1e30
VMEM_LIMIT = 56 * 1024 * 1024

F32 = jnp.float32
BF16 = jnp.bfloat16

CB_NA_Q, CB_NA_K, CB_NA_V = 0, 8, 16
CB_DF_Q, CB_DF_K, CB_DF_V = 24, 32, 40
COL_GATE_A, COL_GATE_B = 6144, 8192


def _params(sem):
    return pltpu.CompilerParams(dimension_semantics=sem, vmem_limit_bytes=VMEM_LIMIT)


def _rms_rows(x, g):
    ms = jnp.mean(x * x, axis=-1, keepdims=True)
    return x * lax.rsqrt(ms + EPS) * g


ONES_ROWS = 16


def _with_ones_rows(v_t):
    row = lax.broadcasted_iota(jnp.int32, (ONES_ROWS, v_t.shape[1]), 0)
    return jnp.concatenate([v_t, jnp.where(row == 0, 1.0, 0.0).astype(v_t.dtype)], axis=0)


NORM_ROWS = 256


def _norm_block(x_ref, g_ref, h_ref):
    def body(c, carry):
        r = pl.ds(pl.multiple_of(c * NORM_ROWS, NORM_ROWS), NORM_ROWS)
        h_ref[r, :] = _rms_rows(x_ref[r, :], g_ref[...]).astype(BF16)
        return carry

    lax.fori_loop(0, x_ref.shape[0] // NORM_ROWS, body, 0)


def _rms_norm_kernel(x_ref, g_ref, o_ref):
    _norm_block(x_ref, g_ref, o_ref)


def _rms_norm(x, g, tm=512):
    n, d = x.shape
    return pl.pallas_call(
        _rms_norm_kernel,
        out_shape=jax.ShapeDtypeStruct((n, d), BF16),
        grid=(n // tm,),
        in_specs=[pl.BlockSpec((tm, d), lambda i: (i, 0)), pl.BlockSpec((1, d), lambda i: (0, 0))],
        out_specs=pl.BlockSpec((tm, d), lambda i: (i, 0)),
        compiler_params=_params(("parallel",)),
        name="rms_norm",
    )(x, g)


CAST_ROWS = 256


def _cast_resident(w_ref, wb_ref):
    rows = min(CAST_ROWS, w_ref.shape[0])

    def body(c, carry):
        r = pl.ds(pl.multiple_of(c * rows, rows), rows)
        wb_ref[r, :] = w_ref[r, :].astype(BF16)
        return carry

    lax.fori_loop(0, w_ref.shape[0] // rows, body, 0)


def _resident(shape):
    return pl.BlockSpec(shape, lambda i: (0, 0), pipeline_mode=pl.Buffered(1))


ROW_TM = 512


IN_TM, IN_TN = 2048, 1024
IN_RB = 256
SEG_TILES = 1024 // IN_TN


def _in_proj_kernel(h_ref, w_ref, gq_ref, gk_ref, cos_ref, sin_ref, rq_ref, rk_ref, o_ref):
    seg = pl.program_id(1) // SEG_TILES
    heads = IN_TN // LANES

    def rotary_layout(w):
        lane = lax.broadcasted_iota(jnp.int32, (w.shape[0], LANES), 1)
        quarter = DF_DQ // 2
        src = (lane // quarter % 2) * DF_DQ + (lane // DF_DQ) * quarter + lane % quarter
        return jnp.concatenate([jnp.take_along_axis(w[:, k * LANES:(k + 1) * LANES], src, axis=1)
                                for k in range(heads)], axis=1)

    def run(w, epilogue):
        wb = w.astype(BF16)
        for r in range(IN_TM // IN_RB):
            rows = slice(r * IN_RB, (r + 1) * IN_RB)
            epilogue(jnp.dot(h_ref[rows, :], wb, preferred_element_type=F32), rows)

    def na_norm(g_ref):
        def epilogue(y, rows):
            for k in range(heads):
                cols = slice(k * LANES, (k + 1) * LANES)
                o_ref[k, rows, :] = _rms_rows(y[:, cols], g_ref[...]).astype(BF16)
        return epilogue

    def df_norm_rope(r_ref):
        def epilogue(y, rows):
            lane = lax.broadcasted_iota(jnp.int32, (1, LANES), 1)
            comp0 = (lane % 64) < 32
            ca = cos_ref[rows, :] * r_ref[0:1, :]
            sb = sin_ref[rows, :] * r_ref[1:2, :]
            for k in range(heads):
                cols = slice(k * LANES, (k + 1) * LANES)
                yk = y[:, cols]
                pk = pltpu.roll(yk, 64, 1)
                z = yk * yk + pk * pk
                s0 = jnp.sum(jnp.where(comp0, z, 0.0), axis=-1, keepdims=True)
                s1 = jnp.sum(jnp.where(comp0, 0.0, z), axis=-1, keepdims=True)
                r = jnp.where(comp0, lax.rsqrt(s0 * (0.5 / DF_DQ) + EPS), lax.rsqrt(s1 * (0.5 / DF_DQ) + EPS))
                o_ref[k, rows, :] = ((yk * ca + pk * sb) * r).astype(BF16)
        return epilogue

    def plain(y, rows):
        for k in range(heads):
            o_ref[k, rows, :] = y[:, k * LANES:(k + 1) * LANES].astype(BF16)

    def gate(y, rows):
        for k in range(heads):
            o_ref[k, rows, :] = jax.nn.sigmoid(y[:, k * LANES:(k + 1) * LANES]).astype(BF16)

    @pl.when(seg == 0)
    def _():
        run(w_ref[...], na_norm(gq_ref))

    @pl.when(seg == 1)
    def _():
        run(w_ref[...], na_norm(gk_ref))

    @pl.when(seg == 3)
    def _():
        run(rotary_layout(w_ref[...]), df_norm_rope(rq_ref))

    @pl.when(seg == 4)
    def _():
        run(rotary_layout(w_ref[...]), df_norm_rope(rk_ref))

    @pl.when((seg == 2) | (seg == 5))
    def _():
        run(w_ref[...], plain)

    @pl.when(seg >= 6)
    def _():
        run(w_ref[...], gate)


def _in_proj(h, w_in, g_na_q, g_na_k, cos, sin, rq, rk, seq):
    n = h.shape[0]
    pos_blocks = seq // IN_TM
    row = lambda i, j: (i, 0)
    const = lambda i, j: (0, 0)
    tab = lambda i, j: (i % pos_blocks, 0)
    return pl.pallas_call(
        _in_proj_kernel,
        out_shape=jax.ShapeDtypeStruct((IN_COLS // LANES, n, LANES), BF16),
        grid=(n // IN_TM, IN_COLS // IN_TN),
        in_specs=[
            pl.BlockSpec((IN_TM, D_MODEL), row),
            pl.BlockSpec((D_MODEL, IN_TN), lambda i, j: (0, j)),
            pl.BlockSpec((1, LANES), const),
            pl.BlockSpec((1, LANES), const),
            pl.BlockSpec((IN_TM, LANES), tab),
            pl.BlockSpec((IN_TM, LANES), tab),
            pl.BlockSpec((2, LANES), const),
            pl.BlockSpec((2, LANES), const),
        ],
        out_specs=pl.BlockSpec((IN_TN // LANES, IN_TM, LANES), lambda i, j: (j, i, 0)),
        compiler_params=_params(("parallel", "arbitrary")),
        name="in_proj",
    )(h, w_in, g_na_q, g_na_k, cos, sin, rq, rk)


NA_QROWS = 8
NA_WROWS = 16
NA_TQ = NA_QROWS * GRID_W
NA_TK = NA_WROWS * GRID_W
NA_PAIR = LANES // GRID_W
NA_NPAIR = NA_QROWS // NA_PAIR
NA_PROWS = NA_KR + NA_PAIR - 1
NA_PK = NA_PROWS * GRID_W
NA_NTAB = 1 + 2 * NA_NPAIR
NA_NEG_SLAB = 2 * NA_KR - 1


def _na_geometry(rows):
    nt = rows // NA_QROWS
    assert nt >= 3 and rows >= NA_WROWS
    ws = np.zeros((3, NA_NPAIR), np.int64)
    idx = np.full((NA_NTAB, NA_PROWS, NA_PAIR), NA_NEG_SLAB, np.int64)
    for ci, t in enumerate((0, 1, nt - 1)):
        w0 = int(np.clip(t * NA_QROWS - NA_KR // 2, 0, rows - NA_WROWS))
        for pi in range(NA_NPAIR):
            r = [t * NA_QROWS + NA_PAIR * pi + hb for hb in range(NA_PAIR)]
            rs = [int(np.clip(ri - NA_KR // 2, 0, rows - NA_KR)) for ri in r]
            start = min(min(rs) - w0, NA_WROWS - NA_PROWS)
            assert 0 <= start and max(rs) + NA_KR <= w0 + start + NA_PROWS and min(rs) >= w0 + start
            ws[ci, pi] = start
            tab = 0 if ci == 1 else 1 + (0 if ci == 0 else NA_NPAIR) + pi
            for wp in range(NA_PROWS):
                key_row = w0 + start + wp
                for hb in range(NA_PAIR):
                    if rs[hb] <= key_row < rs[hb] + NA_KR:
                        idx[tab, wp, hb] = key_row - r[hb] + NA_KR - 1
    combos = sorted({tuple(int(v) for v in pair) for pair in idx.reshape(-1, NA_PAIR)})
    slab = [[combos.index(tuple(int(v) for v in idx[tab, wp])) for wp in range(NA_PROWS)] for tab in range(NA_NTAB)]
    return [[int(v) for v in row] for row in ws], combos, slab


def _na_kernel(q_ref, k_ref, v_ref, row_ref, o_ref, p_ref, slab_ref, vt_ref, *, rows):
    nt = rows // NA_QROWS
    ws, combos, slab = _na_geometry(rows)

    kc = lax.broadcasted_iota(jnp.int32, (GRID_W, LANES), 0)
    lane = lax.broadcasted_iota(jnp.int32, (GRID_W, LANES), 1)
    c = lane % GRID_W
    cs = jnp.clip(c - NA_KW // 2, 0, GRID_W - NA_KW)
    in_window = (kc >= cs) & (kc < cs + NA_KW)
    src = jnp.clip(kc - c + NA_KW - 1, 0, 2 * NA_KW - 2) + jnp.where(lane < GRID_W, 0, GRID_W)
    for ci, (left, right) in enumerate(combos):
        ok = in_window
        if left == NA_NEG_SLAB:
            ok = ok & (lane >= GRID_W)
        if right == NA_NEG_SLAB:
            ok = ok & (lane < GRID_W)
        row = jnp.broadcast_to(row_ref[0, ci:ci + 1, :], (GRID_W, LANES))
        slab_ref[ci] = jnp.where(ok, jnp.take_along_axis(row, src, axis=1), NEG)

    def window(t):
        return int(np.clip(t * NA_QROWS - NA_KR // 2, 0, rows - NA_WROWS)) * GRID_W

    def probs(t):
        cls = 0 if t == 0 else (2 if t == nt - 1 else 1)
        buf = t
        for pi in range(NA_NPAIR):
            lanes = slice(pi * LANES, (pi + 1) * LANES)
            tab = 0 if cls == 1 else 1 + (0 if cls == 0 else NA_NPAIR) + pi
            r0 = ws[cls][pi] * GRID_W
            kp = k_ref[window(t) + r0:window(t) + r0 + NA_PK, :]
            qp = q_ref[t * NA_TQ + pi * LANES:t * NA_TQ + (pi + 1) * LANES, :]
            sp = lax.dot_general(kp, qp, (((1,), (1,)), ((), ())), preferred_element_type=F32)
            bias = jnp.concatenate([slab_ref[ci] for ci in slab[tab]], axis=0)
            sp = sp * (NA_DH ** -0.5) + bias
            m = jnp.max(sp, axis=0, keepdims=True)
            p_ref[buf, r0:r0 + NA_PK, lanes] = jnp.exp(sp - m).astype(BF16)
            if r0 > 0:
                p_ref[buf, 0:r0, lanes] = jnp.zeros((r0, LANES), BF16)
            if r0 + NA_PK < NA_TK:
                p_ref[buf, r0 + NA_PK:NA_TK, lanes] = jnp.zeros((NA_TK - r0 - NA_PK, LANES), BF16)

    for c in range(rows * GRID_W // NA_TK):
        cols = slice(c * NA_TK, (c + 1) * NA_TK)
        vt_ref[:, cols] = _with_ones_rows(v_ref[cols, :].T)

    def outputs(t):
        vw = vt_ref[:, window(t):window(t) + NA_TK]
        ov = jnp.dot(vw, p_ref[t], preferred_element_type=F32)
        o_ref[t * NA_TQ:(t + 1) * NA_TQ, :] = (ov[0:NA_DH, :] / ov[NA_DH:NA_DH + 1, :]).T.astype(BF16)

    for t in range(nt + 1):
        if t < nt:
            probs(t)
        if t >= 1:
            outputs(t - 1)


def _na_attn(proj, bias_rows, batch, seq):
    rows = seq // GRID_W
    n_slabs = bias_rows.shape[1]
    return pl.pallas_call(
        functools.partial(_na_kernel, rows=rows),
        out_shape=jax.ShapeDtypeStruct((NA_HEADS, batch * seq, LANES), BF16),
        grid=(batch, NA_HEADS),
        in_specs=[
            pl.BlockSpec((None, seq, LANES), lambda b, h: (CB_NA_Q + h, b, 0)),
            pl.BlockSpec((None, seq, LANES), lambda b, h: (CB_NA_K + h, b, 0)),
            pl.BlockSpec((None, seq, LANES), lambda b, h: (CB_NA_V + h, b, 0)),
            pl.BlockSpec((1, n_slabs, LANES), lambda b, h: (h, 0, 0)),
        ],
        out_specs=pl.BlockSpec((None, seq, LANES), lambda b, h: (h, b, 0)),
        scratch_shapes=[pltpu.VMEM((rows // NA_QROWS, NA_TK, NA_TQ), BF16),
                        pltpu.VMEM((n_slabs, GRID_W, LANES), F32),
                        pltpu.VMEM((NA_DH + ONES_ROWS, seq), BF16)],
        compiler_params=_params(("parallel", "arbitrary")),
        name="na_attn",
    )(proj, proj, proj, bias_rows)


def _na_bias_rows(rpb, rows):
    _, combos, _ = _na_geometry(rows)
    padded = jnp.pad(rpb, ((0, 0), (0, 1), (0, GRID_W - (2 * NA_KW - 1))))
    left = jnp.take(padded, jnp.asarray([cb[0] for cb in combos], jnp.int32), axis=1)
    right = jnp.take(padded, jnp.asarray([cb[1] for cb in combos], jnp.int32), axis=1)
    return jnp.concatenate([left, right], axis=-1)


DF_TQ, DF_TK = 1024, 512
DF_TG = 256
DF_SBUF = 2
DF_VROWS = DF_DV + ONES_ROWS


def _df_kernel(q_ref, k_ref, v_ref, lam_ref, gsub_ref, o_ref, q12_ref, q12n_ref, s_ref, acc_ref, vt_ref, *, seq):
    n_chunks = seq // DF_TK
    n_blocks = seq // DF_TQ
    groups = 2 * DF_TQ // DF_TG

    for c in range(n_chunks):
        cols = slice(c * DF_TK, (c + 1) * DF_TK)
        vt_ref[:, cols] = _with_ones_rows(v_ref[cols, :].T)

    lane = lax.broadcasted_iota(jnp.int32, (1, LANES), 1)
    comp0 = (lane % 64) < 32

    def split_maps(block, dst_ref):
        q = q_ref[pl.ds(pl.multiple_of(block * DF_TQ, DF_TQ), DF_TQ), :]
        zero = jnp.zeros_like(q)
        dst_ref[0:DF_TQ, :] = jnp.where(comp0, q, zero)
        dst_ref[DF_TQ:2 * DF_TQ, :] = jnp.where(comp0, zero, q)

    def scores(kc, q_src_ref, g):
        cols = slice(g * DF_TG, (g + 1) * DF_TG)
        return lax.dot_general(kc, q_src_ref[cols, :], (((1,), (1,)), ((), ())), preferred_element_type=F32)

    def keys(c):
        return k_ref[c * DF_TK:(c + 1) * DF_TK, :]

    def step(c, carry):
        cur, nxt = c % DF_SBUF, (c + 1) % DF_SBUF
        vc = vt_ref[:, c * DF_TK:(c + 1) * DF_TK]
        kn, q_next = (keys(c + 1), q12_ref) if c + 1 < n_chunks else (keys(0), q12n_ref)
        out = []
        for g in range(groups):
            m_prev = carry[g]
            cols = slice(g * DF_TG, (g + 1) * DF_TG)
            s_ref[nxt, :, cols] = scores(kn, q_next, g)
            s = s_ref[cur, :, cols]
            m_new = jnp.maximum(m_prev, jnp.max(s, axis=0, keepdims=True))
            alpha = jnp.exp2(m_prev - m_new)
            p = jnp.exp2(s - m_new).astype(BF16)
            pv = jnp.dot(vc, p, preferred_element_type=F32)
            acc_ref[:, cols] = alpha * acc_ref[:, cols] + pv
            out.append(m_new)
        return tuple(out)

    lp = lam_ref[...]
    lam = (jnp.exp(jnp.sum(lp[0:1] * lp[1:2], axis=-1, keepdims=True))
           - jnp.exp(jnp.sum(lp[2:3] * lp[3:4], axis=-1, keepdims=True)) + LAM_INIT)

    def finish(block):
        o12 = acc_ref[0:DF_DV, :] / acc_ref[DF_DV:DF_DV + 1, :]
        o_t = o12[:, 0:DF_TQ] - lam * o12[:, DF_TQ:2 * DF_TQ]
        rows = pl.ds(pl.multiple_of(block * DF_TQ, DF_TQ), DF_TQ)
        o_ref[rows, :] = (_rms_rows(o_t.T, gsub_ref[...]) * (1.0 - LAM_INIT)).astype(BF16)

    assert n_chunks % DF_SBUF == 0
    split_maps(0, q12n_ref)
    k0 = keys(0)
    for g in range(groups):
        s_ref[0, :, g * DF_TG:(g + 1) * DF_TG] = scores(k0, q12n_ref, g)
    acc_ref[...] = jnp.zeros(acc_ref.shape, F32)
    acc_ref[DF_DV:DF_DV + 1, :] = jnp.ones((1, 2 * DF_TQ), F32)

    def body(block, carry):
        finish(jnp.maximum(block - 1, 0))
        acc_ref[...] = jnp.zeros(acc_ref.shape, F32)
        split_maps(block, q12_ref)
        split_maps(jnp.minimum(block + 1, n_blocks - 1), q12n_ref)
        m = tuple(jnp.full((1, DF_TG), -jnp.inf, F32) for _ in range(groups))
        for c in range(n_chunks):
            m = step(c, m)
        return carry

    lax.fori_loop(0, n_blocks, body, 0)
    finish(n_blocks - 1)


def _df_attn(proj, lam_p, g_sub, batch, seq):
    const = lambda b, h: (0, 0)
    return pl.pallas_call(
        functools.partial(_df_kernel, seq=seq),
        out_shape=jax.ShapeDtypeStruct((DF_HEADS, batch * seq, LANES), BF16),
        grid=(batch, DF_HEADS),
        in_specs=[
            pl.BlockSpec((None, seq, LANES), lambda b, h: (CB_DF_Q + h, b, 0)),
            pl.BlockSpec((None, seq, LANES), lambda b, h: (CB_DF_K + h, b, 0)),
            pl.BlockSpec((None, seq, LANES), lambda b, h: (CB_DF_V + h, b, 0)),
            pl.BlockSpec((4, DF_DQ), const),
            pl.BlockSpec((1, DF_DV), const),
        ],
        out_specs=pl.BlockSpec((None, seq, LANES), lambda b, h: (h, b, 0)),
        scratch_shapes=[pltpu.VMEM((2 * DF_TQ, LANES), BF16), pltpu.VMEM((2 * DF_TQ, LANES), BF16),
                        pltpu.VMEM((DF_SBUF, DF_TK, 2 * DF_TQ), F32),
                        pltpu.VMEM((DF_VROWS, 2 * DF_TQ), F32), pltpu.VMEM((DF_VROWS, seq), BF16)],
        compiler_params=_params(("parallel", "parallel")),
        name="df_attn",
    )(proj, proj, proj, lam_p, g_sub)


def _merge_kernel(oa_ref, ob_ref, wa_ref, wb_ref, sa_ref, sb_ref, o_ref, wab_ref, wbb_ref):
    @pl.when(pl.program_id(0) == 0)
    def _():
        _cast_resident(wa_ref, wab_ref)
        _cast_resident(wb_ref, wbb_ref)

    def rows(slab_ref):
        return jnp.concatenate([slab_ref[k] for k in range(slab_ref.shape[0])], axis=1)

    ya = jnp.dot(rows(oa_ref), wab_ref[...], preferred_element_type=F32)
    yb = jnp.dot(rows(ob_ref), wbb_ref[...], preferred_element_type=F32)
    o_ref[...] = (rows(sa_ref).astype(F32) * ya + rows(sb_ref).astype(F32) * yb).astype(BF16)


def _merge(o_a, o_b, w_na_out, w_df_out, proj):
    n = o_a.shape[1]
    gate_slabs = D_MODEL // LANES
    ga, gb = COL_GATE_A // D_MODEL, COL_GATE_B // D_MODEL
    return pl.pallas_call(
        _merge_kernel,
        out_shape=jax.ShapeDtypeStruct((n, D_MODEL), BF16),
        grid=(n // ROW_TM,),
        in_specs=[
            pl.BlockSpec((NA_HEADS, ROW_TM, LANES), lambda i: (0, i, 0)),
            pl.BlockSpec((DF_HEADS, ROW_TM, LANES), lambda i: (0, i, 0)),
            _resident((NA_W, D_MODEL)),
            _resident((DF_VW, D_MODEL)),
            pl.BlockSpec((gate_slabs, ROW_TM, LANES), lambda i: (ga, i, 0)),
            pl.BlockSpec((gate_slabs, ROW_TM, LANES), lambda i: (gb, i, 0)),
        ],
        out_specs=pl.BlockSpec((ROW_TM, D_MODEL), lambda i: (i, 0)),
        scratch_shapes=[pltpu.VMEM((NA_W, D_MODEL), BF16), pltpu.VMEM((DF_VW, D_MODEL), BF16)],
        compiler_params=_params(("arbitrary",)),
        name="merge",
    )(o_a, o_b, w_na_out, w_df_out, proj, proj)


DN_TM = 256
DN_KC = 512


def _ffn_down_kernel(a_ref, w_ref, x_ref, o_ref, wb_ref, *, n_load):
    s = pl.program_id(0)

    @pl.when(s < n_load)
    def _():
        r = pl.ds(pl.multiple_of(s * DN_KC, DN_KC), DN_KC)
        wb_ref[r, :] = w_ref[...].astype(BF16)

    @pl.when(s >= n_load)
    def _():
        o_ref[...] = x_ref[...] + jnp.dot(a_ref[...], wb_ref[...], preferred_element_type=F32)


def _ffn_down(a, w, x):
    n, k = a.shape
    d = w.shape[1]
    n_load = k // DN_KC
    row = lambda s: (jnp.maximum(s - n_load, 0), 0)
    return pl.pallas_call(
        functools.partial(_ffn_down_kernel, n_load=n_load),
        out_shape=jax.ShapeDtypeStruct((n, d), F32),
        grid=(n_load + n // DN_TM,),
        in_specs=[
            pl.BlockSpec((DN_TM, k), row),
            pl.BlockSpec((DN_KC, d), lambda s: (jnp.minimum(s, n_load - 1), 0)),
            pl.BlockSpec((DN_TM, d), row),
        ],
        out_specs=pl.BlockSpec((DN_TM, d), row),
        scratch_shapes=[pltpu.VMEM((k, d), BF16)],
        compiler_params=_params(("arbitrary",)),
        name="ffn_down",
    )(a, w, x)


def _out_proj_norm_kernel(a_ref, w_ref, x_ref, g_ref, o_ref, h_ref, wb_ref):
    @pl.when(pl.program_id(0) == 0)
    def _():
        _cast_resident(w_ref, wb_ref)

    y = x_ref[...] + jnp.dot(a_ref[...], wb_ref[...], preferred_element_type=F32)
    o_ref[...] = y
    h_ref[...] = _rms_rows(y, g_ref[...]).astype(BF16)


def _out_proj_norm(a, w, x, g):
    n, k = a.shape
    d = w.shape[1]
    return pl.pallas_call(
        _out_proj_norm_kernel,
        out_shape=(jax.ShapeDtypeStruct((n, d), F32), jax.ShapeDtypeStruct((n, d), BF16)),
        grid=(n // ROW_TM,),
        in_specs=[
            pl.BlockSpec((ROW_TM, k), lambda i: (i, 0)),
            _resident((k, d)),
            pl.BlockSpec((ROW_TM, d), lambda i: (i, 0)),
            pl.BlockSpec((1, d), lambda i: (0, 0)),
        ],
        out_specs=(pl.BlockSpec((ROW_TM, d), lambda i: (i, 0)), pl.BlockSpec((ROW_TM, d), lambda i: (i, 0))),
        scratch_shapes=[pltpu.VMEM((k, d), BF16)],
        compiler_params=_params(("arbitrary",)),
        name="out_proj",
    )(a, w, x, g)


UP_TM, UP_TN = 2048, 512
UP_RB = 256


def _ffn_up_kernel(h_ref, wg_ref, wu_ref, o_ref):
    wg = wg_ref[...].astype(BF16)
    wu = wu_ref[...].astype(BF16)
    for r in range(UP_TM // UP_RB):
        rows = slice(r * UP_RB, (r + 1) * UP_RB)
        h = h_ref[rows, :]
        gate = jnp.dot(h, wg, preferred_element_type=F32)
        up = jnp.dot(h, wu, preferred_element_type=F32)
        o_ref[rows, :] = (gate * jax.nn.sigmoid(gate) * up).astype(BF16)


def _ffn_up(h, w_gate, w_up):
    n = h.shape[0]
    return pl.pallas_call(
        _ffn_up_kernel,
        out_shape=jax.ShapeDtypeStruct((n, FFN_HID), BF16),
        grid=(n // UP_TM, FFN_HID // UP_TN),
        in_specs=[
            pl.BlockSpec((UP_TM, D_MODEL), lambda i, j: (i, 0)),
            pl.BlockSpec((D_MODEL, UP_TN), lambda i, j: (0, j)),
            pl.BlockSpec((D_MODEL, UP_TN), lambda i, j: (0, j)),
        ],
        out_specs=pl.BlockSpec((UP_TM, UP_TN), lambda i, j: (i, j)),
        compiler_params=_params(("parallel", "arbitrary")),
        name="ffn_up",
    )(h, w_gate, w_up)


def _ple_kernel(x_ref, g_ref, wg_ref, p_ref, wp_ref, o_ref, wgb_ref, wpb_ref):
    @pl.when(pl.program_id(0) == 0)
    def _():
        _cast_resident(wg_ref, wgb_ref)
        _cast_resident(wp_ref, wpb_ref)

    x = x_ref[...]
    h = _rms_rows(x, g_ref[...]).astype(BF16)
    gate = jnp.dot(h, wgb_ref[...], preferred_element_type=F32)
    emb = jnp.dot(p_ref[...].astype(BF16), wpb_ref[...], preferred_element_type=F32)
    o_ref[...] = x + jax.nn.sigmoid(gate) * emb


def _ple(x, g, w_gate, p, w_proj):
    n, d = x.shape
    return pl.pallas_call(
        _ple_kernel,
        out_shape=jax.ShapeDtypeStruct((n, d), F32),
        grid=(n // ROW_TM,),
        in_specs=[
            pl.BlockSpec((ROW_TM, d), lambda i: (i, 0)),
            pl.BlockSpec((1, d), lambda i: (0, 0)),
            _resident((d, d)),
            pl.BlockSpec((ROW_TM, PLE_DIM), lambda i: (i, 0)),
            _resident((PLE_DIM, d)),
        ],
        out_specs=pl.BlockSpec((ROW_TM, d), lambda i: (i, 0)),
        scratch_shapes=[pltpu.VMEM((d, d), BF16), pltpu.VMEM((PLE_DIM, d), BF16)],
        compiler_params=_params(("arbitrary",)),
        name="ple",
    )(x, g, w_gate, p, w_proj)


def _rope_tables(seq):
    inv = 1.0 / (ROPE_THETA ** (jnp.arange(0, DF_DQ, 2, dtype=F32) / DF_DQ))
    ang = jnp.arange(seq, dtype=F32)[:, None] * inv[None, :]
    cos, sin = jnp.cos(ang), jnp.sin(ang)
    return jnp.concatenate([cos, cos, cos, cos], axis=1), jnp.concatenate([-sin, -sin, sin, sin], axis=1)


def _rope_gains(g, scale):
    half = DF_DQ // 2
    g1, g2 = g[:half], g[half:]
    return jnp.stack([jnp.concatenate([g1, g1, g2, g2]), jnp.concatenate([g2, g2, g1, g1])]) * scale


def kernel(x, p, g_mix, w_in, g_na_q, g_na_k, na_rpb, g_df_q, g_df_k, lam_q1, lam_k1, lam_q2, lam_k2,
           g_df_sub, w_na_out, w_df_out, w_o, g_ffn, w_gate, w_up, w_down, g_ple, w_ple_gate, w_ple_proj):
    batch, seq, d = x.shape
    n = batch * seq
    rows = seq // GRID_W
    depth = w_in.shape[0]
    xf = x.reshape(n, d)
    for i in range(depth):
        cos, sin = _rope_tables(seq)
        rq = _rope_gains(g_df_q[i], DF_DQ ** -0.5 * math.log2(math.e))
        rk = _rope_gains(g_df_k[i], 1.0)
        proj = _in_proj(_rms_norm(xf, g_mix[i][None]), w_in[i], g_na_q[i][None], g_na_k[i][None],
                        cos, sin, rq, rk, seq)

        o_a = _na_attn(proj, _na_bias_rows(na_rpb[i], rows), batch, seq)
        lam_p = jnp.stack([lam_q1[i], lam_k1[i], lam_q2[i], lam_k2[i]], axis=0)
        o_b = _df_attn(proj, lam_p, g_df_sub[i][None], batch, seq)

        merged = _merge(o_a, o_b, w_na_out[i], w_df_out[i], proj)
        xf, hf = _out_proj_norm(merged, w_o[i], xf, g_ffn[i][None])

        act = _ffn_up(hf, w_gate[i], w_up[i])
        xf = _ffn_down(act, w_down[i], xf)

        xf = _ple(xf, g_ple[i][None], w_ple_gate[i], p[i].reshape(n, PLE_DIM), w_ple_proj[i])
    return xf.reshape(batch, seq, d)
```

```python
import functools
import math

import numpy as np
import jax
import jax.numpy as jnp
from jax import lax
from jax.experimental import pallas as pl
from jax.experimental.pallas import tpu as pltpu

D_MODEL = 2048
GRID_W = 64
NA_HEADS = 8
NA_DH = 128
NA_KR = 8
NA_KW = 16
DF_HEADS = 8
DF_DQ = 64
DF_DV = 128
FFN_HID = 5632
PLE_DIM = 256
ROPE_THETA = 10000.0
EPS = 1e-6
LAM_INIT = 0.8 - 0.6 * math.exp(-0.3 * 0)

NA_W = NA_HEADS * NA_DH
DF_QW = DF_HEADS * 2 * DF_DQ
DF_VW = DF_HEADS * DF_DV
IN_COLS = 3 * NA_W + 2 * DF_QW + DF_VW + 2 * D_MODEL

LANES = 128
NEG = -1e30
VMEM_LIMIT = 56 * 1024 * 1024

F32 = jnp.float32
BF16 = jnp.bfloat16

CB_NA_Q, CB_NA_K, CB_NA_V = 0, 8, 16
CB_DF_Q, CB_DF_K, CB_DF_V = 24, 32, 40
COL_GATE_A, COL_GATE_B = 6144, 8192


def _params(sem):
    return pltpu.CompilerParams(dimension_semantics=sem, vmem_limit_bytes=VMEM_LIMIT)


def _rms_rows(x, g):
    ms = jnp.mean(x * x, axis=-1, keepdims=True)
    return x * lax.rsqrt(ms + EPS) * g


ONES_ROWS = 16


def _with_ones_rows(v_t):
    row = lax.broadcasted_iota(jnp.int32, (ONES_ROWS, v_t.shape[1]), 0)
    return jnp.concatenate([v_t, jnp.where(row == 0, 1.0, 0.0).astype(v_t.dtype)], axis=0)


NORM_ROWS = 256


def _norm_block(x_ref, g_ref, h_ref):
    def body(c, carry):
        r = pl.ds(pl.multiple_of(c * NORM_ROWS, NORM_ROWS), NORM_ROWS)
        h_ref[r, :] = _rms_rows(x_ref[r, :], g_ref[...]).astype(BF16)
        return carry

    lax.fori_loop(0, x_ref.shape[0] // NORM_ROWS, body, 0)


CAST_ROWS = 256


def _cast_resident(w_ref, wb_ref):
    rows = min(CAST_ROWS, w_ref.shape[0])

    def body(c, carry):
        r = pl.ds(pl.multiple_of(c * rows, rows), rows)
        wb_ref[r, :] = w_ref[r, :].astype(BF16)
        return carry

    lax.fori_loop(0, w_ref.shape[0] // rows, body, 0)


def _resident(shape):
    return pl.BlockSpec(shape, lambda i: (0, 0), pipeline_mode=pl.Buffered(1))


ROW_TM = 512


IN_TM, IN_TN = 2048, 512
IN_RB = 256
SEG_TILES = 1024 // IN_TN


def _in_proj_kernel(x_ref, gmix_ref, w_ref, gq_ref, gk_ref, cos_ref, sin_ref, rq_ref, rk_ref, o_ref, h_ref):
    @pl.when(pl.program_id(1) == 0)
    def _():
        _norm_block(x_ref, gmix_ref, h_ref)

    seg = pl.program_id(1) // SEG_TILES
    heads = IN_TN // LANES

    def rotary_layout(w):
        lane = lax.broadcasted_iota(jnp.int32, (w.shape[0], LANES), 1)
        quarter = DF_DQ // 2
        src = (lane // quarter % 2) * DF_DQ + (lane // DF_DQ) * quarter + lane % quarter
        return jnp.concatenate([jnp.take_along_axis(w[:, k * LANES:(k + 1) * LANES], src, axis=1)
                                for k in range(heads)], axis=1)

    def run(w, epilogue):
        wb = w.astype(BF16)
        for r in range(IN_TM // IN_RB):
            rows = slice(r * IN_RB, (r + 1) * IN_RB)
            epilogue(jnp.dot(h_ref[rows, :], wb, preferred_element_type=F32), rows)

    def na_norm(g_ref):
        def epilogue(y, rows):
            for k in range(heads):
                cols = slice(k * LANES, (k + 1) * LANES)
                o_ref[k, rows, :] = _rms_rows(y[:, cols], g_ref[...]).astype(BF16)
        return epilogue

    def df_norm_rope(r_ref):
        def epilogue(y, rows):
            lane = lax.broadcasted_iota(jnp.int32, (1, LANES), 1)
            comp0 = (lane % 64) < 32
            ca = cos_ref[rows, :] * r_ref[0:1, :]
            sb = sin_ref[rows, :] * r_ref[1:2, :]
            for k in range(heads):
                cols = slice(k * LANES, (k + 1) * LANES)
                yk = y[:, cols]
                pk = pltpu.roll(yk, 64, 1)
                z = yk * yk + pk * pk
                s0 = jnp.sum(jnp.where(comp0, z, 0.0), axis=-1, keepdims=True)
                s1 = jnp.sum(jnp.where(comp0, 0.0, z), axis=-1, keepdims=True)
                r = jnp.where(comp0, lax.rsqrt(s0 * (0.5 / DF_DQ) + EPS), lax.rsqrt(s1 * (0.5 / DF_DQ) + EPS))
                o_ref[k, rows, :] = ((yk * ca + pk * sb) * r).astype(BF16)
        return epilogue

    def plain(y, rows):
        for k in range(heads):
            o_ref[k, rows, :] = y[:, k * LANES:(k + 1) * LANES].astype(BF16)

    def gate(y, rows):
        for k in range(heads):
            o_ref[k, rows, :] = jax.nn.sigmoid(y[:, k * LANES:(k + 1) * LANES]).astype(BF16)

    @pl.when(seg == 0)
    def _():
        run(w_ref[...], na_norm(gq_ref))

    @pl.when(seg == 1)
    def _():
        run(w_ref[...], na_norm(gk_ref))

    @pl.when(seg == 3)
    def _():
        run(rotary_layout(w_ref[...]), df_norm_rope(rq_ref))

    @pl.when(seg == 4)
    def _():
        run(rotary_layout(w_ref[...]), df_norm_rope(rk_ref))

    @pl.when((seg == 2) | (seg == 5))
    def _():
        run(w_ref[...], plain)

    @pl.when(seg >= 6)
    def _():
        run(w_ref[...], gate)


def _in_proj(x, g_mix, w_in, g_na_q, g_na_k, cos, sin, rq, rk, seq):
    n = x.shape[0]
    pos_blocks = seq // IN_TM
    row = lambda i, j: (i, 0)
    const = lambda i, j: (0, 0)
    tab = lambda i, j: (i % pos_blocks, 0)
    return pl.pallas_call(
        _in_proj_kernel,
        out_shape=jax.ShapeDtypeStruct((IN_COLS // LANES, n, LANES), BF16),
        grid=(n // IN_TM, IN_COLS // IN_TN),
        in_specs=[
            pl.BlockSpec((IN_TM, D_MODEL), row, pipeline_mode=pl.Buffered(1)),
            pl.BlockSpec((1, D_MODEL), const),
            pl.BlockSpec((D_MODEL, IN_TN), lambda i, j: (0, j)),
            pl.BlockSpec((1, LANES), const),
            pl.BlockSpec((1, LANES), const),
            pl.BlockSpec((IN_TM, LANES), tab),
            pl.BlockSpec((IN_TM, LANES), tab),
            pl.BlockSpec((2, LANES), const),
            pl.BlockSpec((2, LANES), const),
        ],
        out_specs=pl.BlockSpec((IN_TN // LANES, IN_TM, LANES), lambda i, j: (j, i, 0)),
        scratch_shapes=[pltpu.VMEM((IN_TM, D_MODEL), BF16)],
        compiler_params=_params(("parallel", "arbitrary")),
        name="in_proj",
    )(x, g_mix, w_in, g_na_q, g_na_k, cos, sin, rq, rk)


NA_QROWS = 8
NA_WROWS = 16
NA_TQ = NA_QROWS * GRID_W
NA_TK = NA_WROWS * GRID_W
NA_PAIR = LANES // GRID_W
NA_NPAIR = NA_QROWS // NA_PAIR
NA_PROWS = NA_KR + NA_PAIR - 1
NA_PK = NA_PROWS * GRID_W
NA_NTAB = 1 + 2 * NA_NPAIR
NA_NEG_SLAB = 2 * NA_KR - 1


def _na_geometry(rows):
    nt = rows // NA_QROWS
    assert nt >= 3 and rows >= NA_WROWS
    ws = np.zeros((3, NA_NPAIR), np.int64)
    idx = np.full((NA_NTAB, NA_PROWS, NA_PAIR), NA_NEG_SLAB, np.int64)
    for ci, t in enumerate((0, 1, nt - 1)):
        w0 = int(np.clip(t * NA_QROWS - NA_KR // 2, 0, rows - NA_WROWS))
        for pi in range(NA_NPAIR):
            r = [t * NA_QROWS + NA_PAIR * pi + hb for hb in range(NA_PAIR)]
            rs = [int(np.clip(ri - NA_KR // 2, 0, rows - NA_KR)) for ri in r]
            start = min(min(rs) - w0, NA_WROWS - NA_PROWS)
            assert 0 <= start and max(rs) + NA_KR <= w0 + start + NA_PROWS and min(rs) >= w0 + start
            ws[ci, pi] = start
            tab = 0 if ci == 1 else 1 + (0 if ci == 0 else NA_NPAIR) + pi
            for wp in range(NA_PROWS):
                key_row = w0 + start + wp
                for hb in range(NA_PAIR):
                    if rs[hb] <= key_row < rs[hb] + NA_KR:
                        idx[tab, wp, hb] = key_row - r[hb] + NA_KR - 1
    combos = sorted({tuple(int(v) for v in pair) for pair in idx.reshape(-1, NA_PAIR)})
    slab = [[combos.index(tuple(int(v) for v in idx[tab, wp])) for wp in range(NA_PROWS)] for tab in range(NA_NTAB)]
    return [[int(v) for v in row] for row in ws], combos, slab


def _na_kernel(q_ref, k_ref, v_ref, row_ref, o_ref, p_ref, slab_ref, vt_ref, *, rows):
    nt = rows // NA_QROWS
    ws, combos, slab = _na_geometry(rows)

    kc = lax.broadcasted_iota(jnp.int32, (GRID_W, LANES), 0)
    lane = lax.broadcasted_iota(jnp.int32, (GRID_W, LANES), 1)
    c = lane % GRID_W
    cs = jnp.clip(c - NA_KW // 2, 0, GRID_W - NA_KW)
    in_window = (kc >= cs) & (kc < cs + NA_KW)
    src = jnp.clip(kc - c + NA_KW - 1, 0, 2 * NA_KW - 2) + jnp.where(lane < GRID_W, 0, GRID_W)
    for ci, (left, right) in enumerate(combos):
        ok = in_window
        if left == NA_NEG_SLAB:
            ok = ok & (lane >= GRID_W)
        if right == NA_NEG_SLAB:
            ok = ok & (lane < GRID_W)
        row = jnp.broadcast_to(row_ref[0, ci:ci + 1, :], (GRID_W, LANES))
        slab_ref[ci] = jnp.where(ok, jnp.take_along_axis(row, src, axis=1), NEG)

    def window(t):
        return int(np.clip(t * NA_QROWS - NA_KR // 2, 0, rows - NA_WROWS)) * GRID_W

    def probs(t):
        cls = 0 if t == 0 else (2 if t == nt - 1 else 1)
        buf = t
        for pi in range(NA_NPAIR):
            lanes = slice(pi * LANES, (pi + 1) * LANES)
            tab = 0 if cls == 1 else 1 + (0 if cls == 0 else NA_NPAIR) + pi
            r0 = ws[cls][pi] * GRID_W
            kp = k_ref[window(t) + r0:window(t) + r0 + NA_PK, :]
            qp = q_ref[t * NA_TQ + pi * LANES:t * NA_TQ + (pi + 1) * LANES, :]
            sp = lax.dot_general(kp, qp, (((1,), (1,)), ((), ())), preferred_element_type=F32)
            bias = jnp.concatenate([slab_ref[ci] for ci in slab[tab]], axis=0)
            sp = sp * (NA_DH ** -0.5) + bias
            m = jnp.max(sp, axis=0, keepdims=True)
            p_ref[buf, r0:r0 + NA_PK, lanes] = jnp.exp(sp - m).astype(BF16)
            if r0 > 0:
                p_ref[buf, 0:r0, lanes] = jnp.zeros((r0, LANES), BF16)
            if r0 + NA_PK < NA_TK:
                p_ref[buf, r0 + NA_PK:NA_TK, lanes] = jnp.zeros((NA_TK - r0 - NA_PK, LANES), BF16)

    for c in range(rows * GRID_W // NA_TK):
        cols = slice(c * NA_TK, (c + 1) * NA_TK)
        vt_ref[:, cols] = _with_ones_rows(v_ref[cols, :].T)

    def outputs(t):
        vw = vt_ref[:, window(t):window(t) + NA_TK]
        ov = jnp.dot(vw, p_ref[t], preferred_element_type=F32)
        o_ref[t * NA_TQ:(t + 1) * NA_TQ, :] = (ov[0:NA_DH, :] / ov[NA_DH:NA_DH + 1, :]).T.astype(BF16)

    for t in range(nt + 1):
        if t < nt:
            probs(t)
        if t >= 1:
            outputs(t - 1)


def _na_attn(proj, bias_rows, batch, seq):
    rows = seq // GRID_W
    n_slabs = bias_rows.shape[1]
    return pl.pallas_call(
        functools.partial(_na_kernel, rows=rows),
        out_shape=jax.ShapeDtypeStruct((NA_HEADS, batch * seq, LANES), BF16),
        grid=(batch, NA_HEADS),
        in_specs=[
            pl.BlockSpec((None, seq, LANES), lambda b, h: (CB_NA_Q + h, b, 0)),
            pl.BlockSpec((None, seq, LANES), lambda b, h: (CB_NA_K + h, b, 0)),
            pl.BlockSpec((None, seq, LANES), lambda b, h: (CB_NA_V + h, b, 0)),
            pl.BlockSpec((1, n_slabs, LANES), lambda b, h: (h, 0, 0)),
        ],
        out_specs=pl.BlockSpec((None, seq, LANES), lambda b, h: (h, b, 0)),
        scratch_shapes=[pltpu.VMEM((rows // NA_QROWS, NA_TK, NA_TQ), BF16),
                        pltpu.VMEM((n_slabs, GRID_W, LANES), F32),
                        pltpu.VMEM((NA_DH + ONES_ROWS, seq), BF16)],
        compiler_params=_params(("parallel", "arbitrary")),
        name="na_attn",
    )(proj, proj, proj, bias_rows)


def _na_bias_rows(rpb, rows):
    _, combos, _ = _na_geometry(rows)
    padded = jnp.pad(rpb, ((0, 0), (0, 1), (0, GRID_W - (2 * NA_KW - 1))))
    left = jnp.take(padded, jnp.asarray([cb[0] for cb in combos], jnp.int32), axis=1)
    right = jnp.take(padded, jnp.asarray([cb[1] for cb in combos], jnp.int32), axis=1)
    return jnp.concatenate([left, right], axis=-1)


DF_TQ, DF_TK = 1024, 512
DF_TG = 256
DF_SBUF = 2
DF_VROWS = DF_DV + ONES_ROWS


def _df_kernel(q_ref, k_ref, v_ref, lam_ref, gsub_ref, o_ref, q12_ref, q12n_ref, s_ref, acc_ref, vt_ref, *, seq):
    n_chunks = seq // DF_TK
    n_blocks = seq // DF_TQ
    groups = 2 * DF_TQ // DF_TG

    for c in range(n_chunks):
        cols = slice(c * DF_TK, (c + 1) * DF_TK)
        vt_ref[:, cols] = _with_ones_rows(v_ref[cols, :].T)

    lane = lax.broadcasted_iota(jnp.int32, (1, LANES), 1)
    comp0 = (lane % 64) < 32

    def split_maps(block, dst_ref):
        q = q_ref[pl.ds(pl.multiple_of(block * DF_TQ, DF_TQ), DF_TQ), :]
        zero = jnp.zeros_like(q)
        dst_ref[0:DF_TQ, :] = jnp.where(comp0, q, zero)
        dst_ref[DF_TQ:2 * DF_TQ, :] = jnp.where(comp0, zero, q)

    def scores(kc, q_src_ref, g):
        cols = slice(g * DF_TG, (g + 1) * DF_TG)
        return lax.dot_general(kc, q_src_ref[cols, :], (((1,), (1,)), ((), ())), preferred_element_type=F32)

    def keys(c):
        return k_ref[c * DF_TK:(c + 1) * DF_TK, :]

    def step(c, carry):
        cur, nxt = c % DF_SBUF, (c + 1) % DF_SBUF
        vc = vt_ref[:, c * DF_TK:(c + 1) * DF_TK]
        kn, q_next = (keys(c + 1), q12_ref) if c + 1 < n_chunks else (keys(0), q12n_ref)
        out = []
        for g in range(groups):
            m_prev = carry[g]
            cols = slice(g * DF_TG, (g + 1) * DF_TG)
            s_ref[nxt, :, cols] = scores(kn, q_next, g)
            s = s_ref[cur, :, cols]
            m_new = jnp.maximum(m_prev, jnp.max(s, axis=0, keepdims=True))
            alpha = jnp.exp2(m_prev - m_new)
            p = jnp.exp2(s - m_new).astype(BF16)
            pv = jnp.dot(vc, p, preferred_element_type=F32)
            acc_ref[:, cols] = alpha * acc_ref[:, cols] + pv
            out.append(m_new)
        return tuple(out)

    lp = lam_ref[...]
    lam = (jnp.exp(jnp.sum(lp[0:1] * lp[1:2], axis=-1, keepdims=True))
           - jnp.exp(jnp.sum(lp[2:3] * lp[3:4], axis=-1, keepdims=True)) + LAM_INIT)

    def finish(block):
        o12 = acc_ref[0:DF_DV, :] / acc_ref[DF_DV:DF_DV + 1, :]
        o_t = o12[:, 0:DF_TQ] - lam * o12[:, DF_TQ:2 * DF_TQ]
        rows = pl.ds(pl.multiple_of(block * DF_TQ, DF_TQ), DF_TQ)
        o_ref[rows, :] = (_rms_rows(o_t.T, gsub_ref[...]) * (1.0 - LAM_INIT)).astype(BF16)

    assert n_chunks % DF_SBUF == 0
    split_maps(0, q12n_ref)
    k0 = keys(0)
    for g in range(groups):
        s_ref[0, :, g * DF_TG:(g + 1) * DF_TG] = scores(k0, q12n_ref, g)
    acc_ref[...] = jnp.zeros(acc_ref.shape, F32)
    acc_ref[DF_DV:DF_DV + 1, :] = jnp.ones((1, 2 * DF_TQ), F32)

    def body(block, carry):
        finish(jnp.maximum(block - 1, 0))
        acc_ref[...] = jnp.zeros(acc_ref.shape, F32)
        split_maps(block, q12_ref)
        split_maps(jnp.minimum(block + 1, n_blocks - 1), q12n_ref)
        m = tuple(jnp.full((1, DF_TG), -jnp.inf, F32) for _ in range(groups))
        for c in range(n_chunks):
            m = step(c, m)
        return carry

    lax.fori_loop(0, n_blocks, body, 0)
    finish(n_blocks - 1)


def _df_attn(proj, lam_p, g_sub, batch, seq):
    const = lambda b, h: (0, 0)
    return pl.pallas_call(
        functools.partial(_df_kernel, seq=seq),
        out_shape=jax.ShapeDtypeStruct((DF_HEADS, batch * seq, LANES), BF16),
        grid=(batch, DF_HEADS),
        in_specs=[
            pl.BlockSpec((None, seq, LANES), lambda b, h: (CB_DF_Q + h, b, 0)),
            pl.BlockSpec((None, seq, LANES), lambda b, h: (CB_DF_K + h, b, 0)),
            pl.BlockSpec((None, seq, LANES), lambda b, h: (CB_DF_V + h, b, 0)),
            pl.BlockSpec((4, DF_DQ), const),
            pl.BlockSpec((1, DF_DV), const),
        ],
        out_specs=pl.BlockSpec((None, seq, LANES), lambda b, h: (h, b, 0)),
        scratch_shapes=[pltpu.VMEM((2 * DF_TQ, LANES), BF16), pltpu.VMEM((2 * DF_TQ, LANES), BF16),
                        pltpu.VMEM((DF_SBUF, DF_TK, 2 * DF_TQ), F32),
                        pltpu.VMEM((DF_VROWS, 2 * DF_TQ), F32), pltpu.VMEM((DF_VROWS, seq), BF16)],
        compiler_params=_params(("parallel", "parallel")),
        name="df_attn",
    )(proj, proj, proj, lam_p, g_sub)


def _merge_kernel(oa_ref, ob_ref, wa_ref, wb_ref, sa_ref, sb_ref, o_ref, wab_ref, wbb_ref):
    @pl.when(pl.program_id(0) == 0)
    def _():
        _cast_resident(wa_ref, wab_ref)
        _cast_resident(wb_ref, wbb_ref)

    def rows(slab_ref):
        return jnp.concatenate([slab_ref[k] for k in range(slab_ref.shape[0])], axis=1)

    ya = jnp.dot(rows(oa_ref), wab_ref[...], preferred_element_type=F32)
    yb = jnp.dot(rows(ob_ref), wbb_ref[...], preferred_element_type=F32)
    o_ref[...] = (rows(sa_ref).astype(F32) * ya + rows(sb_ref).astype(F32) * yb).astype(BF16)


def _merge(o_a, o_b, w_na_out, w_df_out, proj):
    n = o_a.shape[1]
    gate_slabs = D_MODEL // LANES
    ga, gb = COL_GATE_A // D_MODEL, COL_GATE_B // D_MODEL
    return pl.pallas_call(
        _merge_kernel,
        out_shape=jax.ShapeDtypeStruct((n, D_MODEL), BF16),
        grid=(n // ROW_TM,),
        in_specs=[
            pl.BlockSpec((NA_HEADS, ROW_TM, LANES), lambda i: (0, i, 0)),
            pl.BlockSpec((DF_HEADS, ROW_TM, LANES), lambda i: (0, i, 0)),
            _resident((NA_W, D_MODEL)),
            _resident((DF_VW, D_MODEL)),
            pl.BlockSpec((gate_slabs, ROW_TM, LANES), lambda i: (ga, i, 0)),
            pl.BlockSpec((gate_slabs, ROW_TM, LANES), lambda i: (gb, i, 0)),
        ],
        out_specs=pl.BlockSpec((ROW_TM, D_MODEL), lambda i: (i, 0)),
        scratch_shapes=[pltpu.VMEM((NA_W, D_MODEL), BF16), pltpu.VMEM((DF_VW, D_MODEL), BF16)],
        compiler_params=_params(("arbitrary",)),
        name="merge",
    )(o_a, o_b, w_na_out, w_df_out, proj, proj)


DN_TM = 256
DN_KC = 512


def _ffn_down_kernel(a_ref, w_ref, x_ref, o_ref, wb_ref, *, n_load):
    s = pl.program_id(0)

    @pl.when(s < n_load)
    def _():
        r = pl.ds(pl.multiple_of(s * DN_KC, DN_KC), DN_KC)
        wb_ref[r, :] = w_ref[...].astype(BF16)

    @pl.when(s >= n_load)
    def _():
        o_ref[...] = x_ref[...] + jnp.dot(a_ref[...], wb_ref[...], preferred_element_type=F32)


def _ffn_down(a, w, x):
    n, k = a.shape
    d = w.shape[1]
    n_load = k // DN_KC
    row = lambda s: (jnp.maximum(s - n_load, 0), 0)
    return pl.pallas_call(
        functools.partial(_ffn_down_kernel, n_load=n_load),
        out_shape=jax.ShapeDtypeStruct((n, d), F32),
        grid=(n_load + n // DN_TM,),
        in_specs=[
            pl.BlockSpec((DN_TM, k), row),
            pl.BlockSpec((DN_KC, d), lambda s: (jnp.minimum(s, n_load - 1), 0)),
            pl.BlockSpec((DN_TM, d), row),
        ],
        out_specs=pl.BlockSpec((DN_TM, d), row),
        scratch_shapes=[pltpu.VMEM((k, d), BF16)],
        compiler_params=_params(("arbitrary",)),
        name="ffn_down",
    )(a, w, x)


def _out_proj_norm_kernel(a_ref, w_ref, x_ref, g_ref, o_ref, h_ref, wb_ref):
    @pl.when(pl.program_id(0) == 0)
    def _():
        _cast_resident(w_ref, wb_ref)

    y = x_ref[...] + jnp.dot(a_ref[...], wb_ref[...], preferred_element_type=F32)
    o_ref[...] = y
    h_ref[...] = _rms_rows(y, g_ref[...]).astype(BF16)


def _out_proj_norm(a, w, x, g):
    n, k = a.shape
    d = w.shape[1]
    return pl.pallas_call(
        _out_proj_norm_kernel,
        out_shape=(jax.ShapeDtypeStruct((n, d), F32), jax.ShapeDtypeStruct((n, d), BF16)),
        grid=(n // ROW_TM,),
        in_specs=[
            pl.BlockSpec((ROW_TM, k), lambda i: (i, 0)),
            _resident((k, d)),
            pl.BlockSpec((ROW_TM, d), lambda i: (i, 0)),
            pl.BlockSpec((1, d), lambda i: (0, 0)),
        ],
        out_specs=(pl.BlockSpec((ROW_TM, d), lambda i: (i, 0)), pl.BlockSpec((ROW_TM, d), lambda i: (i, 0))),
        scratch_shapes=[pltpu.VMEM((k, d), BF16)],
        compiler_params=_params(("arbitrary",)),
        name="out_proj",
    )(a, w, x, g)


UP_TM, UP_TN = 2048, 512
UP_RB = 256


def _ffn_up_kernel(h_ref, wg_ref, wu_ref, o_ref):
    wg = wg_ref[...].astype(BF16)
    wu = wu_ref[...].astype(BF16)
    for r in range(UP_TM // UP_RB):
        rows = slice(r * UP_RB, (r + 1) * UP_RB)
        h = h_ref[rows, :]
        gate = jnp.dot(h, wg, preferred_element_type=F32)
        up = jnp.dot(h, wu, preferred_element_type=F32)
        o_ref[rows, :] = (gate * jax.nn.sigmoid(gate) * up).astype(BF16)


def _ffn_up(h, w_gate, w_up):
    n = h.shape[0]
    return pl.pallas_call(
        _ffn_up_kernel,
        out_shape=jax.ShapeDtypeStruct((n, FFN_HID), BF16),
        grid=(n // UP_TM, FFN_HID // UP_TN),
        in_specs=[
            pl.BlockSpec((UP_TM, D_MODEL), lambda i, j: (i, 0)),
            pl.BlockSpec((D_MODEL, UP_TN), lambda i, j: (0, j)),
            pl.BlockSpec((D_MODEL, UP_TN), lambda i, j: (0, j)),
        ],
        out_specs=pl.BlockSpec((UP_TM, UP_TN), lambda i, j: (i, j)),
        compiler_params=_params(("parallel", "arbitrary")),
        name="ffn_up",
    )(h, w_gate, w_up)


def _ple_kernel(x_ref, g_ref, wg_ref, p_ref, wp_ref, o_ref, wgb_ref, wpb_ref):
    @pl.when(pl.program_id(0) == 0)
    def _():
        _cast_resident(wg_ref, wgb_ref)
        _cast_resident(wp_ref, wpb_ref)

    x = x_ref[...]
    h = _rms_rows(x, g_ref[...]).astype(BF16)
    gate = jnp.dot(h, wgb_ref[...], preferred_element_type=F32)
    emb = jnp.dot(p_ref[...].astype(BF16), wpb_ref[...], preferred_element_type=F32)
    o_ref[...] = x + jax.nn.sigmoid(gate) * emb


def _ple(x, g, w_gate, p, w_proj):
    n, d = x.shape
    return pl.pallas_call(
        _ple_kernel,
        out_shape=jax.ShapeDtypeStruct((n, d), F32),
        grid=(n // ROW_TM,),
        in_specs=[
            pl.BlockSpec((ROW_TM, d), lambda i: (i, 0)),
            pl.BlockSpec((1, d), lambda i: (0, 0)),
            _resident((d, d)),
            pl.BlockSpec((ROW_TM, PLE_DIM), lambda i: (i, 0)),
            _resident((PLE_DIM, d)),
        ],
        out_specs=pl.BlockSpec((ROW_TM, d), lambda i: (i, 0)),
        scratch_shapes=[pltpu.VMEM((d, d), BF16), pltpu.VMEM((PLE_DIM, d), BF16)],
        compiler_params=_params(("arbitrary",)),
        name="ple",
    )(x, g, w_gate, p, w_proj)


def _rope_tables(seq):
    inv = 1.0 / (ROPE_THETA ** (jnp.arange(0, DF_DQ, 2, dtype=F32) / DF_DQ))
    ang = jnp.arange(seq, dtype=F32)[:, None] * inv[None, :]
    cos, sin = jnp.cos(ang), jnp.sin(ang)
    return jnp.concatenate([cos, cos, cos, cos], axis=1), jnp.concatenate([-sin, -sin, sin, sin], axis=1)


def _rope_gains(g, scale):
    half = DF_DQ // 2
    g1, g2 = g[:half], g[half:]
    return jnp.stack([jnp.concatenate([g1, g1, g2, g2]), jnp.concatenate([g2, g2, g1, g1])]) * scale


def kernel(x, p, g_mix, w_in, g_na_q, g_na_k, na_rpb, g_df_q, g_df_k, lam_q1, lam_k1, lam_q2, lam_k2,
           g_df_sub, w_na_out, w_df_out, w_o, g_ffn, w_gate, w_up, w_down, g_ple, w_ple_gate, w_ple_proj):
    batch, seq, d = x.shape
    n = batch * seq
    rows = seq // GRID_W
    depth = w_in.shape[0]
    xf = x.reshape(n, d)
    for i in range(depth):
        cos, sin = _rope_tables(seq)
        rq = _rope_gains(g_df_q[i], DF_DQ ** -0.5 * math.log2(math.e))
        rk = _rope_gains(g_df_k[i], 1.0)
        proj = _in_proj(xf, g_mix[i][None], w_in[i], g_na_q[i][None], g_na_k[i][None],
                        cos, sin, rq, rk, seq)

        o_a = _na_attn(proj, _na_bias_rows(na_rpb[i], rows), batch, seq)
        lam_p = jnp.stack([lam_q1[i], lam_k1[i], lam_q2[i], lam_k2[i]], axis=0)
        o_b = _df_attn(proj, lam_p, g_df_sub[i][None], batch, seq)

        merged = _merge(o_a, o_b, w_na_out[i], w_df_out[i], proj)
        xf, hf = _out_proj_norm(merged, w_o[i], xf, g_ffn[i][None])

        act = _ffn_up(hf, w_gate[i], w_up[i])
        xf = _ffn_down(act, w_down[i], xf)

        xf = _ple(xf, g_ple[i][None], w_ple_gate[i], p[i].reshape(n, PLE_DIM), w_ple_proj[i])
    return xf.reshape(batch, seq, d)
```
